```python
import jax, jax.numpy as jnp
from jax import lax
import numpy as np

D_MODEL = 2048
BATCH = 8
SEQ = 4096
DEPTH = 1

CHUNK = 64
Q_BLOCK = 128
N_MEM = 256
EPS = 1e-6
MLA_HEADS = 16
Q_LORA_RANK = 512
KV_LORA_RANK = 512
QK_NOPE_DIM = 128
QK_ROPE_DIM = 64
V_HEAD_DIM = 128
ROPE_BASE = 10000.0
MLA_WIDTH = MLA_HEADS * V_HEAD_DIM
MLSTM_HEADS = 8
MLSTM_QK_DIM = 128
MLSTM_V_DIM = 256
MLSTM_CONV = 4
MLSTM_QK_WIDTH = MLSTM_HEADS * MLSTM_QK_DIM
MLSTM_V_WIDTH = MLSTM_HEADS * MLSTM_V_DIM
CROSS_HEADS = 4
CROSS_HEAD_DIM = 128
CROSS_WIDTH = CROSS_HEADS * CROSS_HEAD_DIM
D_FF = 5632
FFN_CONV = 3
IN_SPLITS = (Q_LORA_RANK, KV_LORA_RANK, QK_ROPE_DIM,
             MLSTM_QK_WIDTH, MLSTM_QK_WIDTH, MLSTM_V_WIDTH, MLSTM_HEADS, MLSTM_HEADS, MLSTM_V_WIDTH,
             D_MODEL, D_MODEL)
IN_WIDTH = (Q_LORA_RANK + KV_LORA_RANK + QK_ROPE_DIM + 2 * MLSTM_QK_WIDTH + MLSTM_V_WIDTH
            + 2 * MLSTM_HEADS + MLSTM_V_WIDTH + 2 * D_MODEL)

kernel_name = "hybrid_mla_mlstm_gated_streaming_layer"


def rms_norm(x, g):
    xf = x.astype(jnp.float32)
    y = xf * lax.rsqrt(jnp.mean(xf * xf, axis=-1, keepdims=True) + EPS)
    return (y * g.astype(jnp.float32)).astype(x.dtype)


def causal_dwconv(x, w):
    k_width, s = w.shape[0], x.shape[1]
    xp = jnp.pad(x, ((0, 0), (k_width - 1, 0), (0, 0)))
    return sum(xp[:, j:j + s] * w[j] for j in range(k_width))


def rope_tables(positions):
    inv_freq = ROPE_BASE ** (-jnp.arange(0, QK_ROPE_DIM, 2, dtype=jnp.float32) / QK_ROPE_DIM)
    ang = positions.astype(jnp.float32)[..., None] * inv_freq
    return jnp.cos(ang), jnp.sin(ang)


def apply_rope(x, cos, sin):
    half = x.shape[-1] // 2
    x1, x2 = x[..., :half], x[..., half:]
    return jnp.concatenate([x1 * cos - x2 * sin, x2 * cos + x1 * sin], axis=-1).astype(x.dtype)


def mla_attention(z_qa, z_kv, z_kpe, cos, sin, g_qa, w_qb, g_kva, w_kvb,
                  g_qn_nope, g_qn_pe, g_kn_nope, g_kn_pe):
    b, s, _ = z_qa.shape
    q = (rms_norm(z_qa, g_qa) @ w_qb).reshape(b, s, MLA_HEADS, QK_NOPE_DIM + QK_ROPE_DIM)
    kv = (rms_norm(z_kv, g_kva) @ w_kvb).reshape(b, s, MLA_HEADS, QK_NOPE_DIM + V_HEAD_DIM)
    q_nope = rms_norm(q[..., :QK_NOPE_DIM], g_qn_nope)
    q_pe = apply_rope(rms_norm(q[..., QK_NOPE_DIM:], g_qn_pe), cos[:, :, None], sin[:, :, None])
    k_nope = rms_norm(kv[..., :QK_NOPE_DIM], g_kn_nope)
    v = kv[..., QK_NOPE_DIM:]
    k_pe = apply_rope(rms_norm(z_kpe, g_kn_pe), cos, sin)
    scale = (QK_NOPE_DIM + QK_ROPE_DIM) ** -0.5
    outs = []
    for j in range(s // Q_BLOCK):
        q0, q1 = j * Q_BLOCK, (j + 1) * Q_BLOCK
        k_end = q1
        sc = (jnp.einsum('bqhd,bkhd->bhqk', q_nope[:, q0:q1], k_nope[:, :k_end])
              + jnp.einsum('bqhr,bkr->bhqk', q_pe[:, q0:q1], k_pe[:, :k_end])).astype(jnp.float32) * scale
        q_chunk = (q0 + jnp.arange(Q_BLOCK)) // CHUNK
        k_chunk = jnp.arange(k_end) // CHUNK
        sc = jnp.where(k_chunk[None, :] <= q_chunk[:, None], sc, -jnp.inf)
        p = jax.nn.softmax(sc, axis=-1).astype(v.dtype)
        outs.append(jnp.einsum('bhqk,bkhd->bqhd', p, v[:, :k_end]))
    return jnp.concatenate(outs, axis=1).reshape(b, s, MLA_WIDTH)


def mlstm(zq, zk, zv, zi, zf, zo, conv_qk, b_if, g_hnorm):
    b, s, _ = zq.shape
    h_, dk, dv, l_ = MLSTM_HEADS, MLSTM_QK_DIM, MLSTM_V_DIM, CHUNK
    nc = s // l_
    qk = jax.nn.silu(causal_dwconv(jnp.concatenate([zq, zk], axis=-1), conv_qk))

    def to_chunks(t, d):
        return t.astype(jnp.float32).reshape(b, nc, l_, h_, d).transpose(1, 0, 3, 2, 4)

    q = to_chunks(qk[..., :MLSTM_QK_WIDTH], dk) * (dk ** -0.5)
    k = to_chunks(qk[..., MLSTM_QK_WIDTH:], dk)
    v = to_chunks(zv, dv)
    gates = (jnp.concatenate([zi, zf], axis=-1) + b_if).astype(jnp.float32)

    def gate_chunks(t):
        return t.reshape(b, nc, l_, h_).transpose(1, 0, 3, 2)

    log_i = gate_chunks(gates[..., :h_])
    bcum = jnp.cumsum(gate_chunks(jax.nn.log_sigmoid(gates[..., h_:])), axis=-1)
    causal = jnp.tril(jnp.ones((l_, l_), dtype=bool))

    def step(carry, inp):
        c_mat, n_vec, m = carry
        qc, kc, vc, bc, ic = inp
        logw = jnp.where(causal, bc[..., :, None] - bc[..., None, :] + ic[..., None, :], -jnp.inf)
        inter = bc + m[..., None]
        m_t = jnp.maximum(inter, jnp.max(logw, axis=-1))
        w_intra = jnp.exp(logw - m_t[..., None])
        w_inter = jnp.exp(inter - m_t)
        sc = jnp.einsum('bhtd,bhsd->bhts', qc, kc) * w_intra
        num = w_inter[..., None] * jnp.einsum('bhtd,bhde->bhte', qc, c_mat) + jnp.einsum('bhts,bhse->bhte', sc, vc)
        den = w_inter * jnp.einsum('bhtd,bhd->bht', qc, n_vec) + jnp.sum(sc, axis=-1)
        h = num / jnp.maximum(jnp.abs(den), jnp.exp(-m_t))[..., None]
        b_last = bc[..., -1]
        logu = b_last[..., None] - bc + ic
        m_new = jnp.maximum(b_last + m, jnp.max(logu, axis=-1))
        decay = jnp.exp(b_last + m - m_new)
        u = jnp.exp(logu - m_new[..., None])
        c_mat = decay[..., None, None] * c_mat + jnp.einsum('bhs,bhsd,bhse->bhde', u, kc, vc)
        n_vec = decay[..., None] * n_vec + jnp.einsum('bhs,bhsd->bhd', u, kc)
        return (c_mat, n_vec, m_new), h

    init = (jnp.zeros((b, h_, dk, dv), jnp.float32), jnp.zeros((b, h_, dk), jnp.float32),
            jnp.zeros((b, h_), jnp.float32))
    _, h = lax.scan(step, init, (q, k, v, bcum, log_i))
    h = h.transpose(1, 0, 3, 2, 4).reshape(b, s, h_, dv)
    h = rms_norm(h, g_hnorm).reshape(b, s, MLSTM_V_WIDTH).astype(zq.dtype)
    return h * jax.nn.sigmoid(zo)


def cross_attend(u, mm, wq_c, wk_c, wv_c, g_cq, g_ck, wo_c):
    b, s, _ = u.shape
    n_mem = mm.shape[1]
    q = rms_norm((u @ wq_c).reshape(b, s, CROSS_HEADS, CROSS_HEAD_DIM), g_cq)
    k = rms_norm((mm @ wk_c).reshape(b, n_mem, CROSS_HEADS, CROSS_HEAD_DIM), g_ck)
    v = (mm @ wv_c).reshape(b, n_mem, CROSS_HEADS, CROSS_HEAD_DIM)
    sc = jnp.einsum('bqhd,bkhd->bhqk', q, k).astype(jnp.float32) * (CROSS_HEAD_DIM ** -0.5)
    p = jax.nn.softmax(sc, axis=-1).astype(v.dtype)
    o = jnp.einsum('bhqk,bkhd->bqhd', p, v).reshape(b, s, CROSS_WIDTH)
    return o @ wo_c


def conv_glu_ffn(u, w_up, conv_ffn, b_conv_ffn, w_down):
    h = causal_dwconv(u @ w_up, conv_ffn) + b_conv_ffn
    return (jax.nn.silu(h[..., :D_FF]) * h[..., D_FF:]) @ w_down


def hybrid_layer(x, mem, cos, sin, g_mix, w_in, g_qa, w_qb, g_kva, w_kvb, g_qn_nope, g_qn_pe,
                 g_kn_nope, g_kn_pe, conv_qk, b_if, g_hnorm, p_a, p_b, w_out, g_cross, g_mem,
                 wq_c, wk_c, wv_c, g_cq, g_ck, wo_c, g_ffn, w_up, conv_ffn, b_conv_ffn, w_down):
    split_points = [int(p) for p in np.cumsum(IN_SPLITS)[:-1]]
    z = rms_norm(x, g_mix) @ w_in
    z_qa, z_kv, z_kpe, zq, zk, zv, zi, zf, zo, gate_a, gate_b = jnp.split(z, split_points, axis=-1)
    y_a = mla_attention(z_qa, z_kv, z_kpe, cos, sin, g_qa, w_qb, g_kva, w_kvb,
                        g_qn_nope, g_qn_pe, g_kn_nope, g_kn_pe)
    y_b = mlstm(zq, zk, zv, zi, zf, zo, conv_qk, b_if, g_hnorm)
    merged = jax.nn.sigmoid(gate_a) * (y_a @ p_a) + jax.nn.sigmoid(gate_b) * (y_b @ p_b)
    x = x + merged @ w_out
    x = x + cross_attend(rms_norm(x, g_cross), rms_norm(mem, g_mem), wq_c, wk_c, wv_c, g_cq, g_ck, wo_c)
    x = x + conv_glu_ffn(rms_norm(x, g_ffn), w_up, conv_ffn, b_conv_ffn, w_down)
    return x


def _fwd_setup_inputs(seed: int = 0) -> dict:
    key = jax.random.key(seed)
    ks = iter(jax.random.split(key, 64))
    f32 = jnp.float32

    def w(shape, fan_in):
        return jax.random.normal(next(ks), (DEPTH,) + shape, f32) * (fan_in ** -0.5)

    def gain(shape):
        return 1.0 + 0.05 * jax.random.normal(next(ks), (DEPTH,) + shape, f32)

    x = jax.random.normal(next(ks), (BATCH, SEQ, D_MODEL), f32)
    mem = jax.random.normal(next(ks), (BATCH, N_MEM, D_MODEL), f32)
    offset = jax.random.randint(next(ks), (BATCH, 1), 0, 4096, dtype=jnp.int32)
    positions = (offset + jnp.arange(SEQ, dtype=jnp.int32)[None, :]).astype(jnp.int32)
    b_i = 0.1 * jax.random.normal(next(ks), (DEPTH, MLSTM_HEADS), f32)
    b_f = jnp.linspace(3.0, 6.0, MLSTM_HEADS, dtype=f32)[None, :] + 0.1 * jax.random.normal(next(ks), (DEPTH, MLSTM_HEADS), f32)
    return {
        "x": x,
        "mem": mem,
        "positions": positions,
        "g_mix": gain((D_MODEL,)),
        "w_in": w((D_MODEL, IN_WIDTH), D_MODEL),
        "g_qa": gain((Q_LORA_RANK,)),
        "w_qb": w((Q_LORA_RANK, MLA_HEADS * (QK_NOPE_DIM + QK_ROPE_DIM)), Q_LORA_RANK),
        "g_kva": gain((KV_LORA_RANK,)),
        "w_kvb": w((KV_LORA_RANK, MLA_HEADS * (QK_NOPE_DIM + V_HEAD_DIM)), KV_LORA_RANK),
        "g_qn_nope": gain((QK_NOPE_DIM,)),
        "g_qn_pe": gain((QK_ROPE_DIM,)),
        "g_kn_nope": gain((QK_NOPE_DIM,)),
        "g_kn_pe": gain((QK_ROPE_DIM,)),
        "conv_qk": w((MLSTM_CONV, 2 * MLSTM_QK_WIDTH), MLSTM_CONV),
        "b_if": jnp.concatenate([b_i, b_f], axis=-1),
        "g_hnorm": gain((MLSTM_HEADS, MLSTM_V_DIM)),
        "p_a": w((MLA_WIDTH, D_MODEL), MLA_WIDTH),
        "p_b": w((MLSTM_V_WIDTH, D_MODEL), MLSTM_V_WIDTH),
        "w_out": w((D_MODEL, D_MODEL), D_MODEL),
        "g_cross": gain((D_MODEL,)),
        "g_mem": gain((D_MODEL,)),
        "wq_c": w((D_MODEL, CROSS_WIDTH), D_MODEL),
        "wk_c": w((D_MODEL, CROSS_WIDTH), D_MODEL),
        "wv_c": w((D_MODEL, CROSS_WIDTH), D_MODEL),
        "g_cq": gain((CROSS_HEAD_DIM,)),
        "g_ck": gain((CROSS_HEAD_DIM,)),
        "wo_c": w((CROSS_WIDTH, D_MODEL), CROSS_WIDTH),
        "g_ffn": gain((D_MODEL,)),
        "w_up": w((D_MODEL, 2 * D_FF), D_MODEL),
        "conv_ffn": w((FFN_CONV, 2 * D_FF), FFN_CONV),
        "b_conv_ffn": 0.01 * jax.random.normal(next(ks), (DEPTH, 2 * D_FF), f32),
        "w_down": w((D_FF, D_MODEL), D_FF),
    }


def _fwd_reference(x, mem, positions, g_mix, w_in, g_qa, w_qb, g_kva, w_kvb, g_qn_nope, g_qn_pe,
              g_kn_nope, g_kn_pe, conv_qk, b_if, g_hnorm, p_a, p_b, w_out, g_cross, g_mem,
              wq_c, wk_c, wv_c, g_cq, g_ck, wo_c, g_ffn, w_up, conv_ffn, b_conv_ffn, w_down):
    cos, sin = rope_tables(positions)
    for l in range(DEPTH):
        x = hybrid_layer(x, mem, cos, sin, g_mix[l], w_in[l], g_qa[l], w_qb[l], g_kva[l], w_kvb[l],
                         g_qn_nope[l], g_qn_pe[l], g_kn_nope[l], g_kn_pe[l], conv_qk[l], b_if[l],
                         g_hnorm[l], p_a[l], p_b[l], w_out[l], g_cross[l], g_mem[l], wq_c[l], wk_c[l],
                         wv_c[l], g_cq[l], g_ck[l], wo_c[l], g_ffn[l], w_up[l], conv_ffn[l],
                         b_conv_ffn[l], w_down[l])
    return x


import jax as _jax
import jax.numpy as _jnp

TWIN_FORMAT = 'train_step'
FWD_PARAMS = ['x', 'mem', 'positions', 'g_mix', 'w_in', 'g_qa', 'w_qb', 'g_kva', 'w_kvb', 'g_qn_nope', 'g_qn_pe', 'g_kn_nope', 'g_kn_pe', 'conv_qk', 'b_if', 'g_hnorm', 'p_a', 'p_b', 'w_out', 'g_cross', 'g_mem', 'wq_c', 'wk_c', 'wv_c', 'g_cq', 'g_ck', 'wo_c', 'g_ffn', 'w_up', 'conv_ffn', 'b_conv_ffn', 'w_down']
TWIN_WEIGHTS = ['g_mix', 'w_in', 'g_qa', 'w_qb', 'g_kva', 'w_kvb', 'g_qn_nope', 'g_qn_pe', 'g_kn_nope', 'g_kn_pe', 'conv_qk', 'b_if', 'g_hnorm', 'p_a', 'p_b', 'w_out', 'g_cross', 'g_mem', 'wq_c', 'wk_c', 'wv_c', 'g_cq', 'g_ck', 'wo_c', 'g_ffn', 'w_up', 'conv_ffn', 'b_conv_ffn', 'w_down']
TWIN_DIFF_INPUT = 'x'
TWIN_INPUTS = ['x', 'mem', 'positions', 'g_mix', 'w_in', 'g_qa', 'w_qb', 'g_kva', 'w_kvb', 'g_qn_nope', 'g_qn_pe', 'g_kn_nope', 'g_kn_pe', 'conv_qk', 'b_if', 'g_hnorm', 'p_a', 'p_b', 'w_out', 'g_cross', 'g_mem', 'wq_c', 'wk_c', 'wv_c', 'g_cq', 'g_ck', 'wo_c', 'g_ffn', 'w_up', 'conv_ffn', 'b_conv_ffn', 'w_down', 'loss_target', 'm_g_mix', 'm_w_in', 'm_g_qa', 'm_w_qb', 'm_g_kva', 'm_w_kvb', 'm_g_qn_nope', 'm_g_qn_pe', 'm_g_kn_nope', 'm_g_kn_pe', 'm_conv_qk', 'm_b_if', 'm_g_hnorm', 'm_p_a', 'm_p_b', 'm_w_out', 'm_g_cross', 'm_g_mem', 'm_wq_c', 'm_wk_c', 'm_wv_c', 'm_g_cq', 'm_g_ck', 'm_wo_c', 'm_g_ffn', 'm_w_up', 'm_conv_ffn', 'm_b_conv_ffn', 'm_w_down', 'v_g_mix', 'v_w_in', 'v_g_qa', 'v_w_qb', 'v_g_kva', 'v_w_kvb', 'v_g_qn_nope', 'v_g_qn_pe', 'v_g_kn_nope', 'v_g_kn_pe', 'v_conv_qk', 'v_b_if', 'v_g_hnorm', 'v_p_a', 'v_p_b', 'v_w_out', 'v_g_cross', 'v_g_mem', 'v_wq_c', 'v_wk_c', 'v_wv_c', 'v_g_cq', 'v_g_ck', 'v_wo_c', 'v_g_ffn', 'v_w_up', 'v_conv_ffn', 'v_b_conv_ffn', 'v_w_down']
TWIN_OUTPUTS = ['loss', 'grad_x', 'grad_g_mix', 'grad_w_in', 'grad_g_qa', 'grad_w_qb', 'grad_g_kva', 'grad_w_kvb', 'grad_g_qn_nope', 'grad_g_qn_pe', 'grad_g_kn_nope', 'grad_g_kn_pe', 'grad_conv_qk', 'grad_b_if', 'grad_g_hnorm', 'grad_p_a', 'grad_p_b', 'grad_w_out', 'grad_g_cross', 'grad_g_mem', 'grad_wq_c', 'grad_wk_c', 'grad_wv_c', 'grad_g_cq', 'grad_g_ck', 'grad_wo_c', 'grad_g_ffn', 'grad_w_up', 'grad_conv_ffn', 'grad_b_conv_ffn', 'grad_w_down', 'delta_g_mix', 'delta_w_in', 'delta_g_qa', 'delta_w_qb', 'delta_g_kva', 'delta_w_kvb', 'delta_g_qn_nope', 'delta_g_qn_pe', 'delta_g_kn_nope', 'delta_g_kn_pe', 'delta_conv_qk', 'delta_b_if', 'delta_g_hnorm', 'delta_p_a', 'delta_p_b', 'delta_w_out', 'delta_g_cross', 'delta_g_mem', 'delta_wq_c', 'delta_wk_c', 'delta_wv_c', 'delta_g_cq', 'delta_g_ck', 'delta_wo_c', 'delta_g_ffn', 'delta_w_up', 'delta_conv_ffn', 'delta_b_conv_ffn', 'delta_w_down', 'new_m_g_mix', 'new_m_w_in', 'new_m_g_qa', 'new_m_w_qb', 'new_m_g_kva', 'new_m_w_kvb', 'new_m_g_qn_nope', 'new_m_g_qn_pe', 'new_m_g_kn_nope', 'new_m_g_kn_pe', 'new_m_conv_qk', 'new_m_b_if', 'new_m_g_hnorm', 'new_m_p_a', 'new_m_p_b', 'new_m_w_out', 'new_m_g_cross', 'new_m_g_mem', 'new_m_wq_c', 'new_m_wk_c', 'new_m_wv_c', 'new_m_g_cq', 'new_m_g_ck', 'new_m_wo_c', 'new_m_g_ffn', 'new_m_w_up', 'new_m_conv_ffn', 'new_m_b_conv_ffn', 'new_m_w_down', 'new_v_g_mix', 'new_v_w_in', 'new_v_g_qa', 'new_v_w_qb', 'new_v_g_kva', 'new_v_w_kvb', 'new_v_g_qn_nope', 'new_v_g_qn_pe', 'new_v_g_kn_nope', 'new_v_g_kn_pe', 'new_v_conv_qk', 'new_v_b_if', 'new_v_g_hnorm', 'new_v_p_a', 'new_v_p_b', 'new_v_w_out', 'new_v_g_cross', 'new_v_g_mem', 'new_v_wq_c', 'new_v_wk_c', 'new_v_wv_c', 'new_v_g_cq', 'new_v_g_ck', 'new_v_wo_c', 'new_v_g_ffn', 'new_v_w_up', 'new_v_conv_ffn', 'new_v_b_conv_ffn', 'new_v_w_down']
TWIN_LEAF_KINDS = {'loss': 'loss', 'grad_x': 'grad_x', 'grad_g_mix': 'grad_w', 'grad_w_in': 'grad_w', 'grad_g_qa': 'grad_w', 'grad_w_qb': 'grad_w', 'grad_g_kva': 'grad_w', 'grad_w_kvb': 'grad_w', 'grad_g_qn_nope': 'grad_w', 'grad_g_qn_pe': 'grad_w', 'grad_g_kn_nope': 'grad_w', 'grad_g_kn_pe': 'grad_w', 'grad_conv_qk': 'grad_w', 'grad_b_if': 'grad_w', 'grad_g_hnorm': 'grad_w', 'grad_p_a': 'grad_w', 'grad_p_b': 'grad_w', 'grad_w_out': 'grad_w', 'grad_g_cross': 'grad_w', 'grad_g_mem': 'grad_w', 'grad_wq_c': 'grad_w', 'grad_wk_c': 'grad_w', 'grad_wv_c': 'grad_w', 'grad_g_cq': 'grad_w', 'grad_g_ck': 'grad_w', 'grad_wo_c': 'grad_w', 'grad_g_ffn': 'grad_w', 'grad_w_up': 'grad_w', 'grad_conv_ffn': 'grad_w', 'grad_b_conv_ffn': 'grad_w', 'grad_w_down': 'grad_w', 'delta_g_mix': 'delta_w', 'delta_w_in': 'delta_w', 'delta_g_qa': 'delta_w', 'delta_w_qb': 'delta_w', 'delta_g_kva': 'delta_w', 'delta_w_kvb': 'delta_w', 'delta_g_qn_nope': 'delta_w', 'delta_g_qn_pe': 'delta_w', 'delta_g_kn_nope': 'delta_w', 'delta_g_kn_pe': 'delta_w', 'delta_conv_qk': 'delta_w', 'delta_b_if': 'delta_w', 'delta_g_hnorm': 'delta_w', 'delta_p_a': 'delta_w', 'delta_p_b': 'delta_w', 'delta_w_out': 'delta_w', 'delta_g_cross': 'delta_w', 'delta_g_mem': 'delta_w', 'delta_wq_c': 'delta_w', 'delta_wk_c': 'delta_w', 'delta_wv_c': 'delta_w', 'delta_g_cq': 'delta_w', 'delta_g_ck': 'delta_w', 'delta_wo_c': 'delta_w', 'delta_g_ffn': 'delta_w', 'delta_w_up': 'delta_w', 'delta_conv_ffn': 'delta_w', 'delta_b_conv_ffn': 'delta_w', 'delta_w_down': 'delta_w', 'new_m_g_mix': 'new_m', 'new_m_w_in': 'new_m', 'new_m_g_qa': 'new_m', 'new_m_w_qb': 'new_m', 'new_m_g_kva': 'new_m', 'new_m_w_kvb': 'new_m', 'new_m_g_qn_nope': 'new_m', 'new_m_g_qn_pe': 'new_m', 'new_m_g_kn_nope': 'new_m', 'new_m_g_kn_pe': 'new_m', 'new_m_conv_qk': 'new_m', 'new_m_b_if': 'new_m', 'new_m_g_hnorm': 'new_m', 'new_m_p_a': 'new_m', 'new_m_p_b': 'new_m', 'new_m_w_out': 'new_m', 'new_m_g_cross': 'new_m', 'new_m_g_mem': 'new_m', 'new_m_wq_c': 'new_m', 'new_m_wk_c': 'new_m', 'new_m_wv_c': 'new_m', 'new_m_g_cq': 'new_m', 'new_m_g_ck': 'new_m', 'new_m_wo_c': 'new_m', 'new_m_g_ffn': 'new_m', 'new_m_w_up': 'new_m', 'new_m_conv_ffn': 'new_m', 'new_m_b_conv_ffn': 'new_m', 'new_m_w_down': 'new_m', 'new_v_g_mix': 'new_v', 'new_v_w_in': 'new_v', 'new_v_g_qa': 'new_v', 'new_v_w_qb': 'new_v', 'new_v_g_kva': 'new_v', 'new_v_w_kvb': 'new_v', 'new_v_g_qn_nope': 'new_v', 'new_v_g_qn_pe': 'new_v', 'new_v_g_kn_nope': 'new_v', 'new_v_g_kn_pe': 'new_v', 'new_v_conv_qk': 'new_v', 'new_v_b_if': 'new_v', 'new_v_g_hnorm': 'new_v', 'new_v_p_a': 'new_v', 'new_v_p_b': 'new_v', 'new_v_w_out': 'new_v', 'new_v_g_cross': 'new_v', 'new_v_g_mem': 'new_v', 'new_v_wq_c': 'new_v', 'new_v_wk_c': 'new_v', 'new_v_wv_c': 'new_v', 'new_v_g_cq': 'new_v', 'new_v_g_ck': 'new_v', 'new_v_wo_c': 'new_v', 'new_v_g_ffn': 'new_v', 'new_v_w_up': 'new_v', 'new_v_conv_ffn': 'new_v', 'new_v_b_conv_ffn': 'new_v', 'new_v_w_down': 'new_v'}


def _forward(args):
    return _fwd_reference(*[args[k] for k in FWD_PARAMS])


def _output_shape():
    def fwd():
        inp = _fwd_setup_inputs(0)
        return _fwd_reference(*[inp[k] for k in FWD_PARAMS])
    out = _jax.eval_shape(fwd)
    return out.shape, out.dtype

N_MICROBATCH = 1
ADAM_LR = 0.001
ADAM_B1 = 0.9
ADAM_B2 = 0.999
ADAM_EPS = 1e-08
ADAM_WD = 0.01
ADAM_STEP = 10
PER_EXAMPLE_BATCH_AXIS = {'x': 0, 'mem': 0, 'positions': 0, 'loss_target': 0}
SHARED_INPUTS = []
_WEIGHT_DTYPES = {'g_mix': _jnp.float32, 'w_in': _jnp.float32, 'g_qa': _jnp.float32, 'w_qb': _jnp.float32, 'g_kva': _jnp.float32, 'w_kvb': _jnp.float32, 'g_qn_nope': _jnp.float32, 'g_qn_pe': _jnp.float32, 'g_kn_nope': _jnp.float32, 'g_kn_pe': _jnp.float32, 'conv_qk': _jnp.float32, 'b_if': _jnp.float32, 'g_hnorm': _jnp.float32, 'p_a': _jnp.float32, 'p_b': _jnp.float32, 'w_out': _jnp.float32, 'g_cross': _jnp.float32, 'g_mem': _jnp.float32, 'wq_c': _jnp.float32, 'wk_c': _jnp.float32, 'wv_c': _jnp.float32, 'g_cq': _jnp.float32, 'g_ck': _jnp.float32, 'wo_c': _jnp.float32, 'g_ffn': _jnp.float32, 'w_up': _jnp.float32, 'conv_ffn': _jnp.float32, 'b_conv_ffn': _jnp.float32, 'w_down': _jnp.float32}
MOMENT_SCALE = {'g_mix': 4.174574e-01, 'w_in': 1.077251e-01, 'g_qa': 3.661941e-02, 'w_qb': 1.491650e-02, 'g_kva': 1.274001e-01, 'w_kvb': 2.329702e-02, 'g_qn_nope': 2.737348e-01, 'g_qn_pe': 2.358829e-01, 'g_kn_nope': 2.720857e-01, 'g_kn_pe': 2.358105e-01, 'conv_qk': 7.503370e-02, 'b_if': 7.359579e-01, 'g_hnorm': 1.357462e+00, 'p_a': 2.891389e-02, 'p_b': 1.926769e-01, 'w_out': 1.944041e-01, 'g_cross': 2.839495e-02, 'g_mem': 2.035382e-01, 'wq_c': 5.327224e-02, 'wk_c': 5.367335e-02, 'wv_c': 2.376675e-01, 'g_cq': 2.463095e+00, 'g_ck': 2.481013e+00, 'wo_c': 1.170688e-01, 'g_ffn': 1.270760e+01, 'w_up': 1.319102e-01, 'conv_ffn': 1.799692e+00, 'b_conv_ffn': 1.583169e+00, 'w_down': 1.395972e-01}


def _to_microbatches(a, axis):
    t = _jnp.moveaxis(a, axis, 0)
    t = t.reshape((N_MICROBATCH, t.shape[0] // N_MICROBATCH) + t.shape[1:])
    return _jnp.moveaxis(t, 1, axis + 1)


def setup_inputs(seed: int = 0) -> dict:
    inp = _fwd_setup_inputs(seed)
    key = _jax.random.fold_in(_jax.random.key(seed), 7919)
    shape, _ = _output_shape()
    out = dict(inp)
    out["loss_target"] = _jax.random.normal(_jax.random.fold_in(key, 0), shape, _jnp.float32)
    for i, name in enumerate(TWIN_WEIGHTS):
        w = inp[name].astype(_jnp.float32)
        if MOMENT_SCALE is None:
            s = _jnp.sqrt(_jnp.mean(_jnp.square(w)) + 1e-30)
        else:
            s = MOMENT_SCALE[name]
        km, kv = _jax.random.split(_jax.random.fold_in(key, i + 1))
        out[name] = w
        out["m_" + name] = s * _jax.random.normal(km, w.shape, _jnp.float32)
        out["v_" + name] = (s * s) * _jax.random.uniform(kv, w.shape, _jnp.float32, 0.5, 1.5)
    if N_MICROBATCH > 1:
        for name, axis in PER_EXAMPLE_BATCH_AXIS.items():
            out[name] = _to_microbatches(out[name], axis)
    return {'x': out['x'], 'mem': out['mem'], 'positions': out['positions'], 'g_mix': out['g_mix'], 'w_in': out['w_in'], 'g_qa': out['g_qa'], 'w_qb': out['w_qb'], 'g_kva': out['g_kva'], 'w_kvb': out['w_kvb'], 'g_qn_nope': out['g_qn_nope'], 'g_qn_pe': out['g_qn_pe'], 'g_kn_nope': out['g_kn_nope'], 'g_kn_pe': out['g_kn_pe'], 'conv_qk': out['conv_qk'], 'b_if': out['b_if'], 'g_hnorm': out['g_hnorm'], 'p_a': out['p_a'], 'p_b': out['p_b'], 'w_out': out['w_out'], 'g_cross': out['g_cross'], 'g_mem': out['g_mem'], 'wq_c': out['wq_c'], 'wk_c': out['wk_c'], 'wv_c': out['wv_c'], 'g_cq': out['g_cq'], 'g_ck': out['g_ck'], 'wo_c': out['wo_c'], 'g_ffn': out['g_ffn'], 'w_up': out['w_up'], 'conv_ffn': out['conv_ffn'], 'b_conv_ffn': out['b_conv_ffn'], 'w_down': out['w_down'], 'loss_target': out['loss_target'], 'm_g_mix': out['m_g_mix'], 'm_w_in': out['m_w_in'], 'm_g_qa': out['m_g_qa'], 'm_w_qb': out['m_w_qb'], 'm_g_kva': out['m_g_kva'], 'm_w_kvb': out['m_w_kvb'], 'm_g_qn_nope': out['m_g_qn_nope'], 'm_g_qn_pe': out['m_g_qn_pe'], 'm_g_kn_nope': out['m_g_kn_nope'], 'm_g_kn_pe': out['m_g_kn_pe'], 'm_conv_qk': out['m_conv_qk'], 'm_b_if': out['m_b_if'], 'm_g_hnorm': out['m_g_hnorm'], 'm_p_a': out['m_p_a'], 'm_p_b': out['m_p_b'], 'm_w_out': out['m_w_out'], 'm_g_cross': out['m_g_cross'], 'm_g_mem': out['m_g_mem'], 'm_wq_c': out['m_wq_c'], 'm_wk_c': out['m_wk_c'], 'm_wv_c': out['m_wv_c'], 'm_g_cq': out['m_g_cq'], 'm_g_ck': out['m_g_ck'], 'm_wo_c': out['m_wo_c'], 'm_g_ffn': out['m_g_ffn'], 'm_w_up': out['m_w_up'], 'm_conv_ffn': out['m_conv_ffn'], 'm_b_conv_ffn': out['m_b_conv_ffn'], 'm_w_down': out['m_w_down'], 'v_g_mix': out['v_g_mix'], 'v_w_in': out['v_w_in'], 'v_g_qa': out['v_g_qa'], 'v_w_qb': out['v_w_qb'], 'v_g_kva': out['v_g_kva'], 'v_w_kvb': out['v_w_kvb'], 'v_g_qn_nope': out['v_g_qn_nope'], 'v_g_qn_pe': out['v_g_qn_pe'], 'v_g_kn_nope': out['v_g_kn_nope'], 'v_g_kn_pe': out['v_g_kn_pe'], 'v_conv_qk': out['v_conv_qk'], 'v_b_if': out['v_b_if'], 'v_g_hnorm': out['v_g_hnorm'], 'v_p_a': out['v_p_a'], 'v_p_b': out['v_p_b'], 'v_w_out': out['v_w_out'], 'v_g_cross': out['v_g_cross'], 'v_g_mem': out['v_g_mem'], 'v_wq_c': out['v_wq_c'], 'v_wk_c': out['v_wk_c'], 'v_wv_c': out['v_wv_c'], 'v_g_cq': out['v_g_cq'], 'v_g_ck': out['v_g_ck'], 'v_wo_c': out['v_wo_c'], 'v_g_ffn': out['v_g_ffn'], 'v_w_up': out['v_w_up'], 'v_conv_ffn': out['v_conv_ffn'], 'v_b_conv_ffn': out['v_b_conv_ffn'], 'v_w_down': out['v_w_down']}


def _loss(weights, diff, rest, loss_target):
    with _jax.named_scope("forward"):
        args = {**rest, TWIN_DIFF_INPUT: diff, **{k: w.astype(_WEIGHT_DTYPES[k]) for k, w in weights.items()}}
        y = _forward(args)
    with _jax.named_scope("loss_head"):
        err = _jnp.square(y.astype(_jnp.float32) - loss_target)
        return 0.5 * _jnp.sum(_jnp.mean(err, axis=-1)) if err.ndim else 0.5 * err


def _adamw(w, g, m, v):
    m = ADAM_B1 * m + (1.0 - ADAM_B1) * g
    v = ADAM_B2 * v + (1.0 - ADAM_B2) * _jnp.square(g)
    m_hat = m / (1.0 - ADAM_B1 ** ADAM_STEP)
    v_hat = v / (1.0 - ADAM_B2 ** ADAM_STEP)
    delta = -ADAM_LR * (m_hat / (_jnp.sqrt(v_hat) + ADAM_EPS) + ADAM_WD * w)
    return delta, m, v


def reference(x, mem, positions, g_mix, w_in, g_qa, w_qb, g_kva, w_kvb, g_qn_nope, g_qn_pe, g_kn_nope, g_kn_pe, conv_qk, b_if, g_hnorm, p_a, p_b, w_out, g_cross, g_mem, wq_c, wk_c, wv_c, g_cq, g_ck, wo_c, g_ffn, w_up, conv_ffn, b_conv_ffn, w_down, loss_target, m_g_mix, m_w_in, m_g_qa, m_w_qb, m_g_kva, m_w_kvb, m_g_qn_nope, m_g_qn_pe, m_g_kn_nope, m_g_kn_pe, m_conv_qk, m_b_if, m_g_hnorm, m_p_a, m_p_b, m_w_out, m_g_cross, m_g_mem, m_wq_c, m_wk_c, m_wv_c, m_g_cq, m_g_ck, m_wo_c, m_g_ffn, m_w_up, m_conv_ffn, m_b_conv_ffn, m_w_down, v_g_mix, v_w_in, v_g_qa, v_w_qb, v_g_kva, v_w_kvb, v_g_qn_nope, v_g_qn_pe, v_g_kn_nope, v_g_kn_pe, v_conv_qk, v_b_if, v_g_hnorm, v_p_a, v_p_b, v_w_out, v_g_cross, v_g_mem, v_wq_c, v_wk_c, v_wv_c, v_g_cq, v_g_ck, v_wo_c, v_g_ffn, v_w_up, v_conv_ffn, v_b_conv_ffn, v_w_down):
    given = dict(x=x, mem=mem, positions=positions, g_mix=g_mix, w_in=w_in, g_qa=g_qa, w_qb=w_qb, g_kva=g_kva, w_kvb=w_kvb, g_qn_nope=g_qn_nope, g_qn_pe=g_qn_pe, g_kn_nope=g_kn_nope, g_kn_pe=g_kn_pe, conv_qk=conv_qk, b_if=b_if, g_hnorm=g_hnorm, p_a=p_a, p_b=p_b, w_out=w_out, g_cross=g_cross, g_mem=g_mem, wq_c=wq_c, wk_c=wk_c, wv_c=wv_c, g_cq=g_cq, g_ck=g_ck, wo_c=wo_c, g_ffn=g_ffn, w_up=w_up, conv_ffn=conv_ffn, b_conv_ffn=b_conv_ffn, w_down=w_down, loss_target=loss_target, m_g_mix=m_g_mix, m_w_in=m_w_in, m_g_qa=m_g_qa, m_w_qb=m_w_qb, m_g_kva=m_g_kva, m_w_kvb=m_w_kvb, m_g_qn_nope=m_g_qn_nope, m_g_qn_pe=m_g_qn_pe, m_g_kn_nope=m_g_kn_nope, m_g_kn_pe=m_g_kn_pe, m_conv_qk=m_conv_qk, m_b_if=m_b_if, m_g_hnorm=m_g_hnorm, m_p_a=m_p_a, m_p_b=m_p_b, m_w_out=m_w_out, m_g_cross=m_g_cross, m_g_mem=m_g_mem, m_wq_c=m_wq_c, m_wk_c=m_wk_c, m_wv_c=m_wv_c, m_g_cq=m_g_cq, m_g_ck=m_g_ck, m_wo_c=m_wo_c, m_g_ffn=m_g_ffn, m_w_up=m_w_up, m_conv_ffn=m_conv_ffn, m_b_conv_ffn=m_b_conv_ffn, m_w_down=m_w_down, v_g_mix=v_g_mix, v_w_in=v_w_in, v_g_qa=v_g_qa, v_w_qb=v_w_qb, v_g_kva=v_g_kva, v_w_kvb=v_w_kvb, v_g_qn_nope=v_g_qn_nope, v_g_qn_pe=v_g_qn_pe, v_g_kn_nope=v_g_kn_nope, v_g_kn_pe=v_g_kn_pe, v_conv_qk=v_conv_qk, v_b_if=v_b_if, v_g_hnorm=v_g_hnorm, v_p_a=v_p_a, v_p_b=v_p_b, v_w_out=v_w_out, v_g_cross=v_g_cross, v_g_mem=v_g_mem, v_wq_c=v_wq_c, v_wk_c=v_wk_c, v_wv_c=v_wv_c, v_g_cq=v_g_cq, v_g_ck=v_g_ck, v_wo_c=v_wo_c, v_g_ffn=v_g_ffn, v_w_up=v_w_up, v_conv_ffn=v_conv_ffn, v_b_conv_ffn=v_b_conv_ffn, v_w_down=v_w_down)
    weights = {n: given[n] for n in TWIN_WEIGHTS}
    shared = {n: given[n] for n in SHARED_INPUTS}
    per_example = {n: given[n] for n in ['x', 'mem', 'positions']}
    grad_fn = _jax.value_and_grad(_loss, argnums=(0, 1))

    def one_microbatch(ex, loss_target):
        ex = dict(ex)
        diff = ex.pop(TWIN_DIFF_INPUT)
        return grad_fn(weights, diff, {**shared, **ex}, loss_target)

    if N_MICROBATCH == 1:
        loss, (grad_w, grad_x) = one_microbatch(per_example, given["loss_target"])
    else:
        def body(carry, xs):
            loss_sum, grad_sum = carry
            l_k, (gw_k, gx_k) = one_microbatch(xs[0], xs[1])
            with _jax.named_scope("update"):
                return (loss_sum + l_k, _jax.tree.map(_jnp.add, grad_sum, gw_k)), gx_k

        init = (_jnp.zeros((), _jnp.float32), _jax.tree.map(_jnp.zeros_like, weights))
        (loss, grad_w), grad_x = _jax.lax.scan(body, init, (per_example, given["loss_target"]))
    with _jax.named_scope("update"):
        delta_w, new_m, new_v = {}, {}, {}
        for n in TWIN_WEIGHTS:
            delta_w[n], new_m[n], new_v[n] = _adamw(weights[n], grad_w[n], given["m_" + n], given["v_" + n])
    return (loss, grad_x, *[grad_w[n] for n in TWIN_WEIGHTS], *[delta_w[n] for n in TWIN_WEIGHTS],
            *[new_m[n] for n in TWIN_WEIGHTS], *[new_v[n] for n in TWIN_WEIGHTS])
```

```python
import functools
import math

import jax
import jax.numpy as jnp
from jax import lax
from jax.experimental import pallas as pl
from jax.experimental.pallas import tpu as pltpu

F32, BF16 = jnp.float32, jnp.bfloat16
MESH = pl.DeviceIdType.MESH

EPS = 1e-6
CHUNK = 64
LOG2_CHUNK = 6
NOPE, ROPE, VHEAD = 128, 64, 128
MLSTM_DK, MLSTM_DV, MLSTM_CONV = 128, 256, 4
CROSS_DH = 128
FFN_CONV = 3
ROPE_BASE = 10000.0
ADAM_LR, ADAM_B1, ADAM_B2, ADAM_EPS, ADAM_WD, ADAM_STEP = 0.001, 0.9, 0.999, 1e-08, 0.01, 10

LANE = 128
ROW_TILE = 256
ATT_TILE = 512
MM_TILES = (1024, 512, 1024)
VMEM_LIMIT = 56 * 1024 * 1024
N_CHIPS = 4

NN = ((1,), (0,))
NT = ((1,), (1,))
TN = ((0,), (0,))


def _pick(dim, pref):
    if dim <= pref:
        return dim
    for t in range(pref, 0, -LANE):
        if dim % t == 0:
            return t
    return dim


def _bdot(a, b, dims):
    return lax.dot_general(a.astype(BF16), b.astype(BF16), (dims, ((), ())), preferred_element_type=F32)


@jax.custom_vjp
def _dnn(a, b):
    return _bdot(a, b, NN)


_dnn.defvjp(lambda a, b: (_bdot(a, b, NN), (a, b)),
            lambda r, g: (_bdot(g, r[1], NT), _bdot(r[0], g, TN)))


@jax.custom_vjp
def _dnt(a, b):
    return _bdot(a, b, NT)


_dnt.defvjp(lambda a, b: (_bdot(a, b, NT), (a, b)),
            lambda r, g: (_bdot(g, r[1], NN), _bdot(g, r[0], TN)))


@jax.custom_vjp
def _dtn(a, b):
    return _bdot(a, b, TN)


_dtn.defvjp(lambda a, b: (_bdot(a, b, TN), (a, b)),
            lambda r, g: (_bdot(r[1], g, NT), _bdot(r[0], g, NN)))


@functools.partial(jax.custom_vjp, nondiff_argnums=(1,))
def _lane_roll(x, shift):
    return pltpu.roll(x, shift, 1)


_lane_roll.defvjp(lambda x, shift: (pltpu.roll(x, shift, 1), None),
                  lambda shift, _, g: (pltpu.roll(g, (LANE - shift) % LANE, 1),))


def _params(*sem):
    return pltpu.CompilerParams(dimension_semantics=sem, vmem_limit_bytes=VMEM_LIMIT)


def mm(a, b, *, ta=False, tb=False, add=None, out_dtype=F32, name):
    m_dim, k_dim = (a.shape[1], a.shape[0]) if ta else a.shape
    n_dim = b.shape[0] if tb else b.shape[1]
    assert k_dim == (b.shape[1] if tb else b.shape[0]), (name, a.shape, b.shape)
    tm, tn, tk = _pick(m_dim, MM_TILES[0]), _pick(n_dim, MM_TILES[1]), _pick(k_dim, MM_TILES[2])
    nk = k_dim // tk
    dims = ((0,) if ta else (1,), (1,) if tb else (0,))
    has_add = add is not None

    def body(*refs):
        if has_add:
            a_ref, b_ref, c_ref, o_ref, acc = refs
        else:
            a_ref, b_ref, o_ref, acc = refs
        k = pl.program_id(2)

        @pl.when(k == 0)
        def _():
            acc[...] = c_ref[...].astype(F32) if has_add else jnp.zeros_like(acc)

        acc[...] += _bdot(a_ref[...], b_ref[...], dims)

        @pl.when(k == nk - 1)
        def _():
            o_ref[...] = acc[...].astype(o_ref.dtype)

    in_specs = [
        pl.BlockSpec((tk, tm), lambda i, j, k: (k, i)) if ta else pl.BlockSpec((tm, tk), lambda i, j, k: (i, k)),
        pl.BlockSpec((tn, tk), lambda i, j, k: (j, k)) if tb else pl.BlockSpec((tk, tn), lambda i, j, k: (k, j)),
    ]
    args = [a, b]
    if has_add:
        in_specs.append(pl.BlockSpec((tm, tn), lambda i, j, k: (i, j)))
        args.append(add)
    return pl.pallas_call(
        body, grid=(m_dim // tm, n_dim // tn, nk), in_specs=in_specs,
        out_specs=pl.BlockSpec((tm, tn), lambda i, j, k: (i, j)),
        out_shape=jax.ShapeDtypeStruct((m_dim, n_dim), out_dtype),
        scratch_shapes=[pltpu.VMEM((tm, tn), F32)],
        compiler_params=_params("parallel", "parallel", "arbitrary"), name=name)(*args)


def ew(fn, ins, outs, *, gr, gc=1, order="rc", name):
    n_in = len(ins)

    def block(shape, kind):
        r, c = shape
        return (r // gr if kind in ("rc", "r") else r, c // gc if kind in ("rc", "c") else c)

    def imap(kind):
        def f(p0, p1):
            i, j = (p0, p1) if order == "rc" else (p1, p0)
            return {"rc": (i, j), "r": (i, 0), "c": (0, j), "f": (0, 0)}[kind]
        return f

    def body(*refs):
        p0, p1 = pl.program_id(0), pl.program_id(1)
        i, j = (p0, p1) if order == "rc" else (p1, p0)
        vals = fn(*[r[...] for r in refs[:n_in]])
        for ref, val, (_, dtype, kind) in zip(refs[n_in:], vals, outs):
            first = {"rc": None, "r": (j == 0) if gc > 1 else None, "c": (i == 0) if gr > 1 else None,
                     "f": ((i == 0) & (j == 0)) if gr * gc > 1 else None}[kind]
            _store(ref, val.astype(dtype), first)

    grid = (gr, gc) if order == "rc" else (gc, gr)
    return pl.pallas_call(
        body, grid=grid,
        in_specs=[pl.BlockSpec(block(a.shape, k), imap(k)) for a, k in ins],
        out_specs=[pl.BlockSpec(block(s, k), imap(k)) for s, _, k in outs],
        out_shape=[jax.ShapeDtypeStruct(s, d) for s, d, _ in outs],
        compiler_params=_params("arbitrary", "arbitrary"), name=name)(*[a for a, _ in ins])


def _store(ref, val, first):
    if first is None:
        ref[...] = val
        return

    @pl.when(first)
    def _():
        ref[...] = val

    @pl.when(jnp.logical_not(first))
    def _():
        ref[...] += val


def _f32(*xs):
    return [x.astype(F32) for x in xs]


def _rms(x, g, n):
    ms = jnp.sum(x * x, axis=-1, keepdims=True) * (1.0 / n)
    return x * lax.rsqrt(ms + EPS) * g


def _sigmoid(x):
    return 1.0 / (1.0 + jnp.exp(-x))


def _silu(x):
    return x * _sigmoid(x)


def _log_sigmoid(x):
    return jnp.minimum(x, 0.0) - jnp.log(1.0 + jnp.exp(-jnp.abs(x)))


def rms_fwd(x, g, name, out_dtype=BF16):
    t, w = x.shape
    return ew(lambda x_, g_: (_rms(x_, g_, w),), [(x, "r"), (g, "f")], [((t, w), out_dtype, "r")],
              gr=t // _pick(t, ROW_TILE), name=name)[0]


def rms_bwd(x, g, du, res, name):
    t, w = x.shape

    def fn(x_, g_, du_, *res_):
        _, pull = jax.vjp(lambda a, b: _rms(a, b, w), x_, g_)
        dx, dg = pull(du_.astype(F32))
        return (dx + res_[0] if res_ else dx), dg

    ins = [(x, "r"), (g, "f"), (du, "r")] + ([(res, "r")] if res is not None else [])
    return ew(fn, ins, [((t, w), F32, "r"), ((1, w), F32, "f")], gr=t // _pick(t, ROW_TILE), name=name)


def _rope(x, cos_t, sin_lo, sin_hi):
    return x * cos_t + _lane_roll(x, LANE - ROPE // 2) * sin_lo + _lane_roll(x, ROPE // 2) * sin_hi


def _mla_prep(qn, qp, kn, kp, cos_t, sin_lo, sin_hi, g_qn, g_qp, g_kn, g_kp):
    q = jnp.concatenate([_rms(qn, g_qn, NOPE), _rope(_rms(qp, g_qp, ROPE), cos_t, sin_lo, sin_hi)], axis=1)
    k = jnp.concatenate([_rms(kn, g_kn, NOPE), _rope(_rms(kp, g_kp, ROPE), cos_t, sin_lo, sin_hi)], axis=1)
    return q, k


def mla_prep_fwd(qn, qp, kn, kp, tabs, gains, heads):
    t = qn.shape[0]
    ins = [(qn, "rc"), (qp, "rc"), (kn, "rc"), (kp, "r")] + [(a, "r") for a in tabs] + [(g, "f") for g in gains]
    return ew(lambda *a: _mla_prep(*_f32(*a)), ins,
              [((t, heads * 2 * LANE), BF16, "rc"), ((t, heads * 2 * LANE), BF16, "rc")],
              gr=t // _pick(t, ROW_TILE), gc=heads, name="mla_prep_fwd")


def mla_prep_bwd(qn, qp, kn, kp, tabs, gains, dq, dk, heads):
    t = qn.shape[0]

    def fn(qn_, qp_, kn_, kp_, c_, s1_, s2_, g1, g2, g3, g4, dq_, dk_):
        _, pull = jax.vjp(lambda a, b, c, d, e, f, g, h: _mla_prep(a, b, c, d, c_, s1_, s2_, e, f, g, h),
                          qn_, qp_, kn_, kp_, g1, g2, g3, g4)
        return pull((dq_, dk_))

    ins = ([(qn, "rc"), (qp, "rc"), (kn, "rc"), (kp, "r")] + [(a, "r") for a in tabs] + [(g, "f") for g in gains]
           + [(dq, "rc"), (dk, "rc")])
    hw = heads * LANE
    outs = [((t, hw), F32, "rc"), ((t, hw), F32, "rc"), ((t, hw), F32, "rc"), ((t, LANE), F32, "r")] \
        + [((1, LANE), F32, "f")] * 4
    return ew(fn, ins, outs, gr=t // _pick(t, ROW_TILE), gc=heads, name="mla_prep_bwd")


def _chunk_mask(row0, col0, shape, rows_are_queries):
    r = jnp.right_shift(row0 + lax.broadcasted_iota(jnp.int32, shape, 0), LOG2_CHUNK)
    c = jnp.right_shift(col0 + lax.broadcasted_iota(jnp.int32, shape, 1), LOG2_CHUNK)
    return (c <= r) if rows_are_queries else (r <= c)


def attn_fwd(q, k, v, heads):
    t = q.shape[0]
    tq = _pick(t, ATT_TILE)
    nq = t // tq
    scale = (NOPE + ROPE) ** -0.5

    def body(q_ref, k_ref, v_ref, o_ref, lse_ref, m_s, l_s, acc):
        i, j = pl.program_id(1), pl.program_id(2)

        @pl.when(j == 0)
        def _():
            m_s[...] = jnp.full_like(m_s, -jnp.inf)
            l_s[...] = jnp.zeros_like(l_s)
            acc[...] = jnp.zeros_like(acc)

        @pl.when(j <= i)
        def _():
            s = _bdot(q_ref[...], k_ref[...], NT) * scale
            s = jnp.where(_chunk_mask(i * tq, j * tq, (tq, tq), True), s, -jnp.inf)
            m_new = jnp.maximum(m_s[...], jnp.max(s, axis=1, keepdims=True))
            alpha = jnp.exp(m_s[...] - m_new)
            p = jnp.exp(s - m_new)
            l_s[...] = alpha * l_s[...] + jnp.sum(p, axis=1, keepdims=True)
            acc[...] = alpha * acc[...] + _bdot(p, v_ref[...], NN)
            m_s[...] = m_new

        @pl.when(j == nq - 1)
        def _():
            o_ref[...] = acc[...] / l_s[...]
            lse_ref[...] = m_s[...] + jnp.log(l_s[...])

    return pl.pallas_call(
        body, grid=(heads, nq, nq),
        in_specs=[pl.BlockSpec((tq, 2 * LANE), lambda h, i, j: (i, h)),
                  pl.BlockSpec((tq, 2 * LANE), lambda h, i, j: (jnp.minimum(j, i), h)),
                  pl.BlockSpec((tq, VHEAD), lambda h, i, j: (jnp.minimum(j, i), h))],
        out_specs=[pl.BlockSpec((tq, VHEAD), lambda h, i, j: (i, h)),
                   pl.BlockSpec((None, tq, 1), lambda h, i, j: (h, i, 0))],
        out_shape=[jax.ShapeDtypeStruct((t, heads * VHEAD), F32), jax.ShapeDtypeStruct((heads, t, 1), F32)],
        scratch_shapes=[pltpu.VMEM((tq, 1), F32), pltpu.VMEM((tq, 1), F32), pltpu.VMEM((tq, VHEAD), F32)],
        compiler_params=_params("parallel", "parallel", "arbitrary"), name="mla_attn_fwd")(q, k, v)


def attn_bwd(q, k, v, o, do, lse_row, heads):
    t = q.shape[0]
    tq = _pick(t, ATT_TILE)
    nq = t // tq
    scale = (NOPE + ROPE) ** -0.5

    def body(q_ref, k_ref, v_ref, o_ref, do_ref, lse_ref, dq_ref, dk_ref, dv_ref):
        j, i = pl.program_id(1), pl.program_id(2)

        @pl.when((j == 0) & (i == 0))
        def _():
            dq_ref[...] = jnp.zeros_like(dq_ref)

        @pl.when(i == 0)
        def _():
            dk_ref[...] = jnp.zeros_like(dk_ref)
            dv_ref[...] = jnp.zeros_like(dv_ref)

        @pl.when(i >= j)
        def _():
            do_i = do_ref[...]
            prod = do_i * o_ref[...]
            hi = prod.astype(BF16)
            mid = (prod - hi.astype(F32)).astype(BF16)
            lo = (prod - hi.astype(F32) - mid.astype(F32)).astype(BF16)
            ones = jnp.ones((8, VHEAD), BF16)
            delta = (_bdot(ones, hi, NT) + _bdot(ones, mid, NT) + _bdot(ones, lo, NT))[0:1, :]
            st = _bdot(k_ref[...], q_ref[...], NT) * scale
            mask = _chunk_mask(j * tq, i * tq, (tq, tq), False)
            pt = jnp.where(mask, jnp.exp(st - lse_ref[...]), 0.0)
            dv_ref[...] += _bdot(pt, do_i, NN)
            dpt = _bdot(v_ref[...], do_i, NT)
            dst = pt * (dpt - delta) * scale
            dk_ref[...] += _bdot(dst, q_ref[...], NN)
            rows = pl.ds(pl.multiple_of(i * tq, tq), tq)
            dq_ref[rows, :] += _bdot(dst, k_ref[...], TN)

    qmap = lambda h, j, i: (jnp.maximum(i, j), h)
    kmap = lambda h, j, i: (j, h)
    return pl.pallas_call(
        body, grid=(heads, nq, nq),
        in_specs=[pl.BlockSpec((tq, 2 * LANE), qmap), pl.BlockSpec((tq, 2 * LANE), kmap),
                  pl.BlockSpec((tq, VHEAD), kmap), pl.BlockSpec((tq, VHEAD), qmap), pl.BlockSpec((tq, VHEAD), qmap),
                  pl.BlockSpec((None, 1, tq), lambda h, j, i: (h, 0, jnp.maximum(i, j)))],
        out_specs=[pl.BlockSpec((t, 2 * LANE), lambda h, j, i: (0, h)),
                   pl.BlockSpec((tq, 2 * LANE), kmap), pl.BlockSpec((tq, VHEAD), kmap)],
        out_shape=[jax.ShapeDtypeStruct((t, heads * 2 * LANE), F32), jax.ShapeDtypeStruct((t, heads * 2 * LANE), F32),
                   jax.ShapeDtypeStruct((t, heads * VHEAD), F32)],
        compiler_params=_params("parallel", "arbitrary", "arbitrary"), name="mla_attn_bwd")(q, k, v, o, do, lse_row)


def _shift_down(x, s):
    if s == 0:
        return x
    rows = lax.broadcasted_iota(jnp.int32, x.shape, 0)
    return jnp.where(rows >= s, pltpu.roll(x, s, 0), 0.0)


def _shift_up(x, s):
    if s == 0:
        return x
    t = x.shape[0]
    rows = lax.broadcasted_iota(jnp.int32, x.shape, 0)
    return jnp.where(rows < t - s, pltpu.roll(x, t - s, 0), 0.0)


def _conv(x, w_ref, width):
    return sum(_shift_down(x, width - 1 - j) * w_ref[j:j + 1, :] for j in range(width))


def _conv_bwd(x, dpre, w_ref, dw_ref, width):
    dx = sum(_shift_up(dpre, width - 1 - j) * w_ref[j:j + 1, :] for j in range(width))
    for j in range(width):
        dw_ref[j:j + 1, :] = jnp.sum(dpre * _shift_down(x, width - 1 - j), axis=0, keepdims=True)
    return dx


def _dsilu(z):
    s = _sigmoid(z)
    return s * (1.0 + z * (1.0 - s))


def conv_qk_fwd(x, w, colscale):
    t, c = x.shape
    tc = _pick(c, 256)

    def body(x_ref, w_ref, s_ref, o_ref):
        o_ref[...] = (_silu(_conv(x_ref[...], w_ref, MLSTM_CONV)) * s_ref[...]).astype(o_ref.dtype)

    return pl.pallas_call(
        body, grid=(c // tc,),
        in_specs=[pl.BlockSpec((t, tc), lambda j: (0, j)), pl.BlockSpec((MLSTM_CONV, tc), lambda j: (0, j)),
                  pl.BlockSpec((1, tc), lambda j: (0, j))],
        out_specs=pl.BlockSpec((t, tc), lambda j: (0, j)), out_shape=jax.ShapeDtypeStruct((t, c), BF16),
        compiler_params=_params("parallel"), name="conv_qk_fwd")(x, w, colscale)


def conv_qk_bwd(x, w, colscale, dq, dk):
    t, c = x.shape
    tc = _pick(c // 2, 256)
    half = (c // 2) // tc

    def body(x_ref, w_ref, s_ref, dq_ref, dk_ref, dx_ref, dw_ref):
        j = pl.program_id(0)
        x_ = x_ref[...]
        dy = jnp.where(j < half, dq_ref[...], dk_ref[...])
        dpre = dy * s_ref[...] * _dsilu(_conv(x_, w_ref, MLSTM_CONV))
        dx_ref[...] = _conv_bwd(x_, dpre, w_ref, dw_ref, MLSTM_CONV)

    return pl.pallas_call(
        body, grid=(c // tc,),
        in_specs=[pl.BlockSpec((t, tc), lambda j: (0, j)), pl.BlockSpec((MLSTM_CONV, tc), lambda j: (0, j)),
                  pl.BlockSpec((1, tc), lambda j: (0, j)),
                  pl.BlockSpec((t, tc), lambda j: (0, jnp.minimum(j, half - 1))),
                  pl.BlockSpec((t, tc), lambda j: (0, jnp.maximum(j - half, 0)))],
        out_specs=[pl.BlockSpec((t, tc), lambda j: (0, j)), pl.BlockSpec((MLSTM_CONV, tc), lambda j: (0, j))],
        out_shape=[jax.ShapeDtypeStruct((t, c), F32), jax.ShapeDtypeStruct((MLSTM_CONV, c), F32)],
        compiler_params=_params("parallel"), name="conv_qk_bwd")(x, w, colscale, dq, dk)


def _mlstm_chunk(q, k, v, i_col, i_row, f_col, f_row, c_mat, n_vec, m):
    shape = (CHUNK, CHUNK)
    r = lax.broadcasted_iota(jnp.int32, shape, 0)
    c = lax.broadcasted_iota(jnp.int32, shape, 1)
    tril = c <= r
    lf_col, lf_row = _log_sigmoid(f_col), _log_sigmoid(f_row)
    bc_col = jnp.sum(jnp.where(tril, lf_row, 0.0), axis=1, keepdims=True)
    bc_row = jnp.sum(jnp.where(r <= c, lf_col, 0.0), axis=0, keepdims=True)
    logw = jnp.where(tril, bc_col - bc_row + i_row, -jnp.inf)
    inter = bc_col + m
    m_t = lax.stop_gradient(jnp.maximum(inter, jnp.max(logw, axis=1, keepdims=True)))
    w_intra = jnp.exp(logw - m_t)
    w_inter = jnp.exp(inter - m_t)
    sc = _dnt(q, k) * w_intra
    num = w_inter * _dnn(q, c_mat) + _dnn(sc, v)
    den = w_inter * jnp.sum(q * n_vec, axis=1, keepdims=True) + jnp.sum(sc, axis=1, keepdims=True)
    h = num / jnp.maximum(jnp.abs(den), jnp.exp(-m_t))
    b_last = jnp.sum(lf_row, axis=1, keepdims=True)
    m_new = lax.stop_gradient(jnp.maximum(b_last + m, jnp.max(b_last - bc_row + i_row, axis=1, keepdims=True)))
    decay = jnp.exp(b_last + m - m_new)
    uk = jnp.exp(b_last - bc_col + i_col - m_new) * k
    return h, decay * c_mat + _dtn(uk, v), decay * n_vec + jnp.sum(uk, axis=0, keepdims=True), m_new


def _mlstm_specs(heads, rev, nc):
    ci = (lambda c: nc - 1 - c) if rev else (lambda c: c)
    return dict(
        q=pl.BlockSpec((CHUNK, MLSTM_DK), lambda h, c: (ci(c), h)),
        k=pl.BlockSpec((CHUNK, MLSTM_DK), lambda h, c: (ci(c), heads + h)),
        v=pl.BlockSpec((CHUNK, MLSTM_DV), lambda h, c: (ci(c), h)),
        gc=pl.BlockSpec((None, None, CHUNK, 2), lambda h, c: (h, ci(c), 0, 0)),
        gr=pl.BlockSpec((None, None, 2, CHUNK), lambda h, c: (h, ci(c), 0, 0)),
        b=pl.BlockSpec((None, 1, 2), lambda h, c: (h, 0, 0)),
        cm=pl.BlockSpec((None, None, MLSTM_DK, MLSTM_DV), lambda h, c: (h, ci(c), 0, 0)),
        vec=pl.BlockSpec((None, None, 1, LANE), lambda h, c: (h, ci(c), 0, 0)),
    )


def _gates(gc_ref, gr_ref, b_ref):
    bi, bf = b_ref[:, 0:1], b_ref[:, 1:2]
    return gc_ref[:, 0:1] + bi, gr_ref[0:1, :] + bi, gc_ref[:, 1:2] + bf, gr_ref[1:2, :] + bf


def mlstm_fwd(qk, v, gcol, grow, bias, heads):
    t = qk.shape[0]
    nc = t // CHUNK
    sp = _mlstm_specs(heads, False, nc)

    def body(q_ref, k_ref, v_ref, gc_ref, gr_ref, b_ref, h_ref, c_ref, n_ref, m_ref, c_s, n_s, m_s):
        @pl.when(pl.program_id(1) == 0)
        def _():
            c_s[...] = jnp.zeros_like(c_s)
            n_s[...] = jnp.zeros_like(n_s)
            m_s[...] = jnp.zeros_like(m_s)

        c_ref[...] = c_s[...]
        n_ref[...] = n_s[...]
        m_ref[...] = m_s[...]
        q, k, v_ = _f32(q_ref[...], k_ref[...], v_ref[...])
        h, c_new, n_new, m_new = _mlstm_chunk(q, k, v_, *_gates(gc_ref, gr_ref, b_ref), c_s[...], n_s[...], m_s[:, 0:1])
        h_ref[...] = h
        c_s[...] = c_new
        n_s[...] = n_new
        m_s[...] = jnp.broadcast_to(m_new, m_s.shape)

    return pl.pallas_call(
        body, grid=(heads, nc),
        in_specs=[sp["q"], sp["k"], sp["v"], sp["gc"], sp["gr"], sp["b"]],
        out_specs=[sp["v"], sp["cm"], sp["vec"], sp["vec"]],
        out_shape=[jax.ShapeDtypeStruct((t, heads * MLSTM_DV), F32),
                   jax.ShapeDtypeStruct((heads, nc, MLSTM_DK, MLSTM_DV), F32),
                   jax.ShapeDtypeStruct((heads, nc, 1, LANE), F32), jax.ShapeDtypeStruct((heads, nc, 1, LANE), F32)],
        scratch_shapes=[pltpu.VMEM((MLSTM_DK, MLSTM_DV), F32), pltpu.VMEM((1, LANE), F32), pltpu.VMEM((1, LANE), F32)],
        compiler_params=_params("parallel", "arbitrary"), name="mlstm_fwd")(qk, qk, v, gcol, grow, bias)


def mlstm_bwd(qk, v, gcol, grow, bias, c_all, n_all, m_all, dh, heads):
    t = qk.shape[0]
    nc = t // CHUNK
    sp = _mlstm_specs(heads, True, nc)

    def body(q_ref, k_ref, v_ref, gc_ref, gr_ref, b_ref, c_ref, n_ref, m_ref, dh_ref,
             dq_ref, dk_ref, dv_ref, dgc_ref, dgr_ref, dc_s, dn_s):
        @pl.when(pl.program_id(1) == 0)
        def _():
            dc_s[...] = jnp.zeros_like(dc_s)
            dn_s[...] = jnp.zeros_like(dn_s)

        q, k, v_ = _f32(q_ref[...], k_ref[...], v_ref[...])
        m = m_ref[:, 0:1]
        _, pull = jax.vjp(lambda *a: _mlstm_chunk(*a, m)[:3], q, k, v_, *_gates(gc_ref, gr_ref, b_ref),
                          c_ref[...], n_ref[...])
        dq, dk, dv, di_col, di_row, df_col, df_row, dc, dn = pull((dh_ref[...], dc_s[...], dn_s[...]))
        dq_ref[...] = dq
        dk_ref[...] = dk
        dv_ref[...] = dv
        dgc_ref[:, 0:1] = di_col
        dgc_ref[:, 1:2] = df_col
        dgr_ref[0:1, :] = di_row
        dgr_ref[1:2, :] = df_row
        dc_s[...] = dc
        dn_s[...] = dn

    qspec = pl.BlockSpec((CHUNK, MLSTM_DK), lambda h, c: (nc - 1 - c, h))
    return pl.pallas_call(
        body, grid=(heads, nc),
        in_specs=[sp["q"], sp["k"], sp["v"], sp["gc"], sp["gr"], sp["b"], sp["cm"], sp["vec"], sp["vec"], sp["v"]],
        out_specs=[qspec, qspec, sp["v"], sp["gc"], sp["gr"]],
        out_shape=[jax.ShapeDtypeStruct((t, heads * MLSTM_DK), F32), jax.ShapeDtypeStruct((t, heads * MLSTM_DK), F32),
                   jax.ShapeDtypeStruct((t, heads * MLSTM_DV), F32),
                   jax.ShapeDtypeStruct(gcol.shape, F32), jax.ShapeDtypeStruct(grow.shape, F32)],
        scratch_shapes=[pltpu.VMEM((MLSTM_DK, MLSTM_DV), F32), pltpu.VMEM((1, LANE), F32)],
        compiler_params=_params("parallel", "arbitrary"),
        name="mlstm_bwd")(qk, qk, v, gcol, grow, bias, c_all, n_all, m_all, dh)


def _hnorm_gate(h, zo, g):
    return _rms(h, g, MLSTM_DV) * _sigmoid(zo)


def _cross_core(q, k, v, g_q, g_k, heads):
    scale = CROSS_DH ** -0.5
    outs = []
    for h in range(heads):
        s = _dnt(_rms(q[h], g_q, CROSS_DH), _rms(k[h], g_k, CROSS_DH)) * scale
        p = jnp.exp(s - lax.stop_gradient(jnp.max(s, axis=1, keepdims=True)))
        p = p / jnp.sum(p, axis=1, keepdims=True)
        outs.append(_dnn(p, v[h]))
    return jnp.concatenate(outs, axis=1)


def _split_heads(ref, heads):
    return [ref[:, h * CROSS_DH:(h + 1) * CROSS_DH].astype(F32) for h in range(heads)]


def cross_fwd(q, k, v, g_q, g_k, heads):
    t = q.shape[0]
    tm = _pick(t, ATT_TILE)
    full = lambda a: pl.BlockSpec(a.shape, lambda i: (0, 0))

    def body(q_ref, k_ref, v_ref, gq_ref, gk_ref, o_ref):
        o_ref[...] = _cross_core(_split_heads(q_ref, heads), _split_heads(k_ref, heads), _split_heads(v_ref, heads),
                                 gq_ref[...], gk_ref[...], heads).astype(o_ref.dtype)

    return pl.pallas_call(
        body, grid=(t // tm,), in_specs=[pl.BlockSpec((tm, q.shape[1]), lambda i: (i, 0)), full(k), full(v), full(g_q), full(g_k)],
        out_specs=pl.BlockSpec((tm, q.shape[1]), lambda i: (i, 0)), out_shape=jax.ShapeDtypeStruct(q.shape, F32),
        compiler_params=_params("parallel"), name="cross_fwd")(q, k, v, g_q, g_k)


def cross_bwd(q, k, v, g_q, g_k, do, heads):
    t, w = q.shape
    tm = _pick(t, ATT_TILE)
    full = lambda a: pl.BlockSpec(a.shape, lambda i: (0, 0))

    def body(q_ref, k_ref, v_ref, gq_ref, gk_ref, do_ref, dq_ref, dk_ref, dv_ref, dgq_ref, dgk_ref):
        qs, ks, vs = _split_heads(q_ref, heads), _split_heads(k_ref, heads), _split_heads(v_ref, heads)
        _, pull = jax.vjp(lambda a, b, c, d, e: _cross_core(a, b, c, d, e, heads), qs, ks, vs, gq_ref[...], gk_ref[...])
        dqs, dks, dvs, dgq, dgk = pull(do_ref[...])
        first = pl.program_id(0) == 0
        for h in range(heads):
            cols = slice(h * CROSS_DH, (h + 1) * CROSS_DH)
            dq_ref[:, cols] = dqs[h]
            _store(dk_ref.at[:, cols], dks[h], first)
            _store(dv_ref.at[:, cols], dvs[h], first)
        _store(dgq_ref, dgq, first)
        _store(dgk_ref, dgk, first)

    row = pl.BlockSpec((tm, w), lambda i: (i, 0))
    return pl.pallas_call(
        body, grid=(t // tm,), in_specs=[row, full(k), full(v), full(g_q), full(g_k), row],
        out_specs=[row, full(k), full(v), full(g_q), full(g_k)],
        out_shape=[jax.ShapeDtypeStruct(q.shape, F32), jax.ShapeDtypeStruct(k.shape, F32), jax.ShapeDtypeStruct(v.shape, F32),
                   jax.ShapeDtypeStruct(g_q.shape, F32), jax.ShapeDtypeStruct(g_k.shape, F32)],
        compiler_params=_params("arbitrary"), name="cross_bwd")(q, k, v, g_q, g_k, do)


def ffn_glu_fwd(hg, hv, wg, wv, bg, bv):
    t, f = hg.shape
    tc = _pick(f, LANE)
    col = pl.BlockSpec((t, tc), lambda j: (0, j))
    tap = pl.BlockSpec((FFN_CONV, tc), lambda j: (0, j))
    one = pl.BlockSpec((1, tc), lambda j: (0, j))

    def body(hg_ref, hv_ref, wg_ref, wv_ref, bg_ref, bv_ref, o_ref):
        gate = _conv(hg_ref[...], wg_ref, FFN_CONV) + bg_ref[...]
        val = _conv(hv_ref[...], wv_ref, FFN_CONV) + bv_ref[...]
        o_ref[...] = (_silu(gate) * val).astype(o_ref.dtype)

    return pl.pallas_call(body, grid=(f // tc,), in_specs=[col, col, tap, tap, one, one], out_specs=col,
                          out_shape=jax.ShapeDtypeStruct((t, f), BF16), compiler_params=_params("parallel"),
                          name="ffn_glu_fwd")(hg, hv, wg, wv, bg, bv)


def ffn_glu_bwd(hg, hv, wg, wv, bg, bv, dact):
    t, f = hg.shape
    tc = _pick(f, LANE)
    col = pl.BlockSpec((t, tc), lambda j: (0, j))
    tap = pl.BlockSpec((FFN_CONV, tc), lambda j: (0, j))
    one = pl.BlockSpec((1, tc), lambda j: (0, j))

    def body(hg_ref, hv_ref, wg_ref, wv_ref, bg_ref, bv_ref, da_ref, dhg_ref, dhv_ref, dwg_ref, dwv_ref, dbg_ref, dbv_ref):
        xg, xv, da = hg_ref[...], hv_ref[...], da_ref[...]
        gate = _conv(xg, wg_ref, FFN_CONV) + bg_ref[...]
        val = _conv(xv, wv_ref, FFN_CONV) + bv_ref[...]
        dgate = da * val * _dsilu(gate)
        dval = da * _silu(gate)
        dbg_ref[...] = jnp.sum(dgate, axis=0, keepdims=True)
        dbv_ref[...] = jnp.sum(dval, axis=0, keepdims=True)
        dhg_ref[...] = _conv_bwd(xg, dgate, wg_ref, dwg_ref, FFN_CONV)
        dhv_ref[...] = _conv_bwd(xv, dval, wv_ref, dwv_ref, FFN_CONV)

    return pl.pallas_call(
        body, grid=(f // tc,), in_specs=[col, col, tap, tap, one, one, col], out_specs=[col, col, tap, tap, one, one],
        out_shape=[jax.ShapeDtypeStruct((t, f), F32), jax.ShapeDtypeStruct((t, f), F32),
                   jax.ShapeDtypeStruct((FFN_CONV, f), F32), jax.ShapeDtypeStruct((FFN_CONV, f), F32),
                   jax.ShapeDtypeStruct((1, f), F32), jax.ShapeDtypeStruct((1, f), F32)],
        compiler_params=_params("parallel"), name="ffn_glu_bwd")(hg, hv, wg, wv, bg, bv, dact)


def _adamw(g, w, m, v):
    m = ADAM_B1 * m + (1.0 - ADAM_B1) * g
    v = ADAM_B2 * v + (1.0 - ADAM_B2) * (g * g)
    m_hat = m / (1.0 - ADAM_B1 ** ADAM_STEP)
    v_hat = v / (1.0 - ADAM_B2 ** ADAM_STEP)
    return -ADAM_LR * (m_hat / (jnp.sqrt(v_hat) + ADAM_EPS) + ADAM_WD * w), m, v


def adamw(g, w, m, v, name):
    r, c = g.shape
    tr = r
    for cand in (256, 128, 64, 32, 16, 8):
        if r % cand == 0 and cand * c * 4 <= (1 << 21):
            tr = cand
            break
    return ew(_adamw, [(g, "r"), (w, "r"), (m, "r"), (v, "r")], [((r, c), F32, "r")] * 3, gr=r // tr, name=name)


ANY = pl.BlockSpec(memory_space=pl.ANY)


def _place():
    x, y, c = lax.axis_index("x"), lax.axis_index("y"), lax.axis_index("c")
    return x, y, c, [(1 - x, y), (x, 1 - y), (1 - x, 1 - y)]


def _rcopy(src, dst, send, recv, k, to):
    return pltpu.make_async_remote_copy(src_ref=src, dst_ref=dst, send_sem=send.at[k], recv_sem=recv.at[k],
                                        device_id=to, device_id_type=MESH)


def gather_shards(bigs, smalls):
    nb, na = len(bigs), len(bigs) + len(smalls)
    arrays = list(bigs) + list(smalls)

    def body(*refs):
        ins, outs = refs[:na], refs[na:2 * na]
        send, recv, loc = refs[2 * na:]
        x, y, c, chips = _place()
        me, sib = 2 * x + y, (x, y, 1 - c)
        local = [pltpu.make_async_copy(ins[a], outs[a].at[me], loc.at[a]) for a in range(na)]
        for cp in local:
            cp.start()

        def half(ref, a, which):
            rows = arrays[a].shape[0] // 2
            return ref.at[pl.ds(which * rows, rows)]

        started = []
        for a in range(na):
            for j, (cx, cy) in enumerate(chips):
                if a < nb:
                    cp = _rcopy(half(ins[a], a, c), half(outs[a].at[me], a, c), send, recv, 6 * a + j, (cx, cy, c))
                else:
                    cp = _rcopy(ins[a], outs[a].at[me], send, recv, 6 * nb + 3 * (a - nb) + j, (cx, cy, c))
                cp.start()
                started.append(cp)
        for a in range(nb):
            for j, (cx, cy) in enumerate(chips):
                landed = half(outs[a].at[2 * cx + cy], a, c)
                _rcopy(landed, landed, send, recv, 6 * a + j, (cx, cy, c)).wait_recv()
                cp = _rcopy(landed, landed, send, recv, 6 * a + 3 + j, sib)
                cp.start()
                started.append(cp)
        for a in range(na):
            for j, (cx, cy) in enumerate(chips):
                if a < nb:
                    dst = half(outs[a].at[2 * cx + cy], a, 1 - c)
                    _rcopy(dst, dst, send, recv, 6 * a + 3 + j, sib).wait_recv()
                else:
                    dst = outs[a].at[2 * cx + cy]
                    _rcopy(dst, dst, send, recv, 6 * nb + 3 * (a - nb) + j, (cx, cy, c)).wait_recv()
        for cp in started:
            cp.wait_send()
        for cp in local:
            cp.wait()

    n_sem = 6 * nb + 3 * (na - nb)
    return pl.pallas_call(
        body, in_specs=[ANY] * na, out_specs=[ANY] * na,
        out_shape=[jax.ShapeDtypeStruct((N_CHIPS,) + a.shape, a.dtype) for a in arrays],
        scratch_shapes=[pltpu.SemaphoreType.DMA((n_sem,)), pltpu.SemaphoreType.DMA((n_sem,)), pltpu.SemaphoreType.DMA((na,))],
        name="gather_shards")(*arrays)


def sibling_halves(grads):
    na = len(grads)

    def body(*refs):
        ins, outs = refs[:na], refs[na:2 * na]
        send, recv = refs[2 * na:]
        x, y, c, _ = _place()
        cps = []
        for a in range(na):
            rows = grads[a].shape[1] // 2
            cp = _rcopy(ins[a].at[:, pl.ds((1 - c) * rows, rows)], outs[a], send, recv, a, (x, y, 1 - c))
            cp.start()
            cps.append(cp)
        for cp in cps:
            cp.wait()

    return pl.pallas_call(
        body, in_specs=[ANY] * na, out_specs=[ANY] * na,
        out_shape=[jax.ShapeDtypeStruct((g.shape[0], g.shape[1] // 2, g.shape[2]), g.dtype) for g in grads],
        scratch_shapes=[pltpu.SemaphoreType.DMA((na,)), pltpu.SemaphoreType.DMA((na,))], name="sibling_halves")(*grads)


def scatter_partials(parts):
    na = len(parts)

    def body(*refs):
        ins, outs = refs[:na], refs[na:2 * na]
        send, recv = refs[2 * na:]
        _, _, c, chips = _place()
        cps = []
        for a in range(na):
            for j, (cx, cy) in enumerate(chips):
                cp = _rcopy(ins[a].at[2 * cx + cy], outs[a].at[j], send, recv, 3 * a + j, (cx, cy, c))
                cp.start()
                cps.append(cp)
        for cp in cps:
            cp.wait()

    return pl.pallas_call(
        body, in_specs=[ANY] * na, out_specs=[ANY] * na,
        out_shape=[jax.ShapeDtypeStruct((3,) + p.shape[1:], p.dtype) for p in parts],
        scratch_shapes=[pltpu.SemaphoreType.DMA((3 * na,)), pltpu.SemaphoreType.DMA((3 * na,))], name="scatter_partials")(*parts)


def join_halves(halves):
    na = len(halves)

    def body(*refs):
        ins, outs = refs[:na], refs[na:2 * na]
        send, recv, loc = refs[2 * na:]
        x, y, c, _ = _place()
        cps, local = [], []
        for a in range(na):
            lc = pltpu.make_async_copy(ins[a], outs[a].at[c], loc.at[a])
            lc.start()
            local.append(lc)
            cp = _rcopy(ins[a], outs[a].at[c], send, recv, a, (x, y, 1 - c))
            cp.start()
            cps.append(cp)
        for a in range(na):
            dst = outs[a].at[1 - c]
            _rcopy(dst, dst, send, recv, a, (x, y, 1 - c)).wait_recv()
        for cp in cps:
            cp.wait_send()
        for lc in local:
            lc.wait()

    return pl.pallas_call(
        body, in_specs=[ANY] * na, out_specs=[ANY] * na,
        out_shape=[jax.ShapeDtypeStruct((2,) + h.shape, h.dtype) for h in halves],
        scratch_shapes=[pltpu.SemaphoreType.DMA((na,)), pltpu.SemaphoreType.DMA((na,)), pltpu.SemaphoreType.DMA((na,))],
        name="join_halves")(*halves)


def allreduce_small(vec):
    r = vec.shape[0]

    def body(x_ref, sum_ref, all_ref, send, recv):
        x, y, c, _ = _place()
        me = 4 * x + 2 * y + c
        all_ref[me] = x_ref[...]
        cps, peers = [], []
        for mask in range(1, 8):
            px = 1 - x if mask & 4 else x
            py = 1 - y if mask & 2 else y
            pc = 1 - c if mask & 1 else c
            peers.append(4 * px + 2 * py + pc)
            cp = _rcopy(x_ref, all_ref.at[me], send, recv, mask - 1, (px, py, pc))
            cp.start()
            cps.append(cp)
        for k, cp in enumerate(cps):
            _rcopy(x_ref, all_ref.at[peers[k]], send, recv, k, (x, y, c)).wait_recv()
        for cp in cps:
            cp.wait_send()
        total = all_ref[0]
        for d in range(1, 8):
            total = total + all_ref[d]
        sum_ref[...] = total

    vm = pl.BlockSpec(memory_space=pltpu.VMEM)
    return pl.pallas_call(
        body, in_specs=[vm], out_specs=vm, out_shape=jax.ShapeDtypeStruct((r, LANE), F32),
        scratch_shapes=[pltpu.VMEM((8, r, LANE), F32), pltpu.SemaphoreType.DMA((7,)), pltpu.SemaphoreType.DMA((7,))],
        compiler_params=pltpu.CompilerParams(vmem_limit_bytes=VMEM_LIMIT), name="allreduce_small")(vec)


def _row_tile(rows):
    for cand in (256, 128, 64, 32, 16):
        if rows % cand == 0:
            return cand
    return rows


def add_sibling(grad, recv, c_idx):
    _, rows, cols = grad.shape
    hr = rows // 2
    tr = _row_tile(hr)
    nb = hr // tr

    def body(c_ref, g_ref, r_ref, o_ref):
        o_ref[...] = (g_ref[...].astype(F32) + r_ref[...].astype(F32)).astype(o_ref.dtype)

    return pl.pallas_call(
        body,
        grid_spec=pltpu.PrefetchScalarGridSpec(
            num_scalar_prefetch=1, grid=(N_CHIPS, nb),
            in_specs=[pl.BlockSpec((None, tr, cols), lambda k, r, c_ref: (k, c_ref[0] * nb + r, 0)),
                      pl.BlockSpec((None, tr, cols), lambda k, r, c_ref: (k, r, 0))],
            out_specs=pl.BlockSpec((None, tr, cols), lambda k, r, c_ref: (k, r, 0))),
        out_shape=jax.ShapeDtypeStruct((N_CHIPS, hr, cols), BF16),
        compiler_params=_params("parallel", "parallel"), name="add_sibling")(c_idx, grad, recv)


def sum_chips(part, others, chip_idx):
    _, hr, cols = part.shape
    tr = _row_tile(hr)

    def body(k_ref, p_ref, o0_ref, o1_ref, o2_ref, out_ref):
        out_ref[...] = ((p_ref[...].astype(F32) + o0_ref[...].astype(F32)) + o1_ref[...].astype(F32)) + o2_ref[...].astype(F32)

    other = lambda j: pl.BlockSpec((None, tr, cols), lambda r, k_ref: (j, r, 0))
    return pl.pallas_call(
        body,
        grid_spec=pltpu.PrefetchScalarGridSpec(
            num_scalar_prefetch=1, grid=(hr // tr,),
            in_specs=[pl.BlockSpec((None, tr, cols), lambda r, k_ref: (k_ref[0], r, 0)), other(0), other(1), other(2)],
            out_specs=pl.BlockSpec((tr, cols), lambda r, k_ref: (r, 0))),
        out_shape=jax.ShapeDtypeStruct((hr, cols), F32),
        compiler_params=_params("parallel"), name="sum_chips")(chip_idx, part, others, others, others)


def _pad_lanes(a, width=LANE):
    return jnp.pad(a, ((0, 0), (0, width - a.shape[1])))


def _cols_from_shards(g):
    return jnp.transpose(g, (1, 0, 2)).reshape(g.shape[1], -1)


def _cols_to_shards(w):
    k, n4 = w.shape
    return jnp.transpose(w.reshape(k, N_CHIPS, n4 // N_CHIPS), (1, 0, 2))


def kernel(x, mem, positions, g_mix, w_in, g_qa, w_qb, g_kva, w_kvb, g_qn_nope, g_qn_pe, g_kn_nope, g_kn_pe, conv_qk, b_if, g_hnorm, p_a, p_b, w_out, g_cross, g_mem, wq_c, wk_c, wv_c, g_cq, g_ck, wo_c, g_ffn, w_up, conv_ffn, b_conv_ffn, w_down, loss_target, m_g_mix, m_w_in, m_g_qa, m_w_qb, m_g_kva, m_w_kvb, m_g_qn_nope, m_g_qn_pe, m_g_kn_nope, m_g_kn_pe, m_conv_qk, m_b_if, m_g_hnorm, m_p_a, m_p_b, m_w_out, m_g_cross, m_g_mem, m_wq_c, m_wk_c, m_wv_c, m_g_cq, m_g_ck, m_wo_c, m_g_ffn, m_w_up, m_conv_ffn, m_b_conv_ffn, m_w_down, v_g_mix, v_w_in, v_g_qa, v_w_qb, v_g_kva, v_w_kvb, v_g_qn_nope, v_g_qn_pe, v_g_kn_nope, v_g_kn_pe, v_conv_qk, v_b_if, v_g_hnorm, v_p_a, v_p_b, v_w_out, v_g_cross, v_g_mem, v_wq_c, v_wk_c, v_wv_c, v_g_cq, v_g_ck, v_wo_c, v_g_ffn, v_w_up, v_conv_ffn, v_b_conv_ffn, v_w_down):
    names = ["g_mix", "w_in", "g_qa", "w_qb", "g_kva", "w_kvb", "g_qn_nope", "g_qn_pe", "g_kn_nope", "g_kn_pe", "conv_qk",
             "b_if", "g_hnorm", "p_a", "p_b", "w_out", "g_cross", "g_mem", "wq_c", "wk_c", "wv_c", "g_cq", "g_ck", "wo_c",
             "g_ffn", "w_up", "conv_ffn", "b_conv_ffn", "w_down"]
    env = locals()
    wts = {n: env[n] for n in names}
    mom = {n: env["m_" + n] for n in names}
    var = {n: env["v_" + n] for n in names}

    xi, yi, ci = lax.axis_index("x"), lax.axis_index("y"), lax.axis_index("c")
    chip = 2 * xi + yi
    chip_arr = jnp.reshape(chip, (1,)).astype(jnp.int32)
    c_arr = jnp.reshape(ci, (1,)).astype(jnp.int32)

    x2d, tgt, mem2d = x[0], loss_target[0], mem[0]
    t, d = x2d.shape
    mla_h = w_qb.shape[2] * N_CHIPS // (NOPE + ROPE)
    ml_h = b_if.shape[1] // 2
    cr_h = wq_c.shape[2] // CROSS_DH
    f_dim = w_down.shape[1] * N_CHIPS
    q_rank, kv_rank = g_qa.shape[1], g_kva.shape[1]
    qk_w, v_w = ml_h * MLSTM_DK, ml_h * MLSTM_DV
    nc = t // CHUNK

    big_names = ["w_in", "w_qb", "w_kvb", "p_a", "p_b", "w_out", "wq_c", "wk_c", "wv_c", "wo_c", "w_up", "w_down"]
    col_sharded = {"w_in", "w_qb", "w_kvb", "wo_c", "w_up"}
    small_sharded = ["conv_qk", "g_hnorm", "conv_ffn"]
    gathered = gather_shards([wts[n][0].astype(BF16) for n in big_names], [wts[n][0] for n in small_sharded])
    full = {}
    for n, g in zip(big_names + small_sharded, gathered):
        full[n] = _cols_from_shards(g) if (n in col_sharded or n in small_sharded) else g.reshape(-1, g.shape[2])

    o_qa, o_kv, o_kpe = 0, q_rank, q_rank + kv_rank
    o_q = o_kpe + ROPE
    o_v = o_q + 2 * qk_w
    o_if = o_v + v_w
    o_o = o_if + 2 * ml_h
    o_ga, o_gb = o_o + v_w, o_o + v_w + d
    wi = full["w_in"]
    pad_kpe = jnp.zeros((d, LANE - ROPE), BF16)
    pad_if = jnp.zeros((d, LANE - 2 * ml_h), BF16)
    w_small = jnp.concatenate([wi[:, o_qa:o_q], pad_kpe, wi[:, o_if:o_o], pad_if], axis=1)
    o_kpe_s, o_if_s = o_kpe, o_kpe + LANE
    w_qk, w_v, w_o, w_ga, w_gb = wi[:, o_q:o_v], wi[:, o_v:o_if], wi[:, o_o:o_ga], wi[:, o_ga:o_gb], wi[:, o_gb:]

    wq3 = full["w_qb"].reshape(q_rank, mla_h, NOPE + ROPE)
    wq_nope = wq3[:, :, :NOPE].reshape(q_rank, mla_h * NOPE)
    wq_pe = jnp.pad(wq3[:, :, NOPE:], ((0, 0), (0, 0), (0, LANE - ROPE))).reshape(q_rank, mla_h * LANE)
    wkv3 = full["w_kvb"].reshape(kv_rank, mla_h, NOPE + VHEAD)
    wk_nope = wkv3[:, :, :NOPE].reshape(kv_rank, mla_h * NOPE)
    wv_mla = wkv3[:, :, NOPE:].reshape(kv_rank, mla_h * VHEAD)
    wup_g, wup_v = full["w_up"][:, :f_dim], full["w_up"][:, f_dim:]

    inv_freq = ROPE_BASE ** (-jnp.arange(0, ROPE, 2, dtype=F32) / ROPE)
    ang = positions[0].astype(F32)[:, None] * inv_freq
    cos, sin = jnp.cos(ang), jnp.sin(ang)
    zero_h = jnp.zeros_like(cos)
    tabs = [_pad_lanes(jnp.concatenate([cos, cos], axis=1)), _pad_lanes(-sin), _pad_lanes(jnp.concatenate([zero_h, sin], axis=1))]
    mla_gains = [g_qn_nope, _pad_lanes(g_qn_pe), g_kn_nope, _pad_lanes(g_kn_pe)]

    u1 = rms_fwd(x2d, g_mix, "rms_mix")
    z_small = mm(u1, w_small, name="in_small")
    z_qa, z_kv = z_small[:, o_qa:o_kv], z_small[:, o_kv:o_kpe]
    z_kpe, z_if = z_small[:, o_kpe_s:o_kpe_s + LANE], z_small[:, o_if_s:o_if_s + 2 * ml_h]
    z_qk = mm(u1, w_qk, name="in_qk")
    z_v = mm(u1, w_v, name="in_v")
    z_o = mm(u1, w_o, name="in_o")
    z_ga = mm(u1, w_ga, name="in_ga")
    z_gb = mm(u1, w_gb, name="in_gb")

    qa_n = rms_fwd(z_qa, g_qa, "rms_qa")
    kv_n = rms_fwd(z_kv, g_kva, "rms_kva")
    qn_raw = mm(qa_n, wq_nope, name="q_nope")
    qp_raw = mm(qa_n, wq_pe, name="q_pe")
    kn_raw = mm(kv_n, wk_nope, name="k_nope")
    v_mla = mm(kv_n, wv_mla, out_dtype=BF16, name="v_mla")
    q_att, k_att = mla_prep_fwd(qn_raw, qp_raw, kn_raw, z_kpe, tabs, mla_gains, mla_h)
    y_a, lse = attn_fwd(q_att, k_att, v_mla, mla_h)

    colscale = jnp.concatenate([jnp.full((1, qk_w), MLSTM_DK ** -0.5, F32), jnp.ones((1, qk_w), F32)], axis=1)
    qk_c = conv_qk_fwd(z_qk, full["conv_qk"], colscale)
    gates4 = z_if.reshape(nc, CHUNK, 2, ml_h)
    gcol = jnp.transpose(gates4, (3, 0, 1, 2))
    grow = jnp.transpose(gates4, (3, 0, 2, 1))
    bias = jnp.transpose(b_if.reshape(2, ml_h), (1, 0)).reshape(ml_h, 1, 2)
    h_raw, c_all, n_all, m_all = mlstm_fwd(qk_c, z_v, gcol, grow, bias, ml_h)
    g_hn = full["g_hnorm"].reshape(1, v_w)
    hn_gr = t // _pick(t, ROW_TILE)
    y_b = ew(lambda *a: (_hnorm_gate(*a),), [(h_raw, "rc"), (z_o, "rc"), (g_hn, "c")], [((t, v_w), BF16, "rc")], gr=hn_gr, gc=ml_h,
             name="hnorm_gate")[0]

    pa = mm(y_a, full["p_a"], name="proj_a")
    pb = mm(y_b, full["p_b"], name="proj_b")
    merge_fn = lambda ga, gb, a, b: (_sigmoid(ga) * a + _sigmoid(gb) * b,)
    merged = ew(merge_fn, [(z_ga, "r"), (z_gb, "r"), (pa, "r"), (pb, "r")], [((t, d), BF16, "r")], gr=hn_gr, name="merge")[0]
    x1 = mm(merged, full["w_out"], add=x2d, name="out_proj")

    uc = rms_fwd(x1, g_cross, "rms_cross")
    mem_n = rms_fwd(mem2d, g_mem, "rms_mem")
    qc = mm(uc, full["wq_c"], name="cross_q")
    kc = mm(mem_n, full["wk_c"], name="cross_k")
    vc = mm(mem_n, full["wv_c"], name="cross_v")
    oc = cross_fwd(qc, kc, vc, g_cq, g_ck, cr_h)
    x2 = mm(oc, full["wo_c"], add=x1, name="cross_out")

    u3 = rms_fwd(x2, g_ffn, "rms_ffn")
    hg = mm(u3, wup_g, name="ffn_up_gate")
    hv = mm(u3, wup_v, name="ffn_up_val")
    cw, cb = full["conv_ffn"], b_conv_ffn
    act = ffn_glu_fwd(hg, hv, cw[:, :f_dim], cw[:, f_dim:], cb[:, :f_dim], cb[:, f_dim:])
    y = mm(act, full["w_down"], add=x2, name="ffn_down")

    def loss_fn(y_, t_):
        err = y_ - t_
        part = jnp.sum(jnp.sum(err * err, axis=1, keepdims=True), axis=0, keepdims=True) * (0.5 / d)
        return err * (1.0 / d), jnp.broadcast_to(part, (1, LANE))

    dy, loss_part = ew(loss_fn, [(y, "r"), (tgt, "r")], [((t, d), F32, "r"), ((1, LANE), F32, "f")], gr=hn_gr, name="loss")

    gw = {}
    gw["w_down"] = mm(act, dy, ta=True, out_dtype=BF16, name="dw_down")
    dact = mm(dy, full["w_down"], tb=True, name="d_act")
    dhg, dhv, dcw_g, dcw_v, dcb_g, dcb_v = ffn_glu_bwd(hg, hv, cw[:, :f_dim], cw[:, f_dim:], cb[:, :f_dim], cb[:, f_dim:], dact)
    gw["conv_ffn"] = jnp.concatenate([dcw_g, dcw_v], axis=1)
    gw["b_conv_ffn"] = jnp.concatenate([dcb_g, dcb_v], axis=1)
    dwup_g = mm(u3, dhg, ta=True, out_dtype=BF16, name="dw_up_gate")
    dwup_v = mm(u3, dhv, ta=True, out_dtype=BF16, name="dw_up_val")
    du3 = mm(dhg, wup_g, tb=True, name="d_u3_gate")
    du3 = mm(dhv, wup_v, tb=True, add=du3, name="d_u3_val")
    dx2, gw["g_ffn"] = rms_bwd(x2, g_ffn, du3, dy, "rms_ffn_bwd")

    gw["wo_c"] = mm(oc, dx2, ta=True, out_dtype=BF16, name="dw_cross_out")
    doc = mm(dx2, full["wo_c"], tb=True, name="d_cross_o")
    dqc, dkc, dvc, gw["g_cq"], gw["g_ck"] = cross_bwd(qc, kc, vc, g_cq, g_ck, doc, cr_h)
    gw["wq_c"] = mm(uc, dqc, ta=True, out_dtype=BF16, name="dw_cross_q")
    gw["wk_c"] = mm(mem_n, dkc, ta=True, out_dtype=BF16, name="dw_cross_k")
    gw["wv_c"] = mm(mem_n, dvc, ta=True, out_dtype=BF16, name="dw_cross_v")
    duc = mm(dqc, full["wq_c"], tb=True, name="d_uc")
    dmem_n = mm(dkc, full["wk_c"], tb=True, name="d_mem_k")
    dmem_n = mm(dvc, full["wv_c"], tb=True, add=dmem_n, name="d_mem_v")
    _, gw["g_mem"] = rms_bwd(mem2d, g_mem, dmem_n, None, "rms_mem_bwd")
    dx1, gw["g_cross"] = rms_bwd(x1, g_cross, duc, dx2, "rms_cross_bwd")

    gw["w_out"] = mm(merged, dx1, ta=True, out_dtype=BF16, name="dw_out")
    dmerged = mm(dx1, full["w_out"], tb=True, name="d_merged")

    def merge_bwd(ga, gb, a, b, dm):
        _, pull = jax.vjp(lambda *args: merge_fn(*args)[0], ga, gb, a, b)
        return pull(dm)

    dz_ga, dz_gb, dpa, dpb = ew(merge_bwd, [(z_ga, "r"), (z_gb, "r"), (pa, "r"), (pb, "r"), (dmerged, "r")],
                                [((t, d), F32, "r")] * 4, gr=hn_gr, name="merge_bwd")
    gw["p_a"] = mm(y_a, dpa, ta=True, out_dtype=BF16, name="dw_proj_a")
    gw["p_b"] = mm(y_b, dpb, ta=True, out_dtype=BF16, name="dw_proj_b")
    dy_a = mm(dpa, full["p_a"], tb=True, name="d_ya")
    dy_b = mm(dpb, full["p_b"], tb=True, name="d_yb")

    def hnorm_bwd(h_, zo_, g_, dyb_):
        _, pull = jax.vjp(_hnorm_gate, h_, zo_, g_)
        return pull(dyb_)

    dh_raw, dz_o, dg_hn = ew(hnorm_bwd, [(h_raw, "rc"), (z_o, "rc"), (g_hn, "c"), (dy_b, "rc")],
                             [((t, v_w), F32, "rc"), ((t, v_w), F32, "rc"), ((1, v_w), F32, "c")],
                             gr=hn_gr, gc=ml_h, order="cr", name="hnorm_gate_bwd")
    gw["g_hnorm"] = dg_hn.reshape(ml_h, MLSTM_DV)
    dq_m, dk_m, dz_v, dgcol, dgrow = mlstm_bwd(qk_c, z_v, gcol, grow, bias, c_all, n_all, m_all, dh_raw, ml_h)
    dgates4 = jnp.transpose(dgcol, (1, 2, 3, 0)) + jnp.transpose(dgrow, (1, 3, 2, 0))
    dz_if = dgates4.reshape(t, 2 * ml_h)
    gw["b_if"] = ew(lambda a: (jnp.sum(a, axis=0, keepdims=True),), [(dz_if, "r")], [((1, 2 * ml_h), F32, "f")],
                    gr=hn_gr, name="bias_if_bwd")[0]
    dz_qk, gw["conv_qk"] = conv_qk_bwd(z_qk, full["conv_qk"], colscale, dq_m, dk_m)

    lse_row = lse.reshape(mla_h, 1, t)
    dq_att, dk_att, dv_mla = attn_bwd(q_att, k_att, v_mla, y_a, dy_a, lse_row, mla_h)
    dqn_raw, dqp_raw, dkn_raw, dz_kpe, gw["g_qn_nope"], dg_qp, gw["g_kn_nope"], dg_kp = mla_prep_bwd(
        qn_raw, qp_raw, kn_raw, z_kpe, tabs, mla_gains, dq_att, dk_att, mla_h)
    gw["g_qn_pe"], gw["g_kn_pe"] = dg_qp[:, :ROPE], dg_kp[:, :ROPE]
    dwq_nope = mm(qa_n, dqn_raw, ta=True, out_dtype=BF16, name="dw_q_nope")
    dwq_pe = mm(qa_n, dqp_raw, ta=True, out_dtype=BF16, name="dw_q_pe")
    dwk_nope = mm(kv_n, dkn_raw, ta=True, out_dtype=BF16, name="dw_k_nope")
    dwv_mla = mm(kv_n, dv_mla, ta=True, out_dtype=BF16, name="dw_v_mla")
    dqa_n = mm(dqn_raw, wq_nope, tb=True, name="d_qa_nope")
    dqa_n = mm(dqp_raw, wq_pe, tb=True, add=dqa_n, name="d_qa_pe")
    dkv_n = mm(dkn_raw, wk_nope, tb=True, name="d_kv_nope")
    dkv_n = mm(dv_mla, wv_mla, tb=True, add=dkv_n, name="d_kv_v")
    dz_qa, gw["g_qa"] = rms_bwd(z_qa, g_qa, dqa_n, None, "rms_qa_bwd")
    dz_kv, gw["g_kva"] = rms_bwd(z_kv, g_kva, dkv_n, None, "rms_kva_bwd")
    gw["w_qb"] = jnp.concatenate([dwq_nope.reshape(q_rank, mla_h, NOPE), dwq_pe.reshape(q_rank, mla_h, LANE)[:, :, :ROPE]],
                                 axis=2).reshape(q_rank, -1)
    gw["w_kvb"] = jnp.concatenate([dwk_nope.reshape(kv_rank, mla_h, NOPE), dwv_mla.reshape(kv_rank, mla_h, VHEAD)],
                                  axis=2).reshape(kv_rank, -1)

    dz_small = jnp.concatenate([dz_qa, dz_kv, dz_kpe, _pad_lanes(dz_if)], axis=1)
    dw_small = mm(u1, dz_small, ta=True, out_dtype=BF16, name="dw_in_small")
    du1 = mm(dz_small, w_small, tb=True, name="d_u1_small")
    dw_segs = []
    for nm, dz, w_seg in (("qk", dz_qk, w_qk), ("v", dz_v, w_v), ("o", dz_o, w_o), ("ga", dz_ga, w_ga), ("gb", dz_gb, w_gb)):
        dw_segs.append(mm(u1, dz, ta=True, out_dtype=BF16, name="dw_in_" + nm))
        du1 = mm(dz, w_seg, tb=True, add=du1, name="d_u1_" + nm)
    gw["w_in"] = jnp.concatenate([dw_small[:, :o_kpe_s + ROPE], dw_segs[0], dw_segs[1],
                                  dw_small[:, o_if_s:o_if_s + 2 * ml_h], dw_segs[2], dw_segs[3], dw_segs[4]], axis=1)
    gw["w_up"] = jnp.concatenate([dwup_g, dwup_v], axis=1)
    grad_x, gw["g_mix"] = rms_bwd(x2d, g_mix, du1, dx1, "rms_mix_bwd")

    shard_major = [_cols_to_shards(gw[n]) if n in col_sharded else gw[n].reshape(N_CHIPS, -1, gw[n].shape[1]) for n in big_names]
    from_sib = sibling_halves(shard_major)
    parts = [add_sibling(g, r, c_arr) for g, r in zip(shard_major, from_sib)]
    others = scatter_partials(parts)
    halves = [sum_chips(p, o, chip_arr) for p, o in zip(parts, others)]
    joined = join_halves(halves)
    big_grads = {n: j.reshape(-1, j.shape[2]) for n, j in zip(big_names, joined)}

    small_names = [n for n in names if n not in big_names]
    pieces = [loss_part]
    for n in small_names:
        flat = gw[n].reshape(1, -1)
        pieces.append(jnp.pad(flat, ((0, 0), (0, (-flat.shape[1]) % LANE))))
    packed = jnp.concatenate(pieces, axis=1)
    packed = jnp.pad(packed, ((0, 0), (0, (-packed.shape[1]) % (8 * LANE)))).reshape(-1, LANE)
    total = allreduce_small(packed).reshape(1, -1)
    loss = total[0, 0]
    small_grads, off = {}, LANE
    for n in small_names:
        size = gw[n].size
        g_full = total[:, off:off + size].reshape(gw[n].shape)
        off += size + (-size) % LANE
        if n in small_sharded:
            width = wts[n].shape[-1]
            g_full = lax.dynamic_slice_in_dim(g_full, chip * width, width, axis=g_full.ndim - 1)
        small_grads[n] = g_full.reshape(wts[n].shape[1:])

    grads, deltas, new_m, new_v = {}, {}, {}, {}
    for n in big_names:
        w2 = wts[n][0]
        grads[n] = big_grads[n]
        deltas[n], new_m[n], new_v[n] = adamw(big_grads[n], w2, mom[n][0], var[n][0], "adamw_" + n)

    def pack_small(tree):
        flat = jnp.concatenate([tree[n].reshape(1, -1) for n in small_names], axis=1)
        return jnp.pad(flat, ((0, 0), (0, (-flat.shape[1]) % (8 * LANE)))).reshape(8, -1)

    sg = pack_small(small_grads)
    sd, sm, sv = adamw(sg, pack_small({n: wts[n][0] for n in small_names}), pack_small({n: mom[n][0] for n in small_names}),
                       pack_small({n: var[n][0] for n in small_names}), "adamw_small")
    off = 0
    for n in small_names:
        size = small_grads[n].size
        shp = wts[n].shape[1:]
        grads[n] = small_grads[n]
        for dst, src in ((deltas, sd), (new_m, sm), (new_v, sv)):
            dst[n] = src.reshape(1, -1)[:, off:off + size].reshape(shp)
        off += size

    def out(tree):
        return [tree[n].reshape(wts[n].shape) for n in names]

    return (loss, grad_x.reshape(x.shape), *out(grads), *out(deltas), *out(new_m), *out(new_v))
```

```python
import functools
import math

import jax
import jax.numpy as jnp
from jax import lax
from jax.experimental import pallas as pl
from jax.experimental.pallas import tpu as pltpu

F32, BF16 = jnp.float32, jnp.bfloat16
MESH = pl.DeviceIdType.MESH

EPS = 1e-6
CHUNK = 64
LOG2_CHUNK = 6
NOPE, ROPE, VHEAD = 128, 64, 128
MLSTM_DK, MLSTM_DV, MLSTM_CONV = 128, 256, 4
CROSS_DH = 128
FFN_CONV = 3
ROPE_BASE = 10000.0
LOG2_E = math.log2(math.e)
ADAM_LR, ADAM_B1, ADAM_B2, ADAM_EPS, ADAM_WD, ADAM_STEP = 0.001, 0.9, 0.999, 1e-08, 0.01, 10

LANE = 128
ROW_TILE = 256
HEAD_ROW_TILE = 1024
ATT_TILE = 512
MM_TILES = (1024, 512, 1024)
VMEM_LIMIT = 56 * 1024 * 1024
N_CHIPS = 4

NN = ((1,), (0,))
NT = ((1,), (1,))
TN = ((0,), (0,))


def _pick(dim, pref):
    if dim <= pref:
        return dim
    for t in range(pref, 0, -LANE):
        if dim % t == 0:
            return t
    return dim


def _bdot(a, b, dims):
    return lax.dot_general(a.astype(BF16), b.astype(BF16), (dims, ((), ())), preferred_element_type=F32)


@jax.custom_vjp
def _dnn(a, b):
    return _bdot(a, b, NN)


_dnn.defvjp(lambda a, b: (_bdot(a, b, NN), (a, b)),
            lambda r, g: (_bdot(g, r[1], NT), _bdot(r[0], g, TN)))


@jax.custom_vjp
def _dnt(a, b):
    return _bdot(a, b, NT)


_dnt.defvjp(lambda a, b: (_bdot(a, b, NT), (a, b)),
            lambda r, g: (_bdot(g, r[1], NN), _bdot(g, r[0], TN)))


@jax.custom_vjp
def _dtn(a, b):
    return _bdot(a, b, TN)


_dtn.defvjp(lambda a, b: (_bdot(a, b, TN), (a, b)),
            lambda r, g: (_bdot(r[1], g, NT), _bdot(r[0], g, NN)))


@functools.partial(jax.custom_vjp, nondiff_argnums=(1,))
def _lane_roll(x, shift):
    return pltpu.roll(x, shift, 1)


_lane_roll.defvjp(lambda x, shift: (pltpu.roll(x, shift, 1), None),
                  lambda shift, _, g: (pltpu.roll(g, (LANE - shift) % LANE, 1),))


def _params(*sem):
    return pltpu.CompilerParams(dimension_semantics=sem, vmem_limit_bytes=VMEM_LIMIT)


def mm(a, b, *, ta=False, tb=False, add=None, out_dtype=F32, name):
    m_dim, k_dim = (a.shape[1], a.shape[0]) if ta else a.shape
    n_dim = b.shape[0] if tb else b.shape[1]
    assert k_dim == (b.shape[1] if tb else b.shape[0]), (name, a.shape, b.shape)
    tm, tn, tk = _pick(m_dim, MM_TILES[0]), _pick(n_dim, MM_TILES[1]), _pick(k_dim, MM_TILES[2])
    nk = k_dim // tk
    dims = ((0,) if ta else (1,), (1,) if tb else (0,))
    has_add = add is not None

    def body(*refs):
        if has_add:
            a_ref, b_ref, c_ref, o_ref, acc = refs
        else:
            a_ref, b_ref, o_ref, acc = refs
        k = pl.program_id(2)

        @pl.when(k == 0)
        def _():
            acc[...] = c_ref[...].astype(F32) if has_add else jnp.zeros_like(acc)

        acc[...] += _bdot(a_ref[...], b_ref[...], dims)

        @pl.when(k == nk - 1)
        def _():
            o_ref[...] = acc[...].astype(o_ref.dtype)

    in_specs = [
        pl.BlockSpec((tk, tm), lambda i, j, k: (k, i)) if ta else pl.BlockSpec((tm, tk), lambda i, j, k: (i, k)),
        pl.BlockSpec((tn, tk), lambda i, j, k: (j, k)) if tb else pl.BlockSpec((tk, tn), lambda i, j, k: (k, j)),
    ]
    args = [a, b]
    if has_add:
        in_specs.append(pl.BlockSpec((tm, tn), lambda i, j, k: (i, j)))
        args.append(add)
    return pl.pallas_call(
        body, grid=(m_dim // tm, n_dim // tn, nk), in_specs=in_specs,
        out_specs=pl.BlockSpec((tm, tn), lambda i, j, k: (i, j)),
        out_shape=jax.ShapeDtypeStruct((m_dim, n_dim), out_dtype),
        scratch_shapes=[pltpu.VMEM((tm, tn), F32)],
        compiler_params=_params("parallel", "parallel", "arbitrary"), name=name)(*args)


def ew(fn, ins, outs, *, gr, gc=1, order="rc", name):
    n_in = len(ins)

    def block(shape, kind):
        r, c = shape
        return (r // gr if kind in ("rc", "r") else r, c // gc if kind in ("rc", "c") else c)

    def imap(kind):
        def f(p0, p1):
            i, j = (p0, p1) if order == "rc" else (p1, p0)
            return {"rc": (i, j), "r": (i, 0), "c": (0, j), "f": (0, 0)}[kind]
        return f

    def body(*refs):
        p0, p1 = pl.program_id(0), pl.program_id(1)
        i, j = (p0, p1) if order == "rc" else (p1, p0)
        vals = fn(*[r[...] for r in refs[:n_in]])
        for ref, val, (_, dtype, kind) in zip(refs[n_in:], vals, outs):
            first = {"rc": None, "r": (j == 0) if gc > 1 else None, "c": (i == 0) if gr > 1 else None,
                     "f": ((i == 0) & (j == 0)) if gr * gc > 1 else None}[kind]
            _store(ref, val.astype(dtype), first)

    grid = (gr, gc) if order == "rc" else (gc, gr)
    return pl.pallas_call(
        body, grid=grid,
        in_specs=[pl.BlockSpec(block(a.shape, k), imap(k)) for a, k in ins],
        out_specs=[pl.BlockSpec(block(s, k), imap(k)) for s, _, k in outs],
        out_shape=[jax.ShapeDtypeStruct(s, d) for s, d, _ in outs],
        compiler_params=_params("arbitrary", "arbitrary"), name=name)(*[a for a, _ in ins])


def _store(ref, val, first):
    if first is None:
        ref[...] = val
        return

    @pl.when(first)
    def _():
        ref[...] = val

    @pl.when(jnp.logical_not(first))
    def _():
        ref[...] += val


def _f32(*xs):
    return [x.astype(F32) for x in xs]


def _rms(x, g, n):
    ms = jnp.sum(x * x, axis=-1, keepdims=True) * (1.0 / n)
    return x * lax.rsqrt(ms + EPS) * g


def _sigmoid(x):
    return 1.0 / (1.0 + jnp.exp(-x))


def _silu(x):
    return x * _sigmoid(x)


def _log_sigmoid(x):
    return jnp.minimum(x, 0.0) - jnp.log(1.0 + jnp.exp(-jnp.abs(x)))


def rms_fwd(x, g, name, out_dtype=BF16):
    t, w = x.shape
    return ew(lambda x_, g_: (_rms(x_, g_, w),), [(x, "r"), (g, "f")], [((t, w), out_dtype, "r")],
              gr=t // _pick(t, ROW_TILE), name=name)[0]


def rms_bwd(x, g, du, res, name, out_dtype=F32):
    t, w = x.shape

    def fn(x_, g_, du_, *res_):
        _, pull = jax.vjp(lambda a, b: _rms(a, b, w), x_, g_)
        dx, dg = pull(du_.astype(F32))
        return (dx + res_[0] if res_ else dx), dg

    ins = [(x, "r"), (g, "f"), (du, "r")] + ([(res, "r")] if res is not None else [])
    return ew(fn, ins, [((t, w), out_dtype, "r"), ((1, w), F32, "f")], gr=t // _pick(t, ROW_TILE), name=name)


def _rope(x, cos_t, sin_lo, sin_hi):
    return x * cos_t + _lane_roll(x, LANE - ROPE // 2) * sin_lo + _lane_roll(x, ROPE // 2) * sin_hi


def _mla_prep(qn, qp, kn, kp, cos_t, sin_lo, sin_hi, g_qn, g_qp, g_kn, g_kp):
    q = jnp.concatenate([_rms(qn, g_qn, NOPE), _rope(_rms(qp, g_qp, ROPE), cos_t, sin_lo, sin_hi)], axis=1)
    k = jnp.concatenate([_rms(kn, g_kn, NOPE), _rope(_rms(kp, g_kp, ROPE), cos_t, sin_lo, sin_hi)], axis=1)
    return q, k


def mla_prep_fwd(qn, qp, kn, kp, tabs, gains, heads):
    t = qn.shape[0]
    ins = [(qn, "rc"), (qp, "rc"), (kn, "rc"), (kp, "r")] + [(a, "r") for a in tabs] + [(g, "f") for g in gains]
    return ew(lambda *a: _mla_prep(*_f32(*a)), ins,
              [((t, heads * 2 * LANE), BF16, "rc"), ((t, heads * 2 * LANE), BF16, "rc")],
              gr=t // _pick(t, HEAD_ROW_TILE), gc=heads, name="mla_prep_fwd")


def mla_prep_bwd(qn, qp, kn, kp, tabs, gains, dq, dk, heads):
    t = qn.shape[0]

    def fn(qn_, qp_, kn_, kp_, c_, s1_, s2_, g1, g2, g3, g4, dq_, dk_):
        _, pull = jax.vjp(lambda a, b, c, d, e, f, g, h: _mla_prep(a, b, c, d, c_, s1_, s2_, e, f, g, h),
                          qn_, qp_, kn_, kp_, g1, g2, g3, g4)
        return pull((dq_, dk_))

    ins = ([(qn, "rc"), (qp, "rc"), (kn, "rc"), (kp, "r")] + [(a, "r") for a in tabs] + [(g, "f") for g in gains]
           + [(dq, "rc"), (dk, "rc")])
    hw = heads * LANE
    outs = [((t, hw), BF16, "rc"), ((t, hw), BF16, "rc"), ((t, hw), BF16, "rc"), ((t, LANE), F32, "r")] \
        + [((1, LANE), F32, "f")] * 4
    return ew(fn, ins, outs, gr=t // _pick(t, HEAD_ROW_TILE), gc=heads, name="mla_prep_bwd")


def _chunk_mask(row0, col0, shape, rows_are_queries):
    r = jnp.right_shift(row0 + lax.broadcasted_iota(jnp.int32, shape, 0), LOG2_CHUNK)
    c = jnp.right_shift(col0 + lax.broadcasted_iota(jnp.int32, shape, 1), LOG2_CHUNK)
    return (c <= r) if rows_are_queries else (r <= c)


def attn_fwd(q, k, v, heads):
    t = q.shape[0]
    tq = _pick(t, ATT_TILE)
    nq = t // tq
    scale = (NOPE + ROPE) ** -0.5
    scale2 = scale * LOG2_E

    def body(q_ref, k_ref, v_ref, o_ref, lse_ref, m_s, l_s, acc):
        i, j = pl.program_id(1), pl.program_id(2)

        @pl.when(j == 0)
        def _():
            m_s[...] = jnp.full_like(m_s, -jnp.inf)
            l_s[...] = jnp.zeros_like(l_s)
            acc[...] = jnp.zeros_like(acc)

        def step(diagonal):
            s = _bdot(q_ref[...], k_ref[...], NT)
            if diagonal:
                s = jnp.where(_chunk_mask(0, 0, (tq, tq), True), s, -jnp.inf)
            m_new = jnp.maximum(m_s[...], jnp.max(s, axis=1, keepdims=True))
            alpha = jnp.exp2((m_s[...] - m_new) * scale2)
            p = jnp.exp2((s - m_new) * scale2)
            l_s[...] = alpha * l_s[...] + jnp.sum(p, axis=1, keepdims=True)
            acc[...] = alpha * acc[...] + _bdot(p, v_ref[...], NN)
            m_s[...] = m_new

        pl.when(j < i)(functools.partial(step, False))
        pl.when(j == i)(functools.partial(step, True))

        @pl.when(j == nq - 1)
        def _():
            o_ref[...] = acc[...] / l_s[...]
            lse_ref[...] = m_s[...] * scale + jnp.log(l_s[...])

    return pl.pallas_call(
        body, grid=(heads, nq, nq),
        in_specs=[pl.BlockSpec((tq, 2 * LANE), lambda h, i, j: (i, h)),
                  pl.BlockSpec((tq, 2 * LANE), lambda h, i, j: (jnp.minimum(j, i), h)),
                  pl.BlockSpec((tq, VHEAD), lambda h, i, j: (jnp.minimum(j, i), h))],
        out_specs=[pl.BlockSpec((tq, VHEAD), lambda h, i, j: (i, h)),
                   pl.BlockSpec((None, tq, 1), lambda h, i, j: (h, i, 0))],
        out_shape=[jax.ShapeDtypeStruct((t, heads * VHEAD), F32), jax.ShapeDtypeStruct((heads, t, 1), F32)],
        scratch_shapes=[pltpu.VMEM((tq, 1), F32), pltpu.VMEM((tq, 1), F32), pltpu.VMEM((tq, VHEAD), F32)],
        compiler_params=_params("parallel", "parallel", "arbitrary"), name="mla_attn_fwd")(q, k, v)


def attn_bwd(q, k, v, o, do, lse_row, heads):
    t = q.shape[0]
    tq = _pick(t, ATT_TILE)
    nq = t // tq
    scale = (NOPE + ROPE) ** -0.5
    scale2 = scale * LOG2_E

    def body(q_ref, k_ref, v_ref, o_ref, do_ref, lse_ref, dq_ref, dk_ref, dv_ref):
        j, i = pl.program_id(1), pl.program_id(2)

        @pl.when((j == 0) & (i == 0))
        def _():
            dq_ref[...] = jnp.zeros_like(dq_ref)

        @pl.when(i == 0)
        def _():
            dk_ref[...] = jnp.zeros_like(dk_ref)
            dv_ref[...] = jnp.zeros_like(dv_ref)

        def step(diagonal):
            do_i = do_ref[...]
            prod = do_i * o_ref[...]
            hi = prod.astype(BF16)
            mid = (prod - hi.astype(F32)).astype(BF16)
            lo = (prod - hi.astype(F32) - mid.astype(F32)).astype(BF16)
            ones = jnp.ones((8, VHEAD), BF16)
            delta = (_bdot(ones, hi, NT) + _bdot(ones, mid, NT) + _bdot(ones, lo, NT))[0:1, :]
            st = _bdot(k_ref[...], q_ref[...], NT)
            pt = jnp.exp2(st * scale2 - lse_ref[...] * LOG2_E)
            if diagonal:
                pt = jnp.where(_chunk_mask(0, 0, (tq, tq), False), pt, 0.0)
            dv_ref[...] += _bdot(pt, do_i, NN)
            dpt = _bdot(v_ref[...], do_i, NT)
            dst = pt * (dpt - delta) * scale
            dk_ref[...] += _bdot(dst, q_ref[...], NN)
            rows = pl.ds(pl.multiple_of(i * tq, tq), tq)
            dq_ref[rows, :] += _bdot(dst, k_ref[...], TN)

        pl.when(i > j)(functools.partial(step, False))
        pl.when(i == j)(functools.partial(step, True))

    qmap = lambda h, j, i: (jnp.maximum(i, j), h)
    kmap = lambda h, j, i: (j, h)
    return pl.pallas_call(
        body, grid=(heads, nq, nq),
        in_specs=[pl.BlockSpec((tq, 2 * LANE), qmap), pl.BlockSpec((tq, 2 * LANE), kmap),
                  pl.BlockSpec((tq, VHEAD), kmap), pl.BlockSpec((tq, VHEAD), qmap), pl.BlockSpec((tq, VHEAD), qmap),
                  pl.BlockSpec((None, 1, tq), lambda h, j, i: (h, 0, jnp.maximum(i, j)))],
        out_specs=[pl.BlockSpec((t, 2 * LANE), lambda h, j, i: (0, h)),
                   pl.BlockSpec((tq, 2 * LANE), kmap), pl.BlockSpec((tq, VHEAD), kmap)],
        out_shape=[jax.ShapeDtypeStruct((t, heads * 2 * LANE), F32), jax.ShapeDtypeStruct((t, heads * 2 * LANE), F32),
                   jax.ShapeDtypeStruct((t, heads * VHEAD), F32)],
        compiler_params=_params("parallel", "arbitrary", "arbitrary"), name="mla_attn_bwd")(q, k, v, o, do, lse_row)


def _shift_down(x, s):
    if s == 0:
        return x
    rows = lax.broadcasted_iota(jnp.int32, x.shape, 0)
    return jnp.where(rows >= s, pltpu.roll(x, s, 0), 0.0)


def _shift_up(x, s):
    if s == 0:
        return x
    t = x.shape[0]
    rows = lax.broadcasted_iota(jnp.int32, x.shape, 0)
    return jnp.where(rows < t - s, pltpu.roll(x, t - s, 0), 0.0)


def _conv(x, w_ref, width):
    return sum(_shift_down(x, width - 1 - j) * w_ref[j:j + 1, :] for j in range(width))


def _conv_bwd(x, dpre, w_ref, dw_ref, width):
    dx = sum(_shift_up(dpre, width - 1 - j) * w_ref[j:j + 1, :] for j in range(width))
    for j in range(width):
        dw_ref[j:j + 1, :] = jnp.sum(dpre * _shift_down(x, width - 1 - j), axis=0, keepdims=True)
    return dx


def _dsilu(z):
    s = _sigmoid(z)
    return s * (1.0 + z * (1.0 - s))


def conv_qk_fwd(x, w, colscale):
    t, c = x.shape
    tc = _pick(c, 256)

    def body(x_ref, w_ref, s_ref, o_ref):
        o_ref[...] = (_silu(_conv(x_ref[...], w_ref, MLSTM_CONV)) * s_ref[...]).astype(o_ref.dtype)

    return pl.pallas_call(
        body, grid=(c // tc,),
        in_specs=[pl.BlockSpec((t, tc), lambda j: (0, j)), pl.BlockSpec((MLSTM_CONV, tc), lambda j: (0, j)),
                  pl.BlockSpec((1, tc), lambda j: (0, j))],
        out_specs=pl.BlockSpec((t, tc), lambda j: (0, j)), out_shape=jax.ShapeDtypeStruct((t, c), BF16),
        compiler_params=_params("parallel"), name="conv_qk_fwd")(x, w, colscale)


def conv_qk_bwd(x, w, colscale, dq, dk):
    t, c = x.shape
    tc = _pick(c // 2, 256)
    half = (c // 2) // tc

    def body(x_ref, w_ref, s_ref, dq_ref, dk_ref, dx_ref, dw_ref):
        j = pl.program_id(0)
        x_ = x_ref[...]
        dy = jnp.where(j < half, dq_ref[...], dk_ref[...])
        dpre = dy * s_ref[...] * _dsilu(_conv(x_, w_ref, MLSTM_CONV))
        dx_ref[...] = _conv_bwd(x_, dpre, w_ref, dw_ref, MLSTM_CONV).astype(dx_ref.dtype)

    return pl.pallas_call(
        body, grid=(c // tc,),
        in_specs=[pl.BlockSpec((t, tc), lambda j: (0, j)), pl.BlockSpec((MLSTM_CONV, tc), lambda j: (0, j)),
                  pl.BlockSpec((1, tc), lambda j: (0, j)),
                  pl.BlockSpec((t, tc), lambda j: (0, jnp.minimum(j, half - 1))),
                  pl.BlockSpec((t, tc), lambda j: (0, jnp.maximum(j - half, 0)))],
        out_specs=[pl.BlockSpec((t, tc), lambda j: (0, j)), pl.BlockSpec((MLSTM_CONV, tc), lambda j: (0, j))],
        out_shape=[jax.ShapeDtypeStruct((t, c), BF16), jax.ShapeDtypeStruct((MLSTM_CONV, c), F32)],
        compiler_params=_params("parallel"), name="conv_qk_bwd")(x, w, colscale, dq, dk)


def _mlstm_chunk(q, k, v, i_col, i_row, f_col, f_row, c_mat, n_vec, m):
    shape = (CHUNK, CHUNK)
    r = lax.broadcasted_iota(jnp.int32, shape, 0)
    c = lax.broadcasted_iota(jnp.int32, shape, 1)
    tril = c <= r
    lf_col, lf_row = _log_sigmoid(f_col), _log_sigmoid(f_row)
    bc_col = jnp.sum(jnp.where(tril, lf_row, 0.0), axis=1, keepdims=True)
    bc_row = jnp.sum(jnp.where(r <= c, lf_col, 0.0), axis=0, keepdims=True)
    logw = jnp.where(tril, bc_col - bc_row + i_row, -jnp.inf)
    inter = bc_col + m
    m_t = lax.stop_gradient(jnp.maximum(inter, jnp.max(logw, axis=1, keepdims=True)))
    w_intra = jnp.exp(logw - m_t)
    w_inter = jnp.exp(inter - m_t)
    sc = _dnt(q, k) * w_intra
    num = w_inter * _dnn(q, c_mat) + _dnn(sc, v)
    den = w_inter * jnp.sum(q * n_vec, axis=1, keepdims=True) + jnp.sum(sc, axis=1, keepdims=True)
    h = num / jnp.maximum(jnp.abs(den), jnp.exp(-m_t))
    b_last = jnp.sum(lf_row, axis=1, keepdims=True)
    m_new = lax.stop_gradient(jnp.maximum(b_last + m, jnp.max(b_last - bc_row + i_row, axis=1, keepdims=True)))
    decay = jnp.exp(b_last + m - m_new)
    uk = jnp.exp(b_last - bc_col + i_col - m_new) * k
    return h, decay * c_mat + _dtn(uk, v), decay * n_vec + jnp.sum(uk, axis=0, keepdims=True), m_new


def _mlstm_specs(heads, rev, nc):
    ci = (lambda c: nc - 1 - c) if rev else (lambda c: c)
    return dict(
        q=pl.BlockSpec((CHUNK, MLSTM_DK), lambda h, c: (ci(c), h)),
        k=pl.BlockSpec((CHUNK, MLSTM_DK), lambda h, c: (ci(c), heads + h)),
        v=pl.BlockSpec((CHUNK, MLSTM_DV), lambda h, c: (ci(c), h)),
        gc=pl.BlockSpec((None, None, CHUNK, 2), lambda h, c: (h, ci(c), 0, 0)),
        gr=pl.BlockSpec((None, None, 2, CHUNK), lambda h, c: (h, ci(c), 0, 0)),
        b=pl.BlockSpec((None, 1, 2), lambda h, c: (h, 0, 0)),
        cm=pl.BlockSpec((None, None, MLSTM_DK, MLSTM_DV), lambda h, c: (h, ci(c), 0, 0)),
        vec=pl.BlockSpec((None, None, 1, LANE), lambda h, c: (h, ci(c), 0, 0)),
    )


def _gates(gc_ref, gr_ref, b_ref):
    bi, bf = b_ref[:, 0:1], b_ref[:, 1:2]
    return gc_ref[:, 0:1] + bi, gr_ref[0:1, :] + bi, gc_ref[:, 1:2] + bf, gr_ref[1:2, :] + bf


def mlstm_fwd(qk, v, gcol, grow, bias, heads):
    t = qk.shape[0]
    nc = t // CHUNK
    sp = _mlstm_specs(heads, False, nc)

    def body(q_ref, k_ref, v_ref, gc_ref, gr_ref, b_ref, h_ref, c_ref, n_ref, m_ref, c_s, n_s, m_s):
        @pl.when(pl.program_id(1) == 0)
        def _():
            c_s[...] = jnp.zeros_like(c_s)
            n_s[...] = jnp.zeros_like(n_s)
            m_s[...] = jnp.zeros_like(m_s)

        c_ref[...] = c_s[...]
        n_ref[...] = n_s[...]
        m_ref[...] = m_s[...]
        q, k, v_ = _f32(q_ref[...], k_ref[...], v_ref[...])
        h, c_new, n_new, m_new = _mlstm_chunk(q, k, v_, *_gates(gc_ref, gr_ref, b_ref), c_s[...], n_s[...], m_s[:, 0:1])
        h_ref[...] = h
        c_s[...] = c_new
        n_s[...] = n_new
        m_s[...] = jnp.broadcast_to(m_new, m_s.shape)

    return pl.pallas_call(
        body, grid=(heads, nc),
        in_specs=[sp["q"], sp["k"], sp["v"], sp["gc"], sp["gr"], sp["b"]],
        out_specs=[sp["v"], sp["cm"], sp["vec"], sp["vec"]],
        out_shape=[jax.ShapeDtypeStruct((t, heads * MLSTM_DV), F32),
                   jax.ShapeDtypeStruct((heads, nc, MLSTM_DK, MLSTM_DV), F32),
                   jax.ShapeDtypeStruct((heads, nc, 1, LANE), F32), jax.ShapeDtypeStruct((heads, nc, 1, LANE), F32)],
        scratch_shapes=[pltpu.VMEM((MLSTM_DK, MLSTM_DV), F32), pltpu.VMEM((1, LANE), F32), pltpu.VMEM((1, LANE), F32)],
        compiler_params=_params("parallel", "arbitrary"), name="mlstm_fwd")(qk, qk, v, gcol, grow, bias)


def mlstm_bwd(qk, v, gcol, grow, bias, c_all, n_all, m_all, dh, heads):
    t = qk.shape[0]
    nc = t // CHUNK
    sp = _mlstm_specs(heads, True, nc)

    def body(q_ref, k_ref, v_ref, gc_ref, gr_ref, b_ref, c_ref, n_ref, m_ref, dh_ref,
             dq_ref, dk_ref, dv_ref, dgc_ref, dgr_ref, dc_s, dn_s):
        @pl.when(pl.program_id(1) == 0)
        def _():
            dc_s[...] = jnp.zeros_like(dc_s)
            dn_s[...] = jnp.zeros_like(dn_s)

        q, k, v_ = _f32(q_ref[...], k_ref[...], v_ref[...])
        m = m_ref[:, 0:1]
        _, pull = jax.vjp(lambda *a: _mlstm_chunk(*a, m)[:3], q, k, v_, *_gates(gc_ref, gr_ref, b_ref),
                          c_ref[...], n_ref[...])
        dq, dk, dv, di_col, di_row, df_col, df_row, dc, dn = pull((dh_ref[...], dc_s[...], dn_s[...]))
        dq_ref[...] = dq
        dk_ref[...] = dk
        dv_ref[...] = dv.astype(dv_ref.dtype)
        dgc_ref[:, 0:1] = di_col
        dgc_ref[:, 1:2] = df_col
        dgr_ref[0:1, :] = di_row
        dgr_ref[1:2, :] = df_row
        dc_s[...] = dc
        dn_s[...] = dn

    qspec = pl.BlockSpec((CHUNK, MLSTM_DK), lambda h, c: (nc - 1 - c, h))
    return pl.pallas_call(
        body, grid=(heads, nc),
        in_specs=[sp["q"], sp["k"], sp["v"], sp["gc"], sp["gr"], sp["b"], sp["cm"], sp["vec"], sp["vec"], sp["v"]],
        out_specs=[qspec, qspec, sp["v"], sp["gc"], sp["gr"]],
        out_shape=[jax.ShapeDtypeStruct((t, heads * MLSTM_DK), F32), jax.ShapeDtypeStruct((t, heads * MLSTM_DK), F32),
                   jax.ShapeDtypeStruct((t, heads * MLSTM_DV), BF16),
                   jax.ShapeDtypeStruct(gcol.shape, F32), jax.ShapeDtypeStruct(grow.shape, F32)],
        scratch_shapes=[pltpu.VMEM((MLSTM_DK, MLSTM_DV), F32), pltpu.VMEM((1, LANE), F32)],
        compiler_params=_params("parallel", "arbitrary"),
        name="mlstm_bwd")(qk, qk, v, gcol, grow, bias, c_all, n_all, m_all, dh)


def _hnorm_gate(h, zo, g):
    return _rms(h, g, MLSTM_DV) * _sigmoid(zo)


def _cross_core(q, k, v, g_q, g_k, heads):
    scale = CROSS_DH ** -0.5
    outs = []
    for h in range(heads):
        s = _dnt(_rms(q[h], g_q, CROSS_DH), _rms(k[h], g_k, CROSS_DH)) * scale
        p = jnp.exp(s - lax.stop_gradient(jnp.max(s, axis=1, keepdims=True)))
        p = p / jnp.sum(p, axis=1, keepdims=True)
        outs.append(_dnn(p, v[h]))
    return jnp.concatenate(outs, axis=1)


def _split_heads(ref, heads):
    return [ref[:, h * CROSS_DH:(h + 1) * CROSS_DH].astype(F32) for h in range(heads)]


def cross_fwd(q, k, v, g_q, g_k, heads):
    t = q.shape[0]
    tm = _pick(t, ATT_TILE)
    full = lambda a: pl.BlockSpec(a.shape, lambda i: (0, 0))

    def body(q_ref, k_ref, v_ref, gq_ref, gk_ref, o_ref):
        o_ref[...] = _cross_core(_split_heads(q_ref, heads), _split_heads(k_ref, heads), _split_heads(v_ref, heads),
                                 gq_ref[...], gk_ref[...], heads).astype(o_ref.dtype)

    return pl.pallas_call(
        body, grid=(t // tm,), in_specs=[pl.BlockSpec((tm, q.shape[1]), lambda i: (i, 0)), full(k), full(v), full(g_q), full(g_k)],
        out_specs=pl.BlockSpec((tm, q.shape[1]), lambda i: (i, 0)), out_shape=jax.ShapeDtypeStruct(q.shape, BF16),
        compiler_params=_params("parallel"), name="cross_fwd")(q, k, v, g_q, g_k)


def cross_bwd(q, k, v, g_q, g_k, do, heads):
    t, w = q.shape
    tm = _pick(t, ATT_TILE)
    full = lambda a: pl.BlockSpec(a.shape, lambda i: (0, 0))

    def body(q_ref, k_ref, v_ref, gq_ref, gk_ref, do_ref, dq_ref, dk_ref, dv_ref, dgq_ref, dgk_ref):
        qs, ks, vs = _split_heads(q_ref, heads), _split_heads(k_ref, heads), _split_heads(v_ref, heads)
        _, pull = jax.vjp(lambda a, b, c, d, e: _cross_core(a, b, c, d, e, heads), qs, ks, vs, gq_ref[...], gk_ref[...])
        dqs, dks, dvs, dgq, dgk = pull(do_ref[...])
        first = pl.program_id(0) == 0
        for h in range(heads):
            cols = slice(h * CROSS_DH, (h + 1) * CROSS_DH)
            dq_ref[:, cols] = dqs[h].astype(dq_ref.dtype)
            _store(dk_ref.at[:, cols], dks[h], first)
            _store(dv_ref.at[:, cols], dvs[h], first)
        _store(dgq_ref, dgq, first)
        _store(dgk_ref, dgk, first)

    row = pl.BlockSpec((tm, w), lambda i: (i, 0))
    return pl.pallas_call(
        body, grid=(t // tm,), in_specs=[row, full(k), full(v), full(g_q), full(g_k), row],
        out_specs=[row, full(k), full(v), full(g_q), full(g_k)],
        out_shape=[jax.ShapeDtypeStruct(q.shape, BF16), jax.ShapeDtypeStruct(k.shape, F32), jax.ShapeDtypeStruct(v.shape, F32),
                   jax.ShapeDtypeStruct(g_q.shape, F32), jax.ShapeDtypeStruct(g_k.shape, F32)],
        compiler_params=_params("arbitrary"), name="cross_bwd")(q, k, v, g_q, g_k, do)


def ffn_glu_fwd(hg, hv, wg, wv, bg, bv):
    t, f = hg.shape
    tc = _pick(f, LANE)
    col = pl.BlockSpec((t, tc), lambda j: (0, j))
    tap = pl.BlockSpec((FFN_CONV, tc), lambda j: (0, j))
    one = pl.BlockSpec((1, tc), lambda j: (0, j))

    def body(hg_ref, hv_ref, wg_ref, wv_ref, bg_ref, bv_ref, o_ref):
        gate = _conv(hg_ref[...], wg_ref, FFN_CONV) + bg_ref[...]
        val = _conv(hv_ref[...], wv_ref, FFN_CONV) + bv_ref[...]
        o_ref[...] = (_silu(gate) * val).astype(o_ref.dtype)

    return pl.pallas_call(body, grid=(f // tc,), in_specs=[col, col, tap, tap, one, one], out_specs=col,
                          out_shape=jax.ShapeDtypeStruct((t, f), BF16), compiler_params=_params("parallel"),
                          name="ffn_glu_fwd")(hg, hv, wg, wv, bg, bv)


def ffn_glu_bwd(hg, hv, wg, wv, bg, bv, dact):
    t, f = hg.shape
    tc = _pick(f, LANE)
    col = pl.BlockSpec((t, tc), lambda j: (0, j))
    tap = pl.BlockSpec((FFN_CONV, tc), lambda j: (0, j))
    one = pl.BlockSpec((1, tc), lambda j: (0, j))

    def body(hg_ref, hv_ref, wg_ref, wv_ref, bg_ref, bv_ref, da_ref, dhg_ref, dhv_ref, dwg_ref, dwv_ref, dbg_ref, dbv_ref):
        xg, xv, da = hg_ref[...], hv_ref[...], da_ref[...]
        gate = _conv(xg, wg_ref, FFN_CONV) + bg_ref[...]
        val = _conv(xv, wv_ref, FFN_CONV) + bv_ref[...]
        dgate = da * val * _dsilu(gate)
        dval = da * _silu(gate)
        dbg_ref[...] = jnp.sum(dgate, axis=0, keepdims=True)
        dbv_ref[...] = jnp.sum(dval, axis=0, keepdims=True)
        dhg_ref[...] = _conv_bwd(xg, dgate, wg_ref, dwg_ref, FFN_CONV).astype(dhg_ref.dtype)
        dhv_ref[...] = _conv_bwd(xv, dval, wv_ref, dwv_ref, FFN_CONV).astype(dhv_ref.dtype)

    return pl.pallas_call(
        body, grid=(f // tc,), in_specs=[col, col, tap, tap, one, one, col], out_specs=[col, col, tap, tap, one, one],
        out_shape=[jax.ShapeDtypeStruct((t, f), BF16), jax.ShapeDtypeStruct((t, f), BF16),
                   jax.ShapeDtypeStruct((FFN_CONV, f), F32), jax.ShapeDtypeStruct((FFN_CONV, f), F32),
                   jax.ShapeDtypeStruct((1, f), F32), jax.ShapeDtypeStruct((1, f), F32)],
        compiler_params=_params("parallel"), name="ffn_glu_bwd")(hg, hv, wg, wv, bg, bv, dact)


def _adamw(g, w, m, v):
    m = ADAM_B1 * m + (1.0 - ADAM_B1) * g
    v = ADAM_B2 * v + (1.0 - ADAM_B2) * (g * g)
    m_hat = m / (1.0 - ADAM_B1 ** ADAM_STEP)
    v_hat = v / (1.0 - ADAM_B2 ** ADAM_STEP)
    return -ADAM_LR * (m_hat / (jnp.sqrt(v_hat) + ADAM_EPS) + ADAM_WD * w), m, v


def adamw(g, w, m, v, name):
    r, c = g.shape
    tr = r
    for cand in (256, 128, 64, 32, 16, 8):
        if r % cand == 0 and cand * c * 4 <= (1 << 21):
            tr = cand
            break
    return ew(_adamw, [(g, "r"), (w, "r"), (m, "r"), (v, "r")], [((r, c), F32, "r")] * 3, gr=r // tr, name=name)


ANY = pl.BlockSpec(memory_space=pl.ANY)


def _place():
    x, y, c = lax.axis_index("x"), lax.axis_index("y"), lax.axis_index("c")
    return x, y, c, [(1 - x, y), (x, 1 - y), (1 - x, 1 - y)]


def _rcopy(src, dst, send, recv, k, to):
    return pltpu.make_async_remote_copy(src_ref=src, dst_ref=dst, send_sem=send.at[k], recv_sem=recv.at[k],
                                        device_id=to, device_id_type=MESH)


def gather_shards(bigs, smalls):
    nb, na = len(bigs), len(bigs) + len(smalls)
    arrays = list(bigs) + list(smalls)

    def body(*refs):
        ins, outs = refs[:na], refs[na:2 * na]
        send, recv = refs[2 * na:]
        x, y, c, chips = _place()
        me, sib = 2 * x + y, (x, y, 1 - c)

        def half(ref, a, which):
            rows = arrays[a].shape[0] // 2
            return ref.at[pl.ds(which * rows, rows)]

        started = []
        for a in range(na):
            for j, (cx, cy) in enumerate(chips):
                if a < nb:
                    cp = _rcopy(half(ins[a], a, c), half(outs[a].at[me], a, c), send, recv, 6 * a + j, (cx, cy, c))
                else:
                    cp = _rcopy(ins[a], outs[a].at[me], send, recv, 6 * nb + 3 * (a - nb) + j, (cx, cy, c))
                cp.start()
                started.append(cp)
        for a in range(nb):
            for j, (cx, cy) in enumerate(chips):
                landed = half(outs[a].at[2 * cx + cy], a, c)
                _rcopy(landed, landed, send, recv, 6 * a + j, (cx, cy, c)).wait_recv()
                cp = _rcopy(landed, landed, send, recv, 6 * a + 3 + j, sib)
                cp.start()
                started.append(cp)
        for a in range(na):
            for j, (cx, cy) in enumerate(chips):
                if a < nb:
                    dst = half(outs[a].at[2 * cx + cy], a, 1 - c)
                    _rcopy(dst, dst, send, recv, 6 * a + 3 + j, sib).wait_recv()
                else:
                    dst = outs[a].at[2 * cx + cy]
                    _rcopy(dst, dst, send, recv, 6 * nb + 3 * (a - nb) + j, (cx, cy, c)).wait_recv()
        for cp in started:
            cp.wait_send()

    n_sem = 6 * nb + 3 * (na - nb)
    gathered = pl.pallas_call(
        body, in_specs=[ANY] * na, out_specs=[ANY] * na,
        out_shape=[jax.ShapeDtypeStruct((N_CHIPS,) + a.shape, a.dtype) for a in arrays],
        scratch_shapes=[pltpu.SemaphoreType.DMA((n_sem,)), pltpu.SemaphoreType.DMA((n_sem,))],
        name="gather_shards")(*arrays)
    chip = 2 * lax.axis_index("x") + lax.axis_index("y")
    return [lax.dynamic_update_slice(g, a[None], (chip, 0, 0)) for g, a in zip(gathered, arrays)]


def sibling_halves(grads):
    na = len(grads)

    def body(*refs):
        ins, outs = refs[:na], refs[na:2 * na]
        send, recv = refs[2 * na:]
        x, y, c, _ = _place()
        cps = []
        for a in range(na):
            rows = grads[a].shape[1] // 2
            cp = _rcopy(ins[a].at[:, pl.ds((1 - c) * rows, rows)], outs[a], send, recv, a, (x, y, 1 - c))
            cp.start()
            cps.append(cp)
        for cp in cps:
            cp.wait()

    return pl.pallas_call(
        body, in_specs=[ANY] * na, out_specs=[ANY] * na,
        out_shape=[jax.ShapeDtypeStruct((g.shape[0], g.shape[1] // 2, g.shape[2]), g.dtype) for g in grads],
        scratch_shapes=[pltpu.SemaphoreType.DMA((na,)), pltpu.SemaphoreType.DMA((na,))], name="sibling_halves")(*grads)


def scatter_partials(parts):
    na = len(parts)

    def body(*refs):
        ins, outs = refs[:na], refs[na:2 * na]
        send, recv = refs[2 * na:]
        _, _, c, chips = _place()
        cps = []
        for a in range(na):
            for j, (cx, cy) in enumerate(chips):
                cp = _rcopy(ins[a].at[2 * cx + cy], outs[a].at[j], send, recv, 3 * a + j, (cx, cy, c))
                cp.start()
                cps.append(cp)
        for cp in cps:
            cp.wait()

    return pl.pallas_call(
        body, in_specs=[ANY] * na, out_specs=[ANY] * na,
        out_shape=[jax.ShapeDtypeStruct((3,) + p.shape[1:], p.dtype) for p in parts],
        scratch_shapes=[pltpu.SemaphoreType.DMA((3 * na,)), pltpu.SemaphoreType.DMA((3 * na,))], name="scatter_partials")(*parts)


def join_halves(halves):
    na = len(halves)

    def body(*refs):
        ins, outs = refs[:na], refs[na:2 * na]
        send, recv = refs[2 * na:]
        x, y, c, _ = _place()
        cps = []
        for a in range(na):
            cp = _rcopy(ins[a].at[c], outs[a].at[c], send, recv, a, (x, y, 1 - c))
            cp.start()
            cps.append(cp)
        for a in range(na):
            dst = outs[a].at[1 - c]
            _rcopy(dst, dst, send, recv, a, (x, y, 1 - c)).wait_recv()
        for cp in cps:
            cp.wait_send()

    return pl.pallas_call(
        body, in_specs=[ANY] * na, out_specs=[ANY] * na,
        out_shape=[jax.ShapeDtypeStruct(h.shape, h.dtype) for h in halves],
        input_output_aliases={a: a for a in range(na)},
        scratch_shapes=[pltpu.SemaphoreType.DMA((na,)), pltpu.SemaphoreType.DMA((na,))],
        name="join_halves")(*halves)


def allreduce_small(vec):
    r = vec.shape[0]

    def body(x_ref, sum_ref, all_ref, send, recv):
        x, y, c, _ = _place()
        me = 4 * x + 2 * y + c
        all_ref[me] = x_ref[...]
        cps, peers = [], []
        for mask in range(1, 8):
            px = 1 - x if mask & 4 else x
            py = 1 - y if mask & 2 else y
            pc = 1 - c if mask & 1 else c
            peers.append(4 * px + 2 * py + pc)
            cp = _rcopy(x_ref, all_ref.at[me], send, recv, mask - 1, (px, py, pc))
            cp.start()
            cps.append(cp)
        for k, cp in enumerate(cps):
            _rcopy(x_ref, all_ref.at[peers[k]], send, recv, k, (x, y, c)).wait_recv()
        for cp in cps:
            cp.wait_send()
        total = all_ref[0]
        for d in range(1, 8):
            total = total + all_ref[d]
        sum_ref[...] = total

    vm = pl.BlockSpec(memory_space=pltpu.VMEM)
    return pl.pallas_call(
        body, in_specs=[vm], out_specs=vm, out_shape=jax.ShapeDtypeStruct((r, LANE), F32),
        scratch_shapes=[pltpu.VMEM((8, r, LANE), F32), pltpu.SemaphoreType.DMA((7,)), pltpu.SemaphoreType.DMA((7,))],
        compiler_params=pltpu.CompilerParams(vmem_limit_bytes=VMEM_LIMIT), name="allreduce_small")(vec)


def _row_tile(rows):
    for cand in (256, 128, 64, 32, 16):
        if rows % cand == 0:
            return cand
    return rows


def add_sibling(grad, recv, c_idx):
    _, rows, cols = grad.shape
    hr = rows // 2
    tr = _row_tile(hr)
    nb = hr // tr

    def body(c_ref, g_ref, r_ref, o_ref):
        o_ref[...] = (g_ref[...].astype(F32) + r_ref[...].astype(F32)).astype(o_ref.dtype)

    return pl.pallas_call(
        body,
        grid_spec=pltpu.PrefetchScalarGridSpec(
            num_scalar_prefetch=1, grid=(N_CHIPS, nb),
            in_specs=[pl.BlockSpec((None, tr, cols), lambda k, r, c_ref: (k, c_ref[0] * nb + r, 0)),
                      pl.BlockSpec((None, tr, cols), lambda k, r, c_ref: (k, r, 0))],
            out_specs=pl.BlockSpec((None, tr, cols), lambda k, r, c_ref: (k, r, 0))),
        out_shape=jax.ShapeDtypeStruct((N_CHIPS, hr, cols), BF16),
        compiler_params=_params("parallel", "parallel"), name="add_sibling")(c_idx, grad, recv)


def sum_chips(part, others, place_idx):
    _, hr, cols = part.shape
    tr = _row_tile(hr)

    def body(k_ref, p_ref, o0_ref, o1_ref, o2_ref, out_ref):
        out_ref[...] = ((p_ref[...].astype(F32) + o0_ref[...].astype(F32)) + o1_ref[...].astype(F32)) + o2_ref[...].astype(F32)

    other = lambda j: pl.BlockSpec((None, tr, cols), lambda r, k_ref: (j, r, 0))
    return pl.pallas_call(
        body,
        grid_spec=pltpu.PrefetchScalarGridSpec(
            num_scalar_prefetch=1, grid=(hr // tr,),
            in_specs=[pl.BlockSpec((None, tr, cols), lambda r, k_ref: (k_ref[0], r, 0)), other(0), other(1), other(2)],
            out_specs=pl.BlockSpec((None, tr, cols), lambda r, k_ref: (k_ref[1], r, 0))),
        out_shape=jax.ShapeDtypeStruct((2, hr, cols), F32),
        compiler_params=_params("parallel"), name="sum_chips")(place_idx, part, others, others, others)


def _pad_lanes(a, width=LANE):
    return jnp.pad(a, ((0, 0), (0, width - a.shape[1])))


def _cols_from_shards(g):
    return jnp.transpose(g, (1, 0, 2)).reshape(g.shape[1], -1)


def _cols_to_shards(w):
    k, n4 = w.shape
    return jnp.transpose(w.reshape(k, N_CHIPS, n4 // N_CHIPS), (1, 0, 2))


def kernel(x, mem, positions, g_mix, w_in, g_qa, w_qb, g_kva, w_kvb, g_qn_nope, g_qn_pe, g_kn_nope, g_kn_pe, conv_qk, b_if, g_hnorm, p_a, p_b, w_out, g_cross, g_mem, wq_c, wk_c, wv_c, g_cq, g_ck, wo_c, g_ffn, w_up, conv_ffn, b_conv_ffn, w_down, loss_target, m_g_mix, m_w_in, m_g_qa, m_w_qb, m_g_kva, m_w_kvb, m_g_qn_nope, m_g_qn_pe, m_g_kn_nope, m_g_kn_pe, m_conv_qk, m_b_if, m_g_hnorm, m_p_a, m_p_b, m_w_out, m_g_cross, m_g_mem, m_wq_c, m_wk_c, m_wv_c, m_g_cq, m_g_ck, m_wo_c, m_g_ffn, m_w_up, m_conv_ffn, m_b_conv_ffn, m_w_down, v_g_mix, v_w_in, v_g_qa, v_w_qb, v_g_kva, v_w_kvb, v_g_qn_nope, v_g_qn_pe, v_g_kn_nope, v_g_kn_pe, v_conv_qk, v_b_if, v_g_hnorm, v_p_a, v_p_b, v_w_out, v_g_cross, v_g_mem, v_wq_c, v_wk_c, v_wv_c, v_g_cq, v_g_ck, v_wo_c, v_g_ffn, v_w_up, v_conv_ffn, v_b_conv_ffn, v_w_down):
    names = ["g_mix", "w_in", "g_qa", "w_qb", "g_kva", "w_kvb", "g_qn_nope", "g_qn_pe", "g_kn_nope", "g_kn_pe", "conv_qk",
             "b_if", "g_hnorm", "p_a", "p_b", "w_out", "g_cross", "g_mem", "wq_c", "wk_c", "wv_c", "g_cq", "g_ck", "wo_c",
             "g_ffn", "w_up", "conv_ffn", "b_conv_ffn", "w_down"]
    env = locals()
    wts = {n: env[n] for n in names}
    mom = {n: env["m_" + n] for n in names}
    var = {n: env["v_" + n] for n in names}

    xi, yi, ci = lax.axis_index("x"), lax.axis_index("y"), lax.axis_index("c")
    chip = 2 * xi + yi
    place_arr = jnp.stack([chip, ci]).astype(jnp.int32)
    c_arr = jnp.reshape(ci, (1,)).astype(jnp.int32)

    x2d, tgt, mem2d = x[0], loss_target[0], mem[0]
    t, d = x2d.shape
    mla_h = w_qb.shape[2] * N_CHIPS // (NOPE + ROPE)
    ml_h = b_if.shape[1] // 2
    cr_h = wq_c.shape[2] // CROSS_DH
    f_dim = w_down.shape[1] * N_CHIPS
    q_rank, kv_rank = g_qa.shape[1], g_kva.shape[1]
    qk_w, v_w = ml_h * MLSTM_DK, ml_h * MLSTM_DV
    nc = t // CHUNK

    big_names = ["w_in", "w_qb", "w_kvb", "p_a", "p_b", "w_out", "wq_c", "wk_c", "wv_c", "wo_c", "w_up", "w_down"]
    col_sharded = {"w_in", "w_qb", "w_kvb", "wo_c", "w_up"}
    small_sharded = ["conv_qk", "g_hnorm", "conv_ffn"]
    gathered = gather_shards([wts[n][0].astype(BF16) for n in big_names], [wts[n][0] for n in small_sharded])
    full = {}
    for n, g in zip(big_names + small_sharded, gathered):
        full[n] = _cols_from_shards(g) if (n in col_sharded or n in small_sharded) else g.reshape(-1, g.shape[2])

    o_qa, o_kv, o_kpe = 0, q_rank, q_rank + kv_rank
    o_q = o_kpe + ROPE
    o_v = o_q + 2 * qk_w
    o_if = o_v + v_w
    o_o = o_if + 2 * ml_h
    o_ga, o_gb = o_o + v_w, o_o + v_w + d
    wi = full["w_in"]
    pad_kpe = jnp.zeros((d, LANE - ROPE), BF16)
    pad_if = jnp.zeros((d, LANE - 2 * ml_h), BF16)
    w_small = jnp.concatenate([wi[:, o_qa:o_q], pad_kpe, wi[:, o_if:o_o], pad_if], axis=1)
    o_kpe_s, o_if_s = o_kpe, o_kpe + LANE
    w_qk, w_v, w_o, w_ga, w_gb = wi[:, o_q:o_v], wi[:, o_v:o_if], wi[:, o_o:o_ga], wi[:, o_ga:o_gb], wi[:, o_gb:]

    wq3 = full["w_qb"].reshape(q_rank, mla_h, NOPE + ROPE)
    wq_nope = wq3[:, :, :NOPE].reshape(q_rank, mla_h * NOPE)
    wq_pe = jnp.pad(wq3[:, :, NOPE:], ((0, 0), (0, 0), (0, LANE - ROPE))).reshape(q_rank, mla_h * LANE)
    wkv3 = full["w_kvb"].reshape(kv_rank, mla_h, NOPE + VHEAD)
    wk_nope = wkv3[:, :, :NOPE].reshape(kv_rank, mla_h * NOPE)
    wv_mla = wkv3[:, :, NOPE:].reshape(kv_rank, mla_h * VHEAD)
    wup_g, wup_v = full["w_up"][:, :f_dim], full["w_up"][:, f_dim:]

    inv_freq = ROPE_BASE ** (-jnp.arange(0, ROPE, 2, dtype=F32) / ROPE)
    ang = positions[0].astype(F32)[:, None] * inv_freq
    cos, sin = jnp.cos(ang), jnp.sin(ang)
    zero_h = jnp.zeros_like(cos)
    tabs = [_pad_lanes(jnp.concatenate([cos, cos], axis=1)), _pad_lanes(-sin), _pad_lanes(jnp.concatenate([zero_h, sin], axis=1))]
    mla_gains = [g_qn_nope, _pad_lanes(g_qn_pe), g_kn_nope, _pad_lanes(g_kn_pe)]

    u1 = rms_fwd(x2d, g_mix, "rms_mix")
    z_small = mm(u1, w_small, name="in_small")
    z_qa, z_kv = z_small[:, o_qa:o_kv], z_small[:, o_kv:o_kpe]
    z_kpe, z_if = z_small[:, o_kpe_s:o_kpe_s + LANE], z_small[:, o_if_s:o_if_s + 2 * ml_h]
    z_qk = mm(u1, w_qk, name="in_qk")
    z_v = mm(u1, w_v, name="in_v")
    z_o = mm(u1, w_o, name="in_o")
    z_ga = mm(u1, w_ga, name="in_ga")
    z_gb = mm(u1, w_gb, name="in_gb")

    qa_n = rms_fwd(z_qa, g_qa, "rms_qa")
    kv_n = rms_fwd(z_kv, g_kva, "rms_kva")
    qn_raw = mm(qa_n, wq_nope, name="q_nope")
    qp_raw = mm(qa_n, wq_pe, name="q_pe")
    kn_raw = mm(kv_n, wk_nope, name="k_nope")
    v_mla = mm(kv_n, wv_mla, out_dtype=BF16, name="v_mla")
    q_att, k_att = mla_prep_fwd(qn_raw, qp_raw, kn_raw, z_kpe, tabs, mla_gains, mla_h)
    y_a, lse = attn_fwd(q_att, k_att, v_mla, mla_h)

    colscale = jnp.concatenate([jnp.full((1, qk_w), MLSTM_DK ** -0.5, F32), jnp.ones((1, qk_w), F32)], axis=1)
    qk_c = conv_qk_fwd(z_qk, full["conv_qk"], colscale)
    gates4 = z_if.reshape(nc, CHUNK, 2, ml_h)
    gcol = jnp.transpose(gates4, (3, 0, 1, 2))
    grow = jnp.transpose(gates4, (3, 0, 2, 1))
    bias = jnp.transpose(b_if.reshape(2, ml_h), (1, 0)).reshape(ml_h, 1, 2)
    h_raw, c_all, n_all, m_all = mlstm_fwd(qk_c, z_v, gcol, grow, bias, ml_h)
    g_hn = full["g_hnorm"].reshape(1, v_w)
    hn_gr, hd_gr = t // _pick(t, ROW_TILE), t // _pick(t, HEAD_ROW_TILE)
    y_b = ew(lambda *a: (_hnorm_gate(*a),), [(h_raw, "rc"), (z_o, "rc"), (g_hn, "c")], [((t, v_w), BF16, "rc")], gr=hd_gr, gc=ml_h,
             name="hnorm_gate")[0]

    pa = mm(y_a, full["p_a"], name="proj_a")
    pb = mm(y_b, full["p_b"], name="proj_b")
    merge_fn = lambda ga, gb, a, b: (_sigmoid(ga) * a + _sigmoid(gb) * b,)
    merged = ew(merge_fn, [(z_ga, "r"), (z_gb, "r"), (pa, "r"), (pb, "r")], [((t, d), BF16, "r")], gr=hn_gr, name="merge")[0]
    x1 = mm(merged, full["w_out"], add=x2d, name="out_proj")

    uc = rms_fwd(x1, g_cross, "rms_cross")
    mem_n = rms_fwd(mem2d, g_mem, "rms_mem")
    qc = mm(uc, full["wq_c"], name="cross_q")
    kc = mm(mem_n, full["wk_c"], name="cross_k")
    vc = mm(mem_n, full["wv_c"], name="cross_v")
    oc = cross_fwd(qc, kc, vc, g_cq, g_ck, cr_h)
    x2 = mm(oc, full["wo_c"], add=x1, name="cross_out")

    u3 = rms_fwd(x2, g_ffn, "rms_ffn")
    hg = mm(u3, wup_g, name="ffn_up_gate")
    hv = mm(u3, wup_v, name="ffn_up_val")
    cw, cb = full["conv_ffn"], b_conv_ffn
    act = ffn_glu_fwd(hg, hv, cw[:, :f_dim], cw[:, f_dim:], cb[:, :f_dim], cb[:, f_dim:])
    y = mm(act, full["w_down"], add=x2, name="ffn_down")

    def loss_fn(y_, t_):
        err = y_ - t_
        part = jnp.sum(jnp.sum(err * err, axis=1, keepdims=True), axis=0, keepdims=True) * (0.5 / d)
        return err * (1.0 / d), err * (1.0 / d), jnp.broadcast_to(part, (1, LANE))

    dy, dy_mx, loss_part = ew(loss_fn, [(y, "r"), (tgt, "r")], [((t, d), F32, "r"), ((t, d), BF16, "r"), ((1, LANE), F32, "f")],
                              gr=hn_gr, name="loss")

    gw = {}
    gw["w_down"] = mm(act, dy_mx, ta=True, out_dtype=BF16, name="dw_down")
    dact = mm(dy_mx, full["w_down"], tb=True, name="d_act")
    dhg, dhv, dcw_g, dcw_v, dcb_g, dcb_v = ffn_glu_bwd(hg, hv, cw[:, :f_dim], cw[:, f_dim:], cb[:, :f_dim], cb[:, f_dim:], dact)
    gw["conv_ffn"] = jnp.concatenate([dcw_g, dcw_v], axis=1)
    gw["b_conv_ffn"] = jnp.concatenate([dcb_g, dcb_v], axis=1)
    dwup_g = mm(u3, dhg, ta=True, out_dtype=BF16, name="dw_up_gate")
    dwup_v = mm(u3, dhv, ta=True, out_dtype=BF16, name="dw_up_val")
    du3 = mm(dhg, wup_g, tb=True, name="d_u3_gate")
    du3 = mm(dhv, wup_v, tb=True, add=du3, name="d_u3_val")
    dx2, gw["g_ffn"] = rms_bwd(x2, g_ffn, du3, dy, "rms_ffn_bwd")

    gw["wo_c"] = mm(oc, dx2, ta=True, out_dtype=BF16, name="dw_cross_out")
    doc = mm(dx2, full["wo_c"], tb=True, name="d_cross_o")
    dqc, dkc, dvc, gw["g_cq"], gw["g_ck"] = cross_bwd(qc, kc, vc, g_cq, g_ck, doc, cr_h)
    gw["wq_c"] = mm(uc, dqc, ta=True, out_dtype=BF16, name="dw_cross_q")
    gw["wk_c"] = mm(mem_n, dkc, ta=True, out_dtype=BF16, name="dw_cross_k")
    gw["wv_c"] = mm(mem_n, dvc, ta=True, out_dtype=BF16, name="dw_cross_v")
    duc = mm(dqc, full["wq_c"], tb=True, name="d_uc")
    dmem_n = mm(dkc, full["wk_c"], tb=True, name="d_mem_k")
    dmem_n = mm(dvc, full["wv_c"], tb=True, add=dmem_n, name="d_mem_v")
    _, gw["g_mem"] = rms_bwd(mem2d, g_mem, dmem_n, None, "rms_mem_bwd")
    dx1, gw["g_cross"] = rms_bwd(x1, g_cross, duc, dx2, "rms_cross_bwd")

    gw["w_out"] = mm(merged, dx1, ta=True, out_dtype=BF16, name="dw_out")
    dmerged = mm(dx1, full["w_out"], tb=True, name="d_merged")

    def merge_bwd(ga, gb, a, b, dm):
        _, pull = jax.vjp(lambda *args: merge_fn(*args)[0], ga, gb, a, b)
        return pull(dm)

    dz_ga, dz_gb, dpa, dpb = ew(merge_bwd, [(z_ga, "r"), (z_gb, "r"), (pa, "r"), (pb, "r"), (dmerged, "r")],
                                [((t, d), BF16, "r")] * 4, gr=hn_gr, name="merge_bwd")
    gw["p_a"] = mm(y_a, dpa, ta=True, out_dtype=BF16, name="dw_proj_a")
    gw["p_b"] = mm(y_b, dpb, ta=True, out_dtype=BF16, name="dw_proj_b")
    dy_a = mm(dpa, full["p_a"], tb=True, name="d_ya")
    dy_b = mm(dpb, full["p_b"], tb=True, name="d_yb")

    def hnorm_bwd(h_, zo_, g_, dyb_):
        _, pull = jax.vjp(_hnorm_gate, h_, zo_, g_)
        return pull(dyb_)

    dh_raw, dz_o, dg_hn = ew(hnorm_bwd, [(h_raw, "rc"), (z_o, "rc"), (g_hn, "c"), (dy_b, "rc")],
                             [((t, v_w), F32, "rc"), ((t, v_w), BF16, "rc"), ((1, v_w), F32, "c")],
                             gr=hd_gr, gc=ml_h, order="cr", name="hnorm_gate_bwd")
    gw["g_hnorm"] = dg_hn.reshape(ml_h, MLSTM_DV)
    dq_m, dk_m, dz_v, dgcol, dgrow = mlstm_bwd(qk_c, z_v, gcol, grow, bias, c_all, n_all, m_all, dh_raw, ml_h)
    dgates4 = jnp.transpose(dgcol, (1, 2, 3, 0)) + jnp.transpose(dgrow, (1, 3, 2, 0))
    dz_if = dgates4.reshape(t, 2 * ml_h)
    gw["b_if"] = ew(lambda a: (jnp.sum(a, axis=0, keepdims=True),), [(dz_if, "r")], [((1, 2 * ml_h), F32, "f")],
                    gr=hn_gr, name="bias_if_bwd")[0]
    dz_qk, gw["conv_qk"] = conv_qk_bwd(z_qk, full["conv_qk"], colscale, dq_m, dk_m)

    lse_row = lse.reshape(mla_h, 1, t)
    dq_att, dk_att, dv_mla = attn_bwd(q_att, k_att, v_mla, y_a, dy_a, lse_row, mla_h)
    dqn_raw, dqp_raw, dkn_raw, dz_kpe, gw["g_qn_nope"], dg_qp, gw["g_kn_nope"], dg_kp = mla_prep_bwd(
        qn_raw, qp_raw, kn_raw, z_kpe, tabs, mla_gains, dq_att, dk_att, mla_h)
    gw["g_qn_pe"], gw["g_kn_pe"] = dg_qp[:, :ROPE], dg_kp[:, :ROPE]
    dwq_nope = mm(qa_n, dqn_raw, ta=True, out_dtype=BF16, name="dw_q_nope")
    dwq_pe = mm(qa_n, dqp_raw, ta=True, out_dtype=BF16, name="dw_q_pe")
    dwk_nope = mm(kv_n, dkn_raw, ta=True, out_dtype=BF16, name="dw_k_nope")
    dwv_mla = mm(kv_n, dv_mla, ta=True, out_dtype=BF16, name="dw_v_mla")
    dqa_n = mm(dqn_raw, wq_nope, tb=True, name="d_qa_nope")
    dqa_n = mm(dqp_raw, wq_pe, tb=True, add=dqa_n, name="d_qa_pe")
    dkv_n = mm(dkn_raw, wk_nope, tb=True, name="d_kv_nope")
    dkv_n = mm(dv_mla, wv_mla, tb=True, add=dkv_n, name="d_kv_v")
    dz_qa, gw["g_qa"] = rms_bwd(z_qa, g_qa, dqa_n, None, "rms_qa_bwd", BF16)
    dz_kv, gw["g_kva"] = rms_bwd(z_kv, g_kva, dkv_n, None, "rms_kva_bwd", BF16)
    gw["w_qb"] = jnp.concatenate([dwq_nope.reshape(q_rank, mla_h, NOPE), dwq_pe.reshape(q_rank, mla_h, LANE)[:, :, :ROPE]],
                                 axis=2).reshape(q_rank, -1)
    gw["w_kvb"] = jnp.concatenate([dwk_nope.reshape(kv_rank, mla_h, NOPE), dwv_mla.reshape(kv_rank, mla_h, VHEAD)],
                                  axis=2).reshape(kv_rank, -1)

    dz_small = jnp.concatenate([dz_qa, dz_kv, dz_kpe.astype(BF16), _pad_lanes(dz_if).astype(BF16)], axis=1)
    dw_small = mm(u1, dz_small, ta=True, out_dtype=BF16, name="dw_in_small")
    du1 = mm(dz_small, w_small, tb=True, name="d_u1_small")
    dw_segs = []
    for nm, dz, w_seg in (("qk", dz_qk, w_qk), ("v", dz_v, w_v), ("o", dz_o, w_o), ("ga", dz_ga, w_ga), ("gb", dz_gb, w_gb)):
        dw_segs.append(mm(u1, dz, ta=True, out_dtype=BF16, name="dw_in_" + nm))
        du1 = mm(dz, w_seg, tb=True, add=du1, name="d_u1_" + nm)
    gw["w_in"] = jnp.concatenate([dw_small[:, :o_kpe_s + ROPE], dw_segs[0], dw_segs[1],
                                  dw_small[:, o_if_s:o_if_s + 2 * ml_h], dw_segs[2], dw_segs[3], dw_segs[4]], axis=1)
    gw["w_up"] = jnp.concatenate([dwup_g, dwup_v], axis=1)
    grad_x, gw["g_mix"] = rms_bwd(x2d, g_mix, du1, dx1, "rms_mix_bwd")

    shard_major = [_cols_to_shards(gw[n]) if n in col_sharded else gw[n].reshape(N_CHIPS, -1, gw[n].shape[1]) for n in big_names]
    from_sib = sibling_halves(shard_major)
    parts = [add_sibling(g, r, c_arr) for g, r in zip(shard_major, from_sib)]
    others = scatter_partials(parts)
    halves = [sum_chips(p, o, place_arr) for p, o in zip(parts, others)]
    joined = join_halves(halves)
    big_grads = {n: j.reshape(-1, j.shape[2]) for n, j in zip(big_names, joined)}

    small_names = [n for n in names if n not in big_names]
    pieces = [loss_part]
    for n in small_names:
        flat = gw[n].reshape(1, -1)
        pieces.append(jnp.pad(flat, ((0, 0), (0, (-flat.shape[1]) % LANE))))
    packed = jnp.concatenate(pieces, axis=1)
    packed = jnp.pad(packed, ((0, 0), (0, (-packed.shape[1]) % (8 * LANE)))).reshape(-1, LANE)
    total = allreduce_small(packed).reshape(1, -1)
    loss = total[0, 0]
    small_grads, off = {}, LANE
    for n in small_names:
        size = gw[n].size
        g_full = total[:, off:off + size].reshape(gw[n].shape)
        off += size + (-size) % LANE
        if n in small_sharded:
            width = wts[n].shape[-1]
            g_full = lax.dynamic_slice_in_dim(g_full, chip * width, width, axis=g_full.ndim - 1)
        small_grads[n] = g_full.reshape(wts[n].shape[1:])

    grads, deltas, new_m, new_v = {}, {}, {}, {}
    for n in big_names:
        w2 = wts[n][0]
        grads[n] = big_grads[n]
        deltas[n], new_m[n], new_v[n] = adamw(big_grads[n], w2, mom[n][0], var[n][0], "adamw_" + n)

    def pack_small(tree):
        flat = jnp.concatenate([tree[n].reshape(1, -1) for n in small_names], axis=1)
        return jnp.pad(flat, ((0, 0), (0, (-flat.shape[1]) % (8 * LANE)))).reshape(8, -1)

    sg = pack_small(small_grads)
    sd, sm, sv = adamw(sg, pack_small({n: wts[n][0] for n in small_names}), pack_small({n: mom[n][0] for n in small_names}),
                       pack_small({n: var[n][0] for n in small_names}), "adamw_small")
    off = 0
    for n in small_names:
        size = small_grads[n].size
        shp = wts[n].shape[1:]
        grads[n] = small_grads[n]
        for dst, src in ((deltas, sd), (new_m, sm), (new_v, sv)):
            dst[n] = src.reshape(1, -1)[:, off:off + size].reshape(shp)
        off += size

    def out(tree):
        return [tree[n].reshape(wts[n].shape) for n in names]

    return (loss, grad_x.reshape(x.shape), *out(grads), *out(deltas), *out(new_m), *out(new_v))
```

```python
import functools
import math

import jax
import jax.numpy as jnp
from jax import lax
from jax.experimental import pallas as pl
from jax.experimental.pallas import tpu as pltpu

F32, BF16 = jnp.float32, jnp.bfloat16
MESH = pl.DeviceIdType.MESH

EPS = 1e-6
CHUNK = 64
LOG2_CHUNK = 6
NOPE, ROPE, VHEAD = 128, 64, 128
MLSTM_DK, MLSTM_DV, MLSTM_CONV = 128, 256, 4
CROSS_DH = 128
FFN_CONV = 3
ROPE_BASE = 10000.0
LOG2_E = math.log2(math.e)
ADAM_LR, ADAM_B1, ADAM_B2, ADAM_EPS, ADAM_WD, ADAM_STEP = 0.001, 0.9, 0.999, 1e-08, 0.01, 10

LANE = 128
ROW_TILE = 256
HEAD_ROW_TILE = 1024
ATT_TILE = 512
MM_TILES = (1024, 1024, 2048)
VMEM_LIMIT = 56 * 1024 * 1024
N_CHIPS = 4

NN = ((1,), (0,))
NT = ((1,), (1,))
TN = ((0,), (0,))


def _pick(dim, pref):
    if dim <= pref:
        return dim
    for t in range(pref, 0, -LANE):
        if dim % t == 0:
            return t
    return dim


def _bdot(a, b, dims):
    return lax.dot_general(a.astype(BF16), b.astype(BF16), (dims, ((), ())), preferred_element_type=F32)


@jax.custom_vjp
def _dnn(a, b):
    return _bdot(a, b, NN)


_dnn.defvjp(lambda a, b: (_bdot(a, b, NN), (a, b)),
            lambda r, g: (_bdot(g, r[1], NT), _bdot(r[0], g, TN)))


@jax.custom_vjp
def _dnt(a, b):
    return _bdot(a, b, NT)


_dnt.defvjp(lambda a, b: (_bdot(a, b, NT), (a, b)),
            lambda r, g: (_bdot(g, r[1], NN), _bdot(g, r[0], TN)))


@jax.custom_vjp
def _dtn(a, b):
    return _bdot(a, b, TN)


_dtn.defvjp(lambda a, b: (_bdot(a, b, TN), (a, b)),
            lambda r, g: (_bdot(r[1], g, NT), _bdot(r[0], g, NN)))


@functools.partial(jax.custom_vjp, nondiff_argnums=(1,))
def _lane_roll(x, shift):
    return pltpu.roll(x, shift, 1)


_lane_roll.defvjp(lambda x, shift: (pltpu.roll(x, shift, 1), None),
                  lambda shift, _, g: (pltpu.roll(g, (LANE - shift) % LANE, 1),))


def _params(*sem):
    return pltpu.CompilerParams(dimension_semantics=sem, vmem_limit_bytes=VMEM_LIMIT)


def mm(a, b, *, ta=False, tb=False, add=None, out_dtype=F32, name):
    m_dim, k_dim = (a.shape[1], a.shape[0]) if ta else a.shape
    n_dim = b.shape[0] if tb else b.shape[1]
    assert k_dim == (b.shape[1] if tb else b.shape[0]), (name, a.shape, b.shape)
    tm, tn, tk = _pick(m_dim, MM_TILES[0]), _pick(n_dim, MM_TILES[1]), _pick(k_dim, MM_TILES[2])
    nk = k_dim // tk
    dims = ((0,) if ta else (1,), (1,) if tb else (0,))
    has_add = add is not None

    def body(*refs):
        a_ref, b_ref = refs[0], refs[1]
        c_ref = refs[2] if has_add else None
        o_ref = refs[3] if has_add else refs[2]
        prod = _bdot(a_ref[...], b_ref[...], dims)
        if nk == 1:
            o_ref[...] = (prod + c_ref[...].astype(F32) if has_add else prod).astype(o_ref.dtype)
            return
        acc = refs[-1]
        k = pl.program_id(2)

        @pl.when(k == 0)
        def _():
            acc[...] = prod + c_ref[...].astype(F32) if has_add else prod

        @pl.when(k > 0)
        def _():
            acc[...] += prod

        @pl.when(k == nk - 1)
        def _():
            o_ref[...] = acc[...].astype(o_ref.dtype)

    in_specs = [
        pl.BlockSpec((tk, tm), lambda i, j, k: (k, i)) if ta else pl.BlockSpec((tm, tk), lambda i, j, k: (i, k)),
        pl.BlockSpec((tn, tk), lambda i, j, k: (j, k)) if tb else pl.BlockSpec((tk, tn), lambda i, j, k: (k, j)),
    ]
    args = [a, b]
    if has_add:
        in_specs.append(pl.BlockSpec((tm, tn), lambda i, j, k: (i, j)))
        args.append(add)
    return pl.pallas_call(
        body, grid=(m_dim // tm, n_dim // tn, nk), in_specs=in_specs,
        out_specs=pl.BlockSpec((tm, tn), lambda i, j, k: (i, j)),
        out_shape=jax.ShapeDtypeStruct((m_dim, n_dim), out_dtype),
        scratch_shapes=[pltpu.VMEM((tm, tn), F32)] if nk > 1 else [],
        compiler_params=_params("parallel", "parallel", "arbitrary"), name=name)(*args)


def ew(fn, ins, outs, *, gr, gc=1, order="rc", name):
    n_in = len(ins)

    def block(shape, kind):
        r, c = shape
        return (r // gr if kind in ("rc", "r") else r, c // gc if kind in ("rc", "c") else c)

    def imap(kind):
        def f(p0, p1):
            i, j = (p0, p1) if order == "rc" else (p1, p0)
            return {"rc": (i, j), "r": (i, 0), "c": (0, j), "f": (0, 0)}[kind]
        return f

    def body(*refs):
        p0, p1 = pl.program_id(0), pl.program_id(1)
        i, j = (p0, p1) if order == "rc" else (p1, p0)
        vals = fn(*[r[...] for r in refs[:n_in]])
        for ref, val, (_, dtype, kind) in zip(refs[n_in:], vals, outs):
            first = {"rc": None, "r": (j == 0) if gc > 1 else None, "c": (i == 0) if gr > 1 else None,
                     "f": ((i == 0) & (j == 0)) if gr * gc > 1 else None}[kind]
            _store(ref, val.astype(dtype), first)

    grid = (gr, gc) if order == "rc" else (gc, gr)
    return pl.pallas_call(
        body, grid=grid,
        in_specs=[pl.BlockSpec(block(a.shape, k), imap(k)) for a, k in ins],
        out_specs=[pl.BlockSpec(block(s, k), imap(k)) for s, _, k in outs],
        out_shape=[jax.ShapeDtypeStruct(s, d) for s, d, _ in outs],
        compiler_params=_params("arbitrary", "arbitrary"), name=name)(*[a for a, _ in ins])


def _store(ref, val, first):
    if first is None:
        ref[...] = val
        return

    @pl.when(first)
    def _():
        ref[...] = val

    @pl.when(jnp.logical_not(first))
    def _():
        ref[...] += val


def _f32(*xs):
    return [x.astype(F32) for x in xs]


def _rms(x, g, n):
    ms = jnp.sum(x * x, axis=-1, keepdims=True) * (1.0 / n)
    return x * lax.rsqrt(ms + EPS) * g


def _sigmoid(x):
    return 1.0 / (1.0 + jnp.exp(-x))


def _silu(x):
    return x * _sigmoid(x)


def _log_sigmoid(x):
    return jnp.minimum(x, 0.0) - jnp.log(1.0 + jnp.exp(-jnp.abs(x)))


def rms_fwd(x, g, name, out_dtype=BF16):
    t, w = x.shape
    return ew(lambda x_, g_: (_rms(x_, g_, w),), [(x, "r"), (g, "f")], [((t, w), out_dtype, "r")],
              gr=t // _pick(t, ROW_TILE), name=name)[0]


def rms_bwd(x, g, du, res, name, out_dtype=F32):
    t, w = x.shape

    def fn(x_, g_, du_, *res_):
        _, pull = jax.vjp(lambda a, b: _rms(a, b, w), x_, g_)
        dx, dg = pull(du_.astype(F32))
        return (dx + res_[0] if res_ else dx), dg

    ins = [(x, "r"), (g, "f"), (du, "r")] + ([(res, "r")] if res is not None else [])
    return ew(fn, ins, [((t, w), out_dtype, "r"), ((1, w), F32, "f")], gr=t // _pick(t, ROW_TILE), name=name)


def _rope(x, cos_t, sin_lo, sin_hi):
    return x * cos_t + _lane_roll(x, LANE - ROPE // 2) * sin_lo + _lane_roll(x, ROPE // 2) * sin_hi


def _mla_prep(qn, qp, kn, kp, cos_t, sin_lo, sin_hi, g_qn, g_qp, g_kn, g_kp):
    q = jnp.concatenate([_rms(qn, g_qn, NOPE), _rope(_rms(qp, g_qp, ROPE), cos_t, sin_lo, sin_hi)], axis=1)
    k = jnp.concatenate([_rms(kn, g_kn, NOPE), _rope(_rms(kp, g_kp, ROPE), cos_t, sin_lo, sin_hi)], axis=1)
    return q, k


def mla_prep_fwd(qn, qp, kn, kp, tabs, gains, heads):
    t = qn.shape[0]
    ins = [(qn, "rc"), (qp, "rc"), (kn, "rc"), (kp, "r")] + [(a, "r") for a in tabs] + [(g, "f") for g in gains]
    return ew(lambda *a: _mla_prep(*_f32(*a)), ins,
              [((t, heads * 2 * LANE), BF16, "rc"), ((t, heads * 2 * LANE), BF16, "rc")],
              gr=t // _pick(t, HEAD_ROW_TILE), gc=heads, name="mla_prep_fwd")


def mla_prep_bwd(qn, qp, kn, kp, tabs, gains, dq, dk, heads):
    t = qn.shape[0]

    def fn(qn_, qp_, kn_, kp_, c_, s1_, s2_, g1, g2, g3, g4, dq_, dk_):
        _, pull = jax.vjp(lambda a, b, c, d, e, f, g, h: _mla_prep(a, b, c, d, c_, s1_, s2_, e, f, g, h),
                          qn_, qp_, kn_, kp_, g1, g2, g3, g4)
        return pull((dq_, dk_))

    ins = ([(qn, "rc"), (qp, "rc"), (kn, "rc"), (kp, "r")] + [(a, "r") for a in tabs] + [(g, "f") for g in gains]
           + [(dq, "rc"), (dk, "rc")])
    hw = heads * LANE
    outs = [((t, hw), BF16, "rc"), ((t, hw), BF16, "rc"), ((t, hw), BF16, "rc"), ((t, LANE), F32, "r")] \
        + [((1, LANE), F32, "f")] * 4
    return ew(fn, ins, outs, gr=t // _pick(t, HEAD_ROW_TILE), gc=heads, name="mla_prep_bwd")


def _chunk_mask(row0, col0, shape, rows_are_queries):
    r = jnp.right_shift(row0 + lax.broadcasted_iota(jnp.int32, shape, 0), LOG2_CHUNK)
    c = jnp.right_shift(col0 + lax.broadcasted_iota(jnp.int32, shape, 1), LOG2_CHUNK)
    return (c <= r) if rows_are_queries else (r <= c)


def _block_pairs(nq, queries_outer):
    if queries_outer:
        pairs = [(i, j) for i in range(nq) for j in range(i + 1)]
    else:
        pairs = [(i, j) for j in range(nq) for i in range(j, nq)]
    return jnp.asarray([p[0] for p in pairs], jnp.int32), jnp.asarray([p[1] for p in pairs], jnp.int32)


def attn_fwd(q, k, vt, heads):
    t = q.shape[0]
    tq = _pick(t, ATT_TILE)
    qi, kj = _block_pairs(t // tq, True)
    scale = (NOPE + ROPE) ** -0.5
    scale2 = scale * LOG2_E

    def body(qi_ref, kj_ref, q_ref, k_ref, vt_ref, o_ref, lse_ref, m_s, l_s, acc):
        p = pl.program_id(1)
        i, j = qi_ref[p], kj_ref[p]

        @pl.when(j == 0)
        def _():
            m_s[...] = jnp.full_like(m_s, -jnp.inf)
            l_s[...] = jnp.zeros_like(l_s)
            acc[...] = jnp.zeros_like(acc)

        def step(diagonal):
            st = _bdot(k_ref[...], q_ref[...], NT)
            if diagonal:
                st = jnp.where(_chunk_mask(0, 0, (tq, tq), False), st, -jnp.inf)
            m_new = jnp.maximum(m_s[...], jnp.max(st, axis=0, keepdims=True))
            alpha = jnp.exp2((m_s[...] - m_new) * scale2)
            pt = jnp.exp2((st - m_new) * scale2)
            l_s[...] = alpha * l_s[...] + jnp.sum(pt, axis=0, keepdims=True)
            acc[...] = alpha * acc[...] + _bdot(vt_ref[...], pt, NN)
            m_s[...] = m_new

        pl.when(j < i)(functools.partial(step, False))

        @pl.when(j == i)
        def _():
            step(True)
            o_ref[...] = jnp.transpose(acc[...] / l_s[...])
            lse_ref[...] = m_s[...] * scale + jnp.log(l_s[...])

    return pl.pallas_call(
        body,
        grid_spec=pltpu.PrefetchScalarGridSpec(
            num_scalar_prefetch=2, grid=(heads, qi.shape[0]),
            in_specs=[pl.BlockSpec((tq, 2 * LANE), lambda h, p, qi_, kj_: (qi_[p], h)),
                      pl.BlockSpec((tq, 2 * LANE), lambda h, p, qi_, kj_: (kj_[p], h)),
                      pl.BlockSpec((VHEAD, tq), lambda h, p, qi_, kj_: (h, kj_[p]))],
            out_specs=[pl.BlockSpec((tq, VHEAD), lambda h, p, qi_, kj_: (qi_[p], h)),
                       pl.BlockSpec((None, 1, tq), lambda h, p, qi_, kj_: (h, 0, qi_[p]))],
            scratch_shapes=[pltpu.VMEM((1, tq), F32), pltpu.VMEM((1, tq), F32), pltpu.VMEM((VHEAD, tq), F32)]),
        out_shape=[jax.ShapeDtypeStruct((t, heads * VHEAD), F32), jax.ShapeDtypeStruct((heads, 1, t), F32)],
        compiler_params=_params("parallel", "arbitrary"), name="mla_attn_fwd")(qi, kj, q, k, vt)


def attn_bwd(q, k, v, o, do, lse_row, heads):
    t = q.shape[0]
    tq = _pick(t, ATT_TILE)
    qi, kj = _block_pairs(t // tq, False)
    scale = (NOPE + ROPE) ** -0.5
    scale2 = scale * LOG2_E

    def body(qi_ref, kj_ref, q_ref, k_ref, v_ref, o_ref, do_ref, lse_ref, dq_ref, dk_ref, dv_ref):
        p = pl.program_id(1)
        i, j = qi_ref[p], kj_ref[p]

        @pl.when(p == 0)
        def _():
            dq_ref[...] = jnp.zeros_like(dq_ref)

        @pl.when(i == j)
        def _():
            dk_ref[...] = jnp.zeros_like(dk_ref)
            dv_ref[...] = jnp.zeros_like(dv_ref)

        def step(diagonal):
            do_i = do_ref[...]
            prod = do_i * o_ref[...]
            hi = prod.astype(BF16)
            mid = (prod - hi.astype(F32)).astype(BF16)
            lo = (prod - hi.astype(F32) - mid.astype(F32)).astype(BF16)
            ones = jnp.ones((8, VHEAD), BF16)
            delta = (_bdot(ones, hi, NT) + _bdot(ones, mid, NT) + _bdot(ones, lo, NT))[0:1, :]
            st = _bdot(k_ref[...], q_ref[...], NT)
            pt = jnp.exp2(st * scale2 - lse_ref[...] * LOG2_E)
            if diagonal:
                pt = jnp.where(_chunk_mask(0, 0, (tq, tq), False), pt, 0.0)
            dv_ref[...] += _bdot(pt, do_i, NN)
            dpt = _bdot(v_ref[...], do_i, NT)
            dst = pt * (dpt - delta) * scale
            dk_ref[...] += _bdot(dst, q_ref[...], NN)
            rows = pl.ds(pl.multiple_of(i * tq, tq), tq)
            dq_ref[rows, :] += _bdot(dst, k_ref[...], TN)

        pl.when(i > j)(functools.partial(step, False))
        pl.when(i == j)(functools.partial(step, True))

    qmap = lambda h, p, qi_, kj_: (qi_[p], h)
    kmap = lambda h, p, qi_, kj_: (kj_[p], h)
    return pl.pallas_call(
        body,
        grid_spec=pltpu.PrefetchScalarGridSpec(
            num_scalar_prefetch=2, grid=(heads, qi.shape[0]),
            in_specs=[pl.BlockSpec((tq, 2 * LANE), qmap), pl.BlockSpec((tq, 2 * LANE), kmap),
                      pl.BlockSpec((tq, VHEAD), kmap), pl.BlockSpec((tq, VHEAD), qmap), pl.BlockSpec((tq, VHEAD), qmap),
                      pl.BlockSpec((None, 1, tq), lambda h, p, qi_, kj_: (h, 0, qi_[p]))],
            out_specs=[pl.BlockSpec((t, 2 * LANE), lambda h, p, qi_, kj_: (0, h)),
                       pl.BlockSpec((tq, 2 * LANE), kmap), pl.BlockSpec((tq, VHEAD), kmap)]),
        out_shape=[jax.ShapeDtypeStruct((t, heads * 2 * LANE), F32), jax.ShapeDtypeStruct((t, heads * 2 * LANE), F32),
                   jax.ShapeDtypeStruct((t, heads * VHEAD), F32)],
        compiler_params=_params("parallel", "arbitrary"), name="mla_attn_bwd")(qi, kj, q, k, v, o, do, lse_row)


def _shift_down(x, s):
    if s == 0:
        return x
    rows = lax.broadcasted_iota(jnp.int32, x.shape, 0)
    return jnp.where(rows >= s, pltpu.roll(x, s, 0), 0.0)


def _shift_up(x, s):
    if s == 0:
        return x
    t = x.shape[0]
    rows = lax.broadcasted_iota(jnp.int32, x.shape, 0)
    return jnp.where(rows < t - s, pltpu.roll(x, t - s, 0), 0.0)


def _conv(x, w_ref, width):
    return sum(_shift_down(x, width - 1 - j) * w_ref[j:j + 1, :] for j in range(width))


def _conv_bwd(x, dpre, w_ref, dw_ref, width):
    dx = sum(_shift_up(dpre, width - 1 - j) * w_ref[j:j + 1, :] for j in range(width))
    for j in range(width):
        dw_ref[j:j + 1, :] = jnp.sum(dpre * _shift_down(x, width - 1 - j), axis=0, keepdims=True)
    return dx


def _dsilu(z):
    s = _sigmoid(z)
    return s * (1.0 + z * (1.0 - s))


def conv_qk_fwd(x, w, colscale):
    t, c = x.shape
    tc = _pick(c, 256)

    def body(x_ref, w_ref, s_ref, o_ref):
        o_ref[...] = (_silu(_conv(x_ref[...], w_ref, MLSTM_CONV)) * s_ref[...]).astype(o_ref.dtype)

    return pl.pallas_call(
        body, grid=(c // tc,),
        in_specs=[pl.BlockSpec((t, tc), lambda j: (0, j)), pl.BlockSpec((MLSTM_CONV, tc), lambda j: (0, j)),
                  pl.BlockSpec((1, tc), lambda j: (0, j))],
        out_specs=pl.BlockSpec((t, tc), lambda j: (0, j)), out_shape=jax.ShapeDtypeStruct((t, c), BF16),
        compiler_params=_params("parallel"), name="conv_qk_fwd")(x, w, colscale)


def conv_qk_bwd(x, w, colscale, dq, dk):
    t, c = x.shape
    tc = _pick(c // 2, 256)
    half = (c // 2) // tc

    def body(x_ref, w_ref, s_ref, dq_ref, dk_ref, dx_ref, dw_ref):
        j = pl.program_id(0)
        x_ = x_ref[...]
        dy = jnp.where(j < half, dq_ref[...], dk_ref[...])
        dpre = dy * s_ref[...] * _dsilu(_conv(x_, w_ref, MLSTM_CONV))
        dx_ref[...] = _conv_bwd(x_, dpre, w_ref, dw_ref, MLSTM_CONV).astype(dx_ref.dtype)

    return pl.pallas_call(
        body, grid=(c // tc,),
        in_specs=[pl.BlockSpec((t, tc), lambda j: (0, j)), pl.BlockSpec((MLSTM_CONV, tc), lambda j: (0, j)),
                  pl.BlockSpec((1, tc), lambda j: (0, j)),
                  pl.BlockSpec((t, tc), lambda j: (0, jnp.minimum(j, half - 1))),
                  pl.BlockSpec((t, tc), lambda j: (0, jnp.maximum(j - half, 0)))],
        out_specs=[pl.BlockSpec((t, tc), lambda j: (0, j)), pl.BlockSpec((MLSTM_CONV, tc), lambda j: (0, j))],
        out_shape=[jax.ShapeDtypeStruct((t, c), BF16), jax.ShapeDtypeStruct((MLSTM_CONV, c), F32)],
        compiler_params=_params("parallel"), name="conv_qk_bwd")(x, w, colscale, dq, dk)


def _mlstm_chunk(q, k, v, i_col, i_row, f_col, f_row, c_mat, n_vec, m):
    shape = (CHUNK, CHUNK)
    r = lax.broadcasted_iota(jnp.int32, shape, 0)
    c = lax.broadcasted_iota(jnp.int32, shape, 1)
    tril = c <= r
    lf_col, lf_row = _log_sigmoid(f_col), _log_sigmoid(f_row)
    bc_col = jnp.sum(jnp.where(tril, lf_row, 0.0), axis=1, keepdims=True)
    bc_row = jnp.sum(jnp.where(r <= c, lf_col, 0.0), axis=0, keepdims=True)
    logw = jnp.where(tril, bc_col - bc_row + i_row, -jnp.inf)
    inter = bc_col + m
    m_t = lax.stop_gradient(jnp.maximum(inter, jnp.max(logw, axis=1, keepdims=True)))
    w_intra = jnp.exp(logw - m_t)
    w_inter = jnp.exp(inter - m_t)
    sc = _dnt(q, k) * w_intra
    num = w_inter * _dnn(q, c_mat) + _dnn(sc, v)
    den = w_inter * jnp.sum(q * n_vec, axis=1, keepdims=True) + jnp.sum(sc, axis=1, keepdims=True)
    h = num / jnp.maximum(jnp.abs(den), jnp.exp(-m_t))
    b_last = jnp.sum(lf_row, axis=1, keepdims=True)
    m_new = lax.stop_gradient(jnp.maximum(b_last + m, jnp.max(b_last - bc_row + i_row, axis=1, keepdims=True)))
    decay = jnp.exp(b_last + m - m_new)
    uk = jnp.exp(b_last - bc_col + i_col - m_new) * k
    return h, decay * c_mat + _dtn(uk, v), decay * n_vec + jnp.sum(uk, axis=0, keepdims=True), m_new


def _mlstm_specs(heads, rev, nc):
    ci = (lambda c: nc - 1 - c) if rev else (lambda c: c)
    return dict(
        q=pl.BlockSpec((CHUNK, MLSTM_DK), lambda h, c: (ci(c), h)),
        k=pl.BlockSpec((CHUNK, MLSTM_DK), lambda h, c: (ci(c), heads + h)),
        v=pl.BlockSpec((CHUNK, MLSTM_DV), lambda h, c: (ci(c), h)),
        gc=pl.BlockSpec((None, None, CHUNK, 2), lambda h, c: (h, ci(c), 0, 0)),
        gr=pl.BlockSpec((None, None, 2, CHUNK), lambda h, c: (h, ci(c), 0, 0)),
        b=pl.BlockSpec((None, 1, 2), lambda h, c: (h, 0, 0)),
        cm=pl.BlockSpec((None, None, MLSTM_DK, MLSTM_DV), lambda h, c: (h, ci(c), 0, 0)),
        vec=pl.BlockSpec((None, None, 1, LANE), lambda h, c: (h, ci(c), 0, 0)),
    )


def _gates(gc_ref, gr_ref, b_ref):
    bi, bf = b_ref[:, 0:1], b_ref[:, 1:2]
    return gc_ref[:, 0:1] + bi, gr_ref[0:1, :] + bi, gc_ref[:, 1:2] + bf, gr_ref[1:2, :] + bf


def mlstm_fwd(qk, v, gcol, grow, bias, heads):
    t = qk.shape[0]
    nc = t // CHUNK
    sp = _mlstm_specs(heads, False, nc)

    def body(q_ref, k_ref, v_ref, gc_ref, gr_ref, b_ref, h_ref, c_ref, n_ref, m_ref, c_s, n_s, m_s):
        @pl.when(pl.program_id(1) == 0)
        def _():
            c_s[...] = jnp.zeros_like(c_s)
            n_s[...] = jnp.zeros_like(n_s)
            m_s[...] = jnp.zeros_like(m_s)

        c_ref[...] = c_s[...]
        n_ref[...] = n_s[...]
        m_ref[...] = m_s[...]
        q, k, v_ = _f32(q_ref[...], k_ref[...], v_ref[...])
        h, c_new, n_new, m_new = _mlstm_chunk(q, k, v_, *_gates(gc_ref, gr_ref, b_ref), c_s[...], n_s[...], m_s[:, 0:1])
        h_ref[...] = h
        c_s[...] = c_new
        n_s[...] = n_new
        m_s[...] = jnp.broadcast_to(m_new, m_s.shape)

    return pl.pallas_call(
        body, grid=(heads, nc),
        in_specs=[sp["q"], sp["k"], sp["v"], sp["gc"], sp["gr"], sp["b"]],
        out_specs=[sp["v"], sp["cm"], sp["vec"], sp["vec"]],
        out_shape=[jax.ShapeDtypeStruct((t, heads * MLSTM_DV), F32),
                   jax.ShapeDtypeStruct((heads, nc, MLSTM_DK, MLSTM_DV), F32),
                   jax.ShapeDtypeStruct((heads, nc, 1, LANE), F32), jax.ShapeDtypeStruct((heads, nc, 1, LANE), F32)],
        scratch_shapes=[pltpu.VMEM((MLSTM_DK, MLSTM_DV), F32), pltpu.VMEM((1, LANE), F32), pltpu.VMEM((1, LANE), F32)],
        compiler_params=_params("parallel", "arbitrary"), name="mlstm_fwd")(qk, qk, v, gcol, grow, bias)


def mlstm_bwd(qk, v, gcol, grow, bias, c_all, n_all, m_all, dh, heads):
    t = qk.shape[0]
    nc = t // CHUNK
    sp = _mlstm_specs(heads, True, nc)

    def body(q_ref, k_ref, v_ref, gc_ref, gr_ref, b_ref, c_ref, n_ref, m_ref, dh_ref,
             dq_ref, dk_ref, dv_ref, dgc_ref, dgr_ref, dc_s, dn_s):
        @pl.when(pl.program_id(1) == 0)
        def _():
            dc_s[...] = jnp.zeros_like(dc_s)
            dn_s[...] = jnp.zeros_like(dn_s)

        q, k, v_ = _f32(q_ref[...], k_ref[...], v_ref[...])
        m = m_ref[:, 0:1]
        _, pull = jax.vjp(lambda *a: _mlstm_chunk(*a, m)[:3], q, k, v_, *_gates(gc_ref, gr_ref, b_ref),
                          c_ref[...], n_ref[...])
        dq, dk, dv, di_col, di_row, df_col, df_row, dc, dn = pull((dh_ref[...], dc_s[...], dn_s[...]))
        dq_ref[...] = dq
        dk_ref[...] = dk
        dv_ref[...] = dv.astype(dv_ref.dtype)
        dgc_ref[:, 0:1] = di_col
        dgc_ref[:, 1:2] = df_col
        dgr_ref[0:1, :] = di_row
        dgr_ref[1:2, :] = df_row
        dc_s[...] = dc
        dn_s[...] = dn

    qspec = pl.BlockSpec((CHUNK, MLSTM_DK), lambda h, c: (nc - 1 - c, h))
    return pl.pallas_call(
        body, grid=(heads, nc),
        in_specs=[sp["q"], sp["k"], sp["v"], sp["gc"], sp["gr"], sp["b"], sp["cm"], sp["vec"], sp["vec"], sp["v"]],
        out_specs=[qspec, qspec, sp["v"], sp["gc"], sp["gr"]],
        out_shape=[jax.ShapeDtypeStruct((t, heads * MLSTM_DK), F32), jax.ShapeDtypeStruct((t, heads * MLSTM_DK), F32),
                   jax.ShapeDtypeStruct((t, heads * MLSTM_DV), BF16),
                   jax.ShapeDtypeStruct(gcol.shape, F32), jax.ShapeDtypeStruct(grow.shape, F32)],
        scratch_shapes=[pltpu.VMEM((MLSTM_DK, MLSTM_DV), F32), pltpu.VMEM((1, LANE), F32)],
        compiler_params=_params("parallel", "arbitrary"),
        name="mlstm_bwd")(qk, qk, v, gcol, grow, bias, c_all, n_all, m_all, dh)


def _hnorm_gate(h, zo, g):
    return _rms(h, g, MLSTM_DV) * _sigmoid(zo)


def _cross_core(q, k, v, g_q, g_k, heads):
    scale = CROSS_DH ** -0.5
    outs = []
    for h in range(heads):
        s = _dnt(_rms(q[h], g_q, CROSS_DH), _rms(k[h], g_k, CROSS_DH)) * scale
        p = jnp.exp(s - lax.stop_gradient(jnp.max(s, axis=1, keepdims=True)))
        p = p / jnp.sum(p, axis=1, keepdims=True)
        outs.append(_dnn(p, v[h]))
    return jnp.concatenate(outs, axis=1)


def _split_heads(ref, heads):
    return [ref[:, h * CROSS_DH:(h + 1) * CROSS_DH].astype(F32) for h in range(heads)]


def cross_fwd(q, k, v, g_q, g_k, heads):
    t = q.shape[0]
    tm = _pick(t, ATT_TILE)
    full = lambda a: pl.BlockSpec(a.shape, lambda i: (0, 0))

    def body(q_ref, k_ref, v_ref, gq_ref, gk_ref, o_ref):
        o_ref[...] = _cross_core(_split_heads(q_ref, heads), _split_heads(k_ref, heads), _split_heads(v_ref, heads),
                                 gq_ref[...], gk_ref[...], heads).astype(o_ref.dtype)

    return pl.pallas_call(
        body, grid=(t // tm,), in_specs=[pl.BlockSpec((tm, q.shape[1]), lambda i: (i, 0)), full(k), full(v), full(g_q), full(g_k)],
        out_specs=pl.BlockSpec((tm, q.shape[1]), lambda i: (i, 0)), out_shape=jax.ShapeDtypeStruct(q.shape, BF16),
        compiler_params=_params("parallel"), name="cross_fwd")(q, k, v, g_q, g_k)


def cross_bwd(q, k, v, g_q, g_k, do, heads):
    t, w = q.shape
    tm = _pick(t, ATT_TILE)
    full = lambda a: pl.BlockSpec(a.shape, lambda i: (0, 0))

    def body(q_ref, k_ref, v_ref, gq_ref, gk_ref, do_ref, dq_ref, dk_ref, dv_ref, dgq_ref, dgk_ref):
        qs, ks, vs = _split_heads(q_ref, heads), _split_heads(k_ref, heads), _split_heads(v_ref, heads)
        _, pull = jax.vjp(lambda a, b, c, d, e: _cross_core(a, b, c, d, e, heads), qs, ks, vs, gq_ref[...], gk_ref[...])
        dqs, dks, dvs, dgq, dgk = pull(do_ref[...])
        first = pl.program_id(0) == 0
        for h in range(heads):
            cols = slice(h * CROSS_DH, (h + 1) * CROSS_DH)
            dq_ref[:, cols] = dqs[h].astype(dq_ref.dtype)
            _store(dk_ref.at[:, cols], dks[h], first)
            _store(dv_ref.at[:, cols], dvs[h], first)
        _store(dgq_ref, dgq, first)
        _store(dgk_ref, dgk, first)

    row = pl.BlockSpec((tm, w), lambda i: (i, 0))
    return pl.pallas_call(
        body, grid=(t // tm,), in_specs=[row, full(k), full(v), full(g_q), full(g_k), row],
        out_specs=[row, full(k), full(v), full(g_q), full(g_k)],
        out_shape=[jax.ShapeDtypeStruct(q.shape, BF16), jax.ShapeDtypeStruct(k.shape, F32), jax.ShapeDtypeStruct(v.shape, F32),
                   jax.ShapeDtypeStruct(g_q.shape, F32), jax.ShapeDtypeStruct(g_k.shape, F32)],
        compiler_params=_params("arbitrary"), name="cross_bwd")(q, k, v, g_q, g_k, do)


def ffn_glu_fwd(hg, hv, wg, wv, bg, bv):
    t, f = hg.shape
    tc = _pick(f, LANE)
    col = pl.BlockSpec((t, tc), lambda j: (0, j))
    tap = pl.BlockSpec((FFN_CONV, tc), lambda j: (0, j))
    one = pl.BlockSpec((1, tc), lambda j: (0, j))

    def body(hg_ref, hv_ref, wg_ref, wv_ref, bg_ref, bv_ref, o_ref):
        gate = _conv(hg_ref[...], wg_ref, FFN_CONV) + bg_ref[...]
        val = _conv(hv_ref[...], wv_ref, FFN_CONV) + bv_ref[...]
        o_ref[...] = (_silu(gate) * val).astype(o_ref.dtype)

    return pl.pallas_call(body, grid=(f // tc,), in_specs=[col, col, tap, tap, one, one], out_specs=col,
                          out_shape=jax.ShapeDtypeStruct((t, f), BF16), compiler_params=_params("parallel"),
                          name="ffn_glu_fwd")(hg, hv, wg, wv, bg, bv)


def ffn_glu_bwd(hg, hv, wg, wv, bg, bv, dact):
    t, f = hg.shape
    tc = _pick(f, LANE)
    col = pl.BlockSpec((t, tc), lambda j: (0, j))
    tap = pl.BlockSpec((FFN_CONV, tc), lambda j: (0, j))
    one = pl.BlockSpec((1, tc), lambda j: (0, j))

    def body(hg_ref, hv_ref, wg_ref, wv_ref, bg_ref, bv_ref, da_ref, dhg_ref, dhv_ref, dwg_ref, dwv_ref, dbg_ref, dbv_ref):
        xg, xv, da = hg_ref[...], hv_ref[...], da_ref[...]
        gate = _conv(xg, wg_ref, FFN_CONV) + bg_ref[...]
        val = _conv(xv, wv_ref, FFN_CONV) + bv_ref[...]
        dgate = da * val * _dsilu(gate)
        dval = da * _silu(gate)
        dbg_ref[...] = jnp.sum(dgate, axis=0, keepdims=True)
        dbv_ref[...] = jnp.sum(dval, axis=0, keepdims=True)
        dhg_ref[...] = _conv_bwd(xg, dgate, wg_ref, dwg_ref, FFN_CONV).astype(dhg_ref.dtype)
        dhv_ref[...] = _conv_bwd(xv, dval, wv_ref, dwv_ref, FFN_CONV).astype(dhv_ref.dtype)

    return pl.pallas_call(
        body, grid=(f // tc,), in_specs=[col, col, tap, tap, one, one, col], out_specs=[col, col, tap, tap, one, one],
        out_shape=[jax.ShapeDtypeStruct((t, f), BF16), jax.ShapeDtypeStruct((t, f), BF16),
                   jax.ShapeDtypeStruct((FFN_CONV, f), F32), jax.ShapeDtypeStruct((FFN_CONV, f), F32),
                   jax.ShapeDtypeStruct((1, f), F32), jax.ShapeDtypeStruct((1, f), F32)],
        compiler_params=_params("parallel"), name="ffn_glu_bwd")(hg, hv, wg, wv, bg, bv, dact)


def _adamw(g, w, m, v):
    m = ADAM_B1 * m + (1.0 - ADAM_B1) * g
    v = ADAM_B2 * v + (1.0 - ADAM_B2) * (g * g)
    m_hat = m / (1.0 - ADAM_B1 ** ADAM_STEP)
    v_hat = v / (1.0 - ADAM_B2 ** ADAM_STEP)
    return -ADAM_LR * (m_hat / (jnp.sqrt(v_hat) + ADAM_EPS) + ADAM_WD * w), m, v


def adamw(g, w, m, v, name):
    r, c = g.shape
    tr = r
    for cand in (256, 128, 64, 32, 16, 8):
        if r % cand == 0 and cand * c * 4 <= (1 << 21):
            tr = cand
            break
    return ew(_adamw, [(g, "r"), (w, "r"), (m, "r"), (v, "r")], [((r, c), F32, "r")] * 3, gr=r // tr, name=name)


ANY = pl.BlockSpec(memory_space=pl.ANY)


def _place():
    x, y, c = lax.axis_index("x"), lax.axis_index("y"), lax.axis_index("c")
    return x, y, c, [(1 - x, y), (x, 1 - y), (1 - x, 1 - y)]


def _rcopy(src, dst, send, recv, k, to):
    return pltpu.make_async_remote_copy(src_ref=src, dst_ref=dst, send_sem=send.at[k], recv_sem=recv.at[k],
                                        device_id=to, device_id_type=MESH)


def gather_shards(bigs, smalls):
    nb, na = len(bigs), len(bigs) + len(smalls)
    arrays = list(bigs) + list(smalls)

    def body(*refs):
        ins, outs = refs[:na], refs[na:2 * na]
        send, recv = refs[2 * na:]
        x, y, c, chips = _place()
        me, sib = 2 * x + y, (x, y, 1 - c)

        def half(ref, a, which):
            rows = arrays[a].shape[0] // 2
            return ref.at[pl.ds(which * rows, rows)]

        started = []
        for a in range(na):
            for j, (cx, cy) in enumerate(chips):
                if a < nb:
                    cp = _rcopy(half(ins[a], a, c), half(outs[a].at[me], a, c), send, recv, 6 * a + j, (cx, cy, c))
                else:
                    cp = _rcopy(ins[a], outs[a].at[me], send, recv, 6 * nb + 3 * (a - nb) + j, (cx, cy, c))
                cp.start()
                started.append(cp)
        for a in range(nb):
            for j, (cx, cy) in enumerate(chips):
                landed = half(outs[a].at[2 * cx + cy], a, c)
                _rcopy(landed, landed, send, recv, 6 * a + j, (cx, cy, c)).wait_recv()
                cp = _rcopy(landed, landed, send, recv, 6 * a + 3 + j, sib)
                cp.start()
                started.append(cp)
        for a in range(na):
            for j, (cx, cy) in enumerate(chips):
                if a < nb:
                    dst = half(outs[a].at[2 * cx + cy], a, 1 - c)
                    _rcopy(dst, dst, send, recv, 6 * a + 3 + j, sib).wait_recv()
                else:
                    dst = outs[a].at[2 * cx + cy]
                    _rcopy(dst, dst, send, recv, 6 * nb + 3 * (a - nb) + j, (cx, cy, c)).wait_recv()
        for cp in started:
            cp.wait_send()

    n_sem = 6 * nb + 3 * (na - nb)
    gathered = pl.pallas_call(
        body, in_specs=[ANY] * na, out_specs=[ANY] * na,
        out_shape=[jax.ShapeDtypeStruct((N_CHIPS,) + a.shape, a.dtype) for a in arrays],
        scratch_shapes=[pltpu.SemaphoreType.DMA((n_sem,)), pltpu.SemaphoreType.DMA((n_sem,))],
        name="gather_shards")(*arrays)
    chip = 2 * lax.axis_index("x") + lax.axis_index("y")
    return [lax.dynamic_update_slice(g, a[None], (chip, 0, 0)) for g, a in zip(gathered, arrays)]


def sibling_halves(grads):
    na = len(grads)

    def body(*refs):
        ins, outs = refs[:na], refs[na:2 * na]
        send, recv = refs[2 * na:]
        x, y, c, _ = _place()
        cps = []
        for a in range(na):
            rows = grads[a].shape[1] // 2
            cp = _rcopy(ins[a].at[:, pl.ds((1 - c) * rows, rows)], outs[a], send, recv, a, (x, y, 1 - c))
            cp.start()
            cps.append(cp)
        for cp in cps:
            cp.wait()

    return pl.pallas_call(
        body, in_specs=[ANY] * na, out_specs=[ANY] * na,
        out_shape=[jax.ShapeDtypeStruct((g.shape[0], g.shape[1] // 2, g.shape[2]), g.dtype) for g in grads],
        scratch_shapes=[pltpu.SemaphoreType.DMA((na,)), pltpu.SemaphoreType.DMA((na,))], name="sibling_halves")(*grads)


def scatter_partials(parts):
    na = len(parts)

    def body(*refs):
        ins, outs = refs[:na], refs[na:2 * na]
        send, recv = refs[2 * na:]
        _, _, c, chips = _place()
        cps = []
        for a in range(na):
            for j, (cx, cy) in enumerate(chips):
                cp = _rcopy(ins[a].at[2 * cx + cy], outs[a].at[j], send, recv, 3 * a + j, (cx, cy, c))
                cp.start()
                cps.append(cp)
        for cp in cps:
            cp.wait()

    return pl.pallas_call(
        body, in_specs=[ANY] * na, out_specs=[ANY] * na,
        out_shape=[jax.ShapeDtypeStruct((3,) + p.shape[1:], p.dtype) for p in parts],
        scratch_shapes=[pltpu.SemaphoreType.DMA((3 * na,)), pltpu.SemaphoreType.DMA((3 * na,))], name="scatter_partials")(*parts)


def join_halves(halves):
    na = len(halves)

    def body(*refs):
        ins, outs = refs[:na], refs[na:2 * na]
        send, recv = refs[2 * na:]
        x, y, c, _ = _place()
        cps = []
        for a in range(na):
            cp = _rcopy(ins[a].at[c], outs[a].at[c], send, recv, a, (x, y, 1 - c))
            cp.start()
            cps.append(cp)
        for a in range(na):
            dst = outs[a].at[1 - c]
            _rcopy(dst, dst, send, recv, a, (x, y, 1 - c)).wait_recv()
        for cp in cps:
            cp.wait_send()

    return pl.pallas_call(
        body, in_specs=[ANY] * na, out_specs=[ANY] * na,
        out_shape=[jax.ShapeDtypeStruct(h.shape, h.dtype) for h in halves],
        input_output_aliases={a: a for a in range(na)},
        scratch_shapes=[pltpu.SemaphoreType.DMA((na,)), pltpu.SemaphoreType.DMA((na,))],
        name="join_halves")(*halves)


def allreduce_small(vec):
    r = vec.shape[0]

    def body(x_ref, sum_ref, all_ref, send, recv):
        x, y, c, _ = _place()
        me = 4 * x + 2 * y + c
        all_ref[me] = x_ref[...]
        cps, peers = [], []
        for mask in range(1, 8):
            px = 1 - x if mask & 4 else x
            py = 1 - y if mask & 2 else y
            pc = 1 - c if mask & 1 else c
            peers.append(4 * px + 2 * py + pc)
            cp = _rcopy(x_ref, all_ref.at[me], send, recv, mask - 1, (px, py, pc))
            cp.start()
            cps.append(cp)
        for k, cp in enumerate(cps):
            _rcopy(x_ref, all_ref.at[peers[k]], send, recv, k, (x, y, c)).wait_recv()
        for cp in cps:
            cp.wait_send()
        total = all_ref[0]
        for d in range(1, 8):
            total = total + all_ref[d]
        sum_ref[...] = total

    vm = pl.BlockSpec(memory_space=pltpu.VMEM)
    return pl.pallas_call(
        body, in_specs=[vm], out_specs=vm, out_shape=jax.ShapeDtypeStruct((r, LANE), F32),
        scratch_shapes=[pltpu.VMEM((8, r, LANE), F32), pltpu.SemaphoreType.DMA((7,)), pltpu.SemaphoreType.DMA((7,))],
        compiler_params=pltpu.CompilerParams(vmem_limit_bytes=VMEM_LIMIT), name="allreduce_small")(vec)


def _row_tile(rows):
    for cand in (256, 128, 64, 32, 16):
        if rows % cand == 0:
            return cand
    return rows


def add_sibling(grad, recv, c_idx):
    _, rows, cols = grad.shape
    hr = rows // 2
    tr = _row_tile(hr)
    nb = hr // tr

    def body(c_ref, g_ref, r_ref, o_ref):
        o_ref[...] = (g_ref[...].astype(F32) + r_ref[...].astype(F32)).astype(o_ref.dtype)

    return pl.pallas_call(
        body,
        grid_spec=pltpu.PrefetchScalarGridSpec(
            num_scalar_prefetch=1, grid=(N_CHIPS, nb),
            in_specs=[pl.BlockSpec((None, tr, cols), lambda k, r, c_ref: (k, c_ref[0] * nb + r, 0)),
                      pl.BlockSpec((None, tr, cols), lambda k, r, c_ref: (k, r, 0))],
            out_specs=pl.BlockSpec((None, tr, cols), lambda k, r, c_ref: (k, r, 0))),
        out_shape=jax.ShapeDtypeStruct((N_CHIPS, hr, cols), BF16),
        compiler_params=_params("parallel", "parallel"), name="add_sibling")(c_idx, grad, recv)


def sum_chips(part, others, place_idx):
    _, hr, cols = part.shape
    tr = _row_tile(hr)

    def body(k_ref, p_ref, o0_ref, o1_ref, o2_ref, out_ref):
        out_ref[...] = ((p_ref[...].astype(F32) + o0_ref[...].astype(F32)) + o1_ref[...].astype(F32)) + o2_ref[...].astype(F32)

    other = lambda j: pl.BlockSpec((None, tr, cols), lambda r, k_ref: (j, r, 0))
    return pl.pallas_call(
        body,
        grid_spec=pltpu.PrefetchScalarGridSpec(
            num_scalar_prefetch=1, grid=(hr // tr,),
            in_specs=[pl.BlockSpec((None, tr, cols), lambda r, k_ref: (k_ref[0], r, 0)), other(0), other(1), other(2)],
            out_specs=pl.BlockSpec((None, tr, cols), lambda r, k_ref: (k_ref[1], r, 0))),
        out_shape=jax.ShapeDtypeStruct((2, hr, cols), F32),
        compiler_params=_params("parallel"), name="sum_chips")(place_idx, part, others, others, others)


def _pad_lanes(a, width=LANE):
    return jnp.pad(a, ((0, 0), (0, width - a.shape[1])))


def _cols_from_shards(g):
    return jnp.transpose(g, (1, 0, 2)).reshape(g.shape[1], -1)


def _cols_to_shards(w):
    k, n4 = w.shape
    return jnp.transpose(w.reshape(k, N_CHIPS, n4 // N_CHIPS), (1, 0, 2))


def kernel(x, mem, positions, g_mix, w_in, g_qa, w_qb, g_kva, w_kvb, g_qn_nope, g_qn_pe, g_kn_nope, g_kn_pe, conv_qk, b_if, g_hnorm, p_a, p_b, w_out, g_cross, g_mem, wq_c, wk_c, wv_c, g_cq, g_ck, wo_c, g_ffn, w_up, conv_ffn, b_conv_ffn, w_down, loss_target, m_g_mix, m_w_in, m_g_qa, m_w_qb, m_g_kva, m_w_kvb, m_g_qn_nope, m_g_qn_pe, m_g_kn_nope, m_g_kn_pe, m_conv_qk, m_b_if, m_g_hnorm, m_p_a, m_p_b, m_w_out, m_g_cross, m_g_mem, m_wq_c, m_wk_c, m_wv_c, m_g_cq, m_g_ck, m_wo_c, m_g_ffn, m_w_up, m_conv_ffn, m_b_conv_ffn, m_w_down, v_g_mix, v_w_in, v_g_qa, v_w_qb, v_g_kva, v_w_kvb, v_g_qn_nope, v_g_qn_pe, v_g_kn_nope, v_g_kn_pe, v_conv_qk, v_b_if, v_g_hnorm, v_p_a, v_p_b, v_w_out, v_g_cross, v_g_mem, v_wq_c, v_wk_c, v_wv_c, v_g_cq, v_g_ck, v_wo_c, v_g_ffn, v_w_up, v_conv_ffn, v_b_conv_ffn, v_w_down):
    names = ["g_mix", "w_in", "g_qa", "w_qb", "g_kva", "w_kvb", "g_qn_nope", "g_qn_pe", "g_kn_nope", "g_kn_pe", "conv_qk",
             "b_if", "g_hnorm", "p_a", "p_b", "w_out", "g_cross", "g_mem", "wq_c", "wk_c", "wv_c", "g_cq", "g_ck", "wo_c",
             "g_ffn", "w_up", "conv_ffn", "b_conv_ffn", "w_down"]
    env = locals()
    wts = {n: env[n] for n in names}
    mom = {n: env["m_" + n] for n in names}
    var = {n: env["v_" + n] for n in names}

    xi, yi, ci = lax.axis_index("x"), lax.axis_index("y"), lax.axis_index("c")
    chip = 2 * xi + yi
    place_arr = jnp.stack([chip, ci]).astype(jnp.int32)
    c_arr = jnp.reshape(ci, (1,)).astype(jnp.int32)

    x2d, tgt, mem2d = x[0], loss_target[0], mem[0]
    t, d = x2d.shape
    mla_h = w_qb.shape[2] * N_CHIPS // (NOPE + ROPE)
    ml_h = b_if.shape[1] // 2
    cr_h = wq_c.shape[2] // CROSS_DH
    f_dim = w_down.shape[1] * N_CHIPS
    q_rank, kv_rank = g_qa.shape[1], g_kva.shape[1]
    qk_w, v_w = ml_h * MLSTM_DK, ml_h * MLSTM_DV
    nc = t // CHUNK

    big_names = ["w_in", "w_qb", "w_kvb", "p_a", "p_b", "w_out", "wq_c", "wk_c", "wv_c", "wo_c", "w_up", "w_down"]
    col_sharded = {"w_in", "w_qb", "w_kvb", "wo_c", "w_up"}
    small_sharded = ["conv_qk", "g_hnorm", "conv_ffn"]
    gathered = gather_shards([wts[n][0].astype(BF16) for n in big_names], [wts[n][0] for n in small_sharded])
    full = {}
    for n, g in zip(big_names + small_sharded, gathered):
        full[n] = _cols_from_shards(g) if (n in col_sharded or n in small_sharded) else g.reshape(-1, g.shape[2])

    o_qa, o_kv, o_kpe = 0, q_rank, q_rank + kv_rank
    o_q = o_kpe + ROPE
    o_v = o_q + 2 * qk_w
    o_if = o_v + v_w
    o_o = o_if + 2 * ml_h
    o_ga, o_gb = o_o + v_w, o_o + v_w + d
    wi = full["w_in"]
    pad_kpe = jnp.zeros((d, LANE - ROPE), BF16)
    pad_if = jnp.zeros((d, LANE - 2 * ml_h), BF16)
    w_small = jnp.concatenate([wi[:, o_qa:o_q], pad_kpe, wi[:, o_if:o_o], pad_if], axis=1)
    o_kpe_s, o_if_s = o_kpe, o_kpe + LANE
    w_qk, w_v, w_o, w_ga, w_gb = wi[:, o_q:o_v], wi[:, o_v:o_if], wi[:, o_o:o_ga], wi[:, o_ga:o_gb], wi[:, o_gb:]

    wq3 = full["w_qb"].reshape(q_rank, mla_h, NOPE + ROPE)
    wq_nope = wq3[:, :, :NOPE].reshape(q_rank, mla_h * NOPE)
    wq_pe = jnp.pad(wq3[:, :, NOPE:], ((0, 0), (0, 0), (0, LANE - ROPE))).reshape(q_rank, mla_h * LANE)
    wkv3 = full["w_kvb"].reshape(kv_rank, mla_h, NOPE + VHEAD)
    wk_nope = wkv3[:, :, :NOPE].reshape(kv_rank, mla_h * NOPE)
    wv_mla = wkv3[:, :, NOPE:].reshape(kv_rank, mla_h * VHEAD)
    wup_g, wup_v = full["w_up"][:, :f_dim], full["w_up"][:, f_dim:]

    inv_freq = ROPE_BASE ** (-jnp.arange(0, ROPE, 2, dtype=F32) / ROPE)
    ang = positions[0].astype(F32)[:, None] * inv_freq
    cos, sin = jnp.cos(ang), jnp.sin(ang)
    zero_h = jnp.zeros_like(cos)
    tabs = [_pad_lanes(jnp.concatenate([cos, cos], axis=1)), _pad_lanes(-sin), _pad_lanes(jnp.concatenate([zero_h, sin], axis=1))]
    mla_gains = [g_qn_nope, _pad_lanes(g_qn_pe), g_kn_nope, _pad_lanes(g_kn_pe)]

    u1 = rms_fwd(x2d, g_mix, "rms_mix")
    z_small = mm(u1, w_small, name="in_small")
    z_qa, z_kv = z_small[:, o_qa:o_kv], z_small[:, o_kv:o_kpe]
    z_kpe, z_if = z_small[:, o_kpe_s:o_kpe_s + LANE], z_small[:, o_if_s:o_if_s + 2 * ml_h]
    z_qk = mm(u1, w_qk, name="in_qk")
    z_v = mm(u1, w_v, name="in_v")
    z_o = mm(u1, w_o, name="in_o")
    z_ga = mm(u1, w_ga, name="in_ga")
    z_gb = mm(u1, w_gb, name="in_gb")

    qa_n = rms_fwd(z_qa, g_qa, "rms_qa")
    kv_n = rms_fwd(z_kv, g_kva, "rms_kva")
    qn_raw = mm(qa_n, wq_nope, name="q_nope")
    qp_raw = mm(qa_n, wq_pe, name="q_pe")
    kn_raw = mm(kv_n, wk_nope, name="k_nope")
    v_mla = mm(kv_n, wv_mla, out_dtype=BF16, name="v_mla")
    q_att, k_att = mla_prep_fwd(qn_raw, qp_raw, kn_raw, z_kpe, tabs, mla_gains, mla_h)
    y_a, lse_row = attn_fwd(q_att, k_att, jnp.transpose(v_mla), mla_h)

    colscale = jnp.concatenate([jnp.full((1, qk_w), MLSTM_DK ** -0.5, F32), jnp.ones((1, qk_w), F32)], axis=1)
    qk_c = conv_qk_fwd(z_qk, full["conv_qk"], colscale)
    gates4 = z_if.reshape(nc, CHUNK, 2, ml_h)
    gcol = jnp.transpose(gates4, (3, 0, 1, 2))
    grow = jnp.transpose(gates4, (3, 0, 2, 1))
    bias = jnp.transpose(b_if.reshape(2, ml_h), (1, 0)).reshape(ml_h, 1, 2)
    h_raw, c_all, n_all, m_all = mlstm_fwd(qk_c, z_v, gcol, grow, bias, ml_h)
    g_hn = full["g_hnorm"].reshape(1, v_w)
    hn_gr, hd_gr = t // _pick(t, ROW_TILE), t // _pick(t, HEAD_ROW_TILE)
    y_b = ew(lambda *a: (_hnorm_gate(*a),), [(h_raw, "rc"), (z_o, "rc"), (g_hn, "c")], [((t, v_w), BF16, "rc")], gr=hd_gr, gc=ml_h,
             name="hnorm_gate")[0]

    pa = mm(y_a, full["p_a"], name="proj_a")
    pb = mm(y_b, full["p_b"], name="proj_b")
    merge_fn = lambda ga, gb, a, b: (_sigmoid(ga) * a + _sigmoid(gb) * b,)
    merged = ew(merge_fn, [(z_ga, "r"), (z_gb, "r"), (pa, "r"), (pb, "r")], [((t, d), BF16, "r")], gr=hn_gr, name="merge")[0]
    x1 = mm(merged, full["w_out"], add=x2d, name="out_proj")

    uc = rms_fwd(x1, g_cross, "rms_cross")
    mem_n = rms_fwd(mem2d, g_mem, "rms_mem")
    qc = mm(uc, full["wq_c"], name="cross_q")
    kc = mm(mem_n, full["wk_c"], name="cross_k")
    vc = mm(mem_n, full["wv_c"], name="cross_v")
    oc = cross_fwd(qc, kc, vc, g_cq, g_ck, cr_h)
    x2 = mm(oc, full["wo_c"], add=x1, name="cross_out")

    u3 = rms_fwd(x2, g_ffn, "rms_ffn")
    hg = mm(u3, wup_g, name="ffn_up_gate")
    hv = mm(u3, wup_v, name="ffn_up_val")
    cw, cb = full["conv_ffn"], b_conv_ffn
    act = ffn_glu_fwd(hg, hv, cw[:, :f_dim], cw[:, f_dim:], cb[:, :f_dim], cb[:, f_dim:])
    y = mm(act, full["w_down"], add=x2, name="ffn_down")

    def loss_fn(y_, t_):
        err = y_ - t_
        part = jnp.sum(jnp.sum(err * err, axis=1, keepdims=True), axis=0, keepdims=True) * (0.5 / d)
        return err * (1.0 / d), err * (1.0 / d), jnp.broadcast_to(part, (1, LANE))

    dy, dy_mx, loss_part = ew(loss_fn, [(y, "r"), (tgt, "r")], [((t, d), F32, "r"), ((t, d), BF16, "r"), ((1, LANE), F32, "f")],
                              gr=hn_gr, name="loss")

    gw = {}
    gw["w_down"] = mm(act, dy_mx, ta=True, out_dtype=BF16, name="dw_down")
    dact = mm(dy_mx, full["w_down"], tb=True, name="d_act")
    dhg, dhv, dcw_g, dcw_v, dcb_g, dcb_v = ffn_glu_bwd(hg, hv, cw[:, :f_dim], cw[:, f_dim:], cb[:, :f_dim], cb[:, f_dim:], dact)
    gw["conv_ffn"] = jnp.concatenate([dcw_g, dcw_v], axis=1)
    gw["b_conv_ffn"] = jnp.concatenate([dcb_g, dcb_v], axis=1)
    dwup_g = mm(u3, dhg, ta=True, out_dtype=BF16, name="dw_up_gate")
    dwup_v = mm(u3, dhv, ta=True, out_dtype=BF16, name="dw_up_val")
    du3 = mm(dhg, wup_g, tb=True, name="d_u3_gate")
    du3 = mm(dhv, wup_v, tb=True, add=du3, name="d_u3_val")
    dx2, gw["g_ffn"] = rms_bwd(x2, g_ffn, du3, dy, "rms_ffn_bwd")

    gw["wo_c"] = mm(oc, dx2, ta=True, out_dtype=BF16, name="dw_cross_out")
    doc = mm(dx2, full["wo_c"], tb=True, name="d_cross_o")
    dqc, dkc, dvc, gw["g_cq"], gw["g_ck"] = cross_bwd(qc, kc, vc, g_cq, g_ck, doc, cr_h)
    gw["wq_c"] = mm(uc, dqc, ta=True, out_dtype=BF16, name="dw_cross_q")
    gw["wk_c"] = mm(mem_n, dkc, ta=True, out_dtype=BF16, name="dw_cross_k")
    gw["wv_c"] = mm(mem_n, dvc, ta=True, out_dtype=BF16, name="dw_cross_v")
    duc = mm(dqc, full["wq_c"], tb=True, name="d_uc")
    dmem_n = mm(dkc, full["wk_c"], tb=True, name="d_mem_k")
    dmem_n = mm(dvc, full["wv_c"], tb=True, add=dmem_n, name="d_mem_v")
    _, gw["g_mem"] = rms_bwd(mem2d, g_mem, dmem_n, None, "rms_mem_bwd")
    dx1, gw["g_cross"] = rms_bwd(x1, g_cross, duc, dx2, "rms_cross_bwd")

    gw["w_out"] = mm(merged, dx1, ta=True, out_dtype=BF16, name="dw_out")
    dmerged = mm(dx1, full["w_out"], tb=True, name="d_merged")

    def merge_bwd(ga, gb, a, b, dm):
        _, pull = jax.vjp(lambda *args: merge_fn(*args)[0], ga, gb, a, b)
        return pull(dm)

    dz_ga, dz_gb, dpa, dpb = ew(merge_bwd, [(z_ga, "r"), (z_gb, "r"), (pa, "r"), (pb, "r"), (dmerged, "r")],
                                [((t, d), BF16, "r")] * 4, gr=hn_gr, name="merge_bwd")
    gw["p_a"] = mm(y_a, dpa, ta=True, out_dtype=BF16, name="dw_proj_a")
    gw["p_b"] = mm(y_b, dpb, ta=True, out_dtype=BF16, name="dw_proj_b")
    dy_a = mm(dpa, full["p_a"], tb=True, name="d_ya")
    dy_b = mm(dpb, full["p_b"], tb=True, name="d_yb")

    def hnorm_bwd(h_, zo_, g_, dyb_):
        _, pull = jax.vjp(_hnorm_gate, h_, zo_, g_)
        return pull(dyb_)

    dh_raw, dz_o, dg_hn = ew(hnorm_bwd, [(h_raw, "rc"), (z_o, "rc"), (g_hn, "c"), (dy_b, "rc")],
                             [((t, v_w), F32, "rc"), ((t, v_w), BF16, "rc"), ((1, v_w), F32, "c")],
                             gr=hd_gr, gc=ml_h, order="cr", name="hnorm_gate_bwd")
    gw["g_hnorm"] = dg_hn.reshape(ml_h, MLSTM_DV)
    dq_m, dk_m, dz_v, dgcol, dgrow = mlstm_bwd(qk_c, z_v, gcol, grow, bias, c_all, n_all, m_all, dh_raw, ml_h)
    dgates4 = jnp.transpose(dgcol, (1, 2, 3, 0)) + jnp.transpose(dgrow, (1, 3, 2, 0))
    dz_if = dgates4.reshape(t, 2 * ml_h)
    gw["b_if"] = ew(lambda a: (jnp.sum(a, axis=0, keepdims=True),), [(dz_if, "r")], [((1, 2 * ml_h), F32, "f")],
                    gr=hn_gr, name="bias_if_bwd")[0]
    dz_qk, gw["conv_qk"] = conv_qk_bwd(z_qk, full["conv_qk"], colscale, dq_m, dk_m)

    dq_att, dk_att, dv_mla = attn_bwd(q_att, k_att, v_mla, y_a, dy_a, lse_row, mla_h)
    dqn_raw, dqp_raw, dkn_raw, dz_kpe, gw["g_qn_nope"], dg_qp, gw["g_kn_nope"], dg_kp = mla_prep_bwd(
        qn_raw, qp_raw, kn_raw, z_kpe, tabs, mla_gains, dq_att, dk_att, mla_h)
    gw["g_qn_pe"], gw["g_kn_pe"] = dg_qp[:, :ROPE], dg_kp[:, :ROPE]
    dwq_nope = mm(qa_n, dqn_raw, ta=True, out_dtype=BF16, name="dw_q_nope")
    dwq_pe = mm(qa_n, dqp_raw, ta=True, out_dtype=BF16, name="dw_q_pe")
    dwk_nope = mm(kv_n, dkn_raw, ta=True, out_dtype=BF16, name="dw_k_nope")
    dwv_mla = mm(kv_n, dv_mla, ta=True, out_dtype=BF16, name="dw_v_mla")
    dqa_n = mm(dqn_raw, wq_nope, tb=True, name="d_qa_nope")
    dqa_n = mm(dqp_raw, wq_pe, tb=True, add=dqa_n, name="d_qa_pe")
    dkv_n = mm(dkn_raw, wk_nope, tb=True, name="d_kv_nope")
    dkv_n = mm(dv_mla, wv_mla, tb=True, add=dkv_n, name="d_kv_v")
    dz_qa, gw["g_qa"] = rms_bwd(z_qa, g_qa, dqa_n, None, "rms_qa_bwd", BF16)
    dz_kv, gw["g_kva"] = rms_bwd(z_kv, g_kva, dkv_n, None, "rms_kva_bwd", BF16)
    gw["w_qb"] = jnp.concatenate([dwq_nope.reshape(q_rank, mla_h, NOPE), dwq_pe.reshape(q_rank, mla_h, LANE)[:, :, :ROPE]],
                                 axis=2).reshape(q_rank, -1)
    gw["w_kvb"] = jnp.concatenate([dwk_nope.reshape(kv_rank, mla_h, NOPE), dwv_mla.reshape(kv_rank, mla_h, VHEAD)],
                                  axis=2).reshape(kv_rank, -1)

    dz_small = jnp.concatenate([dz_qa, dz_kv, dz_kpe.astype(BF16), _pad_lanes(dz_if).astype(BF16)], axis=1)
    dw_small = mm(u1, dz_small, ta=True, out_dtype=BF16, name="dw_in_small")
    du1 = mm(dz_small, w_small, tb=True, name="d_u1_small")
    dw_segs = []
    for nm, dz, w_seg in (("qk", dz_qk, w_qk), ("v", dz_v, w_v), ("o", dz_o, w_o), ("ga", dz_ga, w_ga), ("gb", dz_gb, w_gb)):
        dw_segs.append(mm(u1, dz, ta=True, out_dtype=BF16, name="dw_in_" + nm))
        du1 = mm(dz, w_seg, tb=True, add=du1, name="d_u1_" + nm)
    gw["w_in"] = jnp.concatenate([dw_small[:, :o_kpe_s + ROPE], dw_segs[0], dw_segs[1],
                                  dw_small[:, o_if_s:o_if_s + 2 * ml_h], dw_segs[2], dw_segs[3], dw_segs[4]], axis=1)
    gw["w_up"] = jnp.concatenate([dwup_g, dwup_v], axis=1)
    grad_x, gw["g_mix"] = rms_bwd(x2d, g_mix, du1, dx1, "rms_mix_bwd")

    shard_major = [_cols_to_shards(gw[n]) if n in col_sharded else gw[n].reshape(N_CHIPS, -1, gw[n].shape[1]) for n in big_names]
    from_sib = sibling_halves(shard_major)
    parts = [add_sibling(g, r, c_arr) for g, r in zip(shard_major, from_sib)]
    others = scatter_partials(parts)
    halves = [sum_chips(p, o, place_arr) for p, o in zip(parts, others)]
    joined = join_halves(halves)
    big_grads = {n: j.reshape(-1, j.shape[2]) for n, j in zip(big_names, joined)}

    small_names = [n for n in names if n not in big_names]
    pieces = [loss_part]
    for n in small_names:
        flat = gw[n].reshape(1, -1)
        pieces.append(jnp.pad(flat, ((0, 0), (0, (-flat.shape[1]) % LANE))))
    packed = jnp.concatenate(pieces, axis=1)
    packed = jnp.pad(packed, ((0, 0), (0, (-packed.shape[1]) % (8 * LANE)))).reshape(-1, LANE)
    total = allreduce_small(packed).reshape(1, -1)
    loss = total[0, 0]
    small_grads, off = {}, LANE
    for n in small_names:
        size = gw[n].size
        g_full = total[:, off:off + size].reshape(gw[n].shape)
        off += size + (-size) % LANE
        if n in small_sharded:
            width = wts[n].shape[-1]
            g_full = lax.dynamic_slice_in_dim(g_full, chip * width, width, axis=g_full.ndim - 1)
        small_grads[n] = g_full.reshape(wts[n].shape[1:])

    grads, deltas, new_m, new_v = {}, {}, {}, {}
    for n in big_names:
        w2 = wts[n][0]
        grads[n] = big_grads[n]
        deltas[n], new_m[n], new_v[n] = adamw(big_grads[n], w2, mom[n][0], var[n][0], "adamw_" + n)

    def pack_small(tree):
        flat = jnp.concatenate([tree[n].reshape(1, -1) for n in small_names], axis=1)
        return jnp.pad(flat, ((0, 0), (0, (-flat.shape[1]) % (8 * LANE)))).reshape(8, -1)

    sg = pack_small(small_grads)
    sd, sm, sv = adamw(sg, pack_small({n: wts[n][0] for n in small_names}), pack_small({n: mom[n][0] for n in small_names}),
                       pack_small({n: var[n][0] for n in small_names}), "adamw_small")
    off = 0
    for n in small_names:
        size = small_grads[n].size
        shp = wts[n].shape[1:]
        grads[n] = small_grads[n]
        for dst, src in ((deltas, sd), (new_m, sm), (new_v, sv)):
            dst[n] = src.reshape(1, -1)[:, off:off + size].reshape(shp)
        off += size

    def out(tree):
        return [tree[n].reshape(wts[n].shape) for n in names]

    return (loss, grad_x.reshape(x.shape), *out(grads), *out(deltas), *out(new_m), *out(new_v))
```

```python
import functools
import math

import jax
import jax.numpy as jnp
from jax import lax
from jax.experimental import pallas as pl
from jax.experimental.pallas import tpu as pltpu

F32, BF16 = jnp.float32, jnp.bfloat16
MESH = pl.DeviceIdType.MESH

EPS = 1e-6
CHUNK = 64
LOG2_CHUNK = 6
NOPE, ROPE, VHEAD = 128, 64, 128
MLSTM_DK, MLSTM_DV, MLSTM_CONV = 128, 256, 4
CROSS_DH = 128
FFN_CONV = 3
ROPE_BASE = 10000.0
LOG2_E = math.log2(math.e)
ADAM_LR, ADAM_B1, ADAM_B2, ADAM_EPS, ADAM_WD, ADAM_STEP = 0.001, 0.9, 0.999, 1e-08, 0.01, 10

LANE = 128
ROW_TILE = 256
HEAD_ROW_TILE = 1024
ATT_TILE = 512
MM_TILES = (1024, 1024, 2048)
VMEM_LIMIT = 56 * 1024 * 1024
N_CHIPS = 4

NN = ((1,), (0,))
NT = ((1,), (1,))
TN = ((0,), (0,))


def _pick(dim, pref):
    if dim <= pref:
        return dim
    for t in range(pref, 0, -LANE):
        if dim % t == 0:
            return t
    return dim


def _bdot(a, b, dims):
    return lax.dot_general(a.astype(BF16), b.astype(BF16), (dims, ((), ())), preferred_element_type=F32)


@jax.custom_vjp
def _dnn(a, b):
    return _bdot(a, b, NN)


_dnn.defvjp(lambda a, b: (_bdot(a, b, NN), (a, b)),
            lambda r, g: (_bdot(g, r[1], NT), _bdot(r[0], g, TN)))


@jax.custom_vjp
def _dnt(a, b):
    return _bdot(a, b, NT)


_dnt.defvjp(lambda a, b: (_bdot(a, b, NT), (a, b)),
            lambda r, g: (_bdot(g, r[1], NN), _bdot(g, r[0], TN)))


@jax.custom_vjp
def _dtn(a, b):
    return _bdot(a, b, TN)


_dtn.defvjp(lambda a, b: (_bdot(a, b, TN), (a, b)),
            lambda r, g: (_bdot(r[1], g, NT), _bdot(r[0], g, NN)))


@functools.partial(jax.custom_vjp, nondiff_argnums=(1,))
def _lane_roll(x, shift):
    return pltpu.roll(x, shift, 1)


_lane_roll.defvjp(lambda x, shift: (pltpu.roll(x, shift, 1), None),
                  lambda shift, _, g: (pltpu.roll(g, (LANE - shift) % LANE, 1),))


def _params(*sem):
    return pltpu.CompilerParams(dimension_semantics=sem, vmem_limit_bytes=VMEM_LIMIT)


def mm(a, b, *, ta=False, tb=False, add=None, out_dtype=F32, name):
    m_dim, k_dim = (a.shape[1], a.shape[0]) if ta else a.shape
    n_dim = b.shape[0] if tb else b.shape[1]
    assert k_dim == (b.shape[1] if tb else b.shape[0]), (name, a.shape, b.shape)
    tm, tn, tk = _pick(m_dim, MM_TILES[0]), _pick(n_dim, MM_TILES[1]), _pick(k_dim, MM_TILES[2])
    nk = k_dim // tk
    dims = ((0,) if ta else (1,), (1,) if tb else (0,))
    has_add = add is not None

    def body(*refs):
        a_ref, b_ref = refs[0], refs[1]
        c_ref = refs[2] if has_add else None
        o_ref = refs[3] if has_add else refs[2]
        prod = _bdot(a_ref[...], b_ref[...], dims)
        if nk == 1:
            o_ref[...] = (prod + c_ref[...].astype(F32) if has_add else prod).astype(o_ref.dtype)
            return
        acc = refs[-1]
        k = pl.program_id(2)

        @pl.when(k == 0)
        def _():
            acc[...] = prod + c_ref[...].astype(F32) if has_add else prod

        @pl.when(k > 0)
        def _():
            acc[...] += prod

        @pl.when(k == nk - 1)
        def _():
            o_ref[...] = acc[...].astype(o_ref.dtype)

    in_specs = [
        pl.BlockSpec((tk, tm), lambda i, j, k: (k, i)) if ta else pl.BlockSpec((tm, tk), lambda i, j, k: (i, k)),
        pl.BlockSpec((tn, tk), lambda i, j, k: (j, k)) if tb else pl.BlockSpec((tk, tn), lambda i, j, k: (k, j)),
    ]
    args = [a, b]
    if has_add:
        in_specs.append(pl.BlockSpec((tm, tn), lambda i, j, k: (i, j)))
        args.append(add)
    return pl.pallas_call(
        body, grid=(m_dim // tm, n_dim // tn, nk), in_specs=in_specs,
        out_specs=pl.BlockSpec((tm, tn), lambda i, j, k: (i, j)),
        out_shape=jax.ShapeDtypeStruct((m_dim, n_dim), out_dtype),
        scratch_shapes=[pltpu.VMEM((tm, tn), F32)] if nk > 1 else [],
        compiler_params=_params("parallel", "parallel", "arbitrary"), name=name)(*args)


def ew(fn, ins, outs, *, gr, gc=1, order="rc", name):
    n_in = len(ins)

    def block(shape, kind):
        r, c = shape
        return (r // gr if kind in ("rc", "r") else r, c // gc if kind in ("rc", "c") else c)

    def imap(kind):
        def f(p0, p1):
            i, j = (p0, p1) if order == "rc" else (p1, p0)
            return {"rc": (i, j), "r": (i, 0), "c": (0, j), "f": (0, 0)}[kind]
        return f

    def body(*refs):
        p0, p1 = pl.program_id(0), pl.program_id(1)
        i, j = (p0, p1) if order == "rc" else (p1, p0)
        vals = fn(*[r[...] for r in refs[:n_in]])
        for ref, val, (_, dtype, kind) in zip(refs[n_in:], vals, outs):
            first = {"rc": None, "r": (j == 0) if gc > 1 else None, "c": (i == 0) if gr > 1 else None,
                     "f": ((i == 0) & (j == 0)) if gr * gc > 1 else None}[kind]
            _store(ref, val.astype(dtype), first)

    grid = (gr, gc) if order == "rc" else (gc, gr)
    return pl.pallas_call(
        body, grid=grid,
        in_specs=[pl.BlockSpec(block(a.shape, k), imap(k)) for a, k in ins],
        out_specs=[pl.BlockSpec(block(s, k), imap(k)) for s, _, k in outs],
        out_shape=[jax.ShapeDtypeStruct(s, d) for s, d, _ in outs],
        compiler_params=_params("arbitrary", "arbitrary"), name=name)(*[a for a, _ in ins])


def _store(ref, val, first):
    if first is None:
        ref[...] = val
        return

    @pl.when(first)
    def _():
        ref[...] = val

    @pl.when(jnp.logical_not(first))
    def _():
        ref[...] += val


def _f32(*xs):
    return [x.astype(F32) for x in xs]


def _rms(x, g, n):
    ms = jnp.sum(x * x, axis=-1, keepdims=True) * (1.0 / n)
    return x * lax.rsqrt(ms + EPS) * g


def _sigmoid(x):
    return 1.0 / (1.0 + jnp.exp(-x))


def _silu(x):
    return x * _sigmoid(x)


def _log_sigmoid(x):
    return jnp.minimum(x, 0.0) - jnp.log(1.0 + jnp.exp(-jnp.abs(x)))


def rms_fwd(x, g, name, out_dtype=BF16):
    t, w = x.shape
    return ew(lambda x_, g_: (_rms(x_, g_, w),), [(x, "r"), (g, "f")], [((t, w), out_dtype, "r")],
              gr=t // _pick(t, ROW_TILE), name=name)[0]


def rms_bwd(x, g, du, res, name, out_dtype=F32):
    t, w = x.shape

    def fn(x_, g_, du_, *res_):
        _, pull = jax.vjp(lambda a, b: _rms(a, b, w), x_, g_)
        dx, dg = pull(du_.astype(F32))
        return (dx + res_[0] if res_ else dx), dg

    ins = [(x, "r"), (g, "f"), (du, "r")] + ([(res, "r")] if res is not None else [])
    return ew(fn, ins, [((t, w), out_dtype, "r"), ((1, w), F32, "f")], gr=t // _pick(t, ROW_TILE), name=name)


def _rope(x, cos_t, sin_lo, sin_hi):
    return x * cos_t + _lane_roll(x, LANE - ROPE // 2) * sin_lo + _lane_roll(x, ROPE // 2) * sin_hi


def _mla_prep(qn, qp, kn, kp, cos_t, sin_lo, sin_hi, g_qn, g_qp, g_kn, g_kp):
    q = jnp.concatenate([_rms(qn, g_qn, NOPE), _rope(_rms(qp, g_qp, ROPE), cos_t, sin_lo, sin_hi)], axis=1)
    k = jnp.concatenate([_rms(kn, g_kn, NOPE), _rope(_rms(kp, g_kp, ROPE), cos_t, sin_lo, sin_hi)], axis=1)
    return q, k


def mla_prep_fwd(qn, qp, kn, kp, tabs, gains, heads):
    t = qn.shape[0]
    ins = [(qn, "rc"), (qp, "rc"), (kn, "rc"), (kp, "r")] + [(a, "r") for a in tabs] + [(g, "f") for g in gains]
    return ew(lambda *a: _mla_prep(*_f32(*a)), ins,
              [((t, heads * 2 * LANE), BF16, "rc"), ((t, heads * 2 * LANE), BF16, "rc")],
              gr=t // _pick(t, HEAD_ROW_TILE), gc=heads, name="mla_prep_fwd")


def mla_prep_bwd(qn, qp, kn, kp, tabs, gains, dq, dk, heads):
    t = qn.shape[0]

    def fn(qn_, qp_, kn_, kp_, c_, s1_, s2_, g1, g2, g3, g4, dq_, dk_):
        _, pull = jax.vjp(lambda a, b, c, d, e, f, g, h: _mla_prep(a, b, c, d, c_, s1_, s2_, e, f, g, h),
                          qn_, qp_, kn_, kp_, g1, g2, g3, g4)
        return pull((dq_, dk_))

    ins = ([(qn, "rc"), (qp, "rc"), (kn, "rc"), (kp, "r")] + [(a, "r") for a in tabs] + [(g, "f") for g in gains]
           + [(dq, "rc"), (dk, "rc")])
    hw = heads * LANE
    outs = [((t, hw), BF16, "rc"), ((t, hw), BF16, "rc"), ((t, hw), BF16, "rc"), ((t, LANE), F32, "r")] \
        + [((1, LANE), F32, "f")] * 4
    return ew(fn, ins, outs, gr=t // _pick(t, HEAD_ROW_TILE), gc=heads, name="mla_prep_bwd")


def _chunk_mask(row0, col0, shape, rows_are_queries):
    r = jnp.right_shift(row0 + lax.broadcasted_iota(jnp.int32, shape, 0), LOG2_CHUNK)
    c = jnp.right_shift(col0 + lax.broadcasted_iota(jnp.int32, shape, 1), LOG2_CHUNK)
    return (c <= r) if rows_are_queries else (r <= c)


def _block_pairs(nq, queries_outer):
    if queries_outer:
        pairs = [(i, j) for i in range(nq) for j in range(i + 1)]
    else:
        pairs = [(i, j) for j in range(nq) for i in range(j, nq)]
    return jnp.asarray([p[0] for p in pairs], jnp.int32), jnp.asarray([p[1] for p in pairs], jnp.int32)


def attn_fwd(q, k, vt, heads):
    t = q.shape[0]
    tq = _pick(t, ATT_TILE)
    qi, kj = _block_pairs(t // tq, True)
    scale = (NOPE + ROPE) ** -0.5
    scale2 = scale * LOG2_E

    def body(qi_ref, kj_ref, q_ref, k_ref, vt_ref, o_ref, lse_ref, m_s, l_s, acc):
        p = pl.program_id(1)
        i, j = qi_ref[p], kj_ref[p]

        @pl.when(j == 0)
        def _():
            m_s[...] = jnp.full_like(m_s, -jnp.inf)
            l_s[...] = jnp.zeros_like(l_s)
            acc[...] = jnp.zeros_like(acc)

        def step(diagonal):
            st = _bdot(k_ref[...], q_ref[...], NT)
            if diagonal:
                st = jnp.where(_chunk_mask(0, 0, (tq, tq), False), st, -jnp.inf)
            m_new = jnp.maximum(m_s[...], jnp.max(st, axis=0, keepdims=True))
            alpha = jnp.exp2((m_s[...] - m_new) * scale2)
            pt = jnp.exp2((st - m_new) * scale2)
            l_s[...] = alpha * l_s[...] + jnp.sum(pt, axis=0, keepdims=True)
            acc[...] = alpha * acc[...] + _bdot(vt_ref[...], pt, NN)
            m_s[...] = m_new

        pl.when(j < i)(functools.partial(step, False))

        @pl.when(j == i)
        def _():
            step(True)
            o_ref[...] = jnp.transpose(acc[...] / l_s[...])
            lse_ref[...] = m_s[...] * scale + jnp.log(l_s[...])

    return pl.pallas_call(
        body,
        grid_spec=pltpu.PrefetchScalarGridSpec(
            num_scalar_prefetch=2, grid=(heads, qi.shape[0]),
            in_specs=[pl.BlockSpec((tq, 2 * LANE), lambda h, p, qi_, kj_: (qi_[p], h)),
                      pl.BlockSpec((tq, 2 * LANE), lambda h, p, qi_, kj_: (kj_[p], h)),
                      pl.BlockSpec((VHEAD, tq), lambda h, p, qi_, kj_: (h, kj_[p]))],
            out_specs=[pl.BlockSpec((tq, VHEAD), lambda h, p, qi_, kj_: (qi_[p], h)),
                       pl.BlockSpec((None, 1, tq), lambda h, p, qi_, kj_: (h, 0, qi_[p]))],
            scratch_shapes=[pltpu.VMEM((1, tq), F32), pltpu.VMEM((1, tq), F32), pltpu.VMEM((VHEAD, tq), F32)]),
        out_shape=[jax.ShapeDtypeStruct((t, heads * VHEAD), F32), jax.ShapeDtypeStruct((heads, 1, t), F32)],
        compiler_params=_params("parallel", "arbitrary"), name="mla_attn_fwd")(qi, kj, q, k, vt)


def attn_bwd(q, k, v, o, do, lse_row, heads):
    t = q.shape[0]
    tq = _pick(t, ATT_TILE)
    qi, kj = _block_pairs(t // tq, False)
    scale = (NOPE + ROPE) ** -0.5
    scale2 = scale * LOG2_E

    def body(qi_ref, kj_ref, q_ref, k_ref, v_ref, o_ref, do_ref, lse_ref, dq_ref, dk_ref, dv_ref):
        p = pl.program_id(1)
        i, j = qi_ref[p], kj_ref[p]

        @pl.when(p == 0)
        def _():
            dq_ref[...] = jnp.zeros_like(dq_ref)

        @pl.when(i == j)
        def _():
            dk_ref[...] = jnp.zeros_like(dk_ref)
            dv_ref[...] = jnp.zeros_like(dv_ref)

        def step(diagonal):
            do_i = do_ref[...]
            prod = do_i * o_ref[...]
            hi = prod.astype(BF16)
            mid = (prod - hi.astype(F32)).astype(BF16)
            lo = (prod - hi.astype(F32) - mid.astype(F32)).astype(BF16)
            ones = jnp.ones((8, VHEAD), BF16)
            delta = (_bdot(ones, hi, NT) + _bdot(ones, mid, NT) + _bdot(ones, lo, NT))[0:1, :]
            st = _bdot(k_ref[...], q_ref[...], NT)
            pt = jnp.exp2(st * scale2 - lse_ref[...] * LOG2_E)
            if diagonal:
                pt = jnp.where(_chunk_mask(0, 0, (tq, tq), False), pt, 0.0)
            dv_ref[...] += _bdot(pt, do_i, NN)
            dpt = _bdot(v_ref[...], do_i, NT)
            dst = pt * (dpt - delta) * scale
            dk_ref[...] += _bdot(dst, q_ref[...], NN)
            rows = pl.ds(pl.multiple_of(i * tq, tq), tq)
            dq_ref[rows, :] += _bdot(dst, k_ref[...], TN)

        pl.when(i > j)(functools.partial(step, False))
        pl.when(i == j)(functools.partial(step, True))

    qmap = lambda h, p, qi_, kj_: (qi_[p], h)
    kmap = lambda h, p, qi_, kj_: (kj_[p], h)
    return pl.pallas_call(
        body,
        grid_spec=pltpu.PrefetchScalarGridSpec(
            num_scalar_prefetch=2, grid=(heads, qi.shape[0]),
            in_specs=[pl.BlockSpec((tq, 2 * LANE), qmap), pl.BlockSpec((tq, 2 * LANE), kmap),
                      pl.BlockSpec((tq, VHEAD), kmap), pl.BlockSpec((tq, VHEAD), qmap), pl.BlockSpec((tq, VHEAD), qmap),
                      pl.BlockSpec((None, 1, tq), lambda h, p, qi_, kj_: (h, 0, qi_[p]))],
            out_specs=[pl.BlockSpec((t, 2 * LANE), lambda h, p, qi_, kj_: (0, h)),
                       pl.BlockSpec((tq, 2 * LANE), kmap), pl.BlockSpec((tq, VHEAD), kmap)]),
        out_shape=[jax.ShapeDtypeStruct((t, heads * 2 * LANE), F32), jax.ShapeDtypeStruct((t, heads * 2 * LANE), F32),
                   jax.ShapeDtypeStruct((t, heads * VHEAD), F32)],
        compiler_params=_params("parallel", "arbitrary"), name="mla_attn_bwd")(qi, kj, q, k, v, o, do, lse_row)


def _shift_down(x, s):
    if s == 0:
        return x
    rows = lax.broadcasted_iota(jnp.int32, x.shape, 0)
    return jnp.where(rows >= s, pltpu.roll(x, s, 0), 0.0)


def _shift_up(x, s):
    if s == 0:
        return x
    t = x.shape[0]
    rows = lax.broadcasted_iota(jnp.int32, x.shape, 0)
    return jnp.where(rows < t - s, pltpu.roll(x, t - s, 0), 0.0)


def _conv(x, w_ref, width):
    return sum(_shift_down(x, width - 1 - j) * w_ref[j:j + 1, :] for j in range(width))


def _conv_bwd(x, dpre, w_ref, dw_ref, width):
    dx = sum(_shift_up(dpre, width - 1 - j) * w_ref[j:j + 1, :] for j in range(width))
    for j in range(width):
        dw_ref[j:j + 1, :] = jnp.sum(dpre * _shift_down(x, width - 1 - j), axis=0, keepdims=True)
    return dx


def _dsilu(z):
    s = _sigmoid(z)
    return s * (1.0 + z * (1.0 - s))


def conv_qk_fwd(x, w, colscale):
    t, c = x.shape
    tc = _pick(c, 256)

    def body(x_ref, w_ref, s_ref, o_ref):
        o_ref[...] = (_silu(_conv(x_ref[...], w_ref, MLSTM_CONV)) * s_ref[...]).astype(o_ref.dtype)

    return pl.pallas_call(
        body, grid=(c // tc,),
        in_specs=[pl.BlockSpec((t, tc), lambda j: (0, j)), pl.BlockSpec((MLSTM_CONV, tc), lambda j: (0, j)),
                  pl.BlockSpec((1, tc), lambda j: (0, j))],
        out_specs=pl.BlockSpec((t, tc), lambda j: (0, j)), out_shape=jax.ShapeDtypeStruct((t, c), BF16),
        compiler_params=_params("parallel"), name="conv_qk_fwd")(x, w, colscale)


def conv_qk_bwd(x, w, colscale, dq, dk):
    t, c = x.shape
    tc = _pick(c // 2, 256)
    half = (c // 2) // tc

    def body(x_ref, w_ref, s_ref, dq_ref, dk_ref, dx_ref, dw_ref):
        j = pl.program_id(0)
        x_ = x_ref[...]
        dy = jnp.where(j < half, dq_ref[...], dk_ref[...])
        dpre = dy * s_ref[...] * _dsilu(_conv(x_, w_ref, MLSTM_CONV))
        dx_ref[...] = _conv_bwd(x_, dpre, w_ref, dw_ref, MLSTM_CONV).astype(dx_ref.dtype)

    return pl.pallas_call(
        body, grid=(c // tc,),
        in_specs=[pl.BlockSpec((t, tc), lambda j: (0, j)), pl.BlockSpec((MLSTM_CONV, tc), lambda j: (0, j)),
                  pl.BlockSpec((1, tc), lambda j: (0, j)),
                  pl.BlockSpec((t, tc), lambda j: (0, jnp.minimum(j, half - 1))),
                  pl.BlockSpec((t, tc), lambda j: (0, jnp.maximum(j - half, 0)))],
        out_specs=[pl.BlockSpec((t, tc), lambda j: (0, j)), pl.BlockSpec((MLSTM_CONV, tc), lambda j: (0, j))],
        out_shape=[jax.ShapeDtypeStruct((t, c), BF16), jax.ShapeDtypeStruct((MLSTM_CONV, c), F32)],
        compiler_params=_params("parallel"), name="conv_qk_bwd")(x, w, colscale, dq, dk)


def _mlstm_chunk(q, k, v, i_col, i_row, f_col, f_row, c_mat, n_vec, m):
    shape = (CHUNK, CHUNK)
    r = lax.broadcasted_iota(jnp.int32, shape, 0)
    c = lax.broadcasted_iota(jnp.int32, shape, 1)
    tril = c <= r
    lf_col, lf_row = _log_sigmoid(f_col), _log_sigmoid(f_row)
    bc_col = jnp.sum(jnp.where(tril, lf_row, 0.0), axis=1, keepdims=True)
    bc_row = jnp.sum(jnp.where(r <= c, lf_col, 0.0), axis=0, keepdims=True)
    logw = jnp.where(tril, bc_col - bc_row + i_row, -jnp.inf)
    inter = bc_col + m
    m_t = lax.stop_gradient(jnp.maximum(inter, jnp.max(logw, axis=1, keepdims=True)))
    w_intra = jnp.exp(logw - m_t)
    w_inter = jnp.exp(inter - m_t)
    sc = _dnt(q, k) * w_intra
    num = w_inter * _dnn(q, c_mat) + _dnn(sc, v)
    den = w_inter * jnp.sum(q * n_vec, axis=1, keepdims=True) + jnp.sum(sc, axis=1, keepdims=True)
    h = num / jnp.maximum(jnp.abs(den), jnp.exp(-m_t))
    b_last = jnp.sum(lf_row, axis=1, keepdims=True)
    m_new = lax.stop_gradient(jnp.maximum(b_last + m, jnp.max(b_last - bc_row + i_row, axis=1, keepdims=True)))
    decay = jnp.exp(b_last + m - m_new)
    uk = jnp.exp(b_last - bc_col + i_col - m_new) * k
    return h, decay * c_mat + _dtn(uk, v), decay * n_vec + jnp.sum(uk, axis=0, keepdims=True), m_new


def _mlstm_specs(heads, rev, nc):
    ci = (lambda c: nc - 1 - c) if rev else (lambda c: c)
    return dict(
        q=pl.BlockSpec((CHUNK, MLSTM_DK), lambda h, c: (ci(c), h)),
        k=pl.BlockSpec((CHUNK, MLSTM_DK), lambda h, c: (ci(c), heads + h)),
        v=pl.BlockSpec((CHUNK, MLSTM_DV), lambda h, c: (ci(c), h)),
        gc=pl.BlockSpec((None, None, CHUNK, 2), lambda h, c: (h, ci(c), 0, 0)),
        gr=pl.BlockSpec((None, None, 2, CHUNK), lambda h, c: (h, ci(c), 0, 0)),
        b=pl.BlockSpec((None, 1, 2), lambda h, c: (h, 0, 0)),
        cm=pl.BlockSpec((None, None, MLSTM_DK, MLSTM_DV), lambda h, c: (h, ci(c), 0, 0)),
        vec=pl.BlockSpec((None, None, 1, LANE), lambda h, c: (h, ci(c), 0, 0)),
    )


def _gates(gc_ref, gr_ref, b_ref):
    bi, bf = b_ref[:, 0:1], b_ref[:, 1:2]
    return gc_ref[:, 0:1] + bi, gr_ref[0:1, :] + bi, gc_ref[:, 1:2] + bf, gr_ref[1:2, :] + bf


def mlstm_fwd(qk, v, gcol, grow, bias, heads):
    t = qk.shape[0]
    nc = t // CHUNK
    sp = _mlstm_specs(heads, False, nc)

    def body(q_ref, k_ref, v_ref, gc_ref, gr_ref, b_ref, h_ref, c_ref, n_ref, m_ref, c_s, n_s, m_s):
        @pl.when(pl.program_id(1) == 0)
        def _():
            c_s[...] = jnp.zeros_like(c_s)
            n_s[...] = jnp.zeros_like(n_s)
            m_s[...] = jnp.zeros_like(m_s)

        c_ref[...] = c_s[...]
        n_ref[...] = n_s[...]
        m_ref[...] = m_s[...]
        q, k, v_ = _f32(q_ref[...], k_ref[...], v_ref[...])
        h, c_new, n_new, m_new = _mlstm_chunk(q, k, v_, *_gates(gc_ref, gr_ref, b_ref), c_s[...], n_s[...], m_s[:, 0:1])
        h_ref[...] = h
        c_s[...] = c_new
        n_s[...] = n_new
        m_s[...] = jnp.broadcast_to(m_new, m_s.shape)

    return pl.pallas_call(
        body, grid=(heads, nc),
        in_specs=[sp["q"], sp["k"], sp["v"], sp["gc"], sp["gr"], sp["b"]],
        out_specs=[sp["v"], sp["cm"], sp["vec"], sp["vec"]],
        out_shape=[jax.ShapeDtypeStruct((t, heads * MLSTM_DV), F32),
                   jax.ShapeDtypeStruct((heads, nc, MLSTM_DK, MLSTM_DV), F32),
                   jax.ShapeDtypeStruct((heads, nc, 1, LANE), F32), jax.ShapeDtypeStruct((heads, nc, 1, LANE), F32)],
        scratch_shapes=[pltpu.VMEM((MLSTM_DK, MLSTM_DV), F32), pltpu.VMEM((1, LANE), F32), pltpu.VMEM((1, LANE), F32)],
        compiler_params=_params("parallel", "arbitrary"), name="mlstm_fwd")(qk, qk, v, gcol, grow, bias)


def mlstm_bwd(qk, v, gcol, grow, bias, c_all, n_all, m_all, dh, heads):
    t = qk.shape[0]
    nc = t // CHUNK
    sp = _mlstm_specs(heads, True, nc)

    def body(q_ref, k_ref, v_ref, gc_ref, gr_ref, b_ref, c_ref, n_ref, m_ref, dh_ref,
             dq_ref, dk_ref, dv_ref, dgc_ref, dgr_ref, dc_s, dn_s):
        @pl.when(pl.program_id(1) == 0)
        def _():
            dc_s[...] = jnp.zeros_like(dc_s)
            dn_s[...] = jnp.zeros_like(dn_s)

        q, k, v_ = _f32(q_ref[...], k_ref[...], v_ref[...])
        m = m_ref[:, 0:1]
        _, pull = jax.vjp(lambda *a: _mlstm_chunk(*a, m)[:3], q, k, v_, *_gates(gc_ref, gr_ref, b_ref),
                          c_ref[...], n_ref[...])
        dq, dk, dv, di_col, di_row, df_col, df_row, dc, dn = pull((dh_ref[...], dc_s[...], dn_s[...]))
        dq_ref[...] = dq
        dk_ref[...] = dk
        dv_ref[...] = dv.astype(dv_ref.dtype)
        dgc_ref[:, 0:1] = di_col
        dgc_ref[:, 1:2] = df_col
        dgr_ref[0:1, :] = di_row
        dgr_ref[1:2, :] = df_row
        dc_s[...] = dc
        dn_s[...] = dn

    qspec = pl.BlockSpec((CHUNK, MLSTM_DK), lambda h, c: (nc - 1 - c, h))
    return pl.pallas_call(
        body, grid=(heads, nc),
        in_specs=[sp["q"], sp["k"], sp["v"], sp["gc"], sp["gr"], sp["b"], sp["cm"], sp["vec"], sp["vec"], sp["v"]],
        out_specs=[qspec, qspec, sp["v"], sp["gc"], sp["gr"]],
        out_shape=[jax.ShapeDtypeStruct((t, heads * MLSTM_DK), F32), jax.ShapeDtypeStruct((t, heads * MLSTM_DK), F32),
                   jax.ShapeDtypeStruct((t, heads * MLSTM_DV), BF16),
                   jax.ShapeDtypeStruct(gcol.shape, F32), jax.ShapeDtypeStruct(grow.shape, F32)],
        scratch_shapes=[pltpu.VMEM((MLSTM_DK, MLSTM_DV), F32), pltpu.VMEM((1, LANE), F32)],
        compiler_params=_params("parallel", "arbitrary"),
        name="mlstm_bwd")(qk, qk, v, gcol, grow, bias, c_all, n_all, m_all, dh)


def _hnorm_gate(h, zo, g):
    return _rms(h, g, MLSTM_DV) * _sigmoid(zo)


def _cross_core(q, k, v, g_q, g_k, heads):
    scale = CROSS_DH ** -0.5
    outs = []
    for h in range(heads):
        s = _dnt(_rms(q[h], g_q, CROSS_DH), _rms(k[h], g_k, CROSS_DH)) * scale
        p = jnp.exp(s - lax.stop_gradient(jnp.max(s, axis=1, keepdims=True)))
        p = p / jnp.sum(p, axis=1, keepdims=True)
        outs.append(_dnn(p, v[h]))
    return jnp.concatenate(outs, axis=1)


def _split_heads(ref, heads):
    return [ref[:, h * CROSS_DH:(h + 1) * CROSS_DH].astype(F32) for h in range(heads)]


def cross_fwd(q, k, v, g_q, g_k, heads):
    t = q.shape[0]
    tm = _pick(t, ATT_TILE)
    full = lambda a: pl.BlockSpec(a.shape, lambda i: (0, 0))

    def body(q_ref, k_ref, v_ref, gq_ref, gk_ref, o_ref):
        o_ref[...] = _cross_core(_split_heads(q_ref, heads), _split_heads(k_ref, heads), _split_heads(v_ref, heads),
                                 gq_ref[...], gk_ref[...], heads).astype(o_ref.dtype)

    return pl.pallas_call(
        body, grid=(t // tm,), in_specs=[pl.BlockSpec((tm, q.shape[1]), lambda i: (i, 0)), full(k), full(v), full(g_q), full(g_k)],
        out_specs=pl.BlockSpec((tm, q.shape[1]), lambda i: (i, 0)), out_shape=jax.ShapeDtypeStruct(q.shape, BF16),
        compiler_params=_params("parallel"), name="cross_fwd")(q, k, v, g_q, g_k)


def cross_bwd(q, k, v, g_q, g_k, do, heads):
    t, w = q.shape
    tm = _pick(t, ATT_TILE)
    full = lambda a: pl.BlockSpec(a.shape, lambda i: (0, 0))

    def body(q_ref, k_ref, v_ref, gq_ref, gk_ref, do_ref, dq_ref, dk_ref, dv_ref, dgq_ref, dgk_ref):
        qs, ks, vs = _split_heads(q_ref, heads), _split_heads(k_ref, heads), _split_heads(v_ref, heads)
        _, pull = jax.vjp(lambda a, b, c, d, e: _cross_core(a, b, c, d, e, heads), qs, ks, vs, gq_ref[...], gk_ref[...])
        dqs, dks, dvs, dgq, dgk = pull(do_ref[...])
        first = pl.program_id(0) == 0
        for h in range(heads):
            cols = slice(h * CROSS_DH, (h + 1) * CROSS_DH)
            dq_ref[:, cols] = dqs[h].astype(dq_ref.dtype)
            _store(dk_ref.at[:, cols], dks[h], first)
            _store(dv_ref.at[:, cols], dvs[h], first)
        _store(dgq_ref, dgq, first)
        _store(dgk_ref, dgk, first)

    row = pl.BlockSpec((tm, w), lambda i: (i, 0))
    return pl.pallas_call(
        body, grid=(t // tm,), in_specs=[row, full(k), full(v), full(g_q), full(g_k), row],
        out_specs=[row, full(k), full(v), full(g_q), full(g_k)],
        out_shape=[jax.ShapeDtypeStruct(q.shape, BF16), jax.ShapeDtypeStruct(k.shape, F32), jax.ShapeDtypeStruct(v.shape, F32),
                   jax.ShapeDtypeStruct(g_q.shape, F32), jax.ShapeDtypeStruct(g_k.shape, F32)],
        compiler_params=_params("arbitrary"), name="cross_bwd")(q, k, v, g_q, g_k, do)


def ffn_glu_fwd(hg, hv, wg, wv, bg, bv):
    t, f = hg.shape
    tc = _pick(f, LANE)
    col = pl.BlockSpec((t, tc), lambda j: (0, j))
    tap = pl.BlockSpec((FFN_CONV, tc), lambda j: (0, j))
    one = pl.BlockSpec((1, tc), lambda j: (0, j))

    def body(hg_ref, hv_ref, wg_ref, wv_ref, bg_ref, bv_ref, o_ref):
        gate = _conv(hg_ref[...], wg_ref, FFN_CONV) + bg_ref[...]
        val = _conv(hv_ref[...], wv_ref, FFN_CONV) + bv_ref[...]
        o_ref[...] = (_silu(gate) * val).astype(o_ref.dtype)

    return pl.pallas_call(body, grid=(f // tc,), in_specs=[col, col, tap, tap, one, one], out_specs=col,
                          out_shape=jax.ShapeDtypeStruct((t, f), BF16), compiler_params=_params("parallel"),
                          name="ffn_glu_fwd")(hg, hv, wg, wv, bg, bv)


def ffn_glu_bwd(hg, hv, wg, wv, bg, bv, dact):
    t, f = hg.shape
    tc = _pick(f, LANE)
    col = pl.BlockSpec((t, tc), lambda j: (0, j))
    tap = pl.BlockSpec((FFN_CONV, tc), lambda j: (0, j))
    one = pl.BlockSpec((1, tc), lambda j: (0, j))

    def body(hg_ref, hv_ref, wg_ref, wv_ref, bg_ref, bv_ref, da_ref, dhg_ref, dhv_ref, dwg_ref, dwv_ref, dbg_ref, dbv_ref):
        xg, xv, da = hg_ref[...], hv_ref[...], da_ref[...]
        gate = _conv(xg, wg_ref, FFN_CONV) + bg_ref[...]
        val = _conv(xv, wv_ref, FFN_CONV) + bv_ref[...]
        dgate = da * val * _dsilu(gate)
        dval = da * _silu(gate)
        dbg_ref[...] = jnp.sum(dgate, axis=0, keepdims=True)
        dbv_ref[...] = jnp.sum(dval, axis=0, keepdims=True)
        dhg_ref[...] = _conv_bwd(xg, dgate, wg_ref, dwg_ref, FFN_CONV).astype(dhg_ref.dtype)
        dhv_ref[...] = _conv_bwd(xv, dval, wv_ref, dwv_ref, FFN_CONV).astype(dhv_ref.dtype)

    return pl.pallas_call(
        body, grid=(f // tc,), in_specs=[col, col, tap, tap, one, one, col], out_specs=[col, col, tap, tap, one, one],
        out_shape=[jax.ShapeDtypeStruct((t, f), BF16), jax.ShapeDtypeStruct((t, f), BF16),
                   jax.ShapeDtypeStruct((FFN_CONV, f), F32), jax.ShapeDtypeStruct((FFN_CONV, f), F32),
                   jax.ShapeDtypeStruct((1, f), F32), jax.ShapeDtypeStruct((1, f), F32)],
        compiler_params=_params("parallel"), name="ffn_glu_bwd")(hg, hv, wg, wv, bg, bv, dact)


def _adamw(g, w, m, v):
    m = ADAM_B1 * m + (1.0 - ADAM_B1) * g
    v = ADAM_B2 * v + (1.0 - ADAM_B2) * (g * g)
    m_hat = m / (1.0 - ADAM_B1 ** ADAM_STEP)
    v_hat = v / (1.0 - ADAM_B2 ** ADAM_STEP)
    return -ADAM_LR * (m_hat / (jnp.sqrt(v_hat) + ADAM_EPS) + ADAM_WD * w), m, v


def adamw(g, w, m, v, name):
    r, c = g.shape
    tr = r
    for cand in (256, 128, 64, 32, 16, 8):
        if r % cand == 0 and cand * c * 4 <= (1 << 21):
            tr = cand
            break
    return ew(_adamw, [(g, "r"), (w, "r"), (m, "r"), (v, "r")], [((r, c), F32, "r")] * 3, gr=r // tr, name=name)


ANY = pl.BlockSpec(memory_space=pl.ANY)


def _place():
    x, y, c = lax.axis_index("x"), lax.axis_index("y"), lax.axis_index("c")
    return x, y, c, [(1 - x, y), (x, 1 - y), (1 - x, 1 - y)]


def _rcopy(src, dst, send, recv, k, to):
    return pltpu.make_async_remote_copy(src_ref=src, dst_ref=dst, send_sem=send.at[k], recv_sem=recv.at[k],
                                        device_id=to, device_id_type=MESH)


def gather_shards(bigs, smalls):
    nb, na = len(bigs), len(bigs) + len(smalls)
    arrays = list(bigs) + list(smalls)

    def body(*refs):
        ins, outs = refs[:na], refs[na:2 * na]
        send, recv = refs[2 * na:]
        x, y, c, chips = _place()
        me, sib = 2 * x + y, (x, y, 1 - c)

        def half(ref, a, which):
            rows = arrays[a].shape[0] // 2
            return ref.at[pl.ds(which * rows, rows)]

        started = []
        for a in range(na):
            for j, (cx, cy) in enumerate(chips):
                if a < nb:
                    cp = _rcopy(half(ins[a], a, c), half(outs[a].at[me], a, c), send, recv, 6 * a + j, (cx, cy, c))
                else:
                    cp = _rcopy(ins[a], outs[a].at[me], send, recv, 6 * nb + 3 * (a - nb) + j, (cx, cy, c))
                cp.start()
                started.append(cp)
        for a in range(nb):
            for j, (cx, cy) in enumerate(chips):
                landed = half(outs[a].at[2 * cx + cy], a, c)
                _rcopy(landed, landed, send, recv, 6 * a + j, (cx, cy, c)).wait_recv()
                cp = _rcopy(landed, landed, send, recv, 6 * a + 3 + j, sib)
                cp.start()
                started.append(cp)
        for a in range(na):
            for j, (cx, cy) in enumerate(chips):
                if a < nb:
                    dst = half(outs[a].at[2 * cx + cy], a, 1 - c)
                    _rcopy(dst, dst, send, recv, 6 * a + 3 + j, sib).wait_recv()
                else:
                    dst = outs[a].at[2 * cx + cy]
                    _rcopy(dst, dst, send, recv, 6 * nb + 3 * (a - nb) + j, (cx, cy, c)).wait_recv()
        for cp in started:
            cp.wait_send()

    n_sem = 6 * nb + 3 * (na - nb)
    gathered = pl.pallas_call(
        body, in_specs=[ANY] * na, out_specs=[ANY] * na,
        out_shape=[jax.ShapeDtypeStruct((N_CHIPS,) + a.shape, a.dtype) for a in arrays],
        scratch_shapes=[pltpu.SemaphoreType.DMA((n_sem,)), pltpu.SemaphoreType.DMA((n_sem,))],
        name="gather_shards")(*arrays)
    chip = 2 * lax.axis_index("x") + lax.axis_index("y")
    return [lax.dynamic_update_slice(g, a[None], (chip, 0, 0)) for g, a in zip(gathered, arrays)]


def sibling_halves(grads):
    na = len(grads)

    def body(*refs):
        ins, outs = refs[:na], refs[na:2 * na]
        send, recv = refs[2 * na:]
        x, y, c, _ = _place()
        cps = []
        for a in range(na):
            rows = grads[a].shape[1] // 2
            cp = _rcopy(ins[a].at[:, pl.ds((1 - c) * rows, rows)], outs[a], send, recv, a, (x, y, 1 - c))
            cp.start()
            cps.append(cp)
        for cp in cps:
            cp.wait()

    return pl.pallas_call(
        body, in_specs=[ANY] * na, out_specs=[ANY] * na,
        out_shape=[jax.ShapeDtypeStruct((g.shape[0], g.shape[1] // 2, g.shape[2]), g.dtype) for g in grads],
        scratch_shapes=[pltpu.SemaphoreType.DMA((na,)), pltpu.SemaphoreType.DMA((na,))], name="sibling_halves")(*grads)


def scatter_partials(parts):
    na = len(parts)

    def body(*refs):
        ins, outs = refs[:na], refs[na:2 * na]
        send, recv = refs[2 * na:]
        _, _, c, chips = _place()
        cps = []
        for a in range(na):
            for j, (cx, cy) in enumerate(chips):
                cp = _rcopy(ins[a].at[2 * cx + cy], outs[a].at[j], send, recv, 3 * a + j, (cx, cy, c))
                cp.start()
                cps.append(cp)
        for cp in cps:
            cp.wait()

    return pl.pallas_call(
        body, in_specs=[ANY] * na, out_specs=[ANY] * na,
        out_shape=[jax.ShapeDtypeStruct((3,) + p.shape[1:], p.dtype) for p in parts],
        scratch_shapes=[pltpu.SemaphoreType.DMA((3 * na,)), pltpu.SemaphoreType.DMA((3 * na,))], name="scatter_partials")(*parts)


def join_halves(halves):
    na = len(halves)

    def body(*refs):
        ins, outs = refs[:na], refs[na:2 * na]
        send, recv = refs[2 * na:]
        x, y, c, _ = _place()
        cps = []
        for a in range(na):
            cp = _rcopy(ins[a].at[c], outs[a].at[c], send, recv, a, (x, y, 1 - c))
            cp.start()
            cps.append(cp)
        for a in range(na):
            dst = outs[a].at[1 - c]
            _rcopy(dst, dst, send, recv, a, (x, y, 1 - c)).wait_recv()
        for cp in cps:
            cp.wait_send()

    return pl.pallas_call(
        body, in_specs=[ANY] * na, out_specs=[ANY] * na,
        out_shape=[jax.ShapeDtypeStruct(h.shape, h.dtype) for h in halves],
        input_output_aliases={a: a for a in range(na)},
        scratch_shapes=[pltpu.SemaphoreType.DMA((na,)), pltpu.SemaphoreType.DMA((na,))],
        name="join_halves")(*halves)


HBM = pl.BlockSpec(memory_space=pltpu.HBM)
SEM = pl.BlockSpec(memory_space=pltpu.SEMAPHORE)
SIDE_EFFECT = pltpu.SideEffectType.DATAFLOW_SIDE_EFFECTING


def split_start(name, srcs, land_shapes, n_copies, copies_fn, after):
    ns, nl = len(srcs), len(land_shapes)

    def body(*refs):
        ins, lands = refs[:ns], refs[ns:ns + nl]
        send, recv, token = refs[ns + nl + 1], refs[ns + nl + 2], refs[-1]
        for k, (src, dst, dev) in enumerate(copies_fn(ins, lands, False)):
            pltpu.make_async_remote_copy(src_ref=src, dst_ref=dst, send_sem=send.at[k], recv_sem=recv.at[k],
                                         device_id=dev, device_id_type=MESH).start()
        token[...] = jnp.zeros_like(token)

    outs = pl.pallas_call(
        body, name=name,
        out_shape=(pltpu.SemaphoreType.DMA((n_copies,)), pltpu.SemaphoreType.DMA((n_copies,)),
                   *[pltpu.HBM(a.shape, a.dtype) for a in srcs], *[pltpu.HBM(s, dt) for s, dt in land_shapes],
                   jax.ShapeDtypeStruct((8, LANE), F32)),
        in_specs=[HBM] * (ns + nl) + [ANY],
        out_specs=(SEM, SEM, *[HBM] * (ns + nl), pl.BlockSpec(memory_space=pltpu.VMEM)),
        input_output_aliases={i: 2 + i for i in range(ns + nl)},
        compiler_params=pltpu.CompilerParams(has_side_effects=SIDE_EFFECT),
    )(*[pltpu.with_memory_space_constraint(a, pltpu.HBM) for a in srcs],
      *[pltpu.with_memory_space_constraint(lax.empty(s, dt), pltpu.HBM) for s, dt in land_shapes], after)
    return outs[0], outs[1], list(outs[2:2 + ns]), list(outs[2 + ns:2 + ns + nl]), outs[-1]


def split_wait(name, started, n_copies, copies_fn, after):
    send, recv, srcs, lands, _ = started
    ns, nl = len(srcs), len(lands)

    def body(*refs):
        ins, lnd = refs[:ns], refs[ns:ns + nl]
        send_ref, recv_ref = refs[ns + nl], refs[ns + nl + 1]
        for k, (src, dst, dev) in enumerate(copies_fn(ins, lnd, True)):
            cp = pltpu.make_async_remote_copy(src_ref=src, dst_ref=dst, send_sem=send_ref.at[k], recv_sem=recv_ref.at[k],
                                              device_id=dev, device_id_type=MESH)
            cp.wait_send()
            cp.wait_recv()

    outs = pl.pallas_call(
        body, name=name,
        out_shape=tuple(pltpu.HBM(a.shape, a.dtype) for a in srcs + lands),
        in_specs=[HBM] * (ns + nl) + [SEM, SEM, ANY], out_specs=tuple([HBM] * (ns + nl)),
        input_output_aliases={i: i for i in range(ns + nl)},
        compiler_params=pltpu.CompilerParams(has_side_effects=SIDE_EFFECT),
    )(*srcs, *lands, send, recv, after)
    return list(outs[:ns]), list(outs[ns:])


def _gather_copies(ins, lands, waiting):
    x, y, c, chips = _place()
    return [(ins[a], lands[a].at[2 * cx + cy] if waiting else lands[a].at[2 * x + y], (cx, cy, c))
            for a in range(len(ins)) for cx, cy in chips]


def _scatter_copies(ins, lands, waiting):
    del waiting
    _, _, c, chips = _place()
    return [(ins[a].at[2 * cx + cy], lands[a].at[j], (cx, cy, c)) for a in range(len(ins)) for j, (cx, cy) in enumerate(chips)]


def allreduce_small(vec):
    r = vec.shape[0]

    def body(x_ref, sum_ref, all_ref, send, recv):
        x, y, c, _ = _place()
        me = 4 * x + 2 * y + c
        all_ref[me] = x_ref[...]
        cps, peers = [], []
        for mask in range(1, 8):
            px = 1 - x if mask & 4 else x
            py = 1 - y if mask & 2 else y
            pc = 1 - c if mask & 1 else c
            peers.append(4 * px + 2 * py + pc)
            cp = _rcopy(x_ref, all_ref.at[me], send, recv, mask - 1, (px, py, pc))
            cp.start()
            cps.append(cp)
        for k, cp in enumerate(cps):
            _rcopy(x_ref, all_ref.at[peers[k]], send, recv, k, (x, y, c)).wait_recv()
        for cp in cps:
            cp.wait_send()
        total = all_ref[0]
        for d in range(1, 8):
            total = total + all_ref[d]
        sum_ref[...] = total

    vm = pl.BlockSpec(memory_space=pltpu.VMEM)
    return pl.pallas_call(
        body, in_specs=[vm], out_specs=vm, out_shape=jax.ShapeDtypeStruct((r, LANE), F32),
        scratch_shapes=[pltpu.VMEM((8, r, LANE), F32), pltpu.SemaphoreType.DMA((7,)), pltpu.SemaphoreType.DMA((7,))],
        compiler_params=pltpu.CompilerParams(vmem_limit_bytes=VMEM_LIMIT), name="allreduce_small")(vec)


def _row_tile(rows):
    for cand in (256, 128, 64, 32, 16):
        if rows % cand == 0:
            return cand
    return rows


def add_sibling(grad, recv, c_idx):
    _, rows, cols = grad.shape
    hr = rows // 2
    tr = _row_tile(hr)
    nb = hr // tr

    def body(c_ref, g_ref, r_ref, o_ref):
        o_ref[...] = (g_ref[...].astype(F32) + r_ref[...].astype(F32)).astype(o_ref.dtype)

    return pl.pallas_call(
        body,
        grid_spec=pltpu.PrefetchScalarGridSpec(
            num_scalar_prefetch=1, grid=(N_CHIPS, nb),
            in_specs=[pl.BlockSpec((None, tr, cols), lambda k, r, c_ref: (k, c_ref[0] * nb + r, 0)),
                      pl.BlockSpec((None, tr, cols), lambda k, r, c_ref: (k, r, 0))],
            out_specs=pl.BlockSpec((None, tr, cols), lambda k, r, c_ref: (k, r, 0))),
        out_shape=jax.ShapeDtypeStruct((N_CHIPS, hr, cols), BF16),
        compiler_params=_params("parallel", "parallel"), name="add_sibling")(c_idx, grad, recv)


def sum_chips(part, others, place_idx):
    _, hr, cols = part.shape
    tr = _row_tile(hr)

    def body(k_ref, p_ref, o0_ref, o1_ref, o2_ref, out_ref):
        out_ref[...] = ((p_ref[...].astype(F32) + o0_ref[...].astype(F32)) + o1_ref[...].astype(F32)) + o2_ref[...].astype(F32)

    other = lambda j: pl.BlockSpec((None, tr, cols), lambda r, k_ref: (j, r, 0))
    return pl.pallas_call(
        body,
        grid_spec=pltpu.PrefetchScalarGridSpec(
            num_scalar_prefetch=1, grid=(hr // tr,),
            in_specs=[pl.BlockSpec((None, tr, cols), lambda r, k_ref: (k_ref[0], r, 0)), other(0), other(1), other(2)],
            out_specs=pl.BlockSpec((None, tr, cols), lambda r, k_ref: (k_ref[1], r, 0))),
        out_shape=jax.ShapeDtypeStruct((2, hr, cols), F32),
        compiler_params=_params("parallel"), name="sum_chips")(place_idx, part, others, others, others)


def _pad_lanes(a, width=LANE):
    return jnp.pad(a, ((0, 0), (0, width - a.shape[1])))


def _cols_from_shards(g):
    return jnp.transpose(g, (1, 0, 2)).reshape(g.shape[1], -1)


def _cols_to_shards(w):
    k, n4 = w.shape
    return jnp.transpose(w.reshape(k, N_CHIPS, n4 // N_CHIPS), (1, 0, 2))


def kernel(x, mem, positions, g_mix, w_in, g_qa, w_qb, g_kva, w_kvb, g_qn_nope, g_qn_pe, g_kn_nope, g_kn_pe, conv_qk, b_if, g_hnorm, p_a, p_b, w_out, g_cross, g_mem, wq_c, wk_c, wv_c, g_cq, g_ck, wo_c, g_ffn, w_up, conv_ffn, b_conv_ffn, w_down, loss_target, m_g_mix, m_w_in, m_g_qa, m_w_qb, m_g_kva, m_w_kvb, m_g_qn_nope, m_g_qn_pe, m_g_kn_nope, m_g_kn_pe, m_conv_qk, m_b_if, m_g_hnorm, m_p_a, m_p_b, m_w_out, m_g_cross, m_g_mem, m_wq_c, m_wk_c, m_wv_c, m_g_cq, m_g_ck, m_wo_c, m_g_ffn, m_w_up, m_conv_ffn, m_b_conv_ffn, m_w_down, v_g_mix, v_w_in, v_g_qa, v_w_qb, v_g_kva, v_w_kvb, v_g_qn_nope, v_g_qn_pe, v_g_kn_nope, v_g_kn_pe, v_conv_qk, v_b_if, v_g_hnorm, v_p_a, v_p_b, v_w_out, v_g_cross, v_g_mem, v_wq_c, v_wk_c, v_wv_c, v_g_cq, v_g_ck, v_wo_c, v_g_ffn, v_w_up, v_conv_ffn, v_b_conv_ffn, v_w_down):
    names = ["g_mix", "w_in", "g_qa", "w_qb", "g_kva", "w_kvb", "g_qn_nope", "g_qn_pe", "g_kn_nope", "g_kn_pe", "conv_qk",
             "b_if", "g_hnorm", "p_a", "p_b", "w_out", "g_cross", "g_mem", "wq_c", "wk_c", "wv_c", "g_cq", "g_ck", "wo_c",
             "g_ffn", "w_up", "conv_ffn", "b_conv_ffn", "w_down"]
    env = locals()
    wts = {n: env[n] for n in names}
    mom = {n: env["m_" + n] for n in names}
    var = {n: env["v_" + n] for n in names}

    xi, yi, ci = lax.axis_index("x"), lax.axis_index("y"), lax.axis_index("c")
    chip = 2 * xi + yi
    place_arr = jnp.stack([chip, ci]).astype(jnp.int32)
    c_arr = jnp.reshape(ci, (1,)).astype(jnp.int32)

    x2d, tgt, mem2d = x[0], loss_target[0], mem[0]
    t, d = x2d.shape
    mla_h = w_qb.shape[2] * N_CHIPS // (NOPE + ROPE)
    ml_h = b_if.shape[1] // 2
    cr_h = wq_c.shape[2] // CROSS_DH
    f_dim = w_down.shape[1] * N_CHIPS
    q_rank, kv_rank = g_qa.shape[1], g_kva.shape[1]
    qk_w, v_w = ml_h * MLSTM_DK, ml_h * MLSTM_DV
    nc = t // CHUNK

    big_names = ["w_in", "w_qb", "w_kvb", "p_a", "p_b", "w_out", "wq_c", "wk_c", "wv_c", "wo_c", "w_up", "w_down"]
    col_sharded = {"w_in", "w_qb", "w_kvb", "wo_c", "w_up"}
    small_sharded = ["conv_qk", "g_hnorm", "conv_ffn"]
    early_big, early_small = ["w_in", "w_qb", "w_kvb"], ["conv_qk", "g_hnorm"]
    late_names = [n for n in big_names if n not in early_big] + ["conv_ffn"]
    full = {}

    def unshard(n, g):
        full[n] = _cols_from_shards(g) if (n in col_sharded or n in small_sharded) else g.reshape(-1, g.shape[2])

    gathered = gather_shards([wts[n][0].astype(BF16) for n in early_big], [wts[n][0] for n in early_small])
    for n, g in zip(early_big + early_small, gathered):
        unshard(n, g)
    late_src = [wts[n][0].astype(BF16) if n in big_names else wts[n][0] for n in late_names]
    late = split_start("gather_late_start", late_src, [((N_CHIPS,) + a.shape, a.dtype) for a in late_src],
                       3 * len(late_src), _gather_copies, gathered[0])
    g_mix_fwd = g_mix + late[4][0:1, 0:1]

    o_qa, o_kv, o_kpe = 0, q_rank, q_rank + kv_rank
    o_q = o_kpe + ROPE
    o_v = o_q + 2 * qk_w
    o_if = o_v + v_w
    o_o = o_if + 2 * ml_h
    o_ga, o_gb = o_o + v_w, o_o + v_w + d
    wi = full["w_in"]
    pad_kpe = jnp.zeros((d, LANE - ROPE), BF16)
    pad_if = jnp.zeros((d, LANE - 2 * ml_h), BF16)
    w_small = jnp.concatenate([wi[:, o_qa:o_q], pad_kpe, wi[:, o_if:o_o], pad_if], axis=1)
    o_kpe_s, o_if_s = o_kpe, o_kpe + LANE
    w_qk, w_v, w_o, w_ga, w_gb = wi[:, o_q:o_v], wi[:, o_v:o_if], wi[:, o_o:o_ga], wi[:, o_ga:o_gb], wi[:, o_gb:]

    wq3 = full["w_qb"].reshape(q_rank, mla_h, NOPE + ROPE)
    wq_nope = wq3[:, :, :NOPE].reshape(q_rank, mla_h * NOPE)
    wq_pe = jnp.pad(wq3[:, :, NOPE:], ((0, 0), (0, 0), (0, LANE - ROPE))).reshape(q_rank, mla_h * LANE)
    wkv3 = full["w_kvb"].reshape(kv_rank, mla_h, NOPE + VHEAD)
    wk_nope = wkv3[:, :, :NOPE].reshape(kv_rank, mla_h * NOPE)
    wv_mla = wkv3[:, :, NOPE:].reshape(kv_rank, mla_h * VHEAD)

    inv_freq = ROPE_BASE ** (-jnp.arange(0, ROPE, 2, dtype=F32) / ROPE)
    ang = positions[0].astype(F32)[:, None] * inv_freq
    cos, sin = jnp.cos(ang), jnp.sin(ang)
    zero_h = jnp.zeros_like(cos)
    tabs = [_pad_lanes(jnp.concatenate([cos, cos], axis=1)), _pad_lanes(-sin), _pad_lanes(jnp.concatenate([zero_h, sin], axis=1))]
    mla_gains = [g_qn_nope, _pad_lanes(g_qn_pe), g_kn_nope, _pad_lanes(g_kn_pe)]

    u1 = rms_fwd(x2d, g_mix_fwd, "rms_mix")
    z_small = mm(u1, w_small, name="in_small")
    z_qa, z_kv = z_small[:, o_qa:o_kv], z_small[:, o_kv:o_kpe]
    z_kpe, z_if = z_small[:, o_kpe_s:o_kpe_s + LANE], z_small[:, o_if_s:o_if_s + 2 * ml_h]
    z_qk = mm(u1, w_qk, name="in_qk")
    z_v = mm(u1, w_v, name="in_v")
    z_o = mm(u1, w_o, name="in_o")
    z_ga = mm(u1, w_ga, name="in_ga")
    z_gb = mm(u1, w_gb, name="in_gb")

    qa_n = rms_fwd(z_qa, g_qa, "rms_qa")
    kv_n = rms_fwd(z_kv, g_kva, "rms_kva")
    qn_raw = mm(qa_n, wq_nope, name="q_nope")
    qp_raw = mm(qa_n, wq_pe, name="q_pe")
    kn_raw = mm(kv_n, wk_nope, name="k_nope")
    v_mla = mm(kv_n, wv_mla, out_dtype=BF16, name="v_mla")
    q_att, k_att = mla_prep_fwd(qn_raw, qp_raw, kn_raw, z_kpe, tabs, mla_gains, mla_h)
    y_a, lse_row = attn_fwd(q_att, k_att, jnp.transpose(v_mla), mla_h)

    colscale = jnp.concatenate([jnp.full((1, qk_w), MLSTM_DK ** -0.5, F32), jnp.ones((1, qk_w), F32)], axis=1)
    qk_c = conv_qk_fwd(z_qk, full["conv_qk"], colscale)
    gates4 = z_if.reshape(nc, CHUNK, 2, ml_h)
    gcol = jnp.transpose(gates4, (3, 0, 1, 2))
    grow = jnp.transpose(gates4, (3, 0, 2, 1))
    bias = jnp.transpose(b_if.reshape(2, ml_h), (1, 0)).reshape(ml_h, 1, 2)
    h_raw, c_all, n_all, m_all = mlstm_fwd(qk_c, z_v, gcol, grow, bias, ml_h)
    g_hn = full["g_hnorm"].reshape(1, v_w)
    hn_gr, hd_gr = t // _pick(t, ROW_TILE), t // _pick(t, HEAD_ROW_TILE)
    y_b = ew(lambda *a: (_hnorm_gate(*a),), [(h_raw, "rc"), (z_o, "rc"), (g_hn, "c")], [((t, v_w), BF16, "rc")], gr=hd_gr, gc=ml_h,
             name="hnorm_gate")[0]

    late_own, landed = split_wait("gather_late_wait", late, 3 * len(late_src), _gather_copies, y_b)
    for n, g, own in zip(late_names, landed, late_own):
        unshard(n, lax.dynamic_update_slice(g, own[None], (chip, 0, 0)))
    wup_g, wup_v = full["w_up"][:, :f_dim], full["w_up"][:, f_dim:]

    pa = mm(y_a, full["p_a"], name="proj_a")
    pb = mm(y_b, full["p_b"], name="proj_b")
    merge_fn = lambda ga, gb, a, b: (_sigmoid(ga) * a + _sigmoid(gb) * b,)
    merged = ew(merge_fn, [(z_ga, "r"), (z_gb, "r"), (pa, "r"), (pb, "r")], [((t, d), BF16, "r")], gr=hn_gr, name="merge")[0]
    x1 = mm(merged, full["w_out"], add=x2d, name="out_proj")

    uc = rms_fwd(x1, g_cross, "rms_cross")
    mem_n = rms_fwd(mem2d, g_mem, "rms_mem")
    qc = mm(uc, full["wq_c"], name="cross_q")
    kc = mm(mem_n, full["wk_c"], name="cross_k")
    vc = mm(mem_n, full["wv_c"], name="cross_v")
    oc = cross_fwd(qc, kc, vc, g_cq, g_ck, cr_h)
    x2 = mm(oc, full["wo_c"], add=x1, name="cross_out")

    u3 = rms_fwd(x2, g_ffn, "rms_ffn")
    hg = mm(u3, wup_g, name="ffn_up_gate")
    hv = mm(u3, wup_v, name="ffn_up_val")
    cw, cb = full["conv_ffn"], b_conv_ffn
    act = ffn_glu_fwd(hg, hv, cw[:, :f_dim], cw[:, f_dim:], cb[:, :f_dim], cb[:, f_dim:])
    y = mm(act, full["w_down"], add=x2, name="ffn_down")

    def loss_fn(y_, t_):
        err = y_ - t_
        part = jnp.sum(jnp.sum(err * err, axis=1, keepdims=True), axis=0, keepdims=True) * (0.5 / d)
        return err * (1.0 / d), err * (1.0 / d), jnp.broadcast_to(part, (1, LANE))

    dy, dy_mx, loss_part = ew(loss_fn, [(y, "r"), (tgt, "r")], [((t, d), F32, "r"), ((t, d), BF16, "r"), ((1, LANE), F32, "f")],
                              gr=hn_gr, name="loss")

    gw = {}
    gw["w_down"] = mm(act, dy_mx, ta=True, out_dtype=BF16, name="dw_down")
    dact = mm(dy_mx, full["w_down"], tb=True, name="d_act")
    dhg, dhv, dcw_g, dcw_v, dcb_g, dcb_v = ffn_glu_bwd(hg, hv, cw[:, :f_dim], cw[:, f_dim:], cb[:, :f_dim], cb[:, f_dim:], dact)
    gw["conv_ffn"] = jnp.concatenate([dcw_g, dcw_v], axis=1)
    gw["b_conv_ffn"] = jnp.concatenate([dcb_g, dcb_v], axis=1)
    dwup_g = mm(u3, dhg, ta=True, out_dtype=BF16, name="dw_up_gate")
    dwup_v = mm(u3, dhv, ta=True, out_dtype=BF16, name="dw_up_val")
    gw["w_up"] = jnp.concatenate([dwup_g, dwup_v], axis=1)

    def shard_major(n):
        return _cols_to_shards(gw[n]) if n in col_sharded else gw[n].reshape(N_CHIPS, -1, gw[n].shape[1])

    def chip_partials(group):
        grads_sm = [shard_major(n) for n in group]
        return [add_sibling(g, r, c_arr) for g, r in zip(grads_sm, sibling_halves(grads_sm))]

    def scatter_group(group, tag, after):
        parts_ = chip_partials(group)
        return split_start("scatter_start_" + tag, parts_, [((3,) + p.shape[1:], p.dtype) for p in parts_],
                           3 * len(parts_), _scatter_copies, after)

    group_a = ["w_up", "w_down"]
    started_a = scatter_group(group_a, "a", gw["w_up"])
    g_ffn_bwd = g_ffn + started_a[4][0:1, 0:1]
    du3 = mm(dhg, wup_g, tb=True, name="d_u3_gate")
    du3 = mm(dhv, wup_v, tb=True, add=du3, name="d_u3_val")
    dx2, gw["g_ffn"] = rms_bwd(x2, g_ffn_bwd, du3, dy, "rms_ffn_bwd")

    gw["wo_c"] = mm(oc, dx2, ta=True, out_dtype=BF16, name="dw_cross_out")
    doc = mm(dx2, full["wo_c"], tb=True, name="d_cross_o")
    dqc, dkc, dvc, gw["g_cq"], gw["g_ck"] = cross_bwd(qc, kc, vc, g_cq, g_ck, doc, cr_h)
    gw["wq_c"] = mm(uc, dqc, ta=True, out_dtype=BF16, name="dw_cross_q")
    gw["wk_c"] = mm(mem_n, dkc, ta=True, out_dtype=BF16, name="dw_cross_k")
    gw["wv_c"] = mm(mem_n, dvc, ta=True, out_dtype=BF16, name="dw_cross_v")
    duc = mm(dqc, full["wq_c"], tb=True, name="d_uc")
    dmem_n = mm(dkc, full["wk_c"], tb=True, name="d_mem_k")
    dmem_n = mm(dvc, full["wv_c"], tb=True, add=dmem_n, name="d_mem_v")
    _, gw["g_mem"] = rms_bwd(mem2d, g_mem, dmem_n, None, "rms_mem_bwd")
    dx1, gw["g_cross"] = rms_bwd(x1, g_cross, duc, dx2, "rms_cross_bwd")

    gw["w_out"] = mm(merged, dx1, ta=True, out_dtype=BF16, name="dw_out")
    dmerged = mm(dx1, full["w_out"], tb=True, name="d_merged")

    def merge_bwd(ga, gb, a, b, dm):
        _, pull = jax.vjp(lambda *args: merge_fn(*args)[0], ga, gb, a, b)
        return pull(dm)

    dz_ga, dz_gb, dpa, dpb = ew(merge_bwd, [(z_ga, "r"), (z_gb, "r"), (pa, "r"), (pb, "r"), (dmerged, "r")],
                                [((t, d), BF16, "r")] * 4, gr=hn_gr, name="merge_bwd")
    gw["p_a"] = mm(y_a, dpa, ta=True, out_dtype=BF16, name="dw_proj_a")
    gw["p_b"] = mm(y_b, dpb, ta=True, out_dtype=BF16, name="dw_proj_b")
    group_b = ["wo_c", "wq_c", "wk_c", "wv_c", "w_out", "p_a", "p_b"]
    started_b = scatter_group(group_b, "b", gw["p_b"])
    g_hn_bwd = g_hn + started_b[4][0:1, 0:1]
    dy_a = mm(dpa, full["p_a"], tb=True, name="d_ya")
    dy_b = mm(dpb, full["p_b"], tb=True, name="d_yb")

    def hnorm_bwd(h_, zo_, g_, dyb_):
        _, pull = jax.vjp(_hnorm_gate, h_, zo_, g_)
        return pull(dyb_)

    dh_raw, dz_o, dg_hn = ew(hnorm_bwd, [(h_raw, "rc"), (z_o, "rc"), (g_hn_bwd, "c"), (dy_b, "rc")],
                             [((t, v_w), F32, "rc"), ((t, v_w), BF16, "rc"), ((1, v_w), F32, "c")],
                             gr=hd_gr, gc=ml_h, order="cr", name="hnorm_gate_bwd")
    gw["g_hnorm"] = dg_hn.reshape(ml_h, MLSTM_DV)
    dq_m, dk_m, dz_v, dgcol, dgrow = mlstm_bwd(qk_c, z_v, gcol, grow, bias, c_all, n_all, m_all, dh_raw, ml_h)
    dgates4 = jnp.transpose(dgcol, (1, 2, 3, 0)) + jnp.transpose(dgrow, (1, 3, 2, 0))
    dz_if = dgates4.reshape(t, 2 * ml_h)
    gw["b_if"] = ew(lambda a: (jnp.sum(a, axis=0, keepdims=True),), [(dz_if, "r")], [((1, 2 * ml_h), F32, "f")],
                    gr=hn_gr, name="bias_if_bwd")[0]
    dz_qk, gw["conv_qk"] = conv_qk_bwd(z_qk, full["conv_qk"], colscale, dq_m, dk_m)

    dq_att, dk_att, dv_mla = attn_bwd(q_att, k_att, v_mla, y_a, dy_a, lse_row, mla_h)
    dqn_raw, dqp_raw, dkn_raw, dz_kpe, gw["g_qn_nope"], dg_qp, gw["g_kn_nope"], dg_kp = mla_prep_bwd(
        qn_raw, qp_raw, kn_raw, z_kpe, tabs, mla_gains, dq_att, dk_att, mla_h)
    gw["g_qn_pe"], gw["g_kn_pe"] = dg_qp[:, :ROPE], dg_kp[:, :ROPE]
    dwq_nope = mm(qa_n, dqn_raw, ta=True, out_dtype=BF16, name="dw_q_nope")
    dwq_pe = mm(qa_n, dqp_raw, ta=True, out_dtype=BF16, name="dw_q_pe")
    dwk_nope = mm(kv_n, dkn_raw, ta=True, out_dtype=BF16, name="dw_k_nope")
    dwv_mla = mm(kv_n, dv_mla, ta=True, out_dtype=BF16, name="dw_v_mla")
    dqa_n = mm(dqn_raw, wq_nope, tb=True, name="d_qa_nope")
    dqa_n = mm(dqp_raw, wq_pe, tb=True, add=dqa_n, name="d_qa_pe")
    dkv_n = mm(dkn_raw, wk_nope, tb=True, name="d_kv_nope")
    dkv_n = mm(dv_mla, wv_mla, tb=True, add=dkv_n, name="d_kv_v")
    dz_qa, gw["g_qa"] = rms_bwd(z_qa, g_qa, dqa_n, None, "rms_qa_bwd", BF16)
    dz_kv, gw["g_kva"] = rms_bwd(z_kv, g_kva, dkv_n, None, "rms_kva_bwd", BF16)
    gw["w_qb"] = jnp.concatenate([dwq_nope.reshape(q_rank, mla_h, NOPE), dwq_pe.reshape(q_rank, mla_h, LANE)[:, :, :ROPE]],
                                 axis=2).reshape(q_rank, -1)
    gw["w_kvb"] = jnp.concatenate([dwk_nope.reshape(kv_rank, mla_h, NOPE), dwv_mla.reshape(kv_rank, mla_h, VHEAD)],
                                  axis=2).reshape(kv_rank, -1)

    dz_small = jnp.concatenate([dz_qa, dz_kv, dz_kpe.astype(BF16), _pad_lanes(dz_if).astype(BF16)], axis=1)
    dw_small = mm(u1, dz_small, ta=True, out_dtype=BF16, name="dw_in_small")
    du1 = mm(dz_small, w_small, tb=True, name="d_u1_small")
    dw_segs = []
    for nm, dz, w_seg in (("qk", dz_qk, w_qk), ("v", dz_v, w_v), ("o", dz_o, w_o), ("ga", dz_ga, w_ga), ("gb", dz_gb, w_gb)):
        dw_segs.append(mm(u1, dz, ta=True, out_dtype=BF16, name="dw_in_" + nm))
        du1 = mm(dz, w_seg, tb=True, add=du1, name="d_u1_" + nm)
    gw["w_in"] = jnp.concatenate([dw_small[:, :o_kpe_s + ROPE], dw_segs[0], dw_segs[1],
                                  dw_small[:, o_if_s:o_if_s + 2 * ml_h], dw_segs[2], dw_segs[3], dw_segs[4]], axis=1)
    grad_x, gw["g_mix"] = rms_bwd(x2d, g_mix, du1, dx1, "rms_mix_bwd")

    group_c = ["w_in", "w_qb", "w_kvb"]
    parts_c = chip_partials(group_c)
    others_c = scatter_partials(parts_c)
    parts_a, others_a = split_wait("scatter_wait_a", started_a, 3 * len(group_a), _scatter_copies, grad_x)
    parts_b, others_b = split_wait("scatter_wait_b", started_b, 3 * len(group_b), _scatter_copies, grad_x)
    reduce_order = group_a + group_b + group_c
    halves = [sum_chips(p, o, place_arr) for p, o in zip(parts_a + parts_b + parts_c, others_a + others_b + list(others_c))]
    joined = join_halves(halves)
    big_grads = {n: j.reshape(-1, j.shape[2]) for n, j in zip(reduce_order, joined)}

    small_names = [n for n in names if n not in big_names]
    pieces = [loss_part]
    for n in small_names:
        flat = gw[n].reshape(1, -1)
        pieces.append(jnp.pad(flat, ((0, 0), (0, (-flat.shape[1]) % LANE))))
    packed = jnp.concatenate(pieces, axis=1)
    packed = jnp.pad(packed, ((0, 0), (0, (-packed.shape[1]) % (8 * LANE)))).reshape(-1, LANE)
    total = allreduce_small(packed).reshape(1, -1)
    loss = total[0, 0]
    small_grads, off = {}, LANE
    for n in small_names:
        size = gw[n].size
        g_full = total[:, off:off + size].reshape(gw[n].shape)
        off += size + (-size) % LANE
        if n in small_sharded:
            width = wts[n].shape[-1]
            g_full = lax.dynamic_slice_in_dim(g_full, chip * width, width, axis=g_full.ndim - 1)
        small_grads[n] = g_full.reshape(wts[n].shape[1:])

    grads, deltas, new_m, new_v = {}, {}, {}, {}
    for n in big_names:
        w2 = wts[n][0]
        grads[n] = big_grads[n]
        deltas[n], new_m[n], new_v[n] = adamw(big_grads[n], w2, mom[n][0], var[n][0], "adamw_" + n)

    def pack_small(tree):
        flat = jnp.concatenate([tree[n].reshape(1, -1) for n in small_names], axis=1)
        return jnp.pad(flat, ((0, 0), (0, (-flat.shape[1]) % (8 * LANE)))).reshape(8, -1)

    sg = pack_small(small_grads)
    sd, sm, sv = adamw(sg, pack_small({n: wts[n][0] for n in small_names}), pack_small({n: mom[n][0] for n in small_names}),
                       pack_small({n: var[n][0] for n in small_names}), "adamw_small")
    off = 0
    for n in small_names:
        size = small_grads[n].size
        shp = wts[n].shape[1:]
        grads[n] = small_grads[n]
        for dst, src in ((deltas, sd), (new_m, sm), (new_v, sv)):
            dst[n] = src.reshape(1, -1)[:, off:off + size].reshape(shp)
        off += size

    def out(tree):
        return [tree[n].reshape(wts[n].shape) for n in names]

    return (loss, grad_x.reshape(x.shape), *out(grads), *out(deltas), *out(new_m), *out(new_v))
```

```python
import functools
import math

import jax
import jax.numpy as jnp
from jax import lax
from jax.experimental import pallas as pl
from jax.experimental.pallas import tpu as pltpu

F32, BF16 = jnp.float32, jnp.bfloat16
MESH = pl.DeviceIdType.MESH

EPS = 1e-6
CHUNK = 64
LOG2_CHUNK = 6
NOPE, ROPE, VHEAD = 128, 64, 128
MLSTM_DK, MLSTM_DV, MLSTM_CONV = 128, 256, 4
MLSTM_HEADS_PER_STEP = 4
CROSS_DH = 128
FFN_CONV = 3
ROPE_BASE = 10000.0
LOG2_E = math.log2(math.e)
ADAM_LR, ADAM_B1, ADAM_B2, ADAM_EPS, ADAM_WD, ADAM_STEP = 0.001, 0.9, 0.999, 1e-08, 0.01, 10

LANE = 128
ROW_TILE = 256
HEAD_ROW_TILE = 1024
ATT_TILE = 512
MM_TILES = (1024, 1024, 2048)
VMEM_LIMIT = 56 * 1024 * 1024
N_CHIPS = 4

NN = ((1,), (0,))
NT = ((1,), (1,))
TN = ((0,), (0,))


def _pick(dim, pref):
    if dim <= pref:
        return dim
    for t in range(pref, 0, -LANE):
        if dim % t == 0:
            return t
    return dim


def _bdot(a, b, dims):
    return lax.dot_general(a.astype(BF16), b.astype(BF16), (dims, ((), ())), preferred_element_type=F32)


@jax.custom_vjp
def _dnn(a, b):
    return _bdot(a, b, NN)


_dnn.defvjp(lambda a, b: (_bdot(a, b, NN), (a, b)),
            lambda r, g: (_bdot(g, r[1], NT), _bdot(r[0], g, TN)))


@jax.custom_vjp
def _dnt(a, b):
    return _bdot(a, b, NT)


_dnt.defvjp(lambda a, b: (_bdot(a, b, NT), (a, b)),
            lambda r, g: (_bdot(g, r[1], NN), _bdot(g, r[0], TN)))


@jax.custom_vjp
def _dtn(a, b):
    return _bdot(a, b, TN)


_dtn.defvjp(lambda a, b: (_bdot(a, b, TN), (a, b)),
            lambda r, g: (_bdot(r[1], g, NT), _bdot(r[0], g, NN)))


@functools.partial(jax.custom_vjp, nondiff_argnums=(1,))
def _lane_roll(x, shift):
    return pltpu.roll(x, shift, 1)


_lane_roll.defvjp(lambda x, shift: (pltpu.roll(x, shift, 1), None),
                  lambda shift, _, g: (pltpu.roll(g, (LANE - shift) % LANE, 1),))


def _params(*sem):
    return pltpu.CompilerParams(dimension_semantics=sem, vmem_limit_bytes=VMEM_LIMIT)


def mm(a, b, *, ta=False, tb=False, add=None, out_dtype=F32, name):
    m_dim, k_dim = (a.shape[1], a.shape[0]) if ta else a.shape
    n_dim = b.shape[0] if tb else b.shape[1]
    assert k_dim == (b.shape[1] if tb else b.shape[0]), (name, a.shape, b.shape)
    tm, tn, tk = _pick(m_dim, MM_TILES[0]), _pick(n_dim, MM_TILES[1]), _pick(k_dim, MM_TILES[2])
    nk = k_dim // tk
    dims = ((0,) if ta else (1,), (1,) if tb else (0,))
    has_add = add is not None

    def body(*refs):
        a_ref, b_ref = refs[0], refs[1]
        c_ref = refs[2] if has_add else None
        o_ref = refs[3] if has_add else refs[2]
        prod = _bdot(a_ref[...], b_ref[...], dims)
        if nk == 1:
            o_ref[...] = (prod + c_ref[...].astype(F32) if has_add else prod).astype(o_ref.dtype)
            return
        acc = refs[-1]
        k = pl.program_id(2)

        @pl.when(k == 0)
        def _():
            acc[...] = prod + c_ref[...].astype(F32) if has_add else prod

        @pl.when(k > 0)
        def _():
            acc[...] += prod

        @pl.when(k == nk - 1)
        def _():
            o_ref[...] = acc[...].astype(o_ref.dtype)

    in_specs = [
        pl.BlockSpec((tk, tm), lambda i, j, k: (k, i)) if ta else pl.BlockSpec((tm, tk), lambda i, j, k: (i, k)),
        pl.BlockSpec((tn, tk), lambda i, j, k: (j, k)) if tb else pl.BlockSpec((tk, tn), lambda i, j, k: (k, j)),
    ]
    args = [a, b]
    if has_add:
        in_specs.append(pl.BlockSpec((tm, tn), lambda i, j, k: (i, j)))
        args.append(add)
    return pl.pallas_call(
        body, grid=(m_dim // tm, n_dim // tn, nk), in_specs=in_specs,
        out_specs=pl.BlockSpec((tm, tn), lambda i, j, k: (i, j)),
        out_shape=jax.ShapeDtypeStruct((m_dim, n_dim), out_dtype),
        scratch_shapes=[pltpu.VMEM((tm, tn), F32)] if nk > 1 else [],
        compiler_params=_params("parallel", "parallel", "arbitrary"), name=name)(*args)


def ew(fn, ins, outs, *, gr, gc=1, order="rc", name):
    n_in = len(ins)

    def block(shape, kind):
        r, c = shape
        return (r // gr if kind in ("rc", "r") else r, c // gc if kind in ("rc", "c") else c)

    def imap(kind):
        def f(p0, p1):
            i, j = (p0, p1) if order == "rc" else (p1, p0)
            return {"rc": (i, j), "r": (i, 0), "c": (0, j), "f": (0, 0)}[kind]
        return f

    def body(*refs):
        p0, p1 = pl.program_id(0), pl.program_id(1)
        i, j = (p0, p1) if order == "rc" else (p1, p0)
        vals = fn(*[r[...] for r in refs[:n_in]])
        for ref, val, (_, dtype, kind) in zip(refs[n_in:], vals, outs):
            first = {"rc": None, "r": (j == 0) if gc > 1 else None, "c": (i == 0) if gr > 1 else None,
                     "f": ((i == 0) & (j == 0)) if gr * gc > 1 else None}[kind]
            _store(ref, val.astype(dtype), first)

    grid = (gr, gc) if order == "rc" else (gc, gr)
    return pl.pallas_call(
        body, grid=grid,
        in_specs=[pl.BlockSpec(block(a.shape, k), imap(k)) for a, k in ins],
        out_specs=[pl.BlockSpec(block(s, k), imap(k)) for s, _, k in outs],
        out_shape=[jax.ShapeDtypeStruct(s, d) for s, d, _ in outs],
        compiler_params=_params("arbitrary", "arbitrary"), name=name)(*[a for a, _ in ins])


def _store(ref, val, first):
    if first is None:
        ref[...] = val
        return

    @pl.when(first)
    def _():
        ref[...] = val

    @pl.when(jnp.logical_not(first))
    def _():
        ref[...] += val


def _f32(*xs):
    return [x.astype(F32) for x in xs]


def _rms(x, g, n):
    ms = jnp.sum(x * x, axis=-1, keepdims=True) * (1.0 / n)
    return x * lax.rsqrt(ms + EPS) * g


def _sigmoid(x):
    return 1.0 / (1.0 + jnp.exp(-x))


def _silu(x):
    return x * _sigmoid(x)


def _log_sigmoid(x):
    return jnp.minimum(x, 0.0) - jnp.log(1.0 + jnp.exp(-jnp.abs(x)))


def rms_fwd(x, g, name, out_dtype=BF16):
    t, w = x.shape
    return ew(lambda x_, g_: (_rms(x_, g_, w),), [(x, "r"), (g, "f")], [((t, w), out_dtype, "r")],
              gr=t // _pick(t, ROW_TILE), name=name)[0]


def rms_bwd(x, g, du, res, name, out_dtype=F32):
    t, w = x.shape

    def fn(x_, g_, du_, *res_):
        _, pull = jax.vjp(lambda a, b: _rms(a, b, w), x_, g_)
        dx, dg = pull(du_.astype(F32))
        return (dx + res_[0] if res_ else dx), dg

    ins = [(x, "r"), (g, "f"), (du, "r")] + ([(res, "r")] if res is not None else [])
    return ew(fn, ins, [((t, w), out_dtype, "r"), ((1, w), F32, "f")], gr=t // _pick(t, ROW_TILE), name=name)


def _rope(x, cos_t, sin_lo, sin_hi):
    return x * cos_t + _lane_roll(x, LANE - ROPE // 2) * sin_lo + _lane_roll(x, ROPE // 2) * sin_hi


def _mla_prep(qn, qp, kn, kp, cos_t, sin_lo, sin_hi, g_qn, g_qp, g_kn, g_kp):
    q = jnp.concatenate([_rms(qn, g_qn, NOPE), _rope(_rms(qp, g_qp, ROPE), cos_t, sin_lo, sin_hi)], axis=1)
    k = jnp.concatenate([_rms(kn, g_kn, NOPE), _rope(_rms(kp, g_kp, ROPE), cos_t, sin_lo, sin_hi)], axis=1)
    return q, k


def mla_prep_fwd(qn, qp, kn, kp, tabs, gains, heads):
    t = qn.shape[0]
    ins = [(qn, "rc"), (qp, "rc"), (kn, "rc"), (kp, "r")] + [(a, "r") for a in tabs] + [(g, "f") for g in gains]
    return ew(lambda *a: _mla_prep(*_f32(*a)), ins,
              [((t, heads * 2 * LANE), BF16, "rc"), ((t, heads * 2 * LANE), BF16, "rc")],
              gr=t // _pick(t, HEAD_ROW_TILE), gc=heads, name="mla_prep_fwd")


def mla_prep_bwd(qn, qp, kn, kp, tabs, gains, dq, dk, heads):
    t = qn.shape[0]

    def fn(qn_, qp_, kn_, kp_, c_, s1_, s2_, g1, g2, g3, g4, dq_, dk_):
        _, pull = jax.vjp(lambda a, b, c, d, e, f, g, h: _mla_prep(a, b, c, d, c_, s1_, s2_, e, f, g, h),
                          qn_, qp_, kn_, kp_, g1, g2, g3, g4)
        return pull((dq_, dk_))

    ins = ([(qn, "rc"), (qp, "rc"), (kn, "rc"), (kp, "r")] + [(a, "r") for a in tabs] + [(g, "f") for g in gains]
           + [(dq, "rc"), (dk, "rc")])
    hw = heads * LANE
    outs = [((t, hw), BF16, "rc"), ((t, hw), BF16, "rc"), ((t, hw), BF16, "rc"), ((t, LANE), F32, "r")] \
        + [((1, LANE), F32, "f")] * 4
    return ew(fn, ins, outs, gr=t // _pick(t, HEAD_ROW_TILE), gc=heads, name="mla_prep_bwd")


def _chunk_mask(row0, col0, shape, rows_are_queries):
    r = jnp.right_shift(row0 + lax.broadcasted_iota(jnp.int32, shape, 0), LOG2_CHUNK)
    c = jnp.right_shift(col0 + lax.broadcasted_iota(jnp.int32, shape, 1), LOG2_CHUNK)
    return (c <= r) if rows_are_queries else (r <= c)


def _block_pairs(nq, queries_outer):
    if queries_outer:
        pairs = [(i, j) for i in range(nq) for j in range(i + 1)]
    else:
        pairs = [(i, j) for j in range(nq) for i in range(j, nq)]
    return jnp.asarray([p[0] for p in pairs], jnp.int32), jnp.asarray([p[1] for p in pairs], jnp.int32)


def attn_fwd(q, k, vt, heads):
    t = q.shape[0]
    tq = _pick(t, ATT_TILE)
    qi, kj = _block_pairs(t // tq, True)
    scale = (NOPE + ROPE) ** -0.5
    scale2 = scale * LOG2_E

    def body(qi_ref, kj_ref, q_ref, k_ref, vt_ref, o_ref, lse_ref, m_s, l_s, acc):
        p = pl.program_id(1)
        i, j = qi_ref[p], kj_ref[p]

        @pl.when(j == 0)
        def _():
            m_s[...] = jnp.full_like(m_s, -jnp.inf)
            l_s[...] = jnp.zeros_like(l_s)
            acc[...] = jnp.zeros_like(acc)

        def step(diagonal):
            st = _bdot(k_ref[...], q_ref[...], NT)
            if diagonal:
                st = jnp.where(_chunk_mask(0, 0, (tq, tq), False), st, -jnp.inf)
            m_new = jnp.maximum(m_s[...], jnp.max(st, axis=0, keepdims=True))
            alpha = jnp.exp2((m_s[...] - m_new) * scale2)
            pt = jnp.exp2((st - m_new) * scale2)
            l_s[...] = alpha * l_s[...] + jnp.sum(pt, axis=0, keepdims=True)
            acc[...] = alpha * acc[...] + _bdot(vt_ref[...], pt, NN)
            m_s[...] = m_new

        pl.when(j < i)(functools.partial(step, False))

        @pl.when(j == i)
        def _():
            step(True)
            o_ref[...] = jnp.transpose(acc[...] / l_s[...])
            lse_ref[...] = m_s[...] * scale + jnp.log(l_s[...])

    return pl.pallas_call(
        body,
        grid_spec=pltpu.PrefetchScalarGridSpec(
            num_scalar_prefetch=2, grid=(heads, qi.shape[0]),
            in_specs=[pl.BlockSpec((tq, 2 * LANE), lambda h, p, qi_, kj_: (qi_[p], h)),
                      pl.BlockSpec((tq, 2 * LANE), lambda h, p, qi_, kj_: (kj_[p], h)),
                      pl.BlockSpec((VHEAD, tq), lambda h, p, qi_, kj_: (h, kj_[p]))],
            out_specs=[pl.BlockSpec((tq, VHEAD), lambda h, p, qi_, kj_: (qi_[p], h)),
                       pl.BlockSpec((None, 1, tq), lambda h, p, qi_, kj_: (h, 0, qi_[p]))],
            scratch_shapes=[pltpu.VMEM((1, tq), F32), pltpu.VMEM((1, tq), F32), pltpu.VMEM((VHEAD, tq), F32)]),
        out_shape=[jax.ShapeDtypeStruct((t, heads * VHEAD), F32), jax.ShapeDtypeStruct((heads, 1, t), F32)],
        compiler_params=_params("parallel", "arbitrary"), name="mla_attn_fwd")(qi, kj, q, k, vt)


def attn_bwd(q, k, v, o, do, lse_row, heads):
    t = q.shape[0]
    tq = _pick(t, ATT_TILE)
    qi, kj = _block_pairs(t // tq, False)
    scale = (NOPE + ROPE) ** -0.5
    scale2 = scale * LOG2_E

    def body(qi_ref, kj_ref, q_ref, k_ref, v_ref, o_ref, do_ref, lse_ref, dq_ref, dk_ref, dv_ref):
        p = pl.program_id(1)
        i, j = qi_ref[p], kj_ref[p]

        @pl.when(p == 0)
        def _():
            dq_ref[...] = jnp.zeros_like(dq_ref)

        @pl.when(i == j)
        def _():
            dk_ref[...] = jnp.zeros_like(dk_ref)
            dv_ref[...] = jnp.zeros_like(dv_ref)

        def step(diagonal):
            do_i = do_ref[...]
            prod = do_i * o_ref[...]
            hi = prod.astype(BF16)
            mid = (prod - hi.astype(F32)).astype(BF16)
            lo = (prod - hi.astype(F32) - mid.astype(F32)).astype(BF16)
            ones = jnp.ones((8, VHEAD), BF16)
            delta = (_bdot(ones, hi, NT) + _bdot(ones, mid, NT) + _bdot(ones, lo, NT))[0:1, :]
            st = _bdot(k_ref[...], q_ref[...], NT)
            pt = jnp.exp2(st * scale2 - lse_ref[...] * LOG2_E)
            if diagonal:
                pt = jnp.where(_chunk_mask(0, 0, (tq, tq), False), pt, 0.0)
            dv_ref[...] += _bdot(pt, do_i, NN)
            dpt = _bdot(v_ref[...], do_i, NT)
            dst = pt * (dpt - delta) * scale
            dk_ref[...] += _bdot(dst, q_ref[...], NN)
            rows = pl.ds(pl.multiple_of(i * tq, tq), tq)
            dq_ref[rows, :] += _bdot(dst, k_ref[...], TN)

        pl.when(i > j)(functools.partial(step, False))
        pl.when(i == j)(functools.partial(step, True))

    qmap = lambda h, p, qi_, kj_: (qi_[p], h)
    kmap = lambda h, p, qi_, kj_: (kj_[p], h)
    return pl.pallas_call(
        body,
        grid_spec=pltpu.PrefetchScalarGridSpec(
            num_scalar_prefetch=2, grid=(heads, qi.shape[0]),
            in_specs=[pl.BlockSpec((tq, 2 * LANE), qmap), pl.BlockSpec((tq, 2 * LANE), kmap),
                      pl.BlockSpec((tq, VHEAD), kmap), pl.BlockSpec((tq, VHEAD), qmap), pl.BlockSpec((tq, VHEAD), qmap),
                      pl.BlockSpec((None, 1, tq), lambda h, p, qi_, kj_: (h, 0, qi_[p]))],
            out_specs=[pl.BlockSpec((t, 2 * LANE), lambda h, p, qi_, kj_: (0, h)),
                       pl.BlockSpec((tq, 2 * LANE), kmap), pl.BlockSpec((tq, VHEAD), kmap)]),
        out_shape=[jax.ShapeDtypeStruct((t, heads * 2 * LANE), F32), jax.ShapeDtypeStruct((t, heads * 2 * LANE), F32),
                   jax.ShapeDtypeStruct((t, heads * VHEAD), F32)],
        compiler_params=_params("parallel", "arbitrary"), name="mla_attn_bwd")(qi, kj, q, k, v, o, do, lse_row)


def _shift_down(x, s):
    if s == 0:
        return x
    rows = lax.broadcasted_iota(jnp.int32, x.shape, 0)
    return jnp.where(rows >= s, pltpu.roll(x, s, 0), 0.0)


def _shift_up(x, s):
    if s == 0:
        return x
    t = x.shape[0]
    rows = lax.broadcasted_iota(jnp.int32, x.shape, 0)
    return jnp.where(rows < t - s, pltpu.roll(x, t - s, 0), 0.0)


def _conv(x, w_ref, width):
    return sum(_shift_down(x, width - 1 - j) * w_ref[j:j + 1, :] for j in range(width))


def _conv_bwd(x, dpre, w_ref, dw_ref, width):
    dx = sum(_shift_up(dpre, width - 1 - j) * w_ref[j:j + 1, :] for j in range(width))
    for j in range(width):
        dw_ref[j:j + 1, :] = jnp.sum(dpre * _shift_down(x, width - 1 - j), axis=0, keepdims=True)
    return dx


def _dsilu(z):
    s = _sigmoid(z)
    return s * (1.0 + z * (1.0 - s))


def conv_qk_fwd(x, w, colscale):
    t, c = x.shape
    tc = _pick(c, 256)

    def body(x_ref, w_ref, s_ref, o_ref):
        o_ref[...] = (_silu(_conv(x_ref[...], w_ref, MLSTM_CONV)) * s_ref[...]).astype(o_ref.dtype)

    return pl.pallas_call(
        body, grid=(c // tc,),
        in_specs=[pl.BlockSpec((t, tc), lambda j: (0, j)), pl.BlockSpec((MLSTM_CONV, tc), lambda j: (0, j)),
                  pl.BlockSpec((1, tc), lambda j: (0, j))],
        out_specs=pl.BlockSpec((t, tc), lambda j: (0, j)), out_shape=jax.ShapeDtypeStruct((t, c), BF16),
        compiler_params=_params("parallel"), name="conv_qk_fwd")(x, w, colscale)


def conv_qk_bwd(x, w, colscale, dq, dk):
    t, c = x.shape
    tc = _pick(c // 2, 256)
    half = (c // 2) // tc

    def body(x_ref, w_ref, s_ref, dq_ref, dk_ref, dx_ref, dw_ref):
        j = pl.program_id(0)
        x_ = x_ref[...]
        dy = jnp.where(j < half, dq_ref[...], dk_ref[...])
        dpre = dy * s_ref[...] * _dsilu(_conv(x_, w_ref, MLSTM_CONV))
        dx_ref[...] = _conv_bwd(x_, dpre, w_ref, dw_ref, MLSTM_CONV).astype(dx_ref.dtype)

    return pl.pallas_call(
        body, grid=(c // tc,),
        in_specs=[pl.BlockSpec((t, tc), lambda j: (0, j)), pl.BlockSpec((MLSTM_CONV, tc), lambda j: (0, j)),
                  pl.BlockSpec((1, tc), lambda j: (0, j)),
                  pl.BlockSpec((t, tc), lambda j: (0, jnp.minimum(j, half - 1))),
                  pl.BlockSpec((t, tc), lambda j: (0, jnp.maximum(j - half, 0)))],
        out_specs=[pl.BlockSpec((t, tc), lambda j: (0, j)), pl.BlockSpec((MLSTM_CONV, tc), lambda j: (0, j))],
        out_shape=[jax.ShapeDtypeStruct((t, c), BF16), jax.ShapeDtypeStruct((MLSTM_CONV, c), F32)],
        compiler_params=_params("parallel"), name="conv_qk_bwd")(x, w, colscale, dq, dk)


def _mlstm_chunk(q, k, v, i_col, i_row, f_col, f_row, c_mat, n_vec, m):
    shape = (CHUNK, CHUNK)
    r = lax.broadcasted_iota(jnp.int32, shape, 0)
    c = lax.broadcasted_iota(jnp.int32, shape, 1)
    tril = c <= r
    lf_col, lf_row = _log_sigmoid(f_col), _log_sigmoid(f_row)
    bc_col = jnp.sum(jnp.where(tril, lf_row, 0.0), axis=1, keepdims=True)
    bc_row = jnp.sum(jnp.where(r <= c, lf_col, 0.0), axis=0, keepdims=True)
    logw = jnp.where(tril, bc_col - bc_row + i_row, -jnp.inf)
    inter = bc_col + m
    m_t = lax.stop_gradient(jnp.maximum(inter, jnp.max(logw, axis=1, keepdims=True)))
    w_intra = jnp.exp(logw - m_t)
    w_inter = jnp.exp(inter - m_t)
    sc = _dnt(q, k) * w_intra
    num = w_inter * _dnn(q, c_mat) + _dnn(sc, v)
    den = w_inter * jnp.sum(q * n_vec, axis=1, keepdims=True) + jnp.sum(sc, axis=1, keepdims=True)
    h = num / jnp.maximum(jnp.abs(den), jnp.exp(-m_t))
    b_last = jnp.sum(lf_row, axis=1, keepdims=True)
    m_new = lax.stop_gradient(jnp.maximum(b_last + m, jnp.max(b_last - bc_row + i_row, axis=1, keepdims=True)))
    decay = jnp.exp(b_last + m - m_new)
    uk = jnp.exp(b_last - bc_col + i_col - m_new) * k
    return h, decay * c_mat + _dtn(uk, v), decay * n_vec + jnp.sum(uk, axis=0, keepdims=True), m_new


def _mlstm_group(heads):
    return MLSTM_HEADS_PER_STEP if heads % MLSTM_HEADS_PER_STEP == 0 else 1


def _mlstm_specs(heads, grp, rev, nc):
    ci = (lambda c: nc - 1 - c) if rev else (lambda c: c)
    return dict(
        q=pl.BlockSpec((CHUNK, grp * MLSTM_DK), lambda g, c: (ci(c), g)),
        k=pl.BlockSpec((CHUNK, grp * MLSTM_DK), lambda g, c: (ci(c), heads // grp + g)),
        v=pl.BlockSpec((CHUNK, grp * MLSTM_DV), lambda g, c: (ci(c), g)),
        gc=pl.BlockSpec((grp, None, CHUNK, 2), lambda g, c: (g, ci(c), 0, 0)),
        gr=pl.BlockSpec((grp, None, 2, CHUNK), lambda g, c: (g, ci(c), 0, 0)),
        b=pl.BlockSpec((grp, 1, 2), lambda g, c: (g, 0, 0)),
        cm=pl.BlockSpec((grp, None, MLSTM_DK, MLSTM_DV), lambda g, c: (g, ci(c), 0, 0)),
        vec=pl.BlockSpec((grp, None, 1, LANE), lambda g, c: (g, ci(c), 0, 0)),
    )


def _gates(gc_ref, gr_ref, b_ref, s):
    bi, bf = b_ref[s, :, 0:1], b_ref[s, :, 1:2]
    return gc_ref[s, :, 0:1] + bi, gr_ref[s, 0:1, :] + bi, gc_ref[s, :, 1:2] + bf, gr_ref[s, 1:2, :] + bf


def mlstm_fwd(qk, v, gcol, grow, bias, heads):
    t = qk.shape[0]
    nc = t // CHUNK
    grp = _mlstm_group(heads)
    sp = _mlstm_specs(heads, grp, False, nc)

    def body(q_ref, k_ref, v_ref, gc_ref, gr_ref, b_ref, h_ref, c_ref, n_ref, m_ref, c_s, n_s, m_s):
        @pl.when(pl.program_id(1) == 0)
        def _():
            c_s[...] = jnp.zeros_like(c_s)
            n_s[...] = jnp.zeros_like(n_s)
            m_s[...] = jnp.zeros_like(m_s)

        c_ref[...] = c_s[...]
        n_ref[...] = n_s[...]
        m_ref[...] = m_s[...]
        for s in range(grp):
            qs, vs = slice(s * MLSTM_DK, (s + 1) * MLSTM_DK), slice(s * MLSTM_DV, (s + 1) * MLSTM_DV)
            q, k, v_ = _f32(q_ref[:, qs], k_ref[:, qs], v_ref[:, vs])
            h, c_new, n_new, m_new = _mlstm_chunk(q, k, v_, *_gates(gc_ref, gr_ref, b_ref, s), c_s[s], n_s[s], m_s[s, :, 0:1])
            h_ref[:, vs] = h
            c_s[s] = c_new
            n_s[s] = n_new
            m_s[s] = jnp.broadcast_to(m_new, (1, LANE))

    return pl.pallas_call(
        body, grid=(heads // grp, nc),
        in_specs=[sp["q"], sp["k"], sp["v"], sp["gc"], sp["gr"], sp["b"]],
        out_specs=[sp["v"], sp["cm"], sp["vec"], sp["vec"]],
        out_shape=[jax.ShapeDtypeStruct((t, heads * MLSTM_DV), F32),
                   jax.ShapeDtypeStruct((heads, nc, MLSTM_DK, MLSTM_DV), F32),
                   jax.ShapeDtypeStruct((heads, nc, 1, LANE), F32), jax.ShapeDtypeStruct((heads, nc, 1, LANE), F32)],
        scratch_shapes=[pltpu.VMEM((grp, MLSTM_DK, MLSTM_DV), F32), pltpu.VMEM((grp, 1, LANE), F32),
                        pltpu.VMEM((grp, 1, LANE), F32)],
        compiler_params=_params("parallel", "arbitrary"), name="mlstm_fwd")(qk, qk, v, gcol, grow, bias)


def mlstm_bwd(qk, v, gcol, grow, bias, c_all, n_all, m_all, dh, heads):
    t = qk.shape[0]
    nc = t // CHUNK
    grp = _mlstm_group(heads)
    sp = _mlstm_specs(heads, grp, True, nc)

    def body(q_ref, k_ref, v_ref, gc_ref, gr_ref, b_ref, c_ref, n_ref, m_ref, dh_ref,
             dq_ref, dk_ref, dv_ref, dgc_ref, dgr_ref, dc_s, dn_s):
        @pl.when(pl.program_id(1) == 0)
        def _():
            dc_s[...] = jnp.zeros_like(dc_s)
            dn_s[...] = jnp.zeros_like(dn_s)

        for s in range(grp):
            qs, vs = slice(s * MLSTM_DK, (s + 1) * MLSTM_DK), slice(s * MLSTM_DV, (s + 1) * MLSTM_DV)
            q, k, v_ = _f32(q_ref[:, qs], k_ref[:, qs], v_ref[:, vs])
            m = m_ref[s, :, 0:1]
            _, pull = jax.vjp(lambda *a: _mlstm_chunk(*a, m)[:3], q, k, v_, *_gates(gc_ref, gr_ref, b_ref, s),
                              c_ref[s], n_ref[s])
            dq, dk, dv, di_col, di_row, df_col, df_row, dc, dn = pull((dh_ref[:, vs], dc_s[s], dn_s[s]))
            dq_ref[:, qs] = dq
            dk_ref[:, qs] = dk
            dv_ref[:, vs] = dv.astype(dv_ref.dtype)
            dgc_ref[s, :, 0:1] = di_col
            dgc_ref[s, :, 1:2] = df_col
            dgr_ref[s, 0:1, :] = di_row
            dgr_ref[s, 1:2, :] = df_row
            dc_s[s] = dc
            dn_s[s] = dn

    qspec = pl.BlockSpec((CHUNK, grp * MLSTM_DK), lambda g, c: (nc - 1 - c, g))
    return pl.pallas_call(
        body, grid=(heads // grp, nc),
        in_specs=[sp["q"], sp["k"], sp["v"], sp["gc"], sp["gr"], sp["b"], sp["cm"], sp["vec"], sp["vec"], sp["v"]],
        out_specs=[qspec, qspec, sp["v"], sp["gc"], sp["gr"]],
        out_shape=[jax.ShapeDtypeStruct((t, heads * MLSTM_DK), F32), jax.ShapeDtypeStruct((t, heads * MLSTM_DK), F32),
                   jax.ShapeDtypeStruct((t, heads * MLSTM_DV), BF16),
                   jax.ShapeDtypeStruct(gcol.shape, F32), jax.ShapeDtypeStruct(grow.shape, F32)],
        scratch_shapes=[pltpu.VMEM((grp, MLSTM_DK, MLSTM_DV), F32), pltpu.VMEM((grp, 1, LANE), F32)],
        compiler_params=_params("parallel", "arbitrary"),
        name="mlstm_bwd")(qk, qk, v, gcol, grow, bias, c_all, n_all, m_all, dh)


def _hnorm_gate(h, zo, g):
    return _rms(h, g, MLSTM_DV) * _sigmoid(zo)


def _cross_core(q, k, v, g_q, g_k, heads):
    scale = CROSS_DH ** -0.5
    outs = []
    for h in range(heads):
        s = _dnt(_rms(q[h], g_q, CROSS_DH), _rms(k[h], g_k, CROSS_DH)) * scale
        p = jnp.exp(s - lax.stop_gradient(jnp.max(s, axis=1, keepdims=True)))
        p = p / jnp.sum(p, axis=1, keepdims=True)
        outs.append(_dnn(p, v[h]))
    return jnp.concatenate(outs, axis=1)


def _split_heads(ref, heads):
    return [ref[:, h * CROSS_DH:(h + 1) * CROSS_DH].astype(F32) for h in range(heads)]


def cross_fwd(q, k, v, g_q, g_k, heads):
    t = q.shape[0]
    tm = _pick(t, ATT_TILE)
    full = lambda a: pl.BlockSpec(a.shape, lambda i: (0, 0))

    def body(q_ref, k_ref, v_ref, gq_ref, gk_ref, o_ref):
        o_ref[...] = _cross_core(_split_heads(q_ref, heads), _split_heads(k_ref, heads), _split_heads(v_ref, heads),
                                 gq_ref[...], gk_ref[...], heads).astype(o_ref.dtype)

    return pl.pallas_call(
        body, grid=(t // tm,), in_specs=[pl.BlockSpec((tm, q.shape[1]), lambda i: (i, 0)), full(k), full(v), full(g_q), full(g_k)],
        out_specs=pl.BlockSpec((tm, q.shape[1]), lambda i: (i, 0)), out_shape=jax.ShapeDtypeStruct(q.shape, BF16),
        compiler_params=_params("parallel"), name="cross_fwd")(q, k, v, g_q, g_k)


def cross_bwd(q, k, v, g_q, g_k, do, heads):
    t, w = q.shape
    tm = _pick(t, ATT_TILE)
    full = lambda a: pl.BlockSpec(a.shape, lambda i: (0, 0))

    def body(q_ref, k_ref, v_ref, gq_ref, gk_ref, do_ref, dq_ref, dk_ref, dv_ref, dgq_ref, dgk_ref):
        qs, ks, vs = _split_heads(q_ref, heads), _split_heads(k_ref, heads), _split_heads(v_ref, heads)
        _, pull = jax.vjp(lambda a, b, c, d, e: _cross_core(a, b, c, d, e, heads), qs, ks, vs, gq_ref[...], gk_ref[...])
        dqs, dks, dvs, dgq, dgk = pull(do_ref[...])
        first = pl.program_id(0) == 0
        for h in range(heads):
            cols = slice(h * CROSS_DH, (h + 1) * CROSS_DH)
            dq_ref[:, cols] = dqs[h].astype(dq_ref.dtype)
            _store(dk_ref.at[:, cols], dks[h], first)
            _store(dv_ref.at[:, cols], dvs[h], first)
        _store(dgq_ref, dgq, first)
        _store(dgk_ref, dgk, first)

    row = pl.BlockSpec((tm, w), lambda i: (i, 0))
    return pl.pallas_call(
        body, grid=(t // tm,), in_specs=[row, full(k), full(v), full(g_q), full(g_k), row],
        out_specs=[row, full(k), full(v), full(g_q), full(g_k)],
        out_shape=[jax.ShapeDtypeStruct(q.shape, BF16), jax.ShapeDtypeStruct(k.shape, F32), jax.ShapeDtypeStruct(v.shape, F32),
                   jax.ShapeDtypeStruct(g_q.shape, F32), jax.ShapeDtypeStruct(g_k.shape, F32)],
        compiler_params=_params("arbitrary"), name="cross_bwd")(q, k, v, g_q, g_k, do)


def ffn_glu_fwd(hg, hv, wg, wv, bg, bv):
    t, f = hg.shape
    tc = _pick(f, LANE)
    col = pl.BlockSpec((t, tc), lambda j: (0, j))
    tap = pl.BlockSpec((FFN_CONV, tc), lambda j: (0, j))
    one = pl.BlockSpec((1, tc), lambda j: (0, j))

    def body(hg_ref, hv_ref, wg_ref, wv_ref, bg_ref, bv_ref, o_ref):
        gate = _conv(hg_ref[...], wg_ref, FFN_CONV) + bg_ref[...]
        val = _conv(hv_ref[...], wv_ref, FFN_CONV) + bv_ref[...]
        o_ref[...] = (_silu(gate) * val).astype(o_ref.dtype)

    return pl.pallas_call(body, grid=(f // tc,), in_specs=[col, col, tap, tap, one, one], out_specs=col,
                          out_shape=jax.ShapeDtypeStruct((t, f), BF16), compiler_params=_params("parallel"),
                          name="ffn_glu_fwd")(hg, hv, wg, wv, bg, bv)


def ffn_glu_bwd(hg, hv, wg, wv, bg, bv, dact):
    t, f = hg.shape
    tc = _pick(f, LANE)
    col = pl.BlockSpec((t, tc), lambda j: (0, j))
    tap = pl.BlockSpec((FFN_CONV, tc), lambda j: (0, j))
    one = pl.BlockSpec((1, tc), lambda j: (0, j))

    def body(hg_ref, hv_ref, wg_ref, wv_ref, bg_ref, bv_ref, da_ref, dhg_ref, dhv_ref, dwg_ref, dwv_ref, dbg_ref, dbv_ref):
        xg, xv, da = hg_ref[...], hv_ref[...], da_ref[...]
        gate = _conv(xg, wg_ref, FFN_CONV) + bg_ref[...]
        val = _conv(xv, wv_ref, FFN_CONV) + bv_ref[...]
        dgate = da * val * _dsilu(gate)
        dval = da * _silu(gate)
        dbg_ref[...] = jnp.sum(dgate, axis=0, keepdims=True)
        dbv_ref[...] = jnp.sum(dval, axis=0, keepdims=True)
        dhg_ref[...] = _conv_bwd(xg, dgate, wg_ref, dwg_ref, FFN_CONV).astype(dhg_ref.dtype)
        dhv_ref[...] = _conv_bwd(xv, dval, wv_ref, dwv_ref, FFN_CONV).astype(dhv_ref.dtype)

    return pl.pallas_call(
        body, grid=(f // tc,), in_specs=[col, col, tap, tap, one, one, col], out_specs=[col, col, tap, tap, one, one],
        out_shape=[jax.ShapeDtypeStruct((t, f), BF16), jax.ShapeDtypeStruct((t, f), BF16),
                   jax.ShapeDtypeStruct((FFN_CONV, f), F32), jax.ShapeDtypeStruct((FFN_CONV, f), F32),
                   jax.ShapeDtypeStruct((1, f), F32), jax.ShapeDtypeStruct((1, f), F32)],
        compiler_params=_params("parallel"), name="ffn_glu_bwd")(hg, hv, wg, wv, bg, bv, dact)


def _adamw(g, w, m, v):
    m = ADAM_B1 * m + (1.0 - ADAM_B1) * g
    v = ADAM_B2 * v + (1.0 - ADAM_B2) * (g * g)
    m_hat = m / (1.0 - ADAM_B1 ** ADAM_STEP)
    v_hat = v / (1.0 - ADAM_B2 ** ADAM_STEP)
    return -ADAM_LR * (m_hat / (jnp.sqrt(v_hat) + ADAM_EPS) + ADAM_WD * w), m, v


def adamw(g, w, m, v, name):
    r, c = g.shape
    tr = r
    for cand in (256, 128, 64, 32, 16, 8):
        if r % cand == 0 and cand * c * 4 <= (1 << 21):
            tr = cand
            break
    return ew(_adamw, [(g, "r"), (w, "r"), (m, "r"), (v, "r")], [((r, c), F32, "r")] * 3, gr=r // tr, name=name)


ANY = pl.BlockSpec(memory_space=pl.ANY)


def _place():
    x, y, c = lax.axis_index("x"), lax.axis_index("y"), lax.axis_index("c")
    return x, y, c, [(1 - x, y), (x, 1 - y), (1 - x, 1 - y)]


def _rcopy(src, dst, send, recv, k, to):
    return pltpu.make_async_remote_copy(src_ref=src, dst_ref=dst, send_sem=send.at[k], recv_sem=recv.at[k],
                                        device_id=to, device_id_type=MESH)


def gather_shards(bigs, smalls):
    nb, na = len(bigs), len(bigs) + len(smalls)
    arrays = list(bigs) + list(smalls)

    def body(*refs):
        ins, outs = refs[:na], refs[na:2 * na]
        send, recv = refs[2 * na:]
        x, y, c, chips = _place()
        me, sib = 2 * x + y, (x, y, 1 - c)

        def half(ref, a, which):
            rows = arrays[a].shape[0] // 2
            return ref.at[pl.ds(which * rows, rows)]

        started = []
        for a in range(na):
            for j, (cx, cy) in enumerate(chips):
                if a < nb:
                    cp = _rcopy(half(ins[a], a, c), half(outs[a].at[me], a, c), send, recv, 6 * a + j, (cx, cy, c))
                else:
                    cp = _rcopy(ins[a], outs[a].at[me], send, recv, 6 * nb + 3 * (a - nb) + j, (cx, cy, c))
                cp.start()
                started.append(cp)
        for a in range(nb):
            for j, (cx, cy) in enumerate(chips):
                landed = half(outs[a].at[2 * cx + cy], a, c)
                _rcopy(landed, landed, send, recv, 6 * a + j, (cx, cy, c)).wait_recv()
                cp = _rcopy(landed, landed, send, recv, 6 * a + 3 + j, sib)
                cp.start()
                started.append(cp)
        for a in range(na):
            for j, (cx, cy) in enumerate(chips):
                if a < nb:
                    dst = half(outs[a].at[2 * cx + cy], a, 1 - c)
                    _rcopy(dst, dst, send, recv, 6 * a + 3 + j, sib).wait_recv()
                else:
                    dst = outs[a].at[2 * cx + cy]
                    _rcopy(dst, dst, send, recv, 6 * nb + 3 * (a - nb) + j, (cx, cy, c)).wait_recv()
        for cp in started:
            cp.wait_send()

    n_sem = 6 * nb + 3 * (na - nb)
    gathered = pl.pallas_call(
        body, in_specs=[ANY] * na, out_specs=[ANY] * na,
        out_shape=[jax.ShapeDtypeStruct((N_CHIPS,) + a.shape, a.dtype) for a in arrays],
        scratch_shapes=[pltpu.SemaphoreType.DMA((n_sem,)), pltpu.SemaphoreType.DMA((n_sem,))],
        name="gather_shards")(*arrays)
    chip = 2 * lax.axis_index("x") + lax.axis_index("y")
    return [lax.dynamic_update_slice(g, a[None], (chip, 0, 0)) for g, a in zip(gathered, arrays)]


def sibling_halves(grads):
    na = len(grads)

    def body(*refs):
        ins, outs = refs[:na], refs[na:2 * na]
        send, recv = refs[2 * na:]
        x, y, c, _ = _place()
        cps = []
        for a in range(na):
            rows = grads[a].shape[1] // 2
            cp = _rcopy(ins[a].at[:, pl.ds((1 - c) * rows, rows)], outs[a], send, recv, a, (x, y, 1 - c))
            cp.start()
            cps.append(cp)
        for cp in cps:
            cp.wait()

    return pl.pallas_call(
        body, in_specs=[ANY] * na, out_specs=[ANY] * na,
        out_shape=[jax.ShapeDtypeStruct((g.shape[0], g.shape[1] // 2, g.shape[2]), g.dtype) for g in grads],
        scratch_shapes=[pltpu.SemaphoreType.DMA((na,)), pltpu.SemaphoreType.DMA((na,))], name="sibling_halves")(*grads)


def join_halves(halves):
    na = len(halves)

    def body(*refs):
        ins, outs = refs[:na], refs[na:2 * na]
        send, recv = refs[2 * na:]
        x, y, c, _ = _place()
        cps = []
        for a in range(na):
            cp = _rcopy(ins[a].at[c], outs[a].at[c], send, recv, a, (x, y, 1 - c))
            cp.start()
            cps.append(cp)
        for a in range(na):
            dst = outs[a].at[1 - c]
            _rcopy(dst, dst, send, recv, a, (x, y, 1 - c)).wait_recv()
        for cp in cps:
            cp.wait_send()

    return pl.pallas_call(
        body, in_specs=[ANY] * na, out_specs=[ANY] * na,
        out_shape=[jax.ShapeDtypeStruct(h.shape, h.dtype) for h in halves],
        input_output_aliases={a: a for a in range(na)},
        scratch_shapes=[pltpu.SemaphoreType.DMA((na,)), pltpu.SemaphoreType.DMA((na,))],
        name="join_halves")(*halves)


HBM = pl.BlockSpec(memory_space=pltpu.HBM)
SEM = pl.BlockSpec(memory_space=pltpu.SEMAPHORE)
SIDE_EFFECT = pltpu.SideEffectType.DATAFLOW_SIDE_EFFECTING


def split_start(name, srcs, land_shapes, n_copies, copies_fn, after):
    ns, nl = len(srcs), len(land_shapes)
    afters = tuple(after) if isinstance(after, (tuple, list)) else (after,)

    def body(*refs):
        ins, lands = refs[:ns], refs[ns:ns + nl]
        send, recv, token = refs[ns + nl + len(afters)], refs[ns + nl + len(afters) + 1], refs[-1]
        for k, (src, dst, dev) in enumerate(copies_fn(ins, lands, False)):
            pltpu.make_async_remote_copy(src_ref=src, dst_ref=dst, send_sem=send.at[k], recv_sem=recv.at[k],
                                         device_id=dev, device_id_type=MESH).start()
        token[...] = jnp.zeros_like(token)

    outs = pl.pallas_call(
        body, name=name,
        out_shape=(pltpu.SemaphoreType.DMA((n_copies,)), pltpu.SemaphoreType.DMA((n_copies,)),
                   *[pltpu.HBM(a.shape, a.dtype) for a in srcs], *[pltpu.HBM(s, dt) for s, dt in land_shapes],
                   jax.ShapeDtypeStruct((8, LANE), F32)),
        in_specs=[HBM] * (ns + nl) + [ANY] * len(afters),
        out_specs=(SEM, SEM, *[HBM] * (ns + nl), pl.BlockSpec(memory_space=pltpu.VMEM)),
        input_output_aliases={i: 2 + i for i in range(ns + nl)},
        compiler_params=pltpu.CompilerParams(has_side_effects=SIDE_EFFECT),
    )(*[pltpu.with_memory_space_constraint(a, pltpu.HBM) for a in srcs],
      *[pltpu.with_memory_space_constraint(lax.empty(s, dt), pltpu.HBM) for s, dt in land_shapes], *afters)
    return outs[0], outs[1], list(outs[2:2 + ns]), list(outs[2 + ns:2 + ns + nl]), outs[-1]


def split_wait(name, started, n_copies, copies_fn, after):
    send, recv, srcs, lands, _ = started
    ns, nl = len(srcs), len(lands)
    afters = tuple(after) if isinstance(after, (tuple, list)) else (after,)

    def body(*refs):
        ins, lnd = refs[:ns], refs[ns:ns + nl]
        send_ref, recv_ref = refs[ns + nl], refs[ns + nl + 1]
        for k, (src, dst, dev) in enumerate(copies_fn(ins, lnd, True)):
            cp = pltpu.make_async_remote_copy(src_ref=src, dst_ref=dst, send_sem=send_ref.at[k], recv_sem=recv_ref.at[k],
                                              device_id=dev, device_id_type=MESH)
            cp.wait_send()
            cp.wait_recv()

    outs = pl.pallas_call(
        body, name=name,
        out_shape=tuple(pltpu.HBM(a.shape, a.dtype) for a in srcs + lands),
        in_specs=[HBM] * (ns + nl) + [SEM, SEM] + [ANY] * len(afters), out_specs=tuple([HBM] * (ns + nl)),
        input_output_aliases={i: i for i in range(ns + nl)},
        compiler_params=pltpu.CompilerParams(has_side_effects=SIDE_EFFECT),
    )(*srcs, *lands, send, recv, *afters)
    return list(outs[:ns]), list(outs[ns:])


def _gather_copies(ins, lands, waiting):
    x, y, c, chips = _place()
    return [(ins[a], lands[a].at[2 * cx + cy] if waiting else lands[a].at[2 * x + y], (cx, cy, c))
            for a in range(len(ins)) for cx, cy in chips]


def _scatter_copies(ins, lands, waiting):
    del waiting
    _, _, c, chips = _place()
    return [(ins[a].at[2 * cx + cy], lands[a].at[j], (cx, cy, c)) for a in range(len(ins)) for j, (cx, cy) in enumerate(chips)]


def allreduce_small(vec):
    r = vec.shape[0]

    def body(x_ref, sum_ref, all_ref, send, recv):
        x, y, c, _ = _place()
        me = 4 * x + 2 * y + c
        all_ref[me] = x_ref[...]
        cps, peers = [], []
        for mask in range(1, 8):
            px = 1 - x if mask & 4 else x
            py = 1 - y if mask & 2 else y
            pc = 1 - c if mask & 1 else c
            peers.append(4 * px + 2 * py + pc)
            cp = _rcopy(x_ref, all_ref.at[me], send, recv, mask - 1, (px, py, pc))
            cp.start()
            cps.append(cp)
        for k, cp in enumerate(cps):
            _rcopy(x_ref, all_ref.at[peers[k]], send, recv, k, (x, y, c)).wait_recv()
        for cp in cps:
            cp.wait_send()
        total = all_ref[0]
        for d in range(1, 8):
            total = total + all_ref[d]
        sum_ref[...] = total

    vm = pl.BlockSpec(memory_space=pltpu.VMEM)
    return pl.pallas_call(
        body, in_specs=[vm], out_specs=vm, out_shape=jax.ShapeDtypeStruct((r, LANE), F32),
        scratch_shapes=[pltpu.VMEM((8, r, LANE), F32), pltpu.SemaphoreType.DMA((7,)), pltpu.SemaphoreType.DMA((7,))],
        compiler_params=pltpu.CompilerParams(vmem_limit_bytes=VMEM_LIMIT), name="allreduce_small")(vec)


def _row_tile(rows):
    for cand in (256, 128, 64, 32, 16):
        if rows % cand == 0:
            return cand
    return rows


def add_sibling(grad, recv, c_idx):
    _, rows, cols = grad.shape
    hr = rows // 2
    tr = _row_tile(hr)
    nb = hr // tr

    def body(c_ref, g_ref, r_ref, o_ref):
        o_ref[...] = (g_ref[...].astype(F32) + r_ref[...].astype(F32)).astype(o_ref.dtype)

    return pl.pallas_call(
        body,
        grid_spec=pltpu.PrefetchScalarGridSpec(
            num_scalar_prefetch=1, grid=(N_CHIPS, nb),
            in_specs=[pl.BlockSpec((None, tr, cols), lambda k, r, c_ref: (k, c_ref[0] * nb + r, 0)),
                      pl.BlockSpec((None, tr, cols), lambda k, r, c_ref: (k, r, 0))],
            out_specs=pl.BlockSpec((None, tr, cols), lambda k, r, c_ref: (k, r, 0))),
        out_shape=jax.ShapeDtypeStruct((N_CHIPS, hr, cols), BF16),
        compiler_params=_params("parallel", "parallel"), name="add_sibling")(c_idx, grad, recv)


def sum_chips(part, others, place_idx):
    _, hr, cols = part.shape
    tr = _row_tile(hr)

    def body(k_ref, p_ref, o0_ref, o1_ref, o2_ref, out_ref):
        out_ref[...] = ((p_ref[...].astype(F32) + o0_ref[...].astype(F32)) + o1_ref[...].astype(F32)) + o2_ref[...].astype(F32)

    other = lambda j: pl.BlockSpec((None, tr, cols), lambda r, k_ref: (j, r, 0))
    return pl.pallas_call(
        body,
        grid_spec=pltpu.PrefetchScalarGridSpec(
            num_scalar_prefetch=1, grid=(hr // tr,),
            in_specs=[pl.BlockSpec((None, tr, cols), lambda r, k_ref: (k_ref[0], r, 0)), other(0), other(1), other(2)],
            out_specs=pl.BlockSpec((None, tr, cols), lambda r, k_ref: (k_ref[1], r, 0))),
        out_shape=jax.ShapeDtypeStruct((2, hr, cols), F32),
        compiler_params=_params("parallel"), name="sum_chips")(place_idx, part, others, others, others)


def _pad_lanes(a, width=LANE):
    return jnp.pad(a, ((0, 0), (0, width - a.shape[1])))


def _cols_from_shards(g):
    return jnp.transpose(g, (1, 0, 2)).reshape(g.shape[1], -1)


def _cols_to_shards(w):
    k, n4 = w.shape
    return jnp.transpose(w.reshape(k, N_CHIPS, n4 // N_CHIPS), (1, 0, 2))


def kernel(x, mem, positions, g_mix, w_in, g_qa, w_qb, g_kva, w_kvb, g_qn_nope, g_qn_pe, g_kn_nope, g_kn_pe, conv_qk, b_if, g_hnorm, p_a, p_b, w_out, g_cross, g_mem, wq_c, wk_c, wv_c, g_cq, g_ck, wo_c, g_ffn, w_up, conv_ffn, b_conv_ffn, w_down, loss_target, m_g_mix, m_w_in, m_g_qa, m_w_qb, m_g_kva, m_w_kvb, m_g_qn_nope, m_g_qn_pe, m_g_kn_nope, m_g_kn_pe, m_conv_qk, m_b_if, m_g_hnorm, m_p_a, m_p_b, m_w_out, m_g_cross, m_g_mem, m_wq_c, m_wk_c, m_wv_c, m_g_cq, m_g_ck, m_wo_c, m_g_ffn, m_w_up, m_conv_ffn, m_b_conv_ffn, m_w_down, v_g_mix, v_w_in, v_g_qa, v_w_qb, v_g_kva, v_w_kvb, v_g_qn_nope, v_g_qn_pe, v_g_kn_nope, v_g_kn_pe, v_conv_qk, v_b_if, v_g_hnorm, v_p_a, v_p_b, v_w_out, v_g_cross, v_g_mem, v_wq_c, v_wk_c, v_wv_c, v_g_cq, v_g_ck, v_wo_c, v_g_ffn, v_w_up, v_conv_ffn, v_b_conv_ffn, v_w_down):
    names = ["g_mix", "w_in", "g_qa", "w_qb", "g_kva", "w_kvb", "g_qn_nope", "g_qn_pe", "g_kn_nope", "g_kn_pe", "conv_qk",
             "b_if", "g_hnorm", "p_a", "p_b", "w_out", "g_cross", "g_mem", "wq_c", "wk_c", "wv_c", "g_cq", "g_ck", "wo_c",
             "g_ffn", "w_up", "conv_ffn", "b_conv_ffn", "w_down"]
    env = locals()
    wts = {n: env[n] for n in names}
    mom = {n: env["m_" + n] for n in names}
    var = {n: env["v_" + n] for n in names}

    xi, yi, ci = lax.axis_index("x"), lax.axis_index("y"), lax.axis_index("c")
    chip = 2 * xi + yi
    place_arr = jnp.stack([chip, ci]).astype(jnp.int32)
    c_arr = jnp.reshape(ci, (1,)).astype(jnp.int32)

    x2d, tgt, mem2d = x[0], loss_target[0], mem[0]
    t, d = x2d.shape
    mla_h = w_qb.shape[2] * N_CHIPS // (NOPE + ROPE)
    ml_h = b_if.shape[1] // 2
    cr_h = wq_c.shape[2] // CROSS_DH
    f_dim = w_down.shape[1] * N_CHIPS
    q_rank, kv_rank = g_qa.shape[1], g_kva.shape[1]
    qk_w, v_w = ml_h * MLSTM_DK, ml_h * MLSTM_DV
    nc = t // CHUNK

    big_names = ["w_in", "w_qb", "w_kvb", "p_a", "p_b", "w_out", "wq_c", "wk_c", "wv_c", "wo_c", "w_up", "w_down"]
    col_sharded = {"w_in", "w_qb", "w_kvb", "wo_c", "w_up"}
    small_sharded = ["conv_qk", "g_hnorm", "conv_ffn"]
    early_big, early_small = ["w_in", "w_qb", "w_kvb"], ["conv_qk", "g_hnorm"]
    late_groups = [["p_a", "p_b", "w_out", "wq_c", "wk_c", "wv_c", "wo_c"], ["w_up", "w_down", "conv_ffn"]]
    full = {}

    def unshard(n, g):
        full[n] = _cols_from_shards(g) if (n in col_sharded or n in small_sharded) else g.reshape(-1, g.shape[2])

    gathered = gather_shards([wts[n][0].astype(BF16) for n in early_big], [wts[n][0] for n in early_small])
    for n, g in zip(early_big + early_small, gathered):
        unshard(n, g)
    late, order_after = [], gathered[0]
    for gi, group in enumerate(late_groups):
        src = [wts[n][0].astype(BF16) if n in big_names else wts[n][0] for n in group]
        late.append(split_start("gather_late%d_start" % gi, src, [((N_CHIPS,) + a.shape, a.dtype) for a in src],
                                3 * len(src), _gather_copies, order_after))
        order_after = late[-1][4]
    g_mix_fwd = g_mix + order_after[0:1, 0:1]

    def land_late(gi, after):
        group = late_groups[gi]
        own, landed = split_wait("gather_late%d_wait" % gi, late[gi], 3 * len(group), _gather_copies, after)
        for n, g, o in zip(group, landed, own):
            unshard(n, lax.dynamic_update_slice(g, o[None], (chip, 0, 0)))

    o_qa, o_kv, o_kpe = 0, q_rank, q_rank + kv_rank
    o_q = o_kpe + ROPE
    o_v = o_q + 2 * qk_w
    o_if = o_v + v_w
    o_o = o_if + 2 * ml_h
    o_ga, o_gb = o_o + v_w, o_o + v_w + d
    wi = full["w_in"]
    pad_kpe = jnp.zeros((d, LANE - ROPE), BF16)
    pad_if = jnp.zeros((d, LANE - 2 * ml_h), BF16)
    w_small = jnp.concatenate([wi[:, o_qa:o_q], pad_kpe, wi[:, o_if:o_o], pad_if], axis=1)
    o_kpe_s, o_if_s = o_kpe, o_kpe + LANE
    w_qk, w_v, w_o, w_ga, w_gb = wi[:, o_q:o_v], wi[:, o_v:o_if], wi[:, o_o:o_ga], wi[:, o_ga:o_gb], wi[:, o_gb:]

    wq3 = full["w_qb"].reshape(q_rank, mla_h, NOPE + ROPE)
    wq_nope = wq3[:, :, :NOPE].reshape(q_rank, mla_h * NOPE)
    wq_pe = jnp.pad(wq3[:, :, NOPE:], ((0, 0), (0, 0), (0, LANE - ROPE))).reshape(q_rank, mla_h * LANE)
    wkv3 = full["w_kvb"].reshape(kv_rank, mla_h, NOPE + VHEAD)
    wk_nope = wkv3[:, :, :NOPE].reshape(kv_rank, mla_h * NOPE)
    wv_mla = wkv3[:, :, NOPE:].reshape(kv_rank, mla_h * VHEAD)

    inv_freq = ROPE_BASE ** (-jnp.arange(0, ROPE, 2, dtype=F32) / ROPE)
    ang = positions[0].astype(F32)[:, None] * inv_freq
    cos, sin = jnp.cos(ang), jnp.sin(ang)
    zero_h = jnp.zeros_like(cos)
    tabs = [_pad_lanes(jnp.concatenate([cos, cos], axis=1)), _pad_lanes(-sin), _pad_lanes(jnp.concatenate([zero_h, sin], axis=1))]
    mla_gains = [g_qn_nope, _pad_lanes(g_qn_pe), g_kn_nope, _pad_lanes(g_kn_pe)]

    u1 = rms_fwd(x2d, g_mix_fwd, "rms_mix")
    z_small = mm(u1, w_small, name="in_small")
    z_qa, z_kv = z_small[:, o_qa:o_kv], z_small[:, o_kv:o_kpe]
    z_kpe, z_if = z_small[:, o_kpe_s:o_kpe_s + LANE], z_small[:, o_if_s:o_if_s + 2 * ml_h]
    z_qk = mm(u1, w_qk, name="in_qk")
    z_v = mm(u1, w_v, name="in_v")
    z_o = mm(u1, w_o, name="in_o")
    z_ga = mm(u1, w_ga, name="in_ga")
    z_gb = mm(u1, w_gb, name="in_gb")

    qa_n = rms_fwd(z_qa, g_qa, "rms_qa")
    kv_n = rms_fwd(z_kv, g_kva, "rms_kva")
    qn_raw = mm(qa_n, wq_nope, name="q_nope")
    qp_raw = mm(qa_n, wq_pe, name="q_pe")
    kn_raw = mm(kv_n, wk_nope, name="k_nope")
    v_mla = mm(kv_n, wv_mla, out_dtype=BF16, name="v_mla")
    q_att, k_att = mla_prep_fwd(qn_raw, qp_raw, kn_raw, z_kpe, tabs, mla_gains, mla_h)
    y_a, lse_row = attn_fwd(q_att, k_att, jnp.transpose(v_mla), mla_h)

    colscale = jnp.concatenate([jnp.full((1, qk_w), MLSTM_DK ** -0.5, F32), jnp.ones((1, qk_w), F32)], axis=1)
    qk_c = conv_qk_fwd(z_qk, full["conv_qk"], colscale)
    gates4 = z_if.reshape(nc, CHUNK, 2, ml_h)
    gcol = jnp.transpose(gates4, (3, 0, 1, 2))
    grow = jnp.transpose(gates4, (3, 0, 2, 1))
    bias = jnp.transpose(b_if.reshape(2, ml_h), (1, 0)).reshape(ml_h, 1, 2)
    h_raw, c_all, n_all, m_all = mlstm_fwd(qk_c, z_v, gcol, grow, bias, ml_h)
    g_hn = full["g_hnorm"].reshape(1, v_w)
    hn_gr, hd_gr = t // _pick(t, ROW_TILE), t // _pick(t, HEAD_ROW_TILE)
    y_b = ew(lambda *a: (_hnorm_gate(*a),), [(h_raw, "rc"), (z_o, "rc"), (g_hn, "c")], [((t, v_w), BF16, "rc")], gr=hd_gr, gc=ml_h,
             name="hnorm_gate")[0]

    land_late(0, y_b)

    pa = mm(y_a, full["p_a"], name="proj_a")
    pb = mm(y_b, full["p_b"], name="proj_b")
    merge_fn = lambda ga, gb, a, b: (_sigmoid(ga) * a + _sigmoid(gb) * b,)
    merged = ew(merge_fn, [(z_ga, "r"), (z_gb, "r"), (pa, "r"), (pb, "r")], [((t, d), BF16, "r")], gr=hn_gr, name="merge")[0]
    x1 = mm(merged, full["w_out"], add=x2d, name="out_proj")

    uc = rms_fwd(x1, g_cross, "rms_cross")
    mem_n = rms_fwd(mem2d, g_mem, "rms_mem")
    qc = mm(uc, full["wq_c"], name="cross_q")
    kc = mm(mem_n, full["wk_c"], name="cross_k")
    vc = mm(mem_n, full["wv_c"], name="cross_v")
    oc = cross_fwd(qc, kc, vc, g_cq, g_ck, cr_h)
    x2 = mm(oc, full["wo_c"], add=x1, name="cross_out")

    land_late(1, x2)
    wup_g, wup_v = full["w_up"][:, :f_dim], full["w_up"][:, f_dim:]
    u3 = rms_fwd(x2, g_ffn, "rms_ffn")
    hg = mm(u3, wup_g, name="ffn_up_gate")
    hv = mm(u3, wup_v, name="ffn_up_val")
    cw, cb = full["conv_ffn"], b_conv_ffn
    act = ffn_glu_fwd(hg, hv, cw[:, :f_dim], cw[:, f_dim:], cb[:, :f_dim], cb[:, f_dim:])
    y = mm(act, full["w_down"], add=x2, name="ffn_down")

    def loss_fn(y_, t_):
        err = y_ - t_
        part = jnp.sum(jnp.sum(err * err, axis=1, keepdims=True), axis=0, keepdims=True) * (0.5 / d)
        return err * (1.0 / d), err * (1.0 / d), jnp.broadcast_to(part, (1, LANE))

    dy, dy_mx, loss_part = ew(loss_fn, [(y, "r"), (tgt, "r")], [((t, d), F32, "r"), ((t, d), BF16, "r"), ((1, LANE), F32, "f")],
                              gr=hn_gr, name="loss")

    gw = {}
    gw["w_down"] = mm(act, dy_mx, ta=True, out_dtype=BF16, name="dw_down")
    dact = mm(dy_mx, full["w_down"], tb=True, name="d_act")
    dhg, dhv, dcw_g, dcw_v, dcb_g, dcb_v = ffn_glu_bwd(hg, hv, cw[:, :f_dim], cw[:, f_dim:], cb[:, :f_dim], cb[:, f_dim:], dact)
    gw["conv_ffn"] = jnp.concatenate([dcw_g, dcw_v], axis=1)
    gw["b_conv_ffn"] = jnp.concatenate([dcb_g, dcb_v], axis=1)
    dwup_g = mm(u3, dhg, ta=True, out_dtype=BF16, name="dw_up_gate")
    dwup_v = mm(u3, dhv, ta=True, out_dtype=BF16, name="dw_up_val")
    gw["w_up"] = jnp.concatenate([dwup_g, dwup_v], axis=1)

    def shard_major(n):
        return _cols_to_shards(gw[n]) if n in col_sharded else gw[n].reshape(N_CHIPS, -1, gw[n].shape[1])

    def chip_partials(group):
        grads_sm = [shard_major(n) for n in group]
        return [add_sibling(g, r, c_arr) for g, r in zip(grads_sm, sibling_halves(grads_sm))]

    def scatter_group(group, tag, after):
        parts_ = chip_partials(group)
        return split_start("scatter_start_" + tag, parts_, [((3,) + p.shape[1:], p.dtype) for p in parts_],
                           3 * len(parts_), _scatter_copies, after)

    group_a = ["w_up", "w_down"]
    started_a = scatter_group(group_a, "a", gw["w_up"])
    g_ffn_bwd = g_ffn + started_a[4][0:1, 0:1]
    du3 = mm(dhg, wup_g, tb=True, name="d_u3_gate")
    du3 = mm(dhv, wup_v, tb=True, add=du3, name="d_u3_val")
    dx2, gw["g_ffn"] = rms_bwd(x2, g_ffn_bwd, du3, dy, "rms_ffn_bwd")

    gw["wo_c"] = mm(oc, dx2, ta=True, out_dtype=BF16, name="dw_cross_out")
    doc = mm(dx2, full["wo_c"], tb=True, name="d_cross_o")
    dqc, dkc, dvc, gw["g_cq"], gw["g_ck"] = cross_bwd(qc, kc, vc, g_cq, g_ck, doc, cr_h)
    gw["wq_c"] = mm(uc, dqc, ta=True, out_dtype=BF16, name="dw_cross_q")
    gw["wk_c"] = mm(mem_n, dkc, ta=True, out_dtype=BF16, name="dw_cross_k")
    gw["wv_c"] = mm(mem_n, dvc, ta=True, out_dtype=BF16, name="dw_cross_v")
    duc = mm(dqc, full["wq_c"], tb=True, name="d_uc")
    dmem_n = mm(dkc, full["wk_c"], tb=True, name="d_mem_k")
    dmem_n = mm(dvc, full["wv_c"], tb=True, add=dmem_n, name="d_mem_v")
    _, gw["g_mem"] = rms_bwd(mem2d, g_mem, dmem_n, None, "rms_mem_bwd")
    dx1, gw["g_cross"] = rms_bwd(x1, g_cross, duc, dx2, "rms_cross_bwd")

    gw["w_out"] = mm(merged, dx1, ta=True, out_dtype=BF16, name="dw_out")
    dmerged = mm(dx1, full["w_out"], tb=True, name="d_merged")

    def merge_bwd(ga, gb, a, b, dm):
        _, pull = jax.vjp(lambda *args: merge_fn(*args)[0], ga, gb, a, b)
        return pull(dm)

    dz_ga, dz_gb, dpa, dpb = ew(merge_bwd, [(z_ga, "r"), (z_gb, "r"), (pa, "r"), (pb, "r"), (dmerged, "r")],
                                [((t, d), BF16, "r")] * 4, gr=hn_gr, name="merge_bwd")
    gw["p_a"] = mm(y_a, dpa, ta=True, out_dtype=BF16, name="dw_proj_a")
    gw["p_b"] = mm(y_b, dpb, ta=True, out_dtype=BF16, name="dw_proj_b")
    group_b = ["wo_c", "wq_c", "wk_c", "wv_c", "w_out", "p_a", "p_b"]
    started_b = scatter_group(group_b, "b", gw["p_b"])
    g_hn_bwd = g_hn + started_b[4][0:1, 0:1]
    dy_a = mm(dpa, full["p_a"], tb=True, name="d_ya")
    dy_b = mm(dpb, full["p_b"], tb=True, name="d_yb")

    def hnorm_bwd(h_, zo_, g_, dyb_):
        _, pull = jax.vjp(_hnorm_gate, h_, zo_, g_)
        return pull(dyb_)

    dh_raw, dz_o, dg_hn = ew(hnorm_bwd, [(h_raw, "rc"), (z_o, "rc"), (g_hn_bwd, "c"), (dy_b, "rc")],
                             [((t, v_w), F32, "rc"), ((t, v_w), BF16, "rc"), ((1, v_w), F32, "c")],
                             gr=hd_gr, gc=ml_h, order="cr", name="hnorm_gate_bwd")
    gw["g_hnorm"] = dg_hn.reshape(ml_h, MLSTM_DV)
    dq_m, dk_m, dz_v, dgcol, dgrow = mlstm_bwd(qk_c, z_v, gcol, grow, bias, c_all, n_all, m_all, dh_raw, ml_h)
    dgates4 = jnp.transpose(dgcol, (1, 2, 3, 0)) + jnp.transpose(dgrow, (1, 3, 2, 0))
    dz_if = dgates4.reshape(t, 2 * ml_h)
    gw["b_if"] = ew(lambda a: (jnp.sum(a, axis=0, keepdims=True),), [(dz_if, "r")], [((1, 2 * ml_h), F32, "f")],
                    gr=hn_gr, name="bias_if_bwd")[0]
    dz_qk, gw["conv_qk"] = conv_qk_bwd(z_qk, full["conv_qk"], colscale, dq_m, dk_m)

    dq_att, dk_att, dv_mla = attn_bwd(q_att, k_att, v_mla, y_a, dy_a, lse_row, mla_h)
    dqn_raw, dqp_raw, dkn_raw, dz_kpe, gw["g_qn_nope"], dg_qp, gw["g_kn_nope"], dg_kp = mla_prep_bwd(
        qn_raw, qp_raw, kn_raw, z_kpe, tabs, mla_gains, dq_att, dk_att, mla_h)
    gw["g_qn_pe"], gw["g_kn_pe"] = dg_qp[:, :ROPE], dg_kp[:, :ROPE]
    dwq_nope = mm(qa_n, dqn_raw, ta=True, out_dtype=BF16, name="dw_q_nope")
    dwq_pe = mm(qa_n, dqp_raw, ta=True, out_dtype=BF16, name="dw_q_pe")
    dwk_nope = mm(kv_n, dkn_raw, ta=True, out_dtype=BF16, name="dw_k_nope")
    dwv_mla = mm(kv_n, dv_mla, ta=True, out_dtype=BF16, name="dw_v_mla")
    dqa_n = mm(dqn_raw, wq_nope, tb=True, name="d_qa_nope")
    dqa_n = mm(dqp_raw, wq_pe, tb=True, add=dqa_n, name="d_qa_pe")
    dkv_n = mm(dkn_raw, wk_nope, tb=True, name="d_kv_nope")
    dkv_n = mm(dv_mla, wv_mla, tb=True, add=dkv_n, name="d_kv_v")
    dz_qa, gw["g_qa"] = rms_bwd(z_qa, g_qa, dqa_n, None, "rms_qa_bwd", BF16)
    dz_kv, gw["g_kva"] = rms_bwd(z_kv, g_kva, dkv_n, None, "rms_kva_bwd", BF16)
    gw["w_qb"] = jnp.concatenate([dwq_nope.reshape(q_rank, mla_h, NOPE), dwq_pe.reshape(q_rank, mla_h, LANE)[:, :, :ROPE]],
                                 axis=2).reshape(q_rank, -1)
    gw["w_kvb"] = jnp.concatenate([dwk_nope.reshape(kv_rank, mla_h, NOPE), dwv_mla.reshape(kv_rank, mla_h, VHEAD)],
                                  axis=2).reshape(kv_rank, -1)

    dz_small = jnp.concatenate([dz_qa, dz_kv, dz_kpe.astype(BF16), _pad_lanes(dz_if).astype(BF16)], axis=1)
    dw_small = mm(u1, dz_small, ta=True, out_dtype=BF16, name="dw_in_small")
    du1 = mm(dz_small, w_small, tb=True, name="d_u1_small")
    dw_segs = []
    for nm, dz, w_seg in (("qk", dz_qk, w_qk), ("v", dz_v, w_v), ("o", dz_o, w_o), ("ga", dz_ga, w_ga), ("gb", dz_gb, w_gb)):
        dw_segs.append(mm(u1, dz, ta=True, out_dtype=BF16, name="dw_in_" + nm))
        du1 = mm(dz, w_seg, tb=True, add=du1, name="d_u1_" + nm)
    gw["w_in"] = jnp.concatenate([dw_small[:, :o_kpe_s + ROPE], dw_segs[0], dw_segs[1],
                                  dw_small[:, o_if_s:o_if_s + 2 * ml_h], dw_segs[2], dw_segs[3], dw_segs[4]], axis=1)
    grad_x, gw["g_mix"] = rms_bwd(x2d, g_mix, du1, dx1, "rms_mix_bwd")

    group_c = ["w_in", "w_qb", "w_kvb"]
    started_c = scatter_group(group_c, "c", grad_x)
    parts_a, others_a = split_wait("scatter_wait_a", started_a, 3 * len(group_a), _scatter_copies, grad_x)
    parts_b, others_b = split_wait("scatter_wait_b", started_b, 3 * len(group_b), _scatter_copies, grad_x)
    place_ab = place_arr + started_c[4][0, 0].astype(jnp.int32)
    grads, deltas, new_m, new_v = {}, {}, {}, {}

    def finish(group, parts_, others_, place):
        joined = join_halves([sum_chips(p, o, place) for p, o in zip(parts_, others_)])
        for n, j in zip(group, joined):
            grads[n] = j.reshape(-1, j.shape[2])
            deltas[n], new_m[n], new_v[n] = adamw(grads[n], wts[n][0], mom[n][0], var[n][0], "adamw_" + n)

    finish(group_a + group_b, parts_a + parts_b, others_a + others_b, place_ab)

    small_names = [n for n in names if n not in big_names]
    pieces = [loss_part]
    for n in small_names:
        flat = gw[n].reshape(1, -1)
        pieces.append(jnp.pad(flat, ((0, 0), (0, (-flat.shape[1]) % LANE))))
    packed = jnp.concatenate(pieces, axis=1)
    packed = jnp.pad(packed, ((0, 0), (0, (-packed.shape[1]) % (8 * LANE)))).reshape(-1, LANE)
    total = allreduce_small(packed).reshape(1, -1)
    loss = total[0, 0]
    small_grads, off = {}, LANE
    for n in small_names:
        size = gw[n].size
        g_full = total[:, off:off + size].reshape(gw[n].shape)
        off += size + (-size) % LANE
        if n in small_sharded:
            width = wts[n].shape[-1]
            g_full = lax.dynamic_slice_in_dim(g_full, chip * width, width, axis=g_full.ndim - 1)
        small_grads[n] = g_full.reshape(wts[n].shape[1:])

    def pack_small(tree):
        flat = jnp.concatenate([tree[n].reshape(1, -1) for n in small_names], axis=1)
        return jnp.pad(flat, ((0, 0), (0, (-flat.shape[1]) % (8 * LANE)))).reshape(8, -1)

    sg = pack_small(small_grads)
    sd, sm, sv = adamw(sg, pack_small({n: wts[n][0] for n in small_names}), pack_small({n: mom[n][0] for n in small_names}),
                       pack_small({n: var[n][0] for n in small_names}), "adamw_small")
    off = 0
    for n in small_names:
        size = small_grads[n].size
        shp = wts[n].shape[1:]
        grads[n] = small_grads[n]
        for dst, src in ((deltas, sd), (new_m, sm), (new_v, sv)):
            dst[n] = src.reshape(1, -1)[:, off:off + size].reshape(shp)
        off += size

    parts_c, others_c = split_wait("scatter_wait_c", started_c, 3 * len(group_c), _scatter_copies,
                                   (sv, new_v[group_b[-1]], new_v[group_a[0]]))
    finish(group_c, parts_c, others_c, place_arr)

    def out(tree):
        return [tree[n].reshape(wts[n].shape) for n in names]

    return (loss, grad_x.reshape(x.shape), *out(grads), *out(deltas), *out(new_m), *out(new_v))
```

```python
import functools
import math

import jax
import jax.numpy as jnp
from jax import lax
from jax.experimental import pallas as pl
from jax.experimental.pallas import tpu as pltpu

F32, BF16 = jnp.float32, jnp.bfloat16
MESH = pl.DeviceIdType.MESH

EPS = 1e-6
CHUNK = 64
LOG2_CHUNK = 6
NOPE, ROPE, VHEAD = 128, 64, 128
MLSTM_DK, MLSTM_DV, MLSTM_CONV = 128, 256, 4
MLSTM_HEADS_PER_STEP = 8
CROSS_DH = 128
FFN_CONV = 3
ROPE_BASE = 10000.0
LOG2_E = math.log2(math.e)
ADAM_LR, ADAM_B1, ADAM_B2, ADAM_EPS, ADAM_WD, ADAM_STEP = 0.001, 0.9, 0.999, 1e-08, 0.01, 10

LANE = 128
ROW_TILE = 256
HEAD_ROW_TILE = 1024
ATT_TILE = 512
ATT_SUB = 256
MM_TILES = (1024, 1024, 2048)
MM_SHARD_TILE = 1536
VMEM_LIMIT = 56 * 1024 * 1024
N_CHIPS = 4

NN = ((1,), (0,))
NT = ((1,), (1,))
TN = ((0,), (0,))


def _pick(dim, pref):
    if dim <= pref:
        return dim
    for t in range(pref, 0, -LANE):
        if dim % t == 0:
            return t
    return dim


def _bdot(a, b, dims):
    return lax.dot_general(a.astype(BF16), b.astype(BF16), (dims, ((), ())), preferred_element_type=F32)


@jax.custom_vjp
def _dnn(a, b):
    return _bdot(a, b, NN)


_dnn.defvjp(lambda a, b: (_bdot(a, b, NN), (a, b)),
            lambda r, g: (_bdot(g, r[1], NT), _bdot(r[0], g, TN)))


@jax.custom_vjp
def _dnt(a, b):
    return _bdot(a, b, NT)


_dnt.defvjp(lambda a, b: (_bdot(a, b, NT), (a, b)),
            lambda r, g: (_bdot(g, r[1], NN), _bdot(g, r[0], TN)))


@jax.custom_vjp
def _dtn(a, b):
    return _bdot(a, b, TN)


_dtn.defvjp(lambda a, b: (_bdot(a, b, TN), (a, b)),
            lambda r, g: (_bdot(r[1], g, NT), _bdot(r[0], g, NN)))


@functools.partial(jax.custom_vjp, nondiff_argnums=(1,))
def _lane_roll(x, shift):
    return pltpu.roll(x, shift, 1)


_lane_roll.defvjp(lambda x, shift: (pltpu.roll(x, shift, 1), None),
                  lambda shift, _, g: (pltpu.roll(g, (LANE - shift) % LANE, 1),))


def _params(*sem):
    return pltpu.CompilerParams(dimension_semantics=sem, vmem_limit_bytes=VMEM_LIMIT)


def mm(a, b, *, ta=False, tb=False, add=None, out_dtype=F32, name, b_shards=None, out_shards=None):
    m_dim, k_dim = (a.shape[1], a.shape[0]) if ta else a.shape
    if b_shards is None:
        n_dim = b.shape[0] if tb else b.shape[1]
        assert k_dim == (b.shape[1] if tb else b.shape[0]), (name, a.shape, b.shape)
    else:
        n_dim = b.shape[1] if tb else b_shards[1] * b.shape[2]
        assert k_dim == (b_shards[1] * b.shape[2] if tb else b.shape[1]), (name, a.shape, b.shape)
    tm, tn, tk = _pick(m_dim, MM_TILES[0]), _pick(n_dim, MM_TILES[1]), _pick(k_dim, MM_TILES[2])
    if b_shards is not None and tb:
        tk = _pick(b.shape[2], MM_SHARD_TILE)
    if (b_shards is not None and not tb) or out_shards is not None:
        tn = _pick(n_dim // (out_shards or b_shards[1]), MM_SHARD_TILE)
    nk = k_dim // tk
    dims = ((0,) if ta else (1,), (1,) if tb else (0,))
    has_add = add is not None

    def body(*refs):
        a_ref, b_ref = refs[0], refs[1]
        c_ref = refs[2] if has_add else None
        o_ref = refs[3] if has_add else refs[2]
        prod = _bdot(a_ref[...], b_ref[...], dims)
        if nk == 1:
            o_ref[...] = (prod + c_ref[...].astype(F32) if has_add else prod).astype(o_ref.dtype)
            return
        acc = refs[-1]
        k = pl.program_id(2)

        @pl.when(k == 0)
        def _():
            acc[...] = prod + c_ref[...].astype(F32) if has_add else prod

        @pl.when(k > 0)
        def _():
            acc[...] += prod

        @pl.when(k == nk - 1)
        def _():
            o_ref[...] = acc[...].astype(o_ref.dtype)

    if b_shards is None:
        b_spec = pl.BlockSpec((tn, tk), lambda i, j, k: (j, k)) if tb else pl.BlockSpec((tk, tn), lambda i, j, k: (k, j))
    elif tb:
        per = b.shape[2] // tk
        b_spec = pl.BlockSpec((None, tn, tk), lambda i, j, k: (b_shards[0] + k // per, j, k % per))
    else:
        per = b.shape[2] // tn
        b_spec = pl.BlockSpec((None, tk, tn), lambda i, j, k: (b_shards[0] + j // per, k, j % per))
    in_specs = [pl.BlockSpec((tk, tm), lambda i, j, k: (k, i)) if ta else pl.BlockSpec((tm, tk), lambda i, j, k: (i, k)), b_spec]
    if out_shards is None:
        out_spec, out_shape = pl.BlockSpec((tm, tn), lambda i, j, k: (i, j)), (m_dim, n_dim)
    else:
        per_o = n_dim // out_shards // tn
        out_spec = pl.BlockSpec((None, tm, tn), lambda i, j, k: (j // per_o, i, j % per_o))
        out_shape = (out_shards, m_dim, n_dim // out_shards)
    args = [a, b]
    if has_add:
        in_specs.append(pl.BlockSpec((tm, tn), lambda i, j, k: (i, j)))
        args.append(add)
    return pl.pallas_call(
        body, grid=(m_dim // tm, n_dim // tn, nk), in_specs=in_specs, out_specs=out_spec,
        out_shape=jax.ShapeDtypeStruct(out_shape, out_dtype),
        scratch_shapes=[pltpu.VMEM((tm, tn), F32)] if nk > 1 else [],
        compiler_params=_params("parallel", "parallel", "arbitrary"), name=name)(*args)


def ew(fn, ins, outs, *, gr, gc=1, order="rc", name):
    n_in = len(ins)

    def block(shape, kind):
        r, c = shape
        return (r // gr if kind in ("rc", "r") else r, c // gc if kind in ("rc", "c") else c)

    def imap(kind):
        def f(p0, p1):
            i, j = (p0, p1) if order == "rc" else (p1, p0)
            return {"rc": (i, j), "r": (i, 0), "c": (0, j), "f": (0, 0)}[kind]
        return f

    def body(*refs):
        p0, p1 = pl.program_id(0), pl.program_id(1)
        i, j = (p0, p1) if order == "rc" else (p1, p0)
        vals = fn(*[r[...] for r in refs[:n_in]])
        for ref, val, (_, dtype, kind) in zip(refs[n_in:], vals, outs):
            first = {"rc": None, "r": (j == 0) if gc > 1 else None, "c": (i == 0) if gr > 1 else None,
                     "f": ((i == 0) & (j == 0)) if gr * gc > 1 else None}[kind]
            _store(ref, val.astype(dtype), first)

    grid = (gr, gc) if order == "rc" else (gc, gr)
    return pl.pallas_call(
        body, grid=grid,
        in_specs=[pl.BlockSpec(block(a.shape, k), imap(k)) for a, k in ins],
        out_specs=[pl.BlockSpec(block(s, k), imap(k)) for s, _, k in outs],
        out_shape=[jax.ShapeDtypeStruct(s, d) for s, d, _ in outs],
        compiler_params=_params("arbitrary", "arbitrary"), name=name)(*[a for a, _ in ins])


def _store(ref, val, first):
    if first is None:
        ref[...] = val
        return

    @pl.when(first)
    def _():
        ref[...] = val

    @pl.when(jnp.logical_not(first))
    def _():
        ref[...] += val


def _f32(*xs):
    return [x.astype(F32) for x in xs]


def _rms(x, g, n):
    ms = jnp.sum(x * x, axis=-1, keepdims=True) * (1.0 / n)
    return x * lax.rsqrt(ms + EPS) * g


def _sigmoid(x):
    return 1.0 / (1.0 + jnp.exp(-x))


def _silu(x):
    return x * _sigmoid(x)


def _log_sigmoid(x):
    return jnp.minimum(x, 0.0) - jnp.log(1.0 + jnp.exp(-jnp.abs(x)))


def rms_fwd(x, g, name, out_dtype=BF16):
    t, w = x.shape
    return ew(lambda x_, g_: (_rms(x_, g_, w),), [(x, "r"), (g, "f")], [((t, w), out_dtype, "r")],
              gr=t // _pick(t, ROW_TILE), name=name)[0]


def rms_bwd(x, g, du, res, name, out_dtype=F32):
    t, w = x.shape

    def fn(x_, g_, du_, *res_):
        _, pull = jax.vjp(lambda a, b: _rms(a, b, w), x_, g_)
        dx, dg = pull(du_.astype(F32))
        return (dx + res_[0] if res_ else dx), dg

    ins = [(x, "r"), (g, "f"), (du, "r")] + ([(res, "r")] if res is not None else [])
    return ew(fn, ins, [((t, w), out_dtype, "r"), ((1, w), F32, "f")], gr=t // _pick(t, ROW_TILE), name=name)


def _rope(x, cos_t, sin_lo, sin_hi):
    return x * cos_t + _lane_roll(x, LANE - ROPE // 2) * sin_lo + _lane_roll(x, ROPE // 2) * sin_hi


def _mla_prep(qn, qp, kn, kp, cos_t, sin_lo, sin_hi, g_qn, g_qp, g_kn, g_kp):
    q = jnp.concatenate([_rms(qn, g_qn, NOPE), _rope(_rms(qp, g_qp, ROPE), cos_t, sin_lo, sin_hi)], axis=1)
    k = jnp.concatenate([_rms(kn, g_kn, NOPE), _rope(_rms(kp, g_kp, ROPE), cos_t, sin_lo, sin_hi)], axis=1)
    return q, k


def mla_prep_fwd(qn, qp, kn, kp, tabs, gains, heads):
    t = qn.shape[0]
    ins = [(qn, "rc"), (qp, "rc"), (kn, "rc"), (kp, "r")] + [(a, "r") for a in tabs] + [(g, "f") for g in gains]
    return ew(lambda *a: _mla_prep(*_f32(*a)), ins,
              [((t, heads * 2 * LANE), BF16, "rc"), ((t, heads * 2 * LANE), BF16, "rc")],
              gr=t // _pick(t, HEAD_ROW_TILE), gc=heads, name="mla_prep_fwd")


def mla_prep_bwd(qn, qp, kn, kp, tabs, gains, dq, dk, heads):
    t = qn.shape[0]

    def fn(qn_, qp_, kn_, kp_, c_, s1_, s2_, g1, g2, g3, g4, dq_, dk_):
        _, pull = jax.vjp(lambda a, b, c, d, e, f, g, h: _mla_prep(a, b, c, d, c_, s1_, s2_, e, f, g, h),
                          qn_, qp_, kn_, kp_, g1, g2, g3, g4)
        return pull((dq_, dk_))

    ins = ([(qn, "rc"), (qp, "rc"), (kn, "rc"), (kp, "r")] + [(a, "r") for a in tabs] + [(g, "f") for g in gains]
           + [(dq, "rc"), (dk, "rc")])
    hw = heads * LANE
    outs = [((t, hw), BF16, "rc"), ((t, hw), BF16, "rc"), ((t, hw), BF16, "rc"), ((t, LANE), F32, "r")] \
        + [((1, LANE), F32, "f")] * 4
    return ew(fn, ins, outs, gr=t // _pick(t, HEAD_ROW_TILE), gc=heads, name="mla_prep_bwd")


def _chunk_mask(row0, col0, shape, rows_are_queries):
    r = jnp.right_shift(row0 + lax.broadcasted_iota(jnp.int32, shape, 0), LOG2_CHUNK)
    c = jnp.right_shift(col0 + lax.broadcasted_iota(jnp.int32, shape, 1), LOG2_CHUNK)
    return (c <= r) if rows_are_queries else (r <= c)


def _block_pairs(nq, queries_outer):
    if queries_outer:
        pairs = [(i, j) for i in range(nq) for j in range(i + 1)]
    else:
        pairs = [(i, j) for j in range(nq) for i in range(j, nq)]
    return jnp.asarray([p[0] for p in pairs], jnp.int32), jnp.asarray([p[1] for p in pairs], jnp.int32)


def attn_fwd(q, k, vt, heads):
    t = q.shape[0]
    tq = _pick(t, ATT_TILE)
    sub = _pick(tq, ATT_SUB)
    qi, kj = _block_pairs(t // tq, True)
    scale = (NOPE + ROPE) ** -0.5
    scale2 = scale * LOG2_E

    def body(qi_ref, kj_ref, q_ref, k_ref, vt_ref, o_ref, lse_ref, m_s, l_s, acc):
        p = pl.program_id(1)
        i, j = qi_ref[p], kj_ref[p]

        @pl.when(j == 0)
        def _():
            m_s[...] = jnp.full_like(m_s, -jnp.inf)
            l_s[...] = jnp.zeros_like(l_s)
            acc[...] = jnp.zeros_like(acc)

        def step(diagonal):
            for b in range(tq // sub):
                cols = pl.ds(b * sub, sub)
                st = _bdot(k_ref[...], q_ref[cols, :], NT)
                if diagonal:
                    st = jnp.where(_chunk_mask(0, b * sub, (tq, sub), False), st, -jnp.inf)
                m_old = m_s[:, cols]
                m_new = jnp.maximum(m_old, jnp.max(st, axis=0, keepdims=True))
                alpha = jnp.exp2((m_old - m_new) * scale2)
                pt = jnp.exp2((st - m_new) * scale2)
                l_s[:, cols] = alpha * l_s[:, cols] + jnp.sum(pt, axis=0, keepdims=True)
                acc[:, cols] = alpha * acc[:, cols] + _bdot(vt_ref[...], pt, NN)
                m_s[:, cols] = m_new

        pl.when(j < i)(functools.partial(step, False))

        @pl.when(j == i)
        def _():
            step(True)
            o_ref[...] = jnp.transpose(acc[...] / l_s[...])
            lse_ref[...] = m_s[...] * scale + jnp.log(l_s[...])

    return pl.pallas_call(
        body,
        grid_spec=pltpu.PrefetchScalarGridSpec(
            num_scalar_prefetch=2, grid=(heads, qi.shape[0]),
            in_specs=[pl.BlockSpec((tq, 2 * LANE), lambda h, p, qi_, kj_: (qi_[p], h)),
                      pl.BlockSpec((tq, 2 * LANE), lambda h, p, qi_, kj_: (kj_[p], h)),
                      pl.BlockSpec((VHEAD, tq), lambda h, p, qi_, kj_: (h, kj_[p]))],
            out_specs=[pl.BlockSpec((tq, VHEAD), lambda h, p, qi_, kj_: (qi_[p], h)),
                       pl.BlockSpec((None, 1, tq), lambda h, p, qi_, kj_: (h, 0, qi_[p]))],
            scratch_shapes=[pltpu.VMEM((1, tq), F32), pltpu.VMEM((1, tq), F32), pltpu.VMEM((VHEAD, tq), F32)]),
        out_shape=[jax.ShapeDtypeStruct((t, heads * VHEAD), F32), jax.ShapeDtypeStruct((heads, 1, t), F32)],
        compiler_params=_params("parallel", "arbitrary"), name="mla_attn_fwd")(qi, kj, q, k, vt)


def attn_bwd(q, k, v, o, do, lse_row, heads):
    t = q.shape[0]
    tq = _pick(t, ATT_TILE)
    sub = _pick(tq, ATT_SUB)
    qi, kj = _block_pairs(t // tq, False)
    scale = (NOPE + ROPE) ** -0.5
    scale2 = scale * LOG2_E

    def body(qi_ref, kj_ref, q_ref, k_ref, v_ref, o_ref, do_ref, lse_ref, dq_ref, dk_ref, dv_ref):
        p = pl.program_id(1)
        i, j = qi_ref[p], kj_ref[p]

        @pl.when(p == 0)
        def _():
            dq_ref[...] = jnp.zeros_like(dq_ref)

        @pl.when(i == j)
        def _():
            dk_ref[...] = jnp.zeros_like(dk_ref)
            dv_ref[...] = jnp.zeros_like(dv_ref)

        def step(diagonal):
            for b in range(tq // sub):
                cols = pl.ds(b * sub, sub)
                do_i = do_ref[cols, :]
                prod = do_i * o_ref[cols, :]
                hi = prod.astype(BF16)
                mid = (prod - hi.astype(F32)).astype(BF16)
                lo = (prod - hi.astype(F32) - mid.astype(F32)).astype(BF16)
                ones = jnp.ones((8, VHEAD), BF16)
                delta = (_bdot(ones, hi, NT) + _bdot(ones, mid, NT) + _bdot(ones, lo, NT))[0:1, :]
                q_b = q_ref[cols, :]
                st = _bdot(k_ref[...], q_b, NT)
                pt = jnp.exp2(st * scale2 - lse_ref[:, cols] * LOG2_E)
                if diagonal:
                    pt = jnp.where(_chunk_mask(0, b * sub, (tq, sub), False), pt, 0.0)
                dv_ref[...] += _bdot(pt, do_i, NN)
                dpt = _bdot(v_ref[...], do_i, NT)
                dst = pt * (dpt - delta) * scale
                dk_ref[...] += _bdot(dst, q_b, NN)
                rows = pl.ds(pl.multiple_of(i * tq + b * sub, sub), sub)
                dq_ref[rows, :] += _bdot(dst, k_ref[...], TN)

        pl.when(i > j)(functools.partial(step, False))
        pl.when(i == j)(functools.partial(step, True))

    qmap = lambda h, p, qi_, kj_: (qi_[p], h)
    kmap = lambda h, p, qi_, kj_: (kj_[p], h)
    return pl.pallas_call(
        body,
        grid_spec=pltpu.PrefetchScalarGridSpec(
            num_scalar_prefetch=2, grid=(heads, qi.shape[0]),
            in_specs=[pl.BlockSpec((tq, 2 * LANE), qmap), pl.BlockSpec((tq, 2 * LANE), kmap),
                      pl.BlockSpec((tq, VHEAD), kmap), pl.BlockSpec((tq, VHEAD), qmap), pl.BlockSpec((tq, VHEAD), qmap),
                      pl.BlockSpec((None, 1, tq), lambda h, p, qi_, kj_: (h, 0, qi_[p]))],
            out_specs=[pl.BlockSpec((t, 2 * LANE), lambda h, p, qi_, kj_: (0, h)),
                       pl.BlockSpec((tq, 2 * LANE), kmap), pl.BlockSpec((tq, VHEAD), kmap)]),
        out_shape=[jax.ShapeDtypeStruct((t, heads * 2 * LANE), F32), jax.ShapeDtypeStruct((t, heads * 2 * LANE), F32),
                   jax.ShapeDtypeStruct((t, heads * VHEAD), F32)],
        compiler_params=_params("parallel", "arbitrary"), name="mla_attn_bwd")(qi, kj, q, k, v, o, do, lse_row)


def _shift_down(x, s):
    if s == 0:
        return x
    rows = lax.broadcasted_iota(jnp.int32, x.shape, 0)
    return jnp.where(rows >= s, pltpu.roll(x, s, 0), 0.0)


def _shift_up(x, s):
    if s == 0:
        return x
    t = x.shape[0]
    rows = lax.broadcasted_iota(jnp.int32, x.shape, 0)
    return jnp.where(rows < t - s, pltpu.roll(x, t - s, 0), 0.0)


def _conv(x, w_ref, width):
    return sum(_shift_down(x, width - 1 - j) * w_ref[j:j + 1, :] for j in range(width))


def _conv_bwd(x, dpre, w_ref, dw_ref, width):
    dx = sum(_shift_up(dpre, width - 1 - j) * w_ref[j:j + 1, :] for j in range(width))
    for j in range(width):
        dw_ref[j:j + 1, :] = jnp.sum(dpre * _shift_down(x, width - 1 - j), axis=0, keepdims=True)
    return dx


def _dsilu(z):
    s = _sigmoid(z)
    return s * (1.0 + z * (1.0 - s))


def conv_qk_fwd(x, w, colscale):
    t, c = x.shape
    tc = _pick(c, 256)

    def body(x_ref, w_ref, s_ref, o_ref):
        o_ref[...] = (_silu(_conv(x_ref[...], w_ref, MLSTM_CONV)) * s_ref[...]).astype(o_ref.dtype)

    return pl.pallas_call(
        body, grid=(c // tc,),
        in_specs=[pl.BlockSpec((t, tc), lambda j: (0, j)), pl.BlockSpec((MLSTM_CONV, tc), lambda j: (0, j)),
                  pl.BlockSpec((1, tc), lambda j: (0, j))],
        out_specs=pl.BlockSpec((t, tc), lambda j: (0, j)), out_shape=jax.ShapeDtypeStruct((t, c), BF16),
        compiler_params=_params("parallel"), name="conv_qk_fwd")(x, w, colscale)


def conv_qk_bwd(x, w, colscale, dq, dk):
    t, c = x.shape
    tc = _pick(c // 2, 256)
    half = (c // 2) // tc

    def body(x_ref, w_ref, s_ref, dq_ref, dk_ref, dx_ref, dw_ref):
        j = pl.program_id(0)
        x_ = x_ref[...]
        dy = jnp.where(j < half, dq_ref[...], dk_ref[...])
        dpre = dy * s_ref[...] * _dsilu(_conv(x_, w_ref, MLSTM_CONV))
        dx_ref[...] = _conv_bwd(x_, dpre, w_ref, dw_ref, MLSTM_CONV).astype(dx_ref.dtype)

    return pl.pallas_call(
        body, grid=(c // tc,),
        in_specs=[pl.BlockSpec((t, tc), lambda j: (0, j)), pl.BlockSpec((MLSTM_CONV, tc), lambda j: (0, j)),
                  pl.BlockSpec((1, tc), lambda j: (0, j)),
                  pl.BlockSpec((t, tc), lambda j: (0, jnp.minimum(j, half - 1))),
                  pl.BlockSpec((t, tc), lambda j: (0, jnp.maximum(j - half, 0)))],
        out_specs=[pl.BlockSpec((t, tc), lambda j: (0, j)), pl.BlockSpec((MLSTM_CONV, tc), lambda j: (0, j))],
        out_shape=[jax.ShapeDtypeStruct((t, c), BF16), jax.ShapeDtypeStruct((MLSTM_CONV, c), F32)],
        compiler_params=_params("parallel"), name="conv_qk_bwd")(x, w, colscale, dq, dk)


def _mlstm_chunk(q, k, v, i_col, i_row, f_col, f_row, c_mat, n_vec, m):
    shape = (CHUNK, CHUNK)
    r = lax.broadcasted_iota(jnp.int32, shape, 0)
    c = lax.broadcasted_iota(jnp.int32, shape, 1)
    tril = c <= r
    lf_col, lf_row = _log_sigmoid(f_col), _log_sigmoid(f_row)
    bc_col = jnp.sum(jnp.where(tril, lf_row, 0.0), axis=1, keepdims=True)
    bc_row = jnp.sum(jnp.where(r <= c, lf_col, 0.0), axis=0, keepdims=True)
    logw = jnp.where(tril, bc_col - bc_row + i_row, -jnp.inf)
    inter = bc_col + m
    m_t = lax.stop_gradient(jnp.maximum(inter, jnp.max(logw, axis=1, keepdims=True)))
    w_intra = jnp.exp(logw - m_t)
    w_inter = jnp.exp(inter - m_t)
    sc = _dnt(q, k) * w_intra
    num = w_inter * _dnn(q, c_mat) + _dnn(sc, v)
    den = w_inter * jnp.sum(q * n_vec, axis=1, keepdims=True) + jnp.sum(sc, axis=1, keepdims=True)
    h = num / jnp.maximum(jnp.abs(den), jnp.exp(-m_t))
    b_last = jnp.sum(lf_row, axis=1, keepdims=True)
    m_new = lax.stop_gradient(jnp.maximum(b_last + m, jnp.max(b_last - bc_row + i_row, axis=1, keepdims=True)))
    decay = jnp.exp(b_last + m - m_new)
    uk = jnp.exp(b_last - bc_col + i_col - m_new) * k
    return h, decay * c_mat + _dtn(uk, v), decay * n_vec + jnp.sum(uk, axis=0, keepdims=True), m_new


def _mlstm_group(heads):
    return MLSTM_HEADS_PER_STEP if heads % MLSTM_HEADS_PER_STEP == 0 else 1


def _mlstm_specs(heads, grp, rev, nc):
    ci = (lambda c: nc - 1 - c) if rev else (lambda c: c)
    return dict(
        q=pl.BlockSpec((CHUNK, grp * MLSTM_DK), lambda g, c: (ci(c), g)),
        k=pl.BlockSpec((CHUNK, grp * MLSTM_DK), lambda g, c: (ci(c), heads // grp + g)),
        v=pl.BlockSpec((CHUNK, grp * MLSTM_DV), lambda g, c: (ci(c), g)),
        gc=pl.BlockSpec((grp, None, CHUNK, 2), lambda g, c: (g, ci(c), 0, 0)),
        gr=pl.BlockSpec((grp, None, 2, CHUNK), lambda g, c: (g, ci(c), 0, 0)),
        b=pl.BlockSpec((grp, 1, 2), lambda g, c: (g, 0, 0)),
        cm=pl.BlockSpec((grp, None, MLSTM_DK, MLSTM_DV), lambda g, c: (g, ci(c), 0, 0)),
        vec=pl.BlockSpec((grp, None, 1, LANE), lambda g, c: (g, ci(c), 0, 0)),
    )


def _gates(gc_ref, gr_ref, b_ref, s):
    bi, bf = b_ref[s, :, 0:1], b_ref[s, :, 1:2]
    return gc_ref[s, :, 0:1] + bi, gr_ref[s, 0:1, :] + bi, gc_ref[s, :, 1:2] + bf, gr_ref[s, 1:2, :] + bf


def mlstm_fwd(qk, v, gcol, grow, bias, heads):
    t = qk.shape[0]
    nc = t // CHUNK
    grp = _mlstm_group(heads)
    sp = _mlstm_specs(heads, grp, False, nc)

    def body(q_ref, k_ref, v_ref, gc_ref, gr_ref, b_ref, h_ref, c_ref, n_ref, m_ref, c_s, n_s, m_s):
        @pl.when(pl.program_id(1) == 0)
        def _():
            c_s[...] = jnp.zeros_like(c_s)
            n_s[...] = jnp.zeros_like(n_s)
            m_s[...] = jnp.zeros_like(m_s)

        c_ref[...] = c_s[...]
        n_ref[...] = n_s[...]
        m_ref[...] = m_s[...]
        for s in range(grp):
            qs, vs = slice(s * MLSTM_DK, (s + 1) * MLSTM_DK), slice(s * MLSTM_DV, (s + 1) * MLSTM_DV)
            q, k, v_ = _f32(q_ref[:, qs], k_ref[:, qs], v_ref[:, vs])
            h, c_new, n_new, m_new = _mlstm_chunk(q, k, v_, *_gates(gc_ref, gr_ref, b_ref, s), c_s[s], n_s[s], m_s[s, :, 0:1])
            h_ref[:, vs] = h
            c_s[s] = c_new
            n_s[s] = n_new
            m_s[s] = jnp.broadcast_to(m_new, (1, LANE))

    return pl.pallas_call(
        body, grid=(heads // grp, nc),
        in_specs=[sp["q"], sp["k"], sp["v"], sp["gc"], sp["gr"], sp["b"]],
        out_specs=[sp["v"], sp["cm"], sp["vec"], sp["vec"]],
        out_shape=[jax.ShapeDtypeStruct((t, heads * MLSTM_DV), F32),
                   jax.ShapeDtypeStruct((heads, nc, MLSTM_DK, MLSTM_DV), F32),
                   jax.ShapeDtypeStruct((heads, nc, 1, LANE), F32), jax.ShapeDtypeStruct((heads, nc, 1, LANE), F32)],
        scratch_shapes=[pltpu.VMEM((grp, MLSTM_DK, MLSTM_DV), F32), pltpu.VMEM((grp, 1, LANE), F32),
                        pltpu.VMEM((grp, 1, LANE), F32)],
        compiler_params=_params("parallel", "arbitrary"), name="mlstm_fwd")(qk, qk, v, gcol, grow, bias)


def mlstm_bwd(qk, v, gcol, grow, bias, c_all, n_all, m_all, dh, heads):
    t = qk.shape[0]
    nc = t // CHUNK
    grp = _mlstm_group(heads)
    sp = _mlstm_specs(heads, grp, True, nc)

    def body(q_ref, k_ref, v_ref, gc_ref, gr_ref, b_ref, c_ref, n_ref, m_ref, dh_ref,
             dq_ref, dk_ref, dv_ref, dgc_ref, dgr_ref, dc_s, dn_s):
        @pl.when(pl.program_id(1) == 0)
        def _():
            dc_s[...] = jnp.zeros_like(dc_s)
            dn_s[...] = jnp.zeros_like(dn_s)

        for s in range(grp):
            qs, vs = slice(s * MLSTM_DK, (s + 1) * MLSTM_DK), slice(s * MLSTM_DV, (s + 1) * MLSTM_DV)
            q, k, v_ = _f32(q_ref[:, qs], k_ref[:, qs], v_ref[:, vs])
            m = m_ref[s, :, 0:1]
            _, pull = jax.vjp(lambda *a: _mlstm_chunk(*a, m)[:3], q, k, v_, *_gates(gc_ref, gr_ref, b_ref, s),
                              c_ref[s], n_ref[s])
            dq, dk, dv, di_col, di_row, df_col, df_row, dc, dn = pull((dh_ref[:, vs], dc_s[s], dn_s[s]))
            dq_ref[:, qs] = dq
            dk_ref[:, qs] = dk
            dv_ref[:, vs] = dv.astype(dv_ref.dtype)
            dgc_ref[s, :, 0:1] = di_col
            dgc_ref[s, :, 1:2] = df_col
            dgr_ref[s, 0:1, :] = di_row
            dgr_ref[s, 1:2, :] = df_row
            dc_s[s] = dc
            dn_s[s] = dn

    qspec = pl.BlockSpec((CHUNK, grp * MLSTM_DK), lambda g, c: (nc - 1 - c, g))
    return pl.pallas_call(
        body, grid=(heads // grp, nc),
        in_specs=[sp["q"], sp["k"], sp["v"], sp["gc"], sp["gr"], sp["b"], sp["cm"], sp["vec"], sp["vec"], sp["v"]],
        out_specs=[qspec, qspec, sp["v"], sp["gc"], sp["gr"]],
        out_shape=[jax.ShapeDtypeStruct((t, heads * MLSTM_DK), F32), jax.ShapeDtypeStruct((t, heads * MLSTM_DK), F32),
                   jax.ShapeDtypeStruct((t, heads * MLSTM_DV), BF16),
                   jax.ShapeDtypeStruct(gcol.shape, F32), jax.ShapeDtypeStruct(grow.shape, F32)],
        scratch_shapes=[pltpu.VMEM((grp, MLSTM_DK, MLSTM_DV), F32), pltpu.VMEM((grp, 1, LANE), F32)],
        compiler_params=_params("parallel", "arbitrary"),
        name="mlstm_bwd")(qk, qk, v, gcol, grow, bias, c_all, n_all, m_all, dh)


def _hnorm_gate(h, zo, g):
    return _rms(h, g, MLSTM_DV) * _sigmoid(zo)


def _cross_core(q, k, v, g_q, g_k, heads):
    scale = CROSS_DH ** -0.5
    outs = []
    for h in range(heads):
        s = _dnt(_rms(q[h], g_q, CROSS_DH), _rms(k[h], g_k, CROSS_DH)) * scale
        p = jnp.exp(s - lax.stop_gradient(jnp.max(s, axis=1, keepdims=True)))
        p = p / jnp.sum(p, axis=1, keepdims=True)
        outs.append(_dnn(p, v[h]))
    return jnp.concatenate(outs, axis=1)


def _split_heads(ref, heads):
    return [ref[:, h * CROSS_DH:(h + 1) * CROSS_DH].astype(F32) for h in range(heads)]


def cross_fwd(q, k, v, g_q, g_k, heads):
    t = q.shape[0]
    tm = _pick(t, ATT_TILE)
    full = lambda a: pl.BlockSpec(a.shape, lambda i: (0, 0))

    def body(q_ref, k_ref, v_ref, gq_ref, gk_ref, o_ref):
        o_ref[...] = _cross_core(_split_heads(q_ref, heads), _split_heads(k_ref, heads), _split_heads(v_ref, heads),
                                 gq_ref[...], gk_ref[...], heads).astype(o_ref.dtype)

    return pl.pallas_call(
        body, grid=(t // tm,), in_specs=[pl.BlockSpec((tm, q.shape[1]), lambda i: (i, 0)), full(k), full(v), full(g_q), full(g_k)],
        out_specs=pl.BlockSpec((tm, q.shape[1]), lambda i: (i, 0)), out_shape=jax.ShapeDtypeStruct(q.shape, BF16),
        compiler_params=_params("parallel"), name="cross_fwd")(q, k, v, g_q, g_k)


def cross_bwd(q, k, v, g_q, g_k, do, heads):
    t, w = q.shape
    tm = _pick(t, ATT_TILE)
    full = lambda a: pl.BlockSpec(a.shape, lambda i: (0, 0))

    def body(q_ref, k_ref, v_ref, gq_ref, gk_ref, do_ref, dq_ref, dk_ref, dv_ref, dgq_ref, dgk_ref):
        qs, ks, vs = _split_heads(q_ref, heads), _split_heads(k_ref, heads), _split_heads(v_ref, heads)
        _, pull = jax.vjp(lambda a, b, c, d, e: _cross_core(a, b, c, d, e, heads), qs, ks, vs, gq_ref[...], gk_ref[...])
        dqs, dks, dvs, dgq, dgk = pull(do_ref[...])
        first = pl.program_id(0) == 0
        for h in range(heads):
            cols = slice(h * CROSS_DH, (h + 1) * CROSS_DH)
            dq_ref[:, cols] = dqs[h].astype(dq_ref.dtype)
            _store(dk_ref.at[:, cols], dks[h], first)
            _store(dv_ref.at[:, cols], dvs[h], first)
        _store(dgq_ref, dgq, first)
        _store(dgk_ref, dgk, first)

    row = pl.BlockSpec((tm, w), lambda i: (i, 0))
    return pl.pallas_call(
        body, grid=(t // tm,), in_specs=[row, full(k), full(v), full(g_q), full(g_k), row],
        out_specs=[row, full(k), full(v), full(g_q), full(g_k)],
        out_shape=[jax.ShapeDtypeStruct(q.shape, BF16), jax.ShapeDtypeStruct(k.shape, F32), jax.ShapeDtypeStruct(v.shape, F32),
                   jax.ShapeDtypeStruct(g_q.shape, F32), jax.ShapeDtypeStruct(g_k.shape, F32)],
        compiler_params=_params("arbitrary"), name="cross_bwd")(q, k, v, g_q, g_k, do)


def ffn_glu_fwd(hg, hv, wg, wv, bg, bv):
    t, f = hg.shape
    tc = _pick(f, LANE)
    col = pl.BlockSpec((t, tc), lambda j: (0, j))
    tap = pl.BlockSpec((FFN_CONV, tc), lambda j: (0, j))
    one = pl.BlockSpec((1, tc), lambda j: (0, j))

    def body(hg_ref, hv_ref, wg_ref, wv_ref, bg_ref, bv_ref, o_ref):
        gate = _conv(hg_ref[...], wg_ref, FFN_CONV) + bg_ref[...]
        val = _conv(hv_ref[...], wv_ref, FFN_CONV) + bv_ref[...]
        o_ref[...] = (_silu(gate) * val).astype(o_ref.dtype)

    return pl.pallas_call(body, grid=(f // tc,), in_specs=[col, col, tap, tap, one, one], out_specs=col,
                          out_shape=jax.ShapeDtypeStruct((t, f), BF16), compiler_params=_params("parallel"),
                          name="ffn_glu_fwd")(hg, hv, wg, wv, bg, bv)


def ffn_glu_bwd(hg, hv, wg, wv, bg, bv, dact):
    t, f = hg.shape
    tc = _pick(f, LANE)
    col = pl.BlockSpec((t, tc), lambda j: (0, j))
    tap = pl.BlockSpec((FFN_CONV, tc), lambda j: (0, j))
    one = pl.BlockSpec((1, tc), lambda j: (0, j))

    def body(hg_ref, hv_ref, wg_ref, wv_ref, bg_ref, bv_ref, da_ref, dhg_ref, dhv_ref, dwg_ref, dwv_ref, dbg_ref, dbv_ref):
        xg, xv, da = hg_ref[...], hv_ref[...], da_ref[...]
        gate = _conv(xg, wg_ref, FFN_CONV) + bg_ref[...]
        val = _conv(xv, wv_ref, FFN_CONV) + bv_ref[...]
        dgate = da * val * _dsilu(gate)
        dval = da * _silu(gate)
        dbg_ref[...] = jnp.sum(dgate, axis=0, keepdims=True)
        dbv_ref[...] = jnp.sum(dval, axis=0, keepdims=True)
        dhg_ref[...] = _conv_bwd(xg, dgate, wg_ref, dwg_ref, FFN_CONV).astype(dhg_ref.dtype)
        dhv_ref[...] = _conv_bwd(xv, dval, wv_ref, dwv_ref, FFN_CONV).astype(dhv_ref.dtype)

    return pl.pallas_call(
        body, grid=(f // tc,), in_specs=[col, col, tap, tap, one, one, col], out_specs=[col, col, tap, tap, one, one],
        out_shape=[jax.ShapeDtypeStruct((t, f), BF16), jax.ShapeDtypeStruct((t, f), BF16),
                   jax.ShapeDtypeStruct((FFN_CONV, f), F32), jax.ShapeDtypeStruct((FFN_CONV, f), F32),
                   jax.ShapeDtypeStruct((1, f), F32), jax.ShapeDtypeStruct((1, f), F32)],
        compiler_params=_params("parallel"), name="ffn_glu_bwd")(hg, hv, wg, wv, bg, bv, dact)


def _adamw(g, w, m, v):
    m = ADAM_B1 * m + (1.0 - ADAM_B1) * g
    v = ADAM_B2 * v + (1.0 - ADAM_B2) * (g * g)
    m_hat = m / (1.0 - ADAM_B1 ** ADAM_STEP)
    v_hat = v / (1.0 - ADAM_B2 ** ADAM_STEP)
    return -ADAM_LR * (m_hat / (jnp.sqrt(v_hat) + ADAM_EPS) + ADAM_WD * w), m, v


def adamw(g, w, m, v, name):
    r, c = g.shape
    tr = r
    for cand in (256, 128, 64, 32, 16, 8):
        if r % cand == 0 and cand * c * 4 <= (1 << 21):
            tr = cand
            break
    return ew(_adamw, [(g, "r"), (w, "r"), (m, "r"), (v, "r")], [((r, c), F32, "r")] * 3, gr=r // tr, name=name)


ANY = pl.BlockSpec(memory_space=pl.ANY)


def _place():
    x, y, c = lax.axis_index("x"), lax.axis_index("y"), lax.axis_index("c")
    return x, y, c, [(1 - x, y), (x, 1 - y), (1 - x, 1 - y)]


def _rcopy(src, dst, send, recv, k, to):
    return pltpu.make_async_remote_copy(src_ref=src, dst_ref=dst, send_sem=send.at[k], recv_sem=recv.at[k],
                                        device_id=to, device_id_type=MESH)


def gather_shards(bigs, smalls):
    nb, na = len(bigs), len(bigs) + len(smalls)
    arrays = list(bigs) + list(smalls)

    def body(*refs):
        ins, outs = refs[:na], refs[na:2 * na]
        send, recv = refs[2 * na:]
        x, y, c, chips = _place()
        me, sib = 2 * x + y, (x, y, 1 - c)

        def half(ref, a, which):
            rows = arrays[a].shape[0] // 2
            return ref.at[pl.ds(which * rows, rows)]

        started = []
        for a in range(na):
            for j, (cx, cy) in enumerate(chips):
                if a < nb:
                    cp = _rcopy(half(ins[a], a, c), half(outs[a].at[me], a, c), send, recv, 6 * a + j, (cx, cy, c))
                else:
                    cp = _rcopy(ins[a], outs[a].at[me], send, recv, 6 * nb + 3 * (a - nb) + j, (cx, cy, c))
                cp.start()
                started.append(cp)
        for a in range(nb):
            for j, (cx, cy) in enumerate(chips):
                landed = half(outs[a].at[2 * cx + cy], a, c)
                _rcopy(landed, landed, send, recv, 6 * a + j, (cx, cy, c)).wait_recv()
                cp = _rcopy(landed, landed, send, recv, 6 * a + 3 + j, sib)
                cp.start()
                started.append(cp)
        for a in range(na):
            for j, (cx, cy) in enumerate(chips):
                if a < nb:
                    dst = half(outs[a].at[2 * cx + cy], a, 1 - c)
                    _rcopy(dst, dst, send, recv, 6 * a + 3 + j, sib).wait_recv()
                else:
                    dst = outs[a].at[2 * cx + cy]
                    _rcopy(dst, dst, send, recv, 6 * nb + 3 * (a - nb) + j, (cx, cy, c)).wait_recv()
        for cp in started:
            cp.wait_send()

    n_sem = 6 * nb + 3 * (na - nb)
    gathered = pl.pallas_call(
        body, in_specs=[ANY] * na, out_specs=[ANY] * na,
        out_shape=[jax.ShapeDtypeStruct((N_CHIPS,) + a.shape, a.dtype) for a in arrays],
        scratch_shapes=[pltpu.SemaphoreType.DMA((n_sem,)), pltpu.SemaphoreType.DMA((n_sem,))],
        name="gather_shards")(*arrays)
    chip = 2 * lax.axis_index("x") + lax.axis_index("y")
    return [lax.dynamic_update_slice(g, a[None], (chip, 0, 0)) for g, a in zip(gathered, arrays)]


def sibling_halves(grads):
    na = len(grads)

    def body(*refs):
        ins, outs = refs[:na], refs[na:2 * na]
        send, recv = refs[2 * na:]
        x, y, c, _ = _place()
        cps = []
        for a in range(na):
            rows = grads[a].shape[1] // 2
            cp = _rcopy(ins[a].at[:, pl.ds((1 - c) * rows, rows)], outs[a], send, recv, a, (x, y, 1 - c))
            cp.start()
            cps.append(cp)
        for cp in cps:
            cp.wait()

    return pl.pallas_call(
        body, in_specs=[ANY] * na, out_specs=[ANY] * na,
        out_shape=[jax.ShapeDtypeStruct((g.shape[0], g.shape[1] // 2, g.shape[2]), g.dtype) for g in grads],
        scratch_shapes=[pltpu.SemaphoreType.DMA((na,)), pltpu.SemaphoreType.DMA((na,))], name="sibling_halves")(*grads)


def join_halves(halves):
    na = len(halves)

    def body(*refs):
        ins, outs = refs[:na], refs[na:2 * na]
        send, recv = refs[2 * na:]
        x, y, c, _ = _place()
        cps = []
        for a in range(na):
            cp = _rcopy(ins[a].at[c], outs[a].at[c], send, recv, a, (x, y, 1 - c))
            cp.start()
            cps.append(cp)
        for a in range(na):
            dst = outs[a].at[1 - c]
            _rcopy(dst, dst, send, recv, a, (x, y, 1 - c)).wait_recv()
        for cp in cps:
            cp.wait_send()

    return pl.pallas_call(
        body, in_specs=[ANY] * na, out_specs=[ANY] * na,
        out_shape=[jax.ShapeDtypeStruct(h.shape, h.dtype) for h in halves],
        input_output_aliases={a: a for a in range(na)},
        scratch_shapes=[pltpu.SemaphoreType.DMA((na,)), pltpu.SemaphoreType.DMA((na,))],
        name="join_halves")(*halves)


HBM = pl.BlockSpec(memory_space=pltpu.HBM)
SEM = pl.BlockSpec(memory_space=pltpu.SEMAPHORE)
SIDE_EFFECT = pltpu.SideEffectType.DATAFLOW_SIDE_EFFECTING


def split_start(name, srcs, land_shapes, n_copies, copies_fn, after):
    ns, nl = len(srcs), len(land_shapes)
    afters = tuple(after) if isinstance(after, (tuple, list)) else (after,)

    def body(*refs):
        ins, lands = refs[:ns], refs[ns:ns + nl]
        send, recv, token = refs[ns + nl + len(afters)], refs[ns + nl + len(afters) + 1], refs[-1]
        for k, (src, dst, dev) in enumerate(copies_fn(ins, lands, False)):
            pltpu.make_async_remote_copy(src_ref=src, dst_ref=dst, send_sem=send.at[k], recv_sem=recv.at[k],
                                         device_id=dev, device_id_type=MESH).start()
        token[...] = jnp.zeros_like(token)

    outs = pl.pallas_call(
        body, name=name,
        out_shape=(pltpu.SemaphoreType.DMA((n_copies,)), pltpu.SemaphoreType.DMA((n_copies,)),
                   *[pltpu.HBM(a.shape, a.dtype) for a in srcs], *[pltpu.HBM(s, dt) for s, dt in land_shapes],
                   jax.ShapeDtypeStruct((8, LANE), F32)),
        in_specs=[HBM] * (ns + nl) + [ANY] * len(afters),
        out_specs=(SEM, SEM, *[HBM] * (ns + nl), pl.BlockSpec(memory_space=pltpu.VMEM)),
        input_output_aliases={i: 2 + i for i in range(ns + nl)},
        compiler_params=pltpu.CompilerParams(has_side_effects=SIDE_EFFECT),
    )(*[pltpu.with_memory_space_constraint(a, pltpu.HBM) for a in srcs],
      *[pltpu.with_memory_space_constraint(lax.empty(s, dt), pltpu.HBM) for s, dt in land_shapes], *afters)
    return outs[0], outs[1], list(outs[2:2 + ns]), list(outs[2 + ns:2 + ns + nl]), outs[-1]


def split_wait(name, started, n_copies, copies_fn, after):
    send, recv, srcs, lands, _ = started
    ns, nl = len(srcs), len(lands)
    afters = tuple(after) if isinstance(after, (tuple, list)) else (after,)

    def body(*refs):
        ins, lnd = refs[:ns], refs[ns:ns + nl]
        send_ref, recv_ref = refs[ns + nl], refs[ns + nl + 1]
        for k, (src, dst, dev) in enumerate(copies_fn(ins, lnd, True)):
            cp = pltpu.make_async_remote_copy(src_ref=src, dst_ref=dst, send_sem=send_ref.at[k], recv_sem=recv_ref.at[k],
                                              device_id=dev, device_id_type=MESH)
            cp.wait_send()
            cp.wait_recv()

    outs = pl.pallas_call(
        body, name=name,
        out_shape=tuple(pltpu.HBM(a.shape, a.dtype) for a in srcs + lands),
        in_specs=[HBM] * (ns + nl) + [SEM, SEM] + [ANY] * len(afters), out_specs=tuple([HBM] * (ns + nl)),
        input_output_aliases={i: i for i in range(ns + nl)},
        compiler_params=pltpu.CompilerParams(has_side_effects=SIDE_EFFECT),
    )(*srcs, *lands, send, recv, *afters)
    return list(outs[:ns]), list(outs[ns:])


def _gather_copies(ins, lands, waiting):
    x, y, c, chips = _place()
    return [(ins[a], lands[a].at[2 * cx + cy] if waiting else lands[a].at[2 * x + y], (cx, cy, c))
            for a in range(len(ins)) for cx, cy in chips]


def _scatter_copies(ins, lands, waiting):
    del waiting
    _, _, c, chips = _place()
    return [(ins[a].at[2 * cx + cy], lands[a].at[j], (cx, cy, c)) for a in range(len(ins)) for j, (cx, cy) in enumerate(chips)]


def allreduce_small(vec):
    r = vec.shape[0]

    def body(x_ref, sum_ref, all_ref, send, recv):
        x, y, c, _ = _place()
        me = 4 * x + 2 * y + c
        all_ref[me] = x_ref[...]
        cps, peers = [], []
        for mask in range(1, 8):
            px = 1 - x if mask & 4 else x
            py = 1 - y if mask & 2 else y
            pc = 1 - c if mask & 1 else c
            peers.append(4 * px + 2 * py + pc)
            cp = _rcopy(x_ref, all_ref.at[me], send, recv, mask - 1, (px, py, pc))
            cp.start()
            cps.append(cp)
        for k, cp in enumerate(cps):
            _rcopy(x_ref, all_ref.at[peers[k]], send, recv, k, (x, y, c)).wait_recv()
        for cp in cps:
            cp.wait_send()
        total = all_ref[0]
        for d in range(1, 8):
            total = total + all_ref[d]
        sum_ref[...] = total

    vm = pl.BlockSpec(memory_space=pltpu.VMEM)
    return pl.pallas_call(
        body, in_specs=[vm], out_specs=vm, out_shape=jax.ShapeDtypeStruct((r, LANE), F32),
        scratch_shapes=[pltpu.VMEM((8, r, LANE), F32), pltpu.SemaphoreType.DMA((7,)), pltpu.SemaphoreType.DMA((7,))],
        compiler_params=pltpu.CompilerParams(vmem_limit_bytes=VMEM_LIMIT), name="allreduce_small")(vec)


def _row_tile(rows):
    for cand in (256, 128, 64, 32, 16):
        if rows % cand == 0:
            return cand
    return rows


def add_sibling(grad, recv, c_idx):
    _, rows, cols = grad.shape
    hr = rows // 2
    tr = _row_tile(hr)
    nb = hr // tr

    def body(c_ref, g_ref, r_ref, o_ref):
        o_ref[...] = (g_ref[...].astype(F32) + r_ref[...].astype(F32)).astype(o_ref.dtype)

    return pl.pallas_call(
        body,
        grid_spec=pltpu.PrefetchScalarGridSpec(
            num_scalar_prefetch=1, grid=(N_CHIPS, nb),
            in_specs=[pl.BlockSpec((None, tr, cols), lambda k, r, c_ref: (k, c_ref[0] * nb + r, 0)),
                      pl.BlockSpec((None, tr, cols), lambda k, r, c_ref: (k, r, 0))],
            out_specs=pl.BlockSpec((None, tr, cols), lambda k, r, c_ref: (k, r, 0))),
        out_shape=jax.ShapeDtypeStruct((N_CHIPS, hr, cols), BF16),
        compiler_params=_params("parallel", "parallel"), name="add_sibling")(c_idx, grad, recv)


def sum_chips(part, others, place_idx):
    _, hr, cols = part.shape
    tr = _row_tile(hr)

    def body(k_ref, p_ref, o0_ref, o1_ref, o2_ref, out_ref):
        out_ref[...] = ((p_ref[...].astype(F32) + o0_ref[...].astype(F32)) + o1_ref[...].astype(F32)) + o2_ref[...].astype(F32)

    other = lambda j: pl.BlockSpec((None, tr, cols), lambda r, k_ref: (j, r, 0))
    return pl.pallas_call(
        body,
        grid_spec=pltpu.PrefetchScalarGridSpec(
            num_scalar_prefetch=1, grid=(hr // tr,),
            in_specs=[pl.BlockSpec((None, tr, cols), lambda r, k_ref: (k_ref[0], r, 0)), other(0), other(1), other(2)],
            out_specs=pl.BlockSpec((None, tr, cols), lambda r, k_ref: (k_ref[1], r, 0))),
        out_shape=jax.ShapeDtypeStruct((2, hr, cols), F32),
        compiler_params=_params("parallel"), name="sum_chips")(place_idx, part, others, others, others)


def _pad_lanes(a, width=LANE):
    return jnp.pad(a, ((0, 0), (0, width - a.shape[1])))


def _cols_from_shards(g):
    return jnp.transpose(g, (1, 0, 2)).reshape(g.shape[1], -1)


def _cols_to_shards(w):
    k, n4 = w.shape
    return jnp.transpose(w.reshape(k, N_CHIPS, n4 // N_CHIPS), (1, 0, 2))


def kernel(x, mem, positions, g_mix, w_in, g_qa, w_qb, g_kva, w_kvb, g_qn_nope, g_qn_pe, g_kn_nope, g_kn_pe, conv_qk, b_if, g_hnorm, p_a, p_b, w_out, g_cross, g_mem, wq_c, wk_c, wv_c, g_cq, g_ck, wo_c, g_ffn, w_up, conv_ffn, b_conv_ffn, w_down, loss_target, m_g_mix, m_w_in, m_g_qa, m_w_qb, m_g_kva, m_w_kvb, m_g_qn_nope, m_g_qn_pe, m_g_kn_nope, m_g_kn_pe, m_conv_qk, m_b_if, m_g_hnorm, m_p_a, m_p_b, m_w_out, m_g_cross, m_g_mem, m_wq_c, m_wk_c, m_wv_c, m_g_cq, m_g_ck, m_wo_c, m_g_ffn, m_w_up, m_conv_ffn, m_b_conv_ffn, m_w_down, v_g_mix, v_w_in, v_g_qa, v_w_qb, v_g_kva, v_w_kvb, v_g_qn_nope, v_g_qn_pe, v_g_kn_nope, v_g_kn_pe, v_conv_qk, v_b_if, v_g_hnorm, v_p_a, v_p_b, v_w_out, v_g_cross, v_g_mem, v_wq_c, v_wk_c, v_wv_c, v_g_cq, v_g_ck, v_wo_c, v_g_ffn, v_w_up, v_conv_ffn, v_b_conv_ffn, v_w_down):
    names = ["g_mix", "w_in", "g_qa", "w_qb", "g_kva", "w_kvb", "g_qn_nope", "g_qn_pe", "g_kn_nope", "g_kn_pe", "conv_qk",
             "b_if", "g_hnorm", "p_a", "p_b", "w_out", "g_cross", "g_mem", "wq_c", "wk_c", "wv_c", "g_cq", "g_ck", "wo_c",
             "g_ffn", "w_up", "conv_ffn", "b_conv_ffn", "w_down"]
    env = locals()
    wts = {n: env[n] for n in names}
    mom = {n: env["m_" + n] for n in names}
    var = {n: env["v_" + n] for n in names}

    xi, yi, ci = lax.axis_index("x"), lax.axis_index("y"), lax.axis_index("c")
    chip = 2 * xi + yi
    place_arr = jnp.stack([chip, ci]).astype(jnp.int32)
    c_arr = jnp.reshape(ci, (1,)).astype(jnp.int32)

    x2d, tgt, mem2d = x[0], loss_target[0], mem[0]
    t, d = x2d.shape
    mla_h = w_qb.shape[2] * N_CHIPS // (NOPE + ROPE)
    ml_h = b_if.shape[1] // 2
    cr_h = wq_c.shape[2] // CROSS_DH
    f_dim = w_down.shape[1] * N_CHIPS
    q_rank, kv_rank = g_qa.shape[1], g_kva.shape[1]
    qk_w, v_w = ml_h * MLSTM_DK, ml_h * MLSTM_DV
    nc = t // CHUNK

    big_names = ["w_in", "w_qb", "w_kvb", "p_a", "p_b", "w_out", "wq_c", "wk_c", "wv_c", "wo_c", "w_up", "w_down"]
    col_sharded = {"w_in", "w_qb", "w_kvb", "wo_c", "w_up"}
    small_sharded = ["conv_qk", "g_hnorm", "conv_ffn"]
    early_big, early_small = ["w_in", "w_qb", "w_kvb"], ["conv_qk", "g_hnorm"]
    late_groups = [["p_a", "p_b", "w_out", "wq_c", "wk_c", "wv_c", "wo_c"], ["w_up", "w_down", "conv_ffn"]]
    full = {}

    def unshard(n, g):
        if n == "w_up":
            full[n] = g
        else:
            full[n] = _cols_from_shards(g) if (n in col_sharded or n in small_sharded) else g.reshape(-1, g.shape[2])

    gathered = gather_shards([wts[n][0].astype(BF16) for n in early_big], [wts[n][0] for n in early_small])
    for n, g in zip(early_big + early_small, gathered):
        unshard(n, g)
    late, order_after = [], gathered[0]
    for gi, group in enumerate(late_groups):
        src = [wts[n][0].astype(BF16) if n in big_names else wts[n][0] for n in group]
        late.append(split_start("gather_late%d_start" % gi, src, [((N_CHIPS,) + a.shape, a.dtype) for a in src],
                                3 * len(src), _gather_copies, order_after))
        order_after = late[-1][4]
    g_mix_fwd = g_mix + order_after[0:1, 0:1]

    def land_late(gi, after):
        group = late_groups[gi]
        own, landed = split_wait("gather_late%d_wait" % gi, late[gi], 3 * len(group), _gather_copies, after)
        for n, g, o in zip(group, landed, own):
            unshard(n, lax.dynamic_update_slice(g, o[None], (chip, 0, 0)))

    o_qa, o_kv, o_kpe = 0, q_rank, q_rank + kv_rank
    o_q = o_kpe + ROPE
    o_v = o_q + 2 * qk_w
    o_if = o_v + v_w
    o_o = o_if + 2 * ml_h
    o_ga, o_gb = o_o + v_w, o_o + v_w + d
    wi = full["w_in"]
    pad_kpe = jnp.zeros((d, LANE - ROPE), BF16)
    pad_if = jnp.zeros((d, LANE - 2 * ml_h), BF16)
    w_small = jnp.concatenate([wi[:, o_qa:o_q], pad_kpe, wi[:, o_if:o_o], pad_if], axis=1)
    o_kpe_s, o_if_s = o_kpe, o_kpe + LANE
    w_qk, w_v, w_o, w_ga, w_gb = wi[:, o_q:o_v], wi[:, o_v:o_if], wi[:, o_o:o_ga], wi[:, o_ga:o_gb], wi[:, o_gb:]

    wq3 = full["w_qb"].reshape(q_rank, mla_h, NOPE + ROPE)
    wq_nope = wq3[:, :, :NOPE].reshape(q_rank, mla_h * NOPE)
    wq_pe = jnp.pad(wq3[:, :, NOPE:], ((0, 0), (0, 0), (0, LANE - ROPE))).reshape(q_rank, mla_h * LANE)
    wkv3 = full["w_kvb"].reshape(kv_rank, mla_h, NOPE + VHEAD)
    wk_nope = wkv3[:, :, :NOPE].reshape(kv_rank, mla_h * NOPE)
    wv_mla = wkv3[:, :, NOPE:].reshape(kv_rank, mla_h * VHEAD)

    inv_freq = ROPE_BASE ** (-jnp.arange(0, ROPE, 2, dtype=F32) / ROPE)
    ang = positions[0].astype(F32)[:, None] * inv_freq
    cos, sin = jnp.cos(ang), jnp.sin(ang)
    zero_h = jnp.zeros_like(cos)
    tabs = [_pad_lanes(jnp.concatenate([cos, cos], axis=1)), _pad_lanes(-sin), _pad_lanes(jnp.concatenate([zero_h, sin], axis=1))]
    mla_gains = [g_qn_nope, _pad_lanes(g_qn_pe), g_kn_nope, _pad_lanes(g_kn_pe)]

    u1 = rms_fwd(x2d, g_mix_fwd, "rms_mix")
    z_small = mm(u1, w_small, name="in_small")
    z_qa, z_kv = z_small[:, o_qa:o_kv], z_small[:, o_kv:o_kpe]
    z_kpe, z_if = z_small[:, o_kpe_s:o_kpe_s + LANE], z_small[:, o_if_s:o_if_s + 2 * ml_h]
    z_qk = mm(u1, w_qk, name="in_qk")
    z_v = mm(u1, w_v, name="in_v")
    z_o = mm(u1, w_o, name="in_o")
    z_ga = mm(u1, w_ga, name="in_ga")
    z_gb = mm(u1, w_gb, name="in_gb")

    qa_n = rms_fwd(z_qa, g_qa, "rms_qa")
    kv_n = rms_fwd(z_kv, g_kva, "rms_kva")
    qn_raw = mm(qa_n, wq_nope, name="q_nope")
    qp_raw = mm(qa_n, wq_pe, name="q_pe")
    kn_raw = mm(kv_n, wk_nope, name="k_nope")
    v_mla = mm(kv_n, wv_mla, out_dtype=BF16, name="v_mla")
    q_att, k_att = mla_prep_fwd(qn_raw, qp_raw, kn_raw, z_kpe, tabs, mla_gains, mla_h)
    y_a, lse_row = attn_fwd(q_att, k_att, jnp.transpose(v_mla), mla_h)

    colscale = jnp.concatenate([jnp.full((1, qk_w), MLSTM_DK ** -0.5, F32), jnp.ones((1, qk_w), F32)], axis=1)
    qk_c = conv_qk_fwd(z_qk, full["conv_qk"], colscale)
    gates4 = z_if.reshape(nc, CHUNK, 2, ml_h)
    gcol = jnp.transpose(gates4, (3, 0, 1, 2))
    grow = jnp.transpose(gates4, (3, 0, 2, 1))
    bias = jnp.transpose(b_if.reshape(2, ml_h), (1, 0)).reshape(ml_h, 1, 2)
    h_raw, c_all, n_all, m_all = mlstm_fwd(qk_c, z_v, gcol, grow, bias, ml_h)
    g_hn = full["g_hnorm"].reshape(1, v_w)
    hn_gr, hd_gr = t // _pick(t, ROW_TILE), t // _pick(t, HEAD_ROW_TILE)
    y_b = ew(lambda *a: (_hnorm_gate(*a),), [(h_raw, "rc"), (z_o, "rc"), (g_hn, "c")], [((t, v_w), BF16, "rc")], gr=hd_gr, gc=ml_h,
             name="hnorm_gate")[0]

    land_late(0, y_b)

    pa = mm(y_a, full["p_a"], name="proj_a")
    pb = mm(y_b, full["p_b"], name="proj_b")
    merge_fn = lambda ga, gb, a, b: (_sigmoid(ga) * a + _sigmoid(gb) * b,)
    merged = ew(merge_fn, [(z_ga, "r"), (z_gb, "r"), (pa, "r"), (pb, "r")], [((t, d), BF16, "r")], gr=hn_gr, name="merge")[0]
    x1 = mm(merged, full["w_out"], add=x2d, name="out_proj")

    uc = rms_fwd(x1, g_cross, "rms_cross")
    mem_n = rms_fwd(mem2d, g_mem, "rms_mem")
    qc = mm(uc, full["wq_c"], name="cross_q")
    kc = mm(mem_n, full["wk_c"], name="cross_k")
    vc = mm(mem_n, full["wv_c"], name="cross_v")
    oc = cross_fwd(qc, kc, vc, g_cq, g_ck, cr_h)
    x2 = mm(oc, full["wo_c"], add=x1, name="cross_out")

    land_late(1, x2)
    half = N_CHIPS // 2
    u3 = rms_fwd(x2, g_ffn, "rms_ffn")
    hg = mm(u3, full["w_up"], b_shards=(0, half), name="ffn_up_gate")
    hv = mm(u3, full["w_up"], b_shards=(half, half), name="ffn_up_val")
    cw, cb = full["conv_ffn"], b_conv_ffn
    act = ffn_glu_fwd(hg, hv, cw[:, :f_dim], cw[:, f_dim:], cb[:, :f_dim], cb[:, f_dim:])
    y = mm(act, full["w_down"], add=x2, name="ffn_down")

    def loss_fn(y_, t_):
        err = y_ - t_
        part = jnp.sum(jnp.sum(err * err, axis=1, keepdims=True), axis=0, keepdims=True) * (0.5 / d)
        return err * (1.0 / d), err * (1.0 / d), jnp.broadcast_to(part, (1, LANE))

    dy, dy_mx, loss_part = ew(loss_fn, [(y, "r"), (tgt, "r")], [((t, d), F32, "r"), ((t, d), BF16, "r"), ((1, LANE), F32, "f")],
                              gr=hn_gr, name="loss")

    gw = {}
    gw["w_down"] = mm(act, dy_mx, ta=True, out_dtype=BF16, name="dw_down")
    dact = mm(dy_mx, full["w_down"], tb=True, name="d_act")
    dhg, dhv, dcw_g, dcw_v, dcb_g, dcb_v = ffn_glu_bwd(hg, hv, cw[:, :f_dim], cw[:, f_dim:], cb[:, :f_dim], cb[:, f_dim:], dact)
    gw["conv_ffn"] = jnp.concatenate([dcw_g, dcw_v], axis=1)
    gw["b_conv_ffn"] = jnp.concatenate([dcb_g, dcb_v], axis=1)
    dwup_g = mm(u3, dhg, ta=True, out_dtype=BF16, out_shards=half, name="dw_up_gate")
    dwup_v = mm(u3, dhv, ta=True, out_dtype=BF16, out_shards=half, name="dw_up_val")
    gw["w_up"] = jnp.concatenate([dwup_g, dwup_v], axis=0)

    def shard_major(n):
        if n == "w_up":
            return gw[n]
        return _cols_to_shards(gw[n]) if n in col_sharded else gw[n].reshape(N_CHIPS, -1, gw[n].shape[1])

    def chip_partials(group):
        grads_sm = [shard_major(n) for n in group]
        return [add_sibling(g, r, c_arr) for g, r in zip(grads_sm, sibling_halves(grads_sm))]

    def scatter_group(group, tag, after):
        parts_ = chip_partials(group)
        return split_start("scatter_start_" + tag, parts_, [((3,) + p.shape[1:], p.dtype) for p in parts_],
                           3 * len(parts_), _scatter_copies, after)

    group_a = ["w_up", "w_down"]
    started_a = scatter_group(group_a, "a", gw["w_up"])
    g_ffn_bwd = g_ffn + started_a[4][0:1, 0:1]
    du3 = mm(dhg, full["w_up"], tb=True, b_shards=(0, half), name="d_u3_gate")
    du3 = mm(dhv, full["w_up"], tb=True, b_shards=(half, half), add=du3, name="d_u3_val")
    dx2, gw["g_ffn"] = rms_bwd(x2, g_ffn_bwd, du3, dy, "rms_ffn_bwd")

    gw["wo_c"] = mm(oc, dx2, ta=True, out_dtype=BF16, name="dw_cross_out")
    doc = mm(dx2, full["wo_c"], tb=True, name="d_cross_o")
    dqc, dkc, dvc, gw["g_cq"], gw["g_ck"] = cross_bwd(qc, kc, vc, g_cq, g_ck, doc, cr_h)
    gw["wq_c"] = mm(uc, dqc, ta=True, out_dtype=BF16, name="dw_cross_q")
    gw["wk_c"] = mm(mem_n, dkc, ta=True, out_dtype=BF16, name="dw_cross_k")
    gw["wv_c"] = mm(mem_n, dvc, ta=True, out_dtype=BF16, name="dw_cross_v")
    duc = mm(dqc, full["wq_c"], tb=True, name="d_uc")
    dmem_n = mm(dkc, full["wk_c"], tb=True, name="d_mem_k")
    dmem_n = mm(dvc, full["wv_c"], tb=True, add=dmem_n, name="d_mem_v")
    _, gw["g_mem"] = rms_bwd(mem2d, g_mem, dmem_n, None, "rms_mem_bwd")
    dx1, gw["g_cross"] = rms_bwd(x1, g_cross, duc, dx2, "rms_cross_bwd")

    gw["w_out"] = mm(merged, dx1, ta=True, out_dtype=BF16, name="dw_out")
    dmerged = mm(dx1, full["w_out"], tb=True, name="d_merged")

    def merge_bwd(ga, gb, a, b, dm):
        _, pull = jax.vjp(lambda *args: merge_fn(*args)[0], ga, gb, a, b)
        return pull(dm)

    dz_ga, dz_gb, dpa, dpb = ew(merge_bwd, [(z_ga, "r"), (z_gb, "r"), (pa, "r"), (pb, "r"), (dmerged, "r")],
                                [((t, d), BF16, "r")] * 4, gr=hn_gr, name="merge_bwd")
    gw["p_a"] = mm(y_a, dpa, ta=True, out_dtype=BF16, name="dw_proj_a")
    gw["p_b"] = mm(y_b, dpb, ta=True, out_dtype=BF16, name="dw_proj_b")
    group_b = ["wo_c", "wq_c", "wk_c", "wv_c", "w_out", "p_a", "p_b"]
    started_b = scatter_group(group_b, "b", gw["p_b"])
    g_hn_bwd = g_hn + started_b[4][0:1, 0:1]
    dy_a = mm(dpa, full["p_a"], tb=True, name="d_ya")
    dy_b = mm(dpb, full["p_b"], tb=True, name="d_yb")

    def hnorm_bwd(h_, zo_, g_, dyb_):
        _, pull = jax.vjp(_hnorm_gate, h_, zo_, g_)
        return pull(dyb_)

    dh_raw, dz_o, dg_hn = ew(hnorm_bwd, [(h_raw, "rc"), (z_o, "rc"), (g_hn_bwd, "c"), (dy_b, "rc")],
                             [((t, v_w), F32, "rc"), ((t, v_w), BF16, "rc"), ((1, v_w), F32, "c")],
                             gr=hd_gr, gc=ml_h, order="cr", name="hnorm_gate_bwd")
    gw["g_hnorm"] = dg_hn.reshape(ml_h, MLSTM_DV)
    dq_m, dk_m, dz_v, dgcol, dgrow = mlstm_bwd(qk_c, z_v, gcol, grow, bias, c_all, n_all, m_all, dh_raw, ml_h)
    dgates4 = jnp.transpose(dgcol, (1, 2, 3, 0)) + jnp.transpose(dgrow, (1, 3, 2, 0))
    dz_if = dgates4.reshape(t, 2 * ml_h)
    gw["b_if"] = ew(lambda a: (jnp.sum(a, axis=0, keepdims=True),), [(dz_if, "r")], [((1, 2 * ml_h), F32, "f")],
                    gr=hn_gr, name="bias_if_bwd")[0]
    dz_qk, gw["conv_qk"] = conv_qk_bwd(z_qk, full["conv_qk"], colscale, dq_m, dk_m)

    dq_att, dk_att, dv_mla = attn_bwd(q_att, k_att, v_mla, y_a, dy_a, lse_row, mla_h)
    dqn_raw, dqp_raw, dkn_raw, dz_kpe, gw["g_qn_nope"], dg_qp, gw["g_kn_nope"], dg_kp = mla_prep_bwd(
        qn_raw, qp_raw, kn_raw, z_kpe, tabs, mla_gains, dq_att, dk_att, mla_h)
    gw["g_qn_pe"], gw["g_kn_pe"] = dg_qp[:, :ROPE], dg_kp[:, :ROPE]
    dwq_nope = mm(qa_n, dqn_raw, ta=True, out_dtype=BF16, name="dw_q_nope")
    dwq_pe = mm(qa_n, dqp_raw, ta=True, out_dtype=BF16, name="dw_q_pe")
    dwk_nope = mm(kv_n, dkn_raw, ta=True, out_dtype=BF16, name="dw_k_nope")
    dwv_mla = mm(kv_n, dv_mla, ta=True, out_dtype=BF16, name="dw_v_mla")
    dqa_n = mm(dqn_raw, wq_nope, tb=True, name="d_qa_nope")
    dqa_n = mm(dqp_raw, wq_pe, tb=True, add=dqa_n, name="d_qa_pe")
    dkv_n = mm(dkn_raw, wk_nope, tb=True, name="d_kv_nope")
    dkv_n = mm(dv_mla, wv_mla, tb=True, add=dkv_n, name="d_kv_v")
    dz_qa, gw["g_qa"] = rms_bwd(z_qa, g_qa, dqa_n, None, "rms_qa_bwd", BF16)
    dz_kv, gw["g_kva"] = rms_bwd(z_kv, g_kva, dkv_n, None, "rms_kva_bwd", BF16)
    gw["w_qb"] = jnp.concatenate([dwq_nope.reshape(q_rank, mla_h, NOPE), dwq_pe.reshape(q_rank, mla_h, LANE)[:, :, :ROPE]],
                                 axis=2).reshape(q_rank, -1)
    gw["w_kvb"] = jnp.concatenate([dwk_nope.reshape(kv_rank, mla_h, NOPE), dwv_mla.reshape(kv_rank, mla_h, VHEAD)],
                                  axis=2).reshape(kv_rank, -1)

    dz_small = jnp.concatenate([dz_qa, dz_kv, dz_kpe.astype(BF16), _pad_lanes(dz_if).astype(BF16)], axis=1)
    dw_small = mm(u1, dz_small, ta=True, out_dtype=BF16, name="dw_in_small")
    du1 = mm(dz_small, w_small, tb=True, name="d_u1_small")
    dw_segs = []
    for nm, dz, w_seg in (("qk", dz_qk, w_qk), ("v", dz_v, w_v), ("o", dz_o, w_o), ("ga", dz_ga, w_ga), ("gb", dz_gb, w_gb)):
        dw_segs.append(mm(u1, dz, ta=True, out_dtype=BF16, name="dw_in_" + nm))
        du1 = mm(dz, w_seg, tb=True, add=du1, name="d_u1_" + nm)
    gw["w_in"] = jnp.concatenate([dw_small[:, :o_kpe_s + ROPE], dw_segs[0], dw_segs[1],
                                  dw_small[:, o_if_s:o_if_s + 2 * ml_h], dw_segs[2], dw_segs[3], dw_segs[4]], axis=1)
    grad_x, gw["g_mix"] = rms_bwd(x2d, g_mix, du1, dx1, "rms_mix_bwd")

    group_c = ["w_in", "w_qb", "w_kvb"]
    started_c = scatter_group(group_c, "c", grad_x)
    parts_a, others_a = split_wait("scatter_wait_a", started_a, 3 * len(group_a), _scatter_copies, grad_x)
    parts_b, others_b = split_wait("scatter_wait_b", started_b, 3 * len(group_b), _scatter_copies, grad_x)
    place_ab = place_arr + started_c[4][0, 0].astype(jnp.int32)
    grads, deltas, new_m, new_v = {}, {}, {}, {}

    def finish(group, parts_, others_, place):
        joined = join_halves([sum_chips(p, o, place) for p, o in zip(parts_, others_)])
        for n, j in zip(group, joined):
            grads[n] = j.reshape(-1, j.shape[2])
            deltas[n], new_m[n], new_v[n] = adamw(grads[n], wts[n][0], mom[n][0], var[n][0], "adamw_" + n)

    finish(group_a + group_b, parts_a + parts_b, others_a + others_b, place_ab)

    small_names = [n for n in names if n not in big_names]
    pieces = [loss_part]
    for n in small_names:
        flat = gw[n].reshape(1, -1)
        pieces.append(jnp.pad(flat, ((0, 0), (0, (-flat.shape[1]) % LANE))))
    packed = jnp.concatenate(pieces, axis=1)
    packed = jnp.pad(packed, ((0, 0), (0, (-packed.shape[1]) % (8 * LANE)))).reshape(-1, LANE)
    total = allreduce_small(packed).reshape(1, -1)
    loss = total[0, 0]
    small_grads, off = {}, LANE
    for n in small_names:
        size = gw[n].size
        g_full = total[:, off:off + size].reshape(gw[n].shape)
        off += size + (-size) % LANE
        if n in small_sharded:
            width = wts[n].shape[-1]
            g_full = lax.dynamic_slice_in_dim(g_full, chip * width, width, axis=g_full.ndim - 1)
        small_grads[n] = g_full.reshape(wts[n].shape[1:])

    def pack_small(tree):
        flat = jnp.concatenate([tree[n].reshape(1, -1) for n in small_names], axis=1)
        return jnp.pad(flat, ((0, 0), (0, (-flat.shape[1]) % (8 * LANE)))).reshape(8, -1)

    sg = pack_small(small_grads)
    sd, sm, sv = adamw(sg, pack_small({n: wts[n][0] for n in small_names}), pack_small({n: mom[n][0] for n in small_names}),
                       pack_small({n: var[n][0] for n in small_names}), "adamw_small")
    off = 0
    for n in small_names:
        size = small_grads[n].size
        shp = wts[n].shape[1:]
        grads[n] = small_grads[n]
        for dst, src in ((deltas, sd), (new_m, sm), (new_v, sv)):
            dst[n] = src.reshape(1, -1)[:, off:off + size].reshape(shp)
        off += size

    parts_c, others_c = split_wait("scatter_wait_c", started_c, 3 * len(group_c), _scatter_copies,
                                   (sv, new_v[group_b[-1]], new_v[group_a[0]]))
    finish(group_c, parts_c, others_c, place_arr)

    def out(tree):
        return [tree[n].reshape(wts[n].shape) for n in names]

    return (loss, grad_x.reshape(x.shape), *out(grads), *out(deltas), *out(new_m), *out(new_v))
```

```python
import functools
import math

import jax
import jax.numpy as jnp
from jax import lax
from jax.experimental import pallas as pl
from jax.experimental.pallas import tpu as pltpu

F32, BF16 = jnp.float32, jnp.bfloat16
MESH = pl.DeviceIdType.MESH

EPS = 1e-6
CHUNK = 64
LOG2_CHUNK = 6
NOPE, ROPE, VHEAD = 128, 64, 128
MLSTM_DK, MLSTM_DV, MLSTM_CONV = 128, 256, 4
MLSTM_HEADS_PER_STEP = 8
CROSS_DH = 128
FFN_CONV = 3
ROPE_BASE = 10000.0
LOG2_E = math.log2(math.e)
ADAM_LR, ADAM_B1, ADAM_B2, ADAM_EPS, ADAM_WD, ADAM_STEP = 0.001, 0.9, 0.999, 1e-08, 0.01, 10

LANE = 128
ROW_TILE = 256
HEAD_ROW_TILE = 1024
ATT_TILE = 512
ATT_SUB = 512
MM_TILES = (1024, 1024, 2048)
MM_SHARD_TILE = 1536
VMEM_LIMIT = 56 * 1024 * 1024
N_CHIPS = 4

NN = ((1,), (0,))
NT = ((1,), (1,))
TN = ((0,), (0,))


def _pick(dim, pref):
    if dim <= pref:
        return dim
    for t in range(pref, 0, -LANE):
        if dim % t == 0:
            return t
    return dim


def _bdot(a, b, dims):
    return lax.dot_general(a.astype(BF16), b.astype(BF16), (dims, ((), ())), preferred_element_type=F32)


@jax.custom_vjp
def _dnn(a, b):
    return _bdot(a, b, NN)


_dnn.defvjp(lambda a, b: (_bdot(a, b, NN), (a, b)),
            lambda r, g: (_bdot(g, r[1], NT), _bdot(r[0], g, TN)))


@jax.custom_vjp
def _dnt(a, b):
    return _bdot(a, b, NT)


_dnt.defvjp(lambda a, b: (_bdot(a, b, NT), (a, b)),
            lambda r, g: (_bdot(g, r[1], NN), _bdot(g, r[0], TN)))


@jax.custom_vjp
def _dtn(a, b):
    return _bdot(a, b, TN)


_dtn.defvjp(lambda a, b: (_bdot(a, b, TN), (a, b)),
            lambda r, g: (_bdot(r[1], g, NT), _bdot(r[0], g, NN)))


@functools.partial(jax.custom_vjp, nondiff_argnums=(1,))
def _lane_roll(x, shift):
    return pltpu.roll(x, shift, 1)


_lane_roll.defvjp(lambda x, shift: (pltpu.roll(x, shift, 1), None),
                  lambda shift, _, g: (pltpu.roll(g, (LANE - shift) % LANE, 1),))


def _params(*sem):
    return pltpu.CompilerParams(dimension_semantics=sem, vmem_limit_bytes=VMEM_LIMIT)


def mm(a, b, *, ta=False, tb=False, add=None, out_dtype=F32, name, b_shards=None, out_shards=None):
    m_dim, k_dim = (a.shape[1], a.shape[0]) if ta else a.shape
    if b_shards is None:
        n_dim = b.shape[0] if tb else b.shape[1]
        assert k_dim == (b.shape[1] if tb else b.shape[0]), (name, a.shape, b.shape)
    else:
        n_dim = b.shape[1] if tb else b_shards[1] * b.shape[2]
        assert k_dim == (b_shards[1] * b.shape[2] if tb else b.shape[1]), (name, a.shape, b.shape)
    tm, tn, tk = _pick(m_dim, MM_TILES[0]), _pick(n_dim, MM_TILES[1]), _pick(k_dim, MM_TILES[2])
    if b_shards is not None and tb:
        tk = _pick(b.shape[2], MM_SHARD_TILE)
    if (b_shards is not None and not tb) or out_shards is not None:
        tn = _pick(n_dim // (out_shards or b_shards[1]), MM_SHARD_TILE)
    nk = k_dim // tk
    dims = ((0,) if ta else (1,), (1,) if tb else (0,))
    has_add = add is not None

    def body(*refs):
        a_ref, b_ref = refs[0], refs[1]
        c_ref = refs[2] if has_add else None
        o_ref = refs[3] if has_add else refs[2]
        prod = _bdot(a_ref[...], b_ref[...], dims)
        if nk == 1:
            o_ref[...] = (prod + c_ref[...].astype(F32) if has_add else prod).astype(o_ref.dtype)
            return
        acc = refs[-1]
        k = pl.program_id(2)

        @pl.when(k == 0)
        def _():
            acc[...] = prod + c_ref[...].astype(F32) if has_add else prod

        @pl.when(k > 0)
        def _():
            acc[...] += prod

        @pl.when(k == nk - 1)
        def _():
            o_ref[...] = acc[...].astype(o_ref.dtype)

    if b_shards is None:
        b_spec = pl.BlockSpec((tn, tk), lambda i, j, k: (j, k)) if tb else pl.BlockSpec((tk, tn), lambda i, j, k: (k, j))
    elif tb:
        per = b.shape[2] // tk
        b_spec = pl.BlockSpec((None, tn, tk), lambda i, j, k: (b_shards[0] + k // per, j, k % per))
    else:
        per = b.shape[2] // tn
        b_spec = pl.BlockSpec((None, tk, tn), lambda i, j, k: (b_shards[0] + j // per, k, j % per))
    in_specs = [pl.BlockSpec((tk, tm), lambda i, j, k: (k, i)) if ta else pl.BlockSpec((tm, tk), lambda i, j, k: (i, k)), b_spec]
    if out_shards is None:
        out_spec, out_shape = pl.BlockSpec((tm, tn), lambda i, j, k: (i, j)), (m_dim, n_dim)
    else:
        per_o = n_dim // out_shards // tn
        out_spec = pl.BlockSpec((None, tm, tn), lambda i, j, k: (j // per_o, i, j % per_o))
        out_shape = (out_shards, m_dim, n_dim // out_shards)
    args = [a, b]
    if has_add:
        in_specs.append(pl.BlockSpec((tm, tn), lambda i, j, k: (i, j)))
        args.append(add)
    return pl.pallas_call(
        body, grid=(m_dim // tm, n_dim // tn, nk), in_specs=in_specs, out_specs=out_spec,
        out_shape=jax.ShapeDtypeStruct(out_shape, out_dtype),
        scratch_shapes=[pltpu.VMEM((tm, tn), F32)] if nk > 1 else [],
        compiler_params=_params("parallel", "parallel", "arbitrary"), name=name)(*args)


def ew(fn, ins, outs, *, gr, gc=1, order="rc", name):
    n_in = len(ins)

    def block(shape, kind):
        r, c = shape
        return (r // gr if kind in ("rc", "r") else r, c // gc if kind in ("rc", "c") else c)

    def imap(kind):
        def f(p0, p1):
            i, j = (p0, p1) if order == "rc" else (p1, p0)
            return {"rc": (i, j), "r": (i, 0), "c": (0, j), "f": (0, 0)}[kind]
        return f

    def body(*refs):
        p0, p1 = pl.program_id(0), pl.program_id(1)
        i, j = (p0, p1) if order == "rc" else (p1, p0)
        vals = fn(*[r[...] for r in refs[:n_in]])
        for ref, val, (_, dtype, kind) in zip(refs[n_in:], vals, outs):
            first = {"rc": None, "r": (j == 0) if gc > 1 else None, "c": (i == 0) if gr > 1 else None,
                     "f": ((i == 0) & (j == 0)) if gr * gc > 1 else None}[kind]
            _store(ref, val.astype(dtype), first)

    grid = (gr, gc) if order == "rc" else (gc, gr)
    return pl.pallas_call(
        body, grid=grid,
        in_specs=[pl.BlockSpec(block(a.shape, k), imap(k)) for a, k in ins],
        out_specs=[pl.BlockSpec(block(s, k), imap(k)) for s, _, k in outs],
        out_shape=[jax.ShapeDtypeStruct(s, d) for s, d, _ in outs],
        compiler_params=_params("arbitrary", "arbitrary"), name=name)(*[a for a, _ in ins])


def _store(ref, val, first):
    if first is None:
        ref[...] = val
        return

    @pl.when(first)
    def _():
        ref[...] = val

    @pl.when(jnp.logical_not(first))
    def _():
        ref[...] += val


def _f32(*xs):
    return [x.astype(F32) for x in xs]


def _rms(x, g, n):
    ms = jnp.sum(x * x, axis=-1, keepdims=True) * (1.0 / n)
    return x * lax.rsqrt(ms + EPS) * g


def _sigmoid(x):
    return 1.0 / (1.0 + jnp.exp(-x))


def _silu(x):
    return x * _sigmoid(x)


def _log_sigmoid(x):
    return jnp.minimum(x, 0.0) - jnp.log(1.0 + jnp.exp(-jnp.abs(x)))


def rms_fwd(x, g, name, out_dtype=BF16):
    t, w = x.shape
    return ew(lambda x_, g_: (_rms(x_, g_, w),), [(x, "r"), (g, "f")], [((t, w), out_dtype, "r")],
              gr=t // _pick(t, ROW_TILE), name=name)[0]


def rms_bwd(x, g, du, res, name, out_dtype=F32):
    t, w = x.shape

    def fn(x_, g_, du_, *res_):
        _, pull = jax.vjp(lambda a, b: _rms(a, b, w), x_, g_)
        dx, dg = pull(du_.astype(F32))
        return (dx + res_[0] if res_ else dx), dg

    ins = [(x, "r"), (g, "f"), (du, "r")] + ([(res, "r")] if res is not None else [])
    return ew(fn, ins, [((t, w), out_dtype, "r"), ((1, w), F32, "f")], gr=t // _pick(t, ROW_TILE), name=name)


def _rope(x, cos_t, sin_lo, sin_hi):
    return x * cos_t + _lane_roll(x, LANE - ROPE // 2) * sin_lo + _lane_roll(x, ROPE // 2) * sin_hi


def _mla_prep(qn, qp, kn, kp, cos_t, sin_lo, sin_hi, g_qn, g_qp, g_kn, g_kp):
    q = jnp.concatenate([_rms(qn, g_qn, NOPE), _rope(_rms(qp, g_qp, ROPE), cos_t, sin_lo, sin_hi)], axis=1)
    k = jnp.concatenate([_rms(kn, g_kn, NOPE), _rope(_rms(kp, g_kp, ROPE), cos_t, sin_lo, sin_hi)], axis=1)
    return q, k


def mla_prep_fwd(qn, qp, kn, kp, tabs, gains, heads):
    t = qn.shape[0]
    ins = [(qn, "rc"), (qp, "rc"), (kn, "rc"), (kp, "r")] + [(a, "r") for a in tabs] + [(g, "f") for g in gains]
    return ew(lambda *a: _mla_prep(*_f32(*a)), ins,
              [((t, heads * 2 * LANE), BF16, "rc"), ((t, heads * 2 * LANE), BF16, "rc")],
              gr=t // _pick(t, HEAD_ROW_TILE), gc=heads, name="mla_prep_fwd")


def mla_prep_bwd(qn, qp, kn, kp, tabs, gains, dq, dk, heads):
    t = qn.shape[0]

    def fn(qn_, qp_, kn_, kp_, c_, s1_, s2_, g1, g2, g3, g4, dq_, dk_):
        _, pull = jax.vjp(lambda a, b, c, d, e, f, g, h: _mla_prep(a, b, c, d, c_, s1_, s2_, e, f, g, h),
                          qn_, qp_, kn_, kp_, g1, g2, g3, g4)
        return pull((dq_, dk_))

    ins = ([(qn, "rc"), (qp, "rc"), (kn, "rc"), (kp, "r")] + [(a, "r") for a in tabs] + [(g, "f") for g in gains]
           + [(dq, "rc"), (dk, "rc")])
    hw = heads * LANE
    outs = [((t, hw), BF16, "rc"), ((t, hw), BF16, "rc"), ((t, hw), BF16, "rc"), ((t, LANE), F32, "r")] \
        + [((1, LANE), F32, "f")] * 4
    return ew(fn, ins, outs, gr=t // _pick(t, HEAD_ROW_TILE), gc=heads, name="mla_prep_bwd")


def _chunk_mask(row0, col0, shape, rows_are_queries):
    r = jnp.right_shift(row0 + lax.broadcasted_iota(jnp.int32, shape, 0), LOG2_CHUNK)
    c = jnp.right_shift(col0 + lax.broadcasted_iota(jnp.int32, shape, 1), LOG2_CHUNK)
    return (c <= r) if rows_are_queries else (r <= c)


def _block_pairs(nq, queries_outer):
    if queries_outer:
        pairs = [(i, j) for i in range(nq) for j in range(i + 1)]
    else:
        pairs = [(i, j) for j in range(nq) for i in range(j, nq)]
    return jnp.asarray([p[0] for p in pairs], jnp.int32), jnp.asarray([p[1] for p in pairs], jnp.int32)


def attn_fwd(q, k, vt, heads):
    t = q.shape[0]
    tq = _pick(t, ATT_TILE)
    sub = _pick(tq, ATT_SUB)
    qi, kj = _block_pairs(t // tq, True)
    scale = (NOPE + ROPE) ** -0.5
    scale2 = scale * LOG2_E

    def body(qi_ref, kj_ref, q_ref, k_ref, vt_ref, o_ref, lse_ref, m_s, l_s, acc):
        p = pl.program_id(1)
        i, j = qi_ref[p], kj_ref[p]

        @pl.when(j == 0)
        def _():
            m_s[...] = jnp.full_like(m_s, -jnp.inf)
            l_s[...] = jnp.zeros_like(l_s)
            acc[...] = jnp.zeros_like(acc)

        def step(diagonal):
            for b in range(tq // sub):
                cols = pl.ds(b * sub, sub)
                st = _bdot(k_ref[...], q_ref[cols, :], NT)
                if diagonal:
                    st = jnp.where(_chunk_mask(0, b * sub, (tq, sub), False), st, -jnp.inf)
                m_old = m_s[:, cols]
                m_new = jnp.maximum(m_old, jnp.max(st, axis=0, keepdims=True))
                alpha = jnp.exp2((m_old - m_new) * scale2)
                pt = jnp.exp2((st - m_new) * scale2)
                l_s[:, cols] = alpha * l_s[:, cols] + jnp.sum(pt, axis=0, keepdims=True)
                acc[:, cols] = alpha * acc[:, cols] + _bdot(vt_ref[...], pt, NN)
                m_s[:, cols] = m_new

        pl.when(j < i)(functools.partial(step, False))

        @pl.when(j == i)
        def _():
            step(True)
            o_ref[...] = jnp.transpose(acc[...] / l_s[...])
            lse_ref[...] = m_s[...] * scale + jnp.log(l_s[...])

    return pl.pallas_call(
        body,
        grid_spec=pltpu.PrefetchScalarGridSpec(
            num_scalar_prefetch=2, grid=(heads, qi.shape[0]),
            in_specs=[pl.BlockSpec((tq, 2 * LANE), lambda h, p, qi_, kj_: (qi_[p], h)),
                      pl.BlockSpec((tq, 2 * LANE), lambda h, p, qi_, kj_: (kj_[p], h)),
                      pl.BlockSpec((VHEAD, tq), lambda h, p, qi_, kj_: (h, kj_[p]))],
            out_specs=[pl.BlockSpec((tq, VHEAD), lambda h, p, qi_, kj_: (qi_[p], h)),
                       pl.BlockSpec((None, 1, tq), lambda h, p, qi_, kj_: (h, 0, qi_[p]))],
            scratch_shapes=[pltpu.VMEM((1, tq), F32), pltpu.VMEM((1, tq), F32), pltpu.VMEM((VHEAD, tq), F32)]),
        out_shape=[jax.ShapeDtypeStruct((t, heads * VHEAD), F32), jax.ShapeDtypeStruct((heads, 1, t), F32)],
        compiler_params=_params("parallel", "arbitrary"), name="mla_attn_fwd")(qi, kj, q, k, vt)


def attn_bwd(q, k, v, o, do, lse_row, heads):
    t = q.shape[0]
    tq = _pick(t, ATT_TILE)
    sub = _pick(tq, ATT_SUB)
    qi, kj = _block_pairs(t // tq, False)
    scale = (NOPE + ROPE) ** -0.5
    scale2 = scale * LOG2_E

    def body(qi_ref, kj_ref, q_ref, k_ref, v_ref, o_ref, do_ref, lse_ref, dq_ref, dk_ref, dv_ref):
        p = pl.program_id(1)
        i, j = qi_ref[p], kj_ref[p]

        @pl.when(p == 0)
        def _():
            dq_ref[...] = jnp.zeros_like(dq_ref)

        @pl.when(i == j)
        def _():
            dk_ref[...] = jnp.zeros_like(dk_ref)
            dv_ref[...] = jnp.zeros_like(dv_ref)

        def step(diagonal):
            for b in range(tq // sub):
                cols = pl.ds(b * sub, sub)
                do_i = do_ref[cols, :]
                prod = do_i * o_ref[cols, :]
                hi = prod.astype(BF16)
                mid = (prod - hi.astype(F32)).astype(BF16)
                lo = (prod - hi.astype(F32) - mid.astype(F32)).astype(BF16)
                ones = jnp.ones((8, VHEAD), BF16)
                delta = (_bdot(ones, hi, NT) + _bdot(ones, mid, NT) + _bdot(ones, lo, NT))[0:1, :]
                q_b = q_ref[cols, :]
                st = _bdot(k_ref[...], q_b, NT)
                pt = jnp.exp2(st * scale2 - lse_ref[:, cols] * LOG2_E)
                if diagonal:
                    pt = jnp.where(_chunk_mask(0, b * sub, (tq, sub), False), pt, 0.0)
                dv_ref[...] += _bdot(pt, do_i, NN)
                dpt = _bdot(v_ref[...], do_i, NT)
                dst = pt * (dpt - delta) * scale
                dk_ref[...] += _bdot(dst, q_b, NN)
                rows = pl.ds(pl.multiple_of(i * tq + b * sub, sub), sub)
                dq_ref[rows, :] += _bdot(dst, k_ref[...], TN)

        pl.when(i > j)(functools.partial(step, False))
        pl.when(i == j)(functools.partial(step, True))

    qmap = lambda h, p, qi_, kj_: (qi_[p], h)
    kmap = lambda h, p, qi_, kj_: (kj_[p], h)
    return pl.pallas_call(
        body,
        grid_spec=pltpu.PrefetchScalarGridSpec(
            num_scalar_prefetch=2, grid=(heads, qi.shape[0]),
            in_specs=[pl.BlockSpec((tq, 2 * LANE), qmap), pl.BlockSpec((tq, 2 * LANE), kmap),
                      pl.BlockSpec((tq, VHEAD), kmap), pl.BlockSpec((tq, VHEAD), qmap), pl.BlockSpec((tq, VHEAD), qmap),
                      pl.BlockSpec((None, 1, tq), lambda h, p, qi_, kj_: (h, 0, qi_[p]))],
            out_specs=[pl.BlockSpec((t, 2 * LANE), lambda h, p, qi_, kj_: (0, h)),
                       pl.BlockSpec((tq, 2 * LANE), kmap), pl.BlockSpec((tq, VHEAD), kmap)]),
        out_shape=[jax.ShapeDtypeStruct((t, heads * 2 * LANE), F32), jax.ShapeDtypeStruct((t, heads * 2 * LANE), F32),
                   jax.ShapeDtypeStruct((t, heads * VHEAD), F32)],
        compiler_params=_params("parallel", "arbitrary"), name="mla_attn_bwd")(qi, kj, q, k, v, o, do, lse_row)


def _shift_down(x, s):
    if s == 0:
        return x
    rows = lax.broadcasted_iota(jnp.int32, x.shape, 0)
    return jnp.where(rows >= s, pltpu.roll(x, s, 0), 0.0)


def _shift_up(x, s):
    if s == 0:
        return x
    t = x.shape[0]
    rows = lax.broadcasted_iota(jnp.int32, x.shape, 0)
    return jnp.where(rows < t - s, pltpu.roll(x, t - s, 0), 0.0)


def _conv(x, w_ref, width):
    return sum(_shift_down(x, width - 1 - j) * w_ref[j:j + 1, :] for j in range(width))


def _conv_bwd(x, dpre, w_ref, dw_ref, width):
    dx = sum(_shift_up(dpre, width - 1 - j) * w_ref[j:j + 1, :] for j in range(width))
    for j in range(width):
        dw_ref[j:j + 1, :] = jnp.sum(dpre * _shift_down(x, width - 1 - j), axis=0, keepdims=True)
    return dx


def _dsilu(z):
    s = _sigmoid(z)
    return s * (1.0 + z * (1.0 - s))


def conv_qk_fwd(x, w, colscale):
    t, c = x.shape
    tc = _pick(c, 256)

    def body(x_ref, w_ref, s_ref, o_ref):
        o_ref[...] = (_silu(_conv(x_ref[...], w_ref, MLSTM_CONV)) * s_ref[...]).astype(o_ref.dtype)

    return pl.pallas_call(
        body, grid=(c // tc,),
        in_specs=[pl.BlockSpec((t, tc), lambda j: (0, j)), pl.BlockSpec((MLSTM_CONV, tc), lambda j: (0, j)),
                  pl.BlockSpec((1, tc), lambda j: (0, j))],
        out_specs=pl.BlockSpec((t, tc), lambda j: (0, j)), out_shape=jax.ShapeDtypeStruct((t, c), BF16),
        compiler_params=_params("parallel"), name="conv_qk_fwd")(x, w, colscale)


def conv_qk_bwd(x, w, colscale, dq, dk):
    t, c = x.shape
    tc = _pick(c // 2, 256)
    half = (c // 2) // tc

    def body(x_ref, w_ref, s_ref, dq_ref, dk_ref, dx_ref, dw_ref):
        j = pl.program_id(0)
        x_ = x_ref[...]
        dy = jnp.where(j < half, dq_ref[...], dk_ref[...])
        dpre = dy * s_ref[...] * _dsilu(_conv(x_, w_ref, MLSTM_CONV))
        dx_ref[...] = _conv_bwd(x_, dpre, w_ref, dw_ref, MLSTM_CONV).astype(dx_ref.dtype)

    return pl.pallas_call(
        body, grid=(c // tc,),
        in_specs=[pl.BlockSpec((t, tc), lambda j: (0, j)), pl.BlockSpec((MLSTM_CONV, tc), lambda j: (0, j)),
                  pl.BlockSpec((1, tc), lambda j: (0, j)),
                  pl.BlockSpec((t, tc), lambda j: (0, jnp.minimum(j, half - 1))),
                  pl.BlockSpec((t, tc), lambda j: (0, jnp.maximum(j - half, 0)))],
        out_specs=[pl.BlockSpec((t, tc), lambda j: (0, j)), pl.BlockSpec((MLSTM_CONV, tc), lambda j: (0, j))],
        out_shape=[jax.ShapeDtypeStruct((t, c), BF16), jax.ShapeDtypeStruct((MLSTM_CONV, c), F32)],
        compiler_params=_params("parallel"), name="conv_qk_bwd")(x, w, colscale, dq, dk)


def _mlstm_chunk(q, k, v, i_col, i_row, f_col, f_row, c_mat, n_vec, m):
    shape = (CHUNK, CHUNK)
    r = lax.broadcasted_iota(jnp.int32, shape, 0)
    c = lax.broadcasted_iota(jnp.int32, shape, 1)
    tril = c <= r
    lf_col, lf_row = _log_sigmoid(f_col), _log_sigmoid(f_row)
    bc_col = jnp.sum(jnp.where(tril, lf_row, 0.0), axis=1, keepdims=True)
    bc_row = jnp.sum(jnp.where(r <= c, lf_col, 0.0), axis=0, keepdims=True)
    logw = jnp.where(tril, bc_col - bc_row + i_row, -jnp.inf)
    inter = bc_col + m
    m_t = lax.stop_gradient(jnp.maximum(inter, jnp.max(logw, axis=1, keepdims=True)))
    w_intra = jnp.exp(logw - m_t)
    w_inter = jnp.exp(inter - m_t)
    sc = _dnt(q, k) * w_intra
    num = w_inter * _dnn(q, c_mat) + _dnn(sc, v)
    den = w_inter * jnp.sum(q * n_vec, axis=1, keepdims=True) + jnp.sum(sc, axis=1, keepdims=True)
    h = num / jnp.maximum(jnp.abs(den), jnp.exp(-m_t))
    b_last = jnp.sum(lf_row, axis=1, keepdims=True)
    m_new = lax.stop_gradient(jnp.maximum(b_last + m, jnp.max(b_last - bc_row + i_row, axis=1, keepdims=True)))
    decay = jnp.exp(b_last + m - m_new)
    uk = jnp.exp(b_last - bc_col + i_col - m_new) * k
    return h, decay * c_mat + _dtn(uk, v), decay * n_vec + jnp.sum(uk, axis=0, keepdims=True), m_new


def _mlstm_group(heads):
    return MLSTM_HEADS_PER_STEP if heads % MLSTM_HEADS_PER_STEP == 0 else 1


def _mlstm_specs(heads, grp, rev, nc):
    ci = (lambda c: nc - 1 - c) if rev else (lambda c: c)
    return dict(
        q=pl.BlockSpec((CHUNK, grp * MLSTM_DK), lambda g, c: (ci(c), g)),
        k=pl.BlockSpec((CHUNK, grp * MLSTM_DK), lambda g, c: (ci(c), heads // grp + g)),
        v=pl.BlockSpec((CHUNK, grp * MLSTM_DV), lambda g, c: (ci(c), g)),
        gc=pl.BlockSpec((grp, None, CHUNK, 2), lambda g, c: (g, ci(c), 0, 0)),
        gr=pl.BlockSpec((grp, None, 2, CHUNK), lambda g, c: (g, ci(c), 0, 0)),
        b=pl.BlockSpec((grp, 1, 2), lambda g, c: (g, 0, 0)),
        cm=pl.BlockSpec((grp, None, MLSTM_DK, MLSTM_DV), lambda g, c: (g, ci(c), 0, 0)),
        vec=pl.BlockSpec((grp, None, 1, LANE), lambda g, c: (g, ci(c), 0, 0)),
    )


def _gates(gc_ref, gr_ref, b_ref, s):
    bi, bf = b_ref[s, :, 0:1], b_ref[s, :, 1:2]
    return gc_ref[s, :, 0:1] + bi, gr_ref[s, 0:1, :] + bi, gc_ref[s, :, 1:2] + bf, gr_ref[s, 1:2, :] + bf


def mlstm_fwd(qk, v, gcol, grow, bias, heads):
    t = qk.shape[0]
    nc = t // CHUNK
    grp = _mlstm_group(heads)
    sp = _mlstm_specs(heads, grp, False, nc)

    def body(q_ref, k_ref, v_ref, gc_ref, gr_ref, b_ref, h_ref, c_ref, n_ref, m_ref, c_s, n_s, m_s):
        @pl.when(pl.program_id(1) == 0)
        def _():
            c_s[...] = jnp.zeros_like(c_s)
            n_s[...] = jnp.zeros_like(n_s)
            m_s[...] = jnp.zeros_like(m_s)

        c_ref[...] = c_s[...]
        n_ref[...] = n_s[...]
        m_ref[...] = m_s[...]
        for s in range(grp):
            qs, vs = slice(s * MLSTM_DK, (s + 1) * MLSTM_DK), slice(s * MLSTM_DV, (s + 1) * MLSTM_DV)
            q, k, v_ = _f32(q_ref[:, qs], k_ref[:, qs], v_ref[:, vs])
            h, c_new, n_new, m_new = _mlstm_chunk(q, k, v_, *_gates(gc_ref, gr_ref, b_ref, s), c_s[s], n_s[s], m_s[s, :, 0:1])
            h_ref[:, vs] = h
            c_s[s] = c_new
            n_s[s] = n_new
            m_s[s] = jnp.broadcast_to(m_new, (1, LANE))

    return pl.pallas_call(
        body, grid=(heads // grp, nc),
        in_specs=[sp["q"], sp["k"], sp["v"], sp["gc"], sp["gr"], sp["b"]],
        out_specs=[sp["v"], sp["cm"], sp["vec"], sp["vec"]],
        out_shape=[jax.ShapeDtypeStruct((t, heads * MLSTM_DV), F32),
                   jax.ShapeDtypeStruct((heads, nc, MLSTM_DK, MLSTM_DV), F32),
                   jax.ShapeDtypeStruct((heads, nc, 1, LANE), F32), jax.ShapeDtypeStruct((heads, nc, 1, LANE), F32)],
        scratch_shapes=[pltpu.VMEM((grp, MLSTM_DK, MLSTM_DV), F32), pltpu.VMEM((grp, 1, LANE), F32),
                        pltpu.VMEM((grp, 1, LANE), F32)],
        compiler_params=_params("parallel", "arbitrary"), name="mlstm_fwd")(qk, qk, v, gcol, grow, bias)


def mlstm_bwd(qk, v, gcol, grow, bias, c_all, n_all, m_all, dh, heads):
    t = qk.shape[0]
    nc = t // CHUNK
    grp = _mlstm_group(heads)
    sp = _mlstm_specs(heads, grp, True, nc)

    def body(q_ref, k_ref, v_ref, gc_ref, gr_ref, b_ref, c_ref, n_ref, m_ref, dh_ref,
             dq_ref, dk_ref, dv_ref, dgc_ref, dgr_ref, dc_s, dn_s):
        @pl.when(pl.program_id(1) == 0)
        def _():
            dc_s[...] = jnp.zeros_like(dc_s)
            dn_s[...] = jnp.zeros_like(dn_s)

        for s in range(grp):
            qs, vs = slice(s * MLSTM_DK, (s + 1) * MLSTM_DK), slice(s * MLSTM_DV, (s + 1) * MLSTM_DV)
            q, k, v_ = _f32(q_ref[:, qs], k_ref[:, qs], v_ref[:, vs])
            m = m_ref[s, :, 0:1]
            _, pull = jax.vjp(lambda *a: _mlstm_chunk(*a, m)[:3], q, k, v_, *_gates(gc_ref, gr_ref, b_ref, s),
                              c_ref[s], n_ref[s])
            dq, dk, dv, di_col, di_row, df_col, df_row, dc, dn = pull((dh_ref[:, vs], dc_s[s], dn_s[s]))
            dq_ref[:, qs] = dq
            dk_ref[:, qs] = dk
            dv_ref[:, vs] = dv.astype(dv_ref.dtype)
            dgc_ref[s, :, 0:1] = di_col
            dgc_ref[s, :, 1:2] = df_col
            dgr_ref[s, 0:1, :] = di_row
            dgr_ref[s, 1:2, :] = df_row
            dc_s[s] = dc
            dn_s[s] = dn

    qspec = pl.BlockSpec((CHUNK, grp * MLSTM_DK), lambda g, c: (nc - 1 - c, g))
    return pl.pallas_call(
        body, grid=(heads // grp, nc),
        in_specs=[sp["q"], sp["k"], sp["v"], sp["gc"], sp["gr"], sp["b"], sp["cm"], sp["vec"], sp["vec"], sp["v"]],
        out_specs=[qspec, qspec, sp["v"], sp["gc"], sp["gr"]],
        out_shape=[jax.ShapeDtypeStruct((t, heads * MLSTM_DK), F32), jax.ShapeDtypeStruct((t, heads * MLSTM_DK), F32),
                   jax.ShapeDtypeStruct((t, heads * MLSTM_DV), BF16),
                   jax.ShapeDtypeStruct(gcol.shape, F32), jax.ShapeDtypeStruct(grow.shape, F32)],
        scratch_shapes=[pltpu.VMEM((grp, MLSTM_DK, MLSTM_DV), F32), pltpu.VMEM((grp, 1, LANE), F32)],
        compiler_params=_params("parallel", "arbitrary"),
        name="mlstm_bwd")(qk, qk, v, gcol, grow, bias, c_all, n_all, m_all, dh)


def _hnorm_gate(h, zo, g):
    return _rms(h, g, MLSTM_DV) * _sigmoid(zo)


def _cross_core(q, k, v, g_q, g_k, heads):
    scale = CROSS_DH ** -0.5
    outs = []
    for h in range(heads):
        s = _dnt(_rms(q[h], g_q, CROSS_DH), _rms(k[h], g_k, CROSS_DH)) * scale
        p = jnp.exp(s - lax.stop_gradient(jnp.max(s, axis=1, keepdims=True)))
        p = p / jnp.sum(p, axis=1, keepdims=True)
        outs.append(_dnn(p, v[h]))
    return jnp.concatenate(outs, axis=1)


def _split_heads(ref, heads):
    return [ref[:, h * CROSS_DH:(h + 1) * CROSS_DH].astype(F32) for h in range(heads)]


def cross_fwd(q, k, v, g_q, g_k, heads):
    t = q.shape[0]
    tm = _pick(t, ATT_TILE)
    full = lambda a: pl.BlockSpec(a.shape, lambda i: (0, 0))

    def body(q_ref, k_ref, v_ref, gq_ref, gk_ref, o_ref):
        o_ref[...] = _cross_core(_split_heads(q_ref, heads), _split_heads(k_ref, heads), _split_heads(v_ref, heads),
                                 gq_ref[...], gk_ref[...], heads).astype(o_ref.dtype)

    return pl.pallas_call(
        body, grid=(t // tm,), in_specs=[pl.BlockSpec((tm, q.shape[1]), lambda i: (i, 0)), full(k), full(v), full(g_q), full(g_k)],
        out_specs=pl.BlockSpec((tm, q.shape[1]), lambda i: (i, 0)), out_shape=jax.ShapeDtypeStruct(q.shape, BF16),
        compiler_params=_params("parallel"), name="cross_fwd")(q, k, v, g_q, g_k)


def cross_bwd(q, k, v, g_q, g_k, do, heads):
    t, w = q.shape
    tm = _pick(t, ATT_TILE)
    full = lambda a: pl.BlockSpec(a.shape, lambda i: (0, 0))

    def body(q_ref, k_ref, v_ref, gq_ref, gk_ref, do_ref, dq_ref, dk_ref, dv_ref, dgq_ref, dgk_ref):
        qs, ks, vs = _split_heads(q_ref, heads), _split_heads(k_ref, heads), _split_heads(v_ref, heads)
        _, pull = jax.vjp(lambda a, b, c, d, e: _cross_core(a, b, c, d, e, heads), qs, ks, vs, gq_ref[...], gk_ref[...])
        dqs, dks, dvs, dgq, dgk = pull(do_ref[...])
        first = pl.program_id(0) == 0
        for h in range(heads):
            cols = slice(h * CROSS_DH, (h + 1) * CROSS_DH)
            dq_ref[:, cols] = dqs[h].astype(dq_ref.dtype)
            _store(dk_ref.at[:, cols], dks[h], first)
            _store(dv_ref.at[:, cols], dvs[h], first)
        _store(dgq_ref, dgq, first)
        _store(dgk_ref, dgk, first)

    row = pl.BlockSpec((tm, w), lambda i: (i, 0))
    return pl.pallas_call(
        body, grid=(t // tm,), in_specs=[row, full(k), full(v), full(g_q), full(g_k), row],
        out_specs=[row, full(k), full(v), full(g_q), full(g_k)],
        out_shape=[jax.ShapeDtypeStruct(q.shape, BF16), jax.ShapeDtypeStruct(k.shape, F32), jax.ShapeDtypeStruct(v.shape, F32),
                   jax.ShapeDtypeStruct(g_q.shape, F32), jax.ShapeDtypeStruct(g_k.shape, F32)],
        compiler_params=_params("arbitrary"), name="cross_bwd")(q, k, v, g_q, g_k, do)


def ffn_glu_fwd(hg, hv, wg, wv, bg, bv):
    t, f = hg.shape
    tc = _pick(f, LANE)
    col = pl.BlockSpec((t, tc), lambda j: (0, j))
    tap = pl.BlockSpec((FFN_CONV, tc), lambda j: (0, j))
    one = pl.BlockSpec((1, tc), lambda j: (0, j))

    def body(hg_ref, hv_ref, wg_ref, wv_ref, bg_ref, bv_ref, o_ref):
        gate = _conv(hg_ref[...], wg_ref, FFN_CONV) + bg_ref[...]
        val = _conv(hv_ref[...], wv_ref, FFN_CONV) + bv_ref[...]
        o_ref[...] = (_silu(gate) * val).astype(o_ref.dtype)

    return pl.pallas_call(body, grid=(f // tc,), in_specs=[col, col, tap, tap, one, one], out_specs=col,
                          out_shape=jax.ShapeDtypeStruct((t, f), BF16), compiler_params=_params("parallel"),
                          name="ffn_glu_fwd")(hg, hv, wg, wv, bg, bv)


def ffn_glu_bwd(hg, hv, wg, wv, bg, bv, dact):
    t, f = hg.shape
    tc = _pick(f, LANE)
    col = pl.BlockSpec((t, tc), lambda j: (0, j))
    tap = pl.BlockSpec((FFN_CONV, tc), lambda j: (0, j))
    one = pl.BlockSpec((1, tc), lambda j: (0, j))

    def body(hg_ref, hv_ref, wg_ref, wv_ref, bg_ref, bv_ref, da_ref, dhg_ref, dhv_ref, dwg_ref, dwv_ref, dbg_ref, dbv_ref):
        xg, xv, da = hg_ref[...], hv_ref[...], da_ref[...]
        gate = _conv(xg, wg_ref, FFN_CONV) + bg_ref[...]
        val = _conv(xv, wv_ref, FFN_CONV) + bv_ref[...]
        dgate = da * val * _dsilu(gate)
        dval = da * _silu(gate)
        dbg_ref[...] = jnp.sum(dgate, axis=0, keepdims=True)
        dbv_ref[...] = jnp.sum(dval, axis=0, keepdims=True)
        dhg_ref[...] = _conv_bwd(xg, dgate, wg_ref, dwg_ref, FFN_CONV).astype(dhg_ref.dtype)
        dhv_ref[...] = _conv_bwd(xv, dval, wv_ref, dwv_ref, FFN_CONV).astype(dhv_ref.dtype)

    return pl.pallas_call(
        body, grid=(f // tc,), in_specs=[col, col, tap, tap, one, one, col], out_specs=[col, col, tap, tap, one, one],
        out_shape=[jax.ShapeDtypeStruct((t, f), BF16), jax.ShapeDtypeStruct((t, f), BF16),
                   jax.ShapeDtypeStruct((FFN_CONV, f), F32), jax.ShapeDtypeStruct((FFN_CONV, f), F32),
                   jax.ShapeDtypeStruct((1, f), F32), jax.ShapeDtypeStruct((1, f), F32)],
        compiler_params=_params("parallel"), name="ffn_glu_bwd")(hg, hv, wg, wv, bg, bv, dact)


def _adamw(g, w, m, v):
    m = ADAM_B1 * m + (1.0 - ADAM_B1) * g
    v = ADAM_B2 * v + (1.0 - ADAM_B2) * (g * g)
    m_hat = m / (1.0 - ADAM_B1 ** ADAM_STEP)
    v_hat = v / (1.0 - ADAM_B2 ** ADAM_STEP)
    return -ADAM_LR * (m_hat / (jnp.sqrt(v_hat) + ADAM_EPS) + ADAM_WD * w), m, v


def adamw(g, w, m, v, name):
    r, c = g.shape
    tr = r
    for cand in (256, 128, 64, 32, 16, 8):
        if r % cand == 0 and cand * c * 4 <= (1 << 21):
            tr = cand
            break
    return ew(_adamw, [(g, "r"), (w, "r"), (m, "r"), (v, "r")], [((r, c), F32, "r")] * 3, gr=r // tr, name=name)


ANY = pl.BlockSpec(memory_space=pl.ANY)


def _place():
    x, y, c = lax.axis_index("x"), lax.axis_index("y"), lax.axis_index("c")
    return x, y, c, [(1 - x, y), (x, 1 - y), (1 - x, 1 - y)]


def _rcopy(src, dst, send, recv, k, to):
    return pltpu.make_async_remote_copy(src_ref=src, dst_ref=dst, send_sem=send.at[k], recv_sem=recv.at[k],
                                        device_id=to, device_id_type=MESH)


def gather_shards(bigs, smalls):
    nb, na = len(bigs), len(bigs) + len(smalls)
    arrays = list(bigs) + list(smalls)

    def body(*refs):
        ins, outs = refs[:na], refs[na:2 * na]
        send, recv = refs[2 * na:]
        x, y, c, chips = _place()
        me, sib = 2 * x + y, (x, y, 1 - c)

        def half(ref, a, which):
            rows = arrays[a].shape[0] // 2
            return ref.at[pl.ds(which * rows, rows)]

        started = []
        for a in range(na):
            for j, (cx, cy) in enumerate(chips):
                if a < nb:
                    cp = _rcopy(half(ins[a], a, c), half(outs[a].at[me], a, c), send, recv, 6 * a + j, (cx, cy, c))
                else:
                    cp = _rcopy(ins[a], outs[a].at[me], send, recv, 6 * nb + 3 * (a - nb) + j, (cx, cy, c))
                cp.start()
                started.append(cp)
        for a in range(nb):
            for j, (cx, cy) in enumerate(chips):
                landed = half(outs[a].at[2 * cx + cy], a, c)
                _rcopy(landed, landed, send, recv, 6 * a + j, (cx, cy, c)).wait_recv()
                cp = _rcopy(landed, landed, send, recv, 6 * a + 3 + j, sib)
                cp.start()
                started.append(cp)
        for a in range(na):
            for j, (cx, cy) in enumerate(chips):
                if a < nb:
                    dst = half(outs[a].at[2 * cx + cy], a, 1 - c)
                    _rcopy(dst, dst, send, recv, 6 * a + 3 + j, sib).wait_recv()
                else:
                    dst = outs[a].at[2 * cx + cy]
                    _rcopy(dst, dst, send, recv, 6 * nb + 3 * (a - nb) + j, (cx, cy, c)).wait_recv()
        for cp in started:
            cp.wait_send()

    n_sem = 6 * nb + 3 * (na - nb)
    gathered = pl.pallas_call(
        body, in_specs=[ANY] * na, out_specs=[ANY] * na,
        out_shape=[jax.ShapeDtypeStruct((N_CHIPS,) + a.shape, a.dtype) for a in arrays],
        scratch_shapes=[pltpu.SemaphoreType.DMA((n_sem,)), pltpu.SemaphoreType.DMA((n_sem,))],
        name="gather_shards")(*arrays)
    chip = 2 * lax.axis_index("x") + lax.axis_index("y")
    return [lax.dynamic_update_slice(g, a[None], (chip, 0, 0)) for g, a in zip(gathered, arrays)]


def sibling_halves(grads):
    na = len(grads)

    def body(*refs):
        ins, outs = refs[:na], refs[na:2 * na]
        send, recv = refs[2 * na:]
        x, y, c, _ = _place()
        cps = []
        for a in range(na):
            rows = grads[a].shape[1] // 2
            cp = _rcopy(ins[a].at[:, pl.ds((1 - c) * rows, rows)], outs[a], send, recv, a, (x, y, 1 - c))
            cp.start()
            cps.append(cp)
        for cp in cps:
            cp.wait()

    return pl.pallas_call(
        body, in_specs=[ANY] * na, out_specs=[ANY] * na,
        out_shape=[jax.ShapeDtypeStruct((g.shape[0], g.shape[1] // 2, g.shape[2]), g.dtype) for g in grads],
        scratch_shapes=[pltpu.SemaphoreType.DMA((na,)), pltpu.SemaphoreType.DMA((na,))], name="sibling_halves")(*grads)


def join_halves(halves):
    na = len(halves)

    def body(*refs):
        ins, outs = refs[:na], refs[na:2 * na]
        send, recv = refs[2 * na:]
        x, y, c, _ = _place()
        cps = []
        for a in range(na):
            cp = _rcopy(ins[a].at[c], outs[a].at[c], send, recv, a, (x, y, 1 - c))
            cp.start()
            cps.append(cp)
        for a in range(na):
            dst = outs[a].at[1 - c]
            _rcopy(dst, dst, send, recv, a, (x, y, 1 - c)).wait_recv()
        for cp in cps:
            cp.wait_send()

    return pl.pallas_call(
        body, in_specs=[ANY] * na, out_specs=[ANY] * na,
        out_shape=[jax.ShapeDtypeStruct(h.shape, h.dtype) for h in halves],
        input_output_aliases={a: a for a in range(na)},
        scratch_shapes=[pltpu.SemaphoreType.DMA((na,)), pltpu.SemaphoreType.DMA((na,))],
        name="join_halves")(*halves)


HBM = pl.BlockSpec(memory_space=pltpu.HBM)
SEM = pl.BlockSpec(memory_space=pltpu.SEMAPHORE)
SIDE_EFFECT = pltpu.SideEffectType.DATAFLOW_SIDE_EFFECTING


def split_start(name, srcs, land_shapes, n_copies, copies_fn, after):
    ns, nl = len(srcs), len(land_shapes)
    afters = tuple(after) if isinstance(after, (tuple, list)) else (after,)

    def body(*refs):
        ins, lands = refs[:ns], refs[ns:ns + nl]
        send, recv, token = refs[ns + nl + len(afters)], refs[ns + nl + len(afters) + 1], refs[-1]
        for k, (src, dst, dev) in enumerate(copies_fn(ins, lands, False)):
            pltpu.make_async_remote_copy(src_ref=src, dst_ref=dst, send_sem=send.at[k], recv_sem=recv.at[k],
                                         device_id=dev, device_id_type=MESH).start()
        token[...] = jnp.zeros_like(token)

    outs = pl.pallas_call(
        body, name=name,
        out_shape=(pltpu.SemaphoreType.DMA((n_copies,)), pltpu.SemaphoreType.DMA((n_copies,)),
                   *[pltpu.HBM(a.shape, a.dtype) for a in srcs], *[pltpu.HBM(s, dt) for s, dt in land_shapes],
                   jax.ShapeDtypeStruct((8, LANE), F32)),
        in_specs=[HBM] * (ns + nl) + [ANY] * len(afters),
        out_specs=(SEM, SEM, *[HBM] * (ns + nl), pl.BlockSpec(memory_space=pltpu.VMEM)),
        input_output_aliases={i: 2 + i for i in range(ns + nl)},
        compiler_params=pltpu.CompilerParams(has_side_effects=SIDE_EFFECT),
    )(*[pltpu.with_memory_space_constraint(a, pltpu.HBM) for a in srcs],
      *[pltpu.with_memory_space_constraint(lax.empty(s, dt), pltpu.HBM) for s, dt in land_shapes], *afters)
    return outs[0], outs[1], list(outs[2:2 + ns]), list(outs[2 + ns:2 + ns + nl]), outs[-1]


def split_wait(name, started, n_copies, copies_fn, after):
    send, recv, srcs, lands, _ = started
    ns, nl = len(srcs), len(lands)
    afters = tuple(after) if isinstance(after, (tuple, list)) else (after,)

    def body(*refs):
        ins, lnd = refs[:ns], refs[ns:ns + nl]
        send_ref, recv_ref = refs[ns + nl], refs[ns + nl + 1]
        for k, (src, dst, dev) in enumerate(copies_fn(ins, lnd, True)):
            cp = pltpu.make_async_remote_copy(src_ref=src, dst_ref=dst, send_sem=send_ref.at[k], recv_sem=recv_ref.at[k],
                                              device_id=dev, device_id_type=MESH)
            cp.wait_send()
            cp.wait_recv()

    outs = pl.pallas_call(
        body, name=name,
        out_shape=tuple(pltpu.HBM(a.shape, a.dtype) for a in srcs + lands),
        in_specs=[HBM] * (ns + nl) + [SEM, SEM] + [ANY] * len(afters), out_specs=tuple([HBM] * (ns + nl)),
        input_output_aliases={i: i for i in range(ns + nl)},
        compiler_params=pltpu.CompilerParams(has_side_effects=SIDE_EFFECT),
    )(*srcs, *lands, send, recv, *afters)
    return list(outs[:ns]), list(outs[ns:])


def _gather_copies(ins, lands, waiting):
    x, y, c, chips = _place()
    return [(ins[a], lands[a].at[2 * cx + cy] if waiting else lands[a].at[2 * x + y], (cx, cy, c))
            for a in range(len(ins)) for cx, cy in chips]


def _scatter_copies(ins, lands, waiting):
    del waiting
    _, _, c, chips = _place()
    return [(ins[a].at[2 * cx + cy], lands[a].at[j], (cx, cy, c)) for a in range(len(ins)) for j, (cx, cy) in enumerate(chips)]


def allreduce_small(vec):
    r = vec.shape[0]

    def body(x_ref, sum_ref, all_ref, send, recv):
        x, y, c, _ = _place()
        me = 4 * x + 2 * y + c
        all_ref[me] = x_ref[...]
        cps, peers = [], []
        for mask in range(1, 8):
            px = 1 - x if mask & 4 else x
            py = 1 - y if mask & 2 else y
            pc = 1 - c if mask & 1 else c
            peers.append(4 * px + 2 * py + pc)
            cp = _rcopy(x_ref, all_ref.at[me], send, recv, mask - 1, (px, py, pc))
            cp.start()
            cps.append(cp)
        for k, cp in enumerate(cps):
            _rcopy(x_ref, all_ref.at[peers[k]], send, recv, k, (x, y, c)).wait_recv()
        for cp in cps:
            cp.wait_send()
        total = all_ref[0]
        for d in range(1, 8):
            total = total + all_ref[d]
        sum_ref[...] = total

    vm = pl.BlockSpec(memory_space=pltpu.VMEM)
    return pl.pallas_call(
        body, in_specs=[vm], out_specs=vm, out_shape=jax.ShapeDtypeStruct((r, LANE), F32),
        scratch_shapes=[pltpu.VMEM((8, r, LANE), F32), pltpu.SemaphoreType.DMA((7,)), pltpu.SemaphoreType.DMA((7,))],
        compiler_params=pltpu.CompilerParams(vmem_limit_bytes=VMEM_LIMIT), name="allreduce_small")(vec)


def _row_tile(rows):
    for cand in (256, 128, 64, 32, 16):
        if rows % cand == 0:
            return cand
    return rows


def add_sibling(grad, recv, c_idx):
    _, rows, cols = grad.shape
    hr = rows // 2
    tr = _row_tile(hr)
    nb = hr // tr

    def body(c_ref, g_ref, r_ref, o_ref):
        o_ref[...] = (g_ref[...].astype(F32) + r_ref[...].astype(F32)).astype(o_ref.dtype)

    return pl.pallas_call(
        body,
        grid_spec=pltpu.PrefetchScalarGridSpec(
            num_scalar_prefetch=1, grid=(N_CHIPS, nb),
            in_specs=[pl.BlockSpec((None, tr, cols), lambda k, r, c_ref: (k, c_ref[0] * nb + r, 0)),
                      pl.BlockSpec((None, tr, cols), lambda k, r, c_ref: (k, r, 0))],
            out_specs=pl.BlockSpec((None, tr, cols), lambda k, r, c_ref: (k, r, 0))),
        out_shape=jax.ShapeDtypeStruct((N_CHIPS, hr, cols), BF16),
        compiler_params=_params("parallel", "parallel"), name="add_sibling")(c_idx, grad, recv)


def sum_chips(part, others, place_idx):
    _, hr, cols = part.shape
    tr = _row_tile(hr)

    def body(k_ref, p_ref, o0_ref, o1_ref, o2_ref, out_ref):
        out_ref[...] = ((p_ref[...].astype(F32) + o0_ref[...].astype(F32)) + o1_ref[...].astype(F32)) + o2_ref[...].astype(F32)

    other = lambda j: pl.BlockSpec((None, tr, cols), lambda r, k_ref: (j, r, 0))
    return pl.pallas_call(
        body,
        grid_spec=pltpu.PrefetchScalarGridSpec(
            num_scalar_prefetch=1, grid=(hr // tr,),
            in_specs=[pl.BlockSpec((None, tr, cols), lambda r, k_ref: (k_ref[0], r, 0)), other(0), other(1), other(2)],
            out_specs=pl.BlockSpec((None, tr, cols), lambda r, k_ref: (k_ref[1], r, 0))),
        out_shape=jax.ShapeDtypeStruct((2, hr, cols), F32),
        compiler_params=_params("parallel"), name="sum_chips")(place_idx, part, others, others, others)


def _pad_lanes(a, width=LANE):
    return jnp.pad(a, ((0, 0), (0, width - a.shape[1])))


def _cols_from_shards(g):
    return jnp.transpose(g, (1, 0, 2)).reshape(g.shape[1], -1)


def _cols_to_shards(w):
    k, n4 = w.shape
    return jnp.transpose(w.reshape(k, N_CHIPS, n4 // N_CHIPS), (1, 0, 2))


def kernel(x, mem, positions, g_mix, w_in, g_qa, w_qb, g_kva, w_kvb, g_qn_nope, g_qn_pe, g_kn_nope, g_kn_pe, conv_qk, b_if, g_hnorm, p_a, p_b, w_out, g_cross, g_mem, wq_c, wk_c, wv_c, g_cq, g_ck, wo_c, g_ffn, w_up, conv_ffn, b_conv_ffn, w_down, loss_target, m_g_mix, m_w_in, m_g_qa, m_w_qb, m_g_kva, m_w_kvb, m_g_qn_nope, m_g_qn_pe, m_g_kn_nope, m_g_kn_pe, m_conv_qk, m_b_if, m_g_hnorm, m_p_a, m_p_b, m_w_out, m_g_cross, m_g_mem, m_wq_c, m_wk_c, m_wv_c, m_g_cq, m_g_ck, m_wo_c, m_g_ffn, m_w_up, m_conv_ffn, m_b_conv_ffn, m_w_down, v_g_mix, v_w_in, v_g_qa, v_w_qb, v_g_kva, v_w_kvb, v_g_qn_nope, v_g_qn_pe, v_g_kn_nope, v_g_kn_pe, v_conv_qk, v_b_if, v_g_hnorm, v_p_a, v_p_b, v_w_out, v_g_cross, v_g_mem, v_wq_c, v_wk_c, v_wv_c, v_g_cq, v_g_ck, v_wo_c, v_g_ffn, v_w_up, v_conv_ffn, v_b_conv_ffn, v_w_down):
    names = ["g_mix", "w_in", "g_qa", "w_qb", "g_kva", "w_kvb", "g_qn_nope", "g_qn_pe", "g_kn_nope", "g_kn_pe", "conv_qk",
             "b_if", "g_hnorm", "p_a", "p_b", "w_out", "g_cross", "g_mem", "wq_c", "wk_c", "wv_c", "g_cq", "g_ck", "wo_c",
             "g_ffn", "w_up", "conv_ffn", "b_conv_ffn", "w_down"]
    env = locals()
    wts = {n: env[n] for n in names}
    mom = {n: env["m_" + n] for n in names}
    var = {n: env["v_" + n] for n in names}

    xi, yi, ci = lax.axis_index("x"), lax.axis_index("y"), lax.axis_index("c")
    chip = 2 * xi + yi
    place_arr = jnp.stack([chip, ci]).astype(jnp.int32)
    c_arr = jnp.reshape(ci, (1,)).astype(jnp.int32)

    x2d, tgt, mem2d = x[0], loss_target[0], mem[0]
    t, d = x2d.shape
    mla_h = w_qb.shape[2] * N_CHIPS // (NOPE + ROPE)
    ml_h = b_if.shape[1] // 2
    cr_h = wq_c.shape[2] // CROSS_DH
    f_dim = w_down.shape[1] * N_CHIPS
    q_rank, kv_rank = g_qa.shape[1], g_kva.shape[1]
    qk_w, v_w = ml_h * MLSTM_DK, ml_h * MLSTM_DV
    nc = t // CHUNK

    big_names = ["w_in", "w_qb", "w_kvb", "p_a", "p_b", "w_out", "wq_c", "wk_c", "wv_c", "wo_c", "w_up", "w_down"]
    col_sharded = {"w_in", "w_qb", "w_kvb", "wo_c", "w_up"}
    small_sharded = ["conv_qk", "g_hnorm", "conv_ffn"]
    early_big, early_small = ["w_in", "w_qb", "w_kvb"], ["conv_qk", "g_hnorm"]
    late_groups = [["p_a", "p_b", "w_out", "wq_c", "wk_c", "wv_c", "wo_c"], ["w_up", "w_down", "conv_ffn"]]
    full = {}

    def unshard(n, g):
        if n == "w_up":
            full[n] = g
        else:
            full[n] = _cols_from_shards(g) if (n in col_sharded or n in small_sharded) else g.reshape(-1, g.shape[2])

    gathered = gather_shards([wts[n][0].astype(BF16) for n in early_big], [wts[n][0] for n in early_small])
    for n, g in zip(early_big + early_small, gathered):
        unshard(n, g)
    late, order_after = [], gathered[0]
    for gi, group in enumerate(late_groups):
        src = [wts[n][0].astype(BF16) if n in big_names else wts[n][0] for n in group]
        late.append(split_start("gather_late%d_start" % gi, src, [((N_CHIPS,) + a.shape, a.dtype) for a in src],
                                3 * len(src), _gather_copies, order_after))
        order_after = late[-1][4]
    g_mix_fwd = g_mix + order_after[0:1, 0:1]

    def land_late(gi, after):
        group = late_groups[gi]
        own, landed = split_wait("gather_late%d_wait" % gi, late[gi], 3 * len(group), _gather_copies, after)
        for n, g, o in zip(group, landed, own):
            unshard(n, lax.dynamic_update_slice(g, o[None], (chip, 0, 0)))

    o_qa, o_kv, o_kpe = 0, q_rank, q_rank + kv_rank
    o_q = o_kpe + ROPE
    o_v = o_q + 2 * qk_w
    o_if = o_v + v_w
    o_o = o_if + 2 * ml_h
    o_ga, o_gb = o_o + v_w, o_o + v_w + d
    wi = full["w_in"]
    pad_kpe = jnp.zeros((d, LANE - ROPE), BF16)
    pad_if = jnp.zeros((d, LANE - 2 * ml_h), BF16)
    w_small = jnp.concatenate([wi[:, o_qa:o_q], pad_kpe, wi[:, o_if:o_o], pad_if], axis=1)
    o_kpe_s, o_if_s = o_kpe, o_kpe + LANE
    w_qk, w_v, w_o, w_ga, w_gb = wi[:, o_q:o_v], wi[:, o_v:o_if], wi[:, o_o:o_ga], wi[:, o_ga:o_gb], wi[:, o_gb:]

    wq3 = full["w_qb"].reshape(q_rank, mla_h, NOPE + ROPE)
    wq_nope = wq3[:, :, :NOPE].reshape(q_rank, mla_h * NOPE)
    wq_pe = jnp.pad(wq3[:, :, NOPE:], ((0, 0), (0, 0), (0, LANE - ROPE))).reshape(q_rank, mla_h * LANE)
    wkv3 = full["w_kvb"].reshape(kv_rank, mla_h, NOPE + VHEAD)
    wk_nope = wkv3[:, :, :NOPE].reshape(kv_rank, mla_h * NOPE)
    wv_mla = wkv3[:, :, NOPE:].reshape(kv_rank, mla_h * VHEAD)

    inv_freq = ROPE_BASE ** (-jnp.arange(0, ROPE, 2, dtype=F32) / ROPE)
    ang = positions[0].astype(F32)[:, None] * inv_freq
    cos, sin = jnp.cos(ang), jnp.sin(ang)
    zero_h = jnp.zeros_like(cos)
    tabs = [_pad_lanes(jnp.concatenate([cos, cos], axis=1)), _pad_lanes(-sin), _pad_lanes(jnp.concatenate([zero_h, sin], axis=1))]
    mla_gains = [g_qn_nope, _pad_lanes(g_qn_pe), g_kn_nope, _pad_lanes(g_kn_pe)]

    u1 = rms_fwd(x2d, g_mix_fwd, "rms_mix")
    z_small = mm(u1, w_small, name="in_small")
    z_qa, z_kv = z_small[:, o_qa:o_kv], z_small[:, o_kv:o_kpe]
    z_kpe, z_if = z_small[:, o_kpe_s:o_kpe_s + LANE], z_small[:, o_if_s:o_if_s + 2 * ml_h]
    z_qk = mm(u1, w_qk, name="in_qk")
    z_v = mm(u1, w_v, name="in_v")
    z_o = mm(u1, w_o, name="in_o")
    z_ga = mm(u1, w_ga, name="in_ga")
    z_gb = mm(u1, w_gb, name="in_gb")

    qa_n = rms_fwd(z_qa, g_qa, "rms_qa")
    kv_n = rms_fwd(z_kv, g_kva, "rms_kva")
    qn_raw = mm(qa_n, wq_nope, name="q_nope")
    qp_raw = mm(qa_n, wq_pe, name="q_pe")
    kn_raw = mm(kv_n, wk_nope, name="k_nope")
    v_mla = mm(kv_n, wv_mla, out_dtype=BF16, name="v_mla")
    q_att, k_att = mla_prep_fwd(qn_raw, qp_raw, kn_raw, z_kpe, tabs, mla_gains, mla_h)
    y_a, lse_row = attn_fwd(q_att, k_att, jnp.transpose(v_mla), mla_h)

    colscale = jnp.concatenate([jnp.full((1, qk_w), MLSTM_DK ** -0.5, F32), jnp.ones((1, qk_w), F32)], axis=1)
    qk_c = conv_qk_fwd(z_qk, full["conv_qk"], colscale)
    gates4 = z_if.reshape(nc, CHUNK, 2, ml_h)
    gcol = jnp.transpose(gates4, (3, 0, 1, 2))
    grow = jnp.transpose(gates4, (3, 0, 2, 1))
    bias = jnp.transpose(b_if.reshape(2, ml_h), (1, 0)).reshape(ml_h, 1, 2)
    h_raw, c_all, n_all, m_all = mlstm_fwd(qk_c, z_v, gcol, grow, bias, ml_h)
    g_hn = full["g_hnorm"].reshape(1, v_w)
    hn_gr, hd_gr = t // _pick(t, ROW_TILE), t // _pick(t, HEAD_ROW_TILE)
    y_b = ew(lambda *a: (_hnorm_gate(*a),), [(h_raw, "rc"), (z_o, "rc"), (g_hn, "c")], [((t, v_w), BF16, "rc")], gr=hd_gr, gc=ml_h,
             name="hnorm_gate")[0]

    land_late(0, y_b)

    pa = mm(y_a, full["p_a"], name="proj_a")
    pb = mm(y_b, full["p_b"], name="proj_b")
    merge_fn = lambda ga, gb, a, b: (_sigmoid(ga) * a + _sigmoid(gb) * b,)
    merged = ew(merge_fn, [(z_ga, "r"), (z_gb, "r"), (pa, "r"), (pb, "r")], [((t, d), BF16, "r")], gr=hn_gr, name="merge")[0]
    x1 = mm(merged, full["w_out"], add=x2d, name="out_proj")

    uc = rms_fwd(x1, g_cross, "rms_cross")
    mem_n = rms_fwd(mem2d, g_mem, "rms_mem")
    qc = mm(uc, full["wq_c"], name="cross_q")
    kc = mm(mem_n, full["wk_c"], name="cross_k")
    vc = mm(mem_n, full["wv_c"], name="cross_v")
    oc = cross_fwd(qc, kc, vc, g_cq, g_ck, cr_h)
    x2 = mm(oc, full["wo_c"], add=x1, name="cross_out")

    land_late(1, x2)
    half = N_CHIPS // 2
    u3 = rms_fwd(x2, g_ffn, "rms_ffn")
    hg = mm(u3, full["w_up"], b_shards=(0, half), name="ffn_up_gate")
    hv = mm(u3, full["w_up"], b_shards=(half, half), name="ffn_up_val")
    cw, cb = full["conv_ffn"], b_conv_ffn
    act = ffn_glu_fwd(hg, hv, cw[:, :f_dim], cw[:, f_dim:], cb[:, :f_dim], cb[:, f_dim:])
    y = mm(act, full["w_down"], add=x2, name="ffn_down")

    def loss_fn(y_, t_):
        err = y_ - t_
        part = jnp.sum(jnp.sum(err * err, axis=1, keepdims=True), axis=0, keepdims=True) * (0.5 / d)
        return err * (1.0 / d), err * (1.0 / d), jnp.broadcast_to(part, (1, LANE))

    dy, dy_mx, loss_part = ew(loss_fn, [(y, "r"), (tgt, "r")], [((t, d), F32, "r"), ((t, d), BF16, "r"), ((1, LANE), F32, "f")],
                              gr=hn_gr, name="loss")

    gw = {}
    gw["w_down"] = mm(act, dy_mx, ta=True, out_dtype=BF16, name="dw_down")
    dact = mm(dy_mx, full["w_down"], tb=True, name="d_act")
    dhg, dhv, dcw_g, dcw_v, dcb_g, dcb_v = ffn_glu_bwd(hg, hv, cw[:, :f_dim], cw[:, f_dim:], cb[:, :f_dim], cb[:, f_dim:], dact)
    gw["conv_ffn"] = jnp.concatenate([dcw_g, dcw_v], axis=1)
    gw["b_conv_ffn"] = jnp.concatenate([dcb_g, dcb_v], axis=1)
    dwup_g = mm(u3, dhg, ta=True, out_dtype=BF16, out_shards=half, name="dw_up_gate")
    dwup_v = mm(u3, dhv, ta=True, out_dtype=BF16, out_shards=half, name="dw_up_val")
    gw["w_up"] = jnp.concatenate([dwup_g, dwup_v], axis=0)

    def shard_major(n):
        if n == "w_up":
            return gw[n]
        return _cols_to_shards(gw[n]) if n in col_sharded else gw[n].reshape(N_CHIPS, -1, gw[n].shape[1])

    def chip_partials(group):
        grads_sm = [shard_major(n) for n in group]
        return [add_sibling(g, r, c_arr) for g, r in zip(grads_sm, sibling_halves(grads_sm))]

    def scatter_group(group, tag, after):
        parts_ = chip_partials(group)
        return split_start("scatter_start_" + tag, parts_, [((3,) + p.shape[1:], p.dtype) for p in parts_],
                           3 * len(parts_), _scatter_copies, after)

    group_a = ["w_up", "w_down"]
    started_a = scatter_group(group_a, "a", gw["w_up"])
    g_ffn_bwd = g_ffn + started_a[4][0:1, 0:1]
    du3 = mm(dhg, full["w_up"], tb=True, b_shards=(0, half), name="d_u3_gate")
    du3 = mm(dhv, full["w_up"], tb=True, b_shards=(half, half), add=du3, name="d_u3_val")
    dx2, gw["g_ffn"] = rms_bwd(x2, g_ffn_bwd, du3, dy, "rms_ffn_bwd")

    gw["wo_c"] = mm(oc, dx2, ta=True, out_dtype=BF16, name="dw_cross_out")
    doc = mm(dx2, full["wo_c"], tb=True, name="d_cross_o")
    dqc, dkc, dvc, gw["g_cq"], gw["g_ck"] = cross_bwd(qc, kc, vc, g_cq, g_ck, doc, cr_h)
    gw["wq_c"] = mm(uc, dqc, ta=True, out_dtype=BF16, name="dw_cross_q")
    gw["wk_c"] = mm(mem_n, dkc, ta=True, out_dtype=BF16, name="dw_cross_k")
    gw["wv_c"] = mm(mem_n, dvc, ta=True, out_dtype=BF16, name="dw_cross_v")
    duc = mm(dqc, full["wq_c"], tb=True, name="d_uc")
    dmem_n = mm(dkc, full["wk_c"], tb=True, name="d_mem_k")
    dmem_n = mm(dvc, full["wv_c"], tb=True, add=dmem_n, name="d_mem_v")
    _, gw["g_mem"] = rms_bwd(mem2d, g_mem, dmem_n, None, "rms_mem_bwd")
    dx1, gw["g_cross"] = rms_bwd(x1, g_cross, duc, dx2, "rms_cross_bwd")

    gw["w_out"] = mm(merged, dx1, ta=True, out_dtype=BF16, name="dw_out")
    dmerged = mm(dx1, full["w_out"], tb=True, name="d_merged")

    def merge_bwd(ga, gb, a, b, dm):
        _, pull = jax.vjp(lambda *args: merge_fn(*args)[0], ga, gb, a, b)
        return pull(dm)

    dz_ga, dz_gb, dpa, dpb = ew(merge_bwd, [(z_ga, "r"), (z_gb, "r"), (pa, "r"), (pb, "r"), (dmerged, "r")],
                                [((t, d), BF16, "r")] * 4, gr=hn_gr, name="merge_bwd")
    gw["p_a"] = mm(y_a, dpa, ta=True, out_dtype=BF16, name="dw_proj_a")
    gw["p_b"] = mm(y_b, dpb, ta=True, out_dtype=BF16, name="dw_proj_b")
    group_b = ["wo_c", "wq_c", "wk_c", "wv_c", "w_out", "p_a", "p_b"]
    started_b = scatter_group(group_b, "b", gw["p_b"])
    g_hn_bwd = g_hn + started_b[4][0:1, 0:1]
    dy_a = mm(dpa, full["p_a"], tb=True, name="d_ya")
    dy_b = mm(dpb, full["p_b"], tb=True, name="d_yb")

    def hnorm_bwd(h_, zo_, g_, dyb_):
        _, pull = jax.vjp(_hnorm_gate, h_, zo_, g_)
        return pull(dyb_)

    dh_raw, dz_o, dg_hn = ew(hnorm_bwd, [(h_raw, "rc"), (z_o, "rc"), (g_hn_bwd, "c"), (dy_b, "rc")],
                             [((t, v_w), F32, "rc"), ((t, v_w), BF16, "rc"), ((1, v_w), F32, "c")],
                             gr=hd_gr, gc=ml_h, order="cr", name="hnorm_gate_bwd")
    gw["g_hnorm"] = dg_hn.reshape(ml_h, MLSTM_DV)
    dq_m, dk_m, dz_v, dgcol, dgrow = mlstm_bwd(qk_c, z_v, gcol, grow, bias, c_all, n_all, m_all, dh_raw, ml_h)
    dgates4 = jnp.transpose(dgcol, (1, 2, 3, 0)) + jnp.transpose(dgrow, (1, 3, 2, 0))
    dz_if = dgates4.reshape(t, 2 * ml_h)
    gw["b_if"] = ew(lambda a: (jnp.sum(a, axis=0, keepdims=True),), [(dz_if, "r")], [((1, 2 * ml_h), F32, "f")],
                    gr=hn_gr, name="bias_if_bwd")[0]
    dz_qk, gw["conv_qk"] = conv_qk_bwd(z_qk, full["conv_qk"], colscale, dq_m, dk_m)

    dq_att, dk_att, dv_mla = attn_bwd(q_att, k_att, v_mla, y_a, dy_a, lse_row, mla_h)
    dqn_raw, dqp_raw, dkn_raw, dz_kpe, gw["g_qn_nope"], dg_qp, gw["g_kn_nope"], dg_kp = mla_prep_bwd(
        qn_raw, qp_raw, kn_raw, z_kpe, tabs, mla_gains, dq_att, dk_att, mla_h)
    gw["g_qn_pe"], gw["g_kn_pe"] = dg_qp[:, :ROPE], dg_kp[:, :ROPE]
    dwq_nope = mm(qa_n, dqn_raw, ta=True, out_dtype=BF16, name="dw_q_nope")
    dwq_pe = mm(qa_n, dqp_raw, ta=True, out_dtype=BF16, name="dw_q_pe")
    dwk_nope = mm(kv_n, dkn_raw, ta=True, out_dtype=BF16, name="dw_k_nope")
    dwv_mla = mm(kv_n, dv_mla, ta=True, out_dtype=BF16, name="dw_v_mla")
    dqa_n = mm(dqn_raw, wq_nope, tb=True, name="d_qa_nope")
    dqa_n = mm(dqp_raw, wq_pe, tb=True, add=dqa_n, name="d_qa_pe")
    dkv_n = mm(dkn_raw, wk_nope, tb=True, name="d_kv_nope")
    dkv_n = mm(dv_mla, wv_mla, tb=True, add=dkv_n, name="d_kv_v")
    dz_qa, gw["g_qa"] = rms_bwd(z_qa, g_qa, dqa_n, None, "rms_qa_bwd", BF16)
    dz_kv, gw["g_kva"] = rms_bwd(z_kv, g_kva, dkv_n, None, "rms_kva_bwd", BF16)
    gw["w_qb"] = jnp.concatenate([dwq_nope.reshape(q_rank, mla_h, NOPE), dwq_pe.reshape(q_rank, mla_h, LANE)[:, :, :ROPE]],
                                 axis=2).reshape(q_rank, -1)
    gw["w_kvb"] = jnp.concatenate([dwk_nope.reshape(kv_rank, mla_h, NOPE), dwv_mla.reshape(kv_rank, mla_h, VHEAD)],
                                  axis=2).reshape(kv_rank, -1)

    dz_small = jnp.concatenate([dz_qa, dz_kv, dz_kpe.astype(BF16), _pad_lanes(dz_if).astype(BF16)], axis=1)
    dw_small = mm(u1, dz_small, ta=True, out_dtype=BF16, name="dw_in_small")
    du1 = mm(dz_small, w_small, tb=True, name="d_u1_small")
    dw_segs = []
    for nm, dz, w_seg in (("qk", dz_qk, w_qk), ("v", dz_v, w_v), ("o", dz_o, w_o), ("ga", dz_ga, w_ga), ("gb", dz_gb, w_gb)):
        dw_segs.append(mm(u1, dz, ta=True, out_dtype=BF16, name="dw_in_" + nm))
        du1 = mm(dz, w_seg, tb=True, add=du1, name="d_u1_" + nm)
    gw["w_in"] = jnp.concatenate([dw_small[:, :o_kpe_s + ROPE], dw_segs[0], dw_segs[1],
                                  dw_small[:, o_if_s:o_if_s + 2 * ml_h], dw_segs[2], dw_segs[3], dw_segs[4]], axis=1)
    grad_x, gw["g_mix"] = rms_bwd(x2d, g_mix, du1, dx1, "rms_mix_bwd")

    group_c = ["w_in", "w_qb", "w_kvb"]
    started_c = scatter_group(group_c, "c", grad_x)
    parts_a, others_a = split_wait("scatter_wait_a", started_a, 3 * len(group_a), _scatter_copies, grad_x)
    parts_b, others_b = split_wait("scatter_wait_b", started_b, 3 * len(group_b), _scatter_copies, grad_x)
    place_ab = place_arr + started_c[4][0, 0].astype(jnp.int32)
    grads, deltas, new_m, new_v = {}, {}, {}, {}

    def finish(group, parts_, others_, place):
        joined = join_halves([sum_chips(p, o, place) for p, o in zip(parts_, others_)])
        for n, j in zip(group, joined):
            grads[n] = j.reshape(-1, j.shape[2])
            deltas[n], new_m[n], new_v[n] = adamw(grads[n], wts[n][0], mom[n][0], var[n][0], "adamw_" + n)

    finish(group_a + group_b, parts_a + parts_b, others_a + others_b, place_ab)

    small_names = [n for n in names if n not in big_names]
    pieces = [loss_part]
    for n in small_names:
        flat = gw[n].reshape(1, -1)
        pieces.append(jnp.pad(flat, ((0, 0), (0, (-flat.shape[1]) % LANE))))
    packed = jnp.concatenate(pieces, axis=1)
    packed = jnp.pad(packed, ((0, 0), (0, (-packed.shape[1]) % (8 * LANE)))).reshape(-1, LANE)
    total = allreduce_small(packed).reshape(1, -1)
    loss = total[0, 0]
    small_grads, off = {}, LANE
    for n in small_names:
        size = gw[n].size
        g_full = total[:, off:off + size].reshape(gw[n].shape)
        off += size + (-size) % LANE
        if n in small_sharded:
            width = wts[n].shape[-1]
            g_full = lax.dynamic_slice_in_dim(g_full, chip * width, width, axis=g_full.ndim - 1)
        small_grads[n] = g_full.reshape(wts[n].shape[1:])

    def pack_small(tree):
        flat = jnp.concatenate([tree[n].reshape(1, -1) for n in small_names], axis=1)
        return jnp.pad(flat, ((0, 0), (0, (-flat.shape[1]) % (8 * LANE)))).reshape(8, -1)

    sg = pack_small(small_grads)
    sd, sm, sv = adamw(sg, pack_small({n: wts[n][0] for n in small_names}), pack_small({n: mom[n][0] for n in small_names}),
                       pack_small({n: var[n][0] for n in small_names}), "adamw_small")
    off = 0
    for n in small_names:
        size = small_grads[n].size
        shp = wts[n].shape[1:]
        grads[n] = small_grads[n]
        for dst, src in ((deltas, sd), (new_m, sm), (new_v, sv)):
            dst[n] = src.reshape(1, -1)[:, off:off + size].reshape(shp)
        off += size

    parts_c, others_c = split_wait("scatter_wait_c", started_c, 3 * len(group_c), _scatter_copies,
                                   (sv, new_v[group_b[-1]], new_v[group_a[0]]))
    finish(group_c, parts_c, others_c, place_arr)

    def out(tree):
        return [tree[n].reshape(wts[n].shape) for n in names]

    return (loss, grad_x.reshape(x.shape), *out(grads), *out(deltas), *out(new_m), *out(new_v))
```

```python
import functools
import math

import jax
import jax.numpy as jnp
from jax import lax
from jax.experimental import pallas as pl
from jax.experimental.pallas import tpu as pltpu

F32, BF16 = jnp.float32, jnp.bfloat16
MESH = pl.DeviceIdType.MESH

EPS = 1e-6
CHUNK = 64
LOG2_CHUNK = 6
NOPE, ROPE, VHEAD = 128, 64, 128
MLSTM_DK, MLSTM_DV, MLSTM_CONV = 128, 256, 4
MLSTM_HEADS_PER_STEP = 8
CROSS_DH = 128
FFN_CONV = 3
ROPE_BASE = 10000.0
LOG2_E = math.log2(math.e)
ADAM_LR, ADAM_B1, ADAM_B2, ADAM_EPS, ADAM_WD, ADAM_STEP = 0.001, 0.9, 0.999, 1e-08, 0.01, 10

LANE = 128
ROW_TILE = 256
HEAD_ROW_TILE = 1024
ATT_TILE = 1024
ATT_SUB = 1024
MM_TILES = (1024, 1024, 2048)
MM_SHARD_TILE = 1536
VMEM_LIMIT = 56 * 1024 * 1024
N_CHIPS = 4

NN = ((1,), (0,))
NT = ((1,), (1,))
TN = ((0,), (0,))


def _pick(dim, pref):
    if dim <= pref:
        return dim
    for t in range(pref, 0, -LANE):
        if dim % t == 0:
            return t
    return dim


def _bdot(a, b, dims):
    return lax.dot_general(a.astype(BF16), b.astype(BF16), (dims, ((), ())), preferred_element_type=F32)


@jax.custom_vjp
def _dnn(a, b):
    return _bdot(a, b, NN)


_dnn.defvjp(lambda a, b: (_bdot(a, b, NN), (a, b)),
            lambda r, g: (_bdot(g, r[1], NT), _bdot(r[0], g, TN)))


@jax.custom_vjp
def _dnt(a, b):
    return _bdot(a, b, NT)


_dnt.defvjp(lambda a, b: (_bdot(a, b, NT), (a, b)),
            lambda r, g: (_bdot(g, r[1], NN), _bdot(g, r[0], TN)))


@jax.custom_vjp
def _dtn(a, b):
    return _bdot(a, b, TN)


_dtn.defvjp(lambda a, b: (_bdot(a, b, TN), (a, b)),
            lambda r, g: (_bdot(r[1], g, NT), _bdot(r[0], g, NN)))


@functools.partial(jax.custom_vjp, nondiff_argnums=(1,))
def _lane_roll(x, shift):
    return pltpu.roll(x, shift, 1)


_lane_roll.defvjp(lambda x, shift: (pltpu.roll(x, shift, 1), None),
                  lambda shift, _, g: (pltpu.roll(g, (LANE - shift) % LANE, 1),))


def _params(*sem):
    return pltpu.CompilerParams(dimension_semantics=sem, vmem_limit_bytes=VMEM_LIMIT)


def mm(a, b, *, ta=False, tb=False, add=None, out_dtype=F32, name, b_shards=None, out_shards=None):
    m_dim, k_dim = (a.shape[1], a.shape[0]) if ta else a.shape
    if b_shards is None:
        n_dim = b.shape[0] if tb else b.shape[1]
        assert k_dim == (b.shape[1] if tb else b.shape[0]), (name, a.shape, b.shape)
    else:
        n_dim = b.shape[1] if tb else b_shards[1] * b.shape[2]
        assert k_dim == (b_shards[1] * b.shape[2] if tb else b.shape[1]), (name, a.shape, b.shape)
    tm, tn, tk = _pick(m_dim, MM_TILES[0]), _pick(n_dim, MM_TILES[1]), _pick(k_dim, MM_TILES[2])
    if b_shards is not None and tb:
        tk = _pick(b.shape[2], MM_SHARD_TILE)
    if (b_shards is not None and not tb) or out_shards is not None:
        tn = _pick(n_dim // (out_shards or b_shards[1]), MM_SHARD_TILE)
    nk = k_dim // tk
    dims = ((0,) if ta else (1,), (1,) if tb else (0,))
    has_add = add is not None

    def body(*refs):
        a_ref, b_ref = refs[0], refs[1]
        c_ref = refs[2] if has_add else None
        o_ref = refs[3] if has_add else refs[2]
        prod = _bdot(a_ref[...], b_ref[...], dims)
        if nk == 1:
            o_ref[...] = (prod + c_ref[...].astype(F32) if has_add else prod).astype(o_ref.dtype)
            return
        acc = refs[-1]
        k = pl.program_id(2)

        @pl.when(k == 0)
        def _():
            acc[...] = prod + c_ref[...].astype(F32) if has_add else prod

        @pl.when(k > 0)
        def _():
            acc[...] += prod

        @pl.when(k == nk - 1)
        def _():
            o_ref[...] = acc[...].astype(o_ref.dtype)

    if b_shards is None:
        b_spec = pl.BlockSpec((tn, tk), lambda i, j, k: (j, k)) if tb else pl.BlockSpec((tk, tn), lambda i, j, k: (k, j))
    elif tb:
        per = b.shape[2] // tk
        b_spec = pl.BlockSpec((None, tn, tk), lambda i, j, k: (b_shards[0] + k // per, j, k % per))
    else:
        per = b.shape[2] // tn
        b_spec = pl.BlockSpec((None, tk, tn), lambda i, j, k: (b_shards[0] + j // per, k, j % per))
    in_specs = [pl.BlockSpec((tk, tm), lambda i, j, k: (k, i)) if ta else pl.BlockSpec((tm, tk), lambda i, j, k: (i, k)), b_spec]
    if out_shards is None:
        out_spec, out_shape = pl.BlockSpec((tm, tn), lambda i, j, k: (i, j)), (m_dim, n_dim)
    else:
        per_o = n_dim // out_shards // tn
        out_spec = pl.BlockSpec((None, tm, tn), lambda i, j, k: (j // per_o, i, j % per_o))
        out_shape = (out_shards, m_dim, n_dim // out_shards)
    args = [a, b]
    if has_add:
        in_specs.append(pl.BlockSpec((tm, tn), lambda i, j, k: (i, j)))
        args.append(add)
    return pl.pallas_call(
        body, grid=(m_dim // tm, n_dim // tn, nk), in_specs=in_specs, out_specs=out_spec,
        out_shape=jax.ShapeDtypeStruct(out_shape, out_dtype),
        scratch_shapes=[pltpu.VMEM((tm, tn), F32)] if nk > 1 else [],
        compiler_params=_params("parallel", "parallel", "arbitrary"), name=name)(*args)


def ew(fn, ins, outs, *, gr, gc=1, order="rc", name):
    n_in = len(ins)

    def block(shape, kind):
        r, c = shape
        return (r // gr if kind in ("rc", "r") else r, c // gc if kind in ("rc", "c") else c)

    def imap(kind):
        def f(p0, p1):
            i, j = (p0, p1) if order == "rc" else (p1, p0)
            return {"rc": (i, j), "r": (i, 0), "c": (0, j), "f": (0, 0)}[kind]
        return f

    def body(*refs):
        p0, p1 = pl.program_id(0), pl.program_id(1)
        i, j = (p0, p1) if order == "rc" else (p1, p0)
        vals = fn(*[r[...] for r in refs[:n_in]])
        for ref, val, (_, dtype, kind) in zip(refs[n_in:], vals, outs):
            first = {"rc": None, "r": (j == 0) if gc > 1 else None, "c": (i == 0) if gr > 1 else None,
                     "f": ((i == 0) & (j == 0)) if gr * gc > 1 else None}[kind]
            _store(ref, val.astype(dtype), first)

    grid = (gr, gc) if order == "rc" else (gc, gr)
    return pl.pallas_call(
        body, grid=grid,
        in_specs=[pl.BlockSpec(block(a.shape, k), imap(k)) for a, k in ins],
        out_specs=[pl.BlockSpec(block(s, k), imap(k)) for s, _, k in outs],
        out_shape=[jax.ShapeDtypeStruct(s, d) for s, d, _ in outs],
        compiler_params=_params("arbitrary", "arbitrary"), name=name)(*[a for a, _ in ins])


def _store(ref, val, first):
    if first is None:
        ref[...] = val
        return

    @pl.when(first)
    def _():
        ref[...] = val

    @pl.when(jnp.logical_not(first))
    def _():
        ref[...] += val


def _f32(*xs):
    return [x.astype(F32) for x in xs]


def _rms(x, g, n):
    ms = jnp.sum(x * x, axis=-1, keepdims=True) * (1.0 / n)
    return x * lax.rsqrt(ms + EPS) * g


def _sigmoid(x):
    return 1.0 / (1.0 + jnp.exp(-x))


def _silu(x):
    return x * _sigmoid(x)


def _log_sigmoid(x):
    return jnp.minimum(x, 0.0) - jnp.log(1.0 + jnp.exp(-jnp.abs(x)))


def rms_fwd(x, g, name, out_dtype=BF16):
    t, w = x.shape
    return ew(lambda x_, g_: (_rms(x_, g_, w),), [(x, "r"), (g, "f")], [((t, w), out_dtype, "r")],
              gr=t // _pick(t, ROW_TILE), name=name)[0]


def rms_bwd(x, g, du, res, name, out_dtype=F32):
    t, w = x.shape

    def fn(x_, g_, du_, *res_):
        _, pull = jax.vjp(lambda a, b: _rms(a, b, w), x_, g_)
        dx, dg = pull(du_.astype(F32))
        return (dx + res_[0] if res_ else dx), dg

    ins = [(x, "r"), (g, "f"), (du, "r")] + ([(res, "r")] if res is not None else [])
    return ew(fn, ins, [((t, w), out_dtype, "r"), ((1, w), F32, "f")], gr=t // _pick(t, ROW_TILE), name=name)


def _rope(x, cos_t, sin_lo, sin_hi):
    return x * cos_t + _lane_roll(x, LANE - ROPE // 2) * sin_lo + _lane_roll(x, ROPE // 2) * sin_hi


def _mla_prep(qn, qp, kn, kp, cos_t, sin_lo, sin_hi, g_qn, g_qp, g_kn, g_kp):
    q = jnp.concatenate([_rms(qn, g_qn, NOPE), _rope(_rms(qp, g_qp, ROPE), cos_t, sin_lo, sin_hi)], axis=1)
    k = jnp.concatenate([_rms(kn, g_kn, NOPE), _rope(_rms(kp, g_kp, ROPE), cos_t, sin_lo, sin_hi)], axis=1)
    return q, k


def mla_prep_fwd(qn, qp, kn, kp, tabs, gains, heads):
    t = qn.shape[0]
    ins = [(qn, "rc"), (qp, "rc"), (kn, "rc"), (kp, "r")] + [(a, "r") for a in tabs] + [(g, "f") for g in gains]
    return ew(lambda *a: _mla_prep(*_f32(*a)), ins,
              [((t, heads * 2 * LANE), BF16, "rc"), ((t, heads * 2 * LANE), BF16, "rc")],
              gr=t // _pick(t, HEAD_ROW_TILE), gc=heads, name="mla_prep_fwd")


def mla_prep_bwd(qn, qp, kn, kp, tabs, gains, dq, dk, heads):
    t = qn.shape[0]

    def fn(qn_, qp_, kn_, kp_, c_, s1_, s2_, g1, g2, g3, g4, dq_, dk_):
        _, pull = jax.vjp(lambda a, b, c, d, e, f, g, h: _mla_prep(a, b, c, d, c_, s1_, s2_, e, f, g, h),
                          qn_, qp_, kn_, kp_, g1, g2, g3, g4)
        return pull((dq_, dk_))

    ins = ([(qn, "rc"), (qp, "rc"), (kn, "rc"), (kp, "r")] + [(a, "r") for a in tabs] + [(g, "f") for g in gains]
           + [(dq, "rc"), (dk, "rc")])
    hw = heads * LANE
    outs = [((t, hw), BF16, "rc"), ((t, hw), BF16, "rc"), ((t, hw), BF16, "rc"), ((t, LANE), F32, "r")] \
        + [((1, LANE), F32, "f")] * 4
    return ew(fn, ins, outs, gr=t // _pick(t, HEAD_ROW_TILE), gc=heads, name="mla_prep_bwd")


def _chunk_mask(row0, col0, shape, rows_are_queries):
    r = jnp.right_shift(row0 + lax.broadcasted_iota(jnp.int32, shape, 0), LOG2_CHUNK)
    c = jnp.right_shift(col0 + lax.broadcasted_iota(jnp.int32, shape, 1), LOG2_CHUNK)
    return (c <= r) if rows_are_queries else (r <= c)


def _block_pairs(nq, queries_outer):
    if queries_outer:
        pairs = [(i, j) for i in range(nq) for j in range(i + 1)]
    else:
        pairs = [(i, j) for j in range(nq) for i in range(j, nq)]
    return jnp.asarray([p[0] for p in pairs], jnp.int32), jnp.asarray([p[1] for p in pairs], jnp.int32)


def attn_fwd(q, k, vt, heads):
    t = q.shape[0]
    tq = _pick(t, ATT_TILE)
    sub = _pick(tq, ATT_SUB)
    qi, kj = _block_pairs(t // tq, True)
    scale = (NOPE + ROPE) ** -0.5
    scale2 = scale * LOG2_E

    def body(qi_ref, kj_ref, q_ref, k_ref, vt_ref, o_ref, lse_ref, m_s, l_s, acc):
        p = pl.program_id(1)
        i, j = qi_ref[p], kj_ref[p]

        @pl.when(j == 0)
        def _():
            m_s[...] = jnp.full_like(m_s, -jnp.inf)
            l_s[...] = jnp.zeros_like(l_s)
            acc[...] = jnp.zeros_like(acc)

        def step(diagonal):
            for b in range(tq // sub):
                cols = pl.ds(b * sub, sub)
                st = _bdot(k_ref[...], q_ref[cols, :], NT)
                if diagonal:
                    st = jnp.where(_chunk_mask(0, b * sub, (tq, sub), False), st, -jnp.inf)
                m_old = m_s[:, cols]
                m_new = jnp.maximum(m_old, jnp.max(st, axis=0, keepdims=True))
                alpha = jnp.exp2((m_old - m_new) * scale2)
                pt = jnp.exp2((st - m_new) * scale2)
                l_s[:, cols] = alpha * l_s[:, cols] + jnp.sum(pt, axis=0, keepdims=True)
                acc[:, cols] = alpha * acc[:, cols] + _bdot(vt_ref[...], pt, NN)
                m_s[:, cols] = m_new

        pl.when(j < i)(functools.partial(step, False))

        @pl.when(j == i)
        def _():
            step(True)
            o_ref[...] = jnp.transpose(acc[...] / l_s[...])
            lse_ref[...] = m_s[...] * scale + jnp.log(l_s[...])

    return pl.pallas_call(
        body,
        grid_spec=pltpu.PrefetchScalarGridSpec(
            num_scalar_prefetch=2, grid=(heads, qi.shape[0]),
            in_specs=[pl.BlockSpec((tq, 2 * LANE), lambda h, p, qi_, kj_: (qi_[p], h)),
                      pl.BlockSpec((tq, 2 * LANE), lambda h, p, qi_, kj_: (kj_[p], h)),
                      pl.BlockSpec((VHEAD, tq), lambda h, p, qi_, kj_: (h, kj_[p]))],
            out_specs=[pl.BlockSpec((tq, VHEAD), lambda h, p, qi_, kj_: (qi_[p], h)),
                       pl.BlockSpec((None, 1, tq), lambda h, p, qi_, kj_: (h, 0, qi_[p]))],
            scratch_shapes=[pltpu.VMEM((1, tq), F32), pltpu.VMEM((1, tq), F32), pltpu.VMEM((VHEAD, tq), F32)]),
        out_shape=[jax.ShapeDtypeStruct((t, heads * VHEAD), F32), jax.ShapeDtypeStruct((heads, 1, t), F32)],
        compiler_params=_params("parallel", "arbitrary"), name="mla_attn_fwd")(qi, kj, q, k, vt)


def attn_bwd(q, k, v, o, do, lse_row, heads):
    t = q.shape[0]
    tq = _pick(t, ATT_TILE)
    sub = _pick(tq, ATT_SUB)
    qi, kj = _block_pairs(t // tq, False)
    scale = (NOPE + ROPE) ** -0.5
    scale2 = scale * LOG2_E

    def body(qi_ref, kj_ref, q_ref, k_ref, v_ref, o_ref, do_ref, lse_ref, dq_ref, dk_ref, dv_ref):
        p = pl.program_id(1)
        i, j = qi_ref[p], kj_ref[p]

        @pl.when(p == 0)
        def _():
            dq_ref[...] = jnp.zeros_like(dq_ref)

        @pl.when(i == j)
        def _():
            dk_ref[...] = jnp.zeros_like(dk_ref)
            dv_ref[...] = jnp.zeros_like(dv_ref)

        def step(diagonal):
            for b in range(tq // sub):
                cols = pl.ds(b * sub, sub)
                do_i = do_ref[cols, :]
                prod = do_i * o_ref[cols, :]
                hi = prod.astype(BF16)
                mid = (prod - hi.astype(F32)).astype(BF16)
                lo = (prod - hi.astype(F32) - mid.astype(F32)).astype(BF16)
                ones = jnp.ones((8, VHEAD), BF16)
                delta = (_bdot(ones, hi, NT) + _bdot(ones, mid, NT) + _bdot(ones, lo, NT))[0:1, :]
                q_b = q_ref[cols, :]
                st = _bdot(k_ref[...], q_b, NT)
                pt = jnp.exp2(st * scale2 - lse_ref[:, cols] * LOG2_E)
                if diagonal:
                    pt = jnp.where(_chunk_mask(0, b * sub, (tq, sub), False), pt, 0.0)
                dv_ref[...] += _bdot(pt, do_i, NN)
                dpt = _bdot(v_ref[...], do_i, NT)
                dst = pt * (dpt - delta) * scale
                dk_ref[...] += _bdot(dst, q_b, NN)
                rows = pl.ds(pl.multiple_of(i * tq + b * sub, sub), sub)
                dq_ref[rows, :] += _bdot(dst, k_ref[...], TN)

        pl.when(i > j)(functools.partial(step, False))
        pl.when(i == j)(functools.partial(step, True))

    qmap = lambda h, p, qi_, kj_: (qi_[p], h)
    kmap = lambda h, p, qi_, kj_: (kj_[p], h)
    return pl.pallas_call(
        body,
        grid_spec=pltpu.PrefetchScalarGridSpec(
            num_scalar_prefetch=2, grid=(heads, qi.shape[0]),
            in_specs=[pl.BlockSpec((tq, 2 * LANE), qmap), pl.BlockSpec((tq, 2 * LANE), kmap),
                      pl.BlockSpec((tq, VHEAD), kmap), pl.BlockSpec((tq, VHEAD), qmap), pl.BlockSpec((tq, VHEAD), qmap),
                      pl.BlockSpec((None, 1, tq), lambda h, p, qi_, kj_: (h, 0, qi_[p]))],
            out_specs=[pl.BlockSpec((t, 2 * LANE), lambda h, p, qi_, kj_: (0, h)),
                       pl.BlockSpec((tq, 2 * LANE), kmap), pl.BlockSpec((tq, VHEAD), kmap)]),
        out_shape=[jax.ShapeDtypeStruct((t, heads * 2 * LANE), F32), jax.ShapeDtypeStruct((t, heads * 2 * LANE), F32),
                   jax.ShapeDtypeStruct((t, heads * VHEAD), F32)],
        compiler_params=_params("parallel", "arbitrary"), name="mla_attn_bwd")(qi, kj, q, k, v, o, do, lse_row)


def _shift_down(x, s):
    if s == 0:
        return x
    rows = lax.broadcasted_iota(jnp.int32, x.shape, 0)
    return jnp.where(rows >= s, pltpu.roll(x, s, 0), 0.0)


def _shift_up(x, s):
    if s == 0:
        return x
    t = x.shape[0]
    rows = lax.broadcasted_iota(jnp.int32, x.shape, 0)
    return jnp.where(rows < t - s, pltpu.roll(x, t - s, 0), 0.0)


def _conv(x, w_ref, width):
    return sum(_shift_down(x, width - 1 - j) * w_ref[j:j + 1, :] for j in range(width))


def _conv_bwd(x, dpre, w_ref, dw_ref, width):
    dx = sum(_shift_up(dpre, width - 1 - j) * w_ref[j:j + 1, :] for j in range(width))
    for j in range(width):
        dw_ref[j:j + 1, :] = jnp.sum(dpre * _shift_down(x, width - 1 - j), axis=0, keepdims=True)
    return dx


def _dsilu(z):
    s = _sigmoid(z)
    return s * (1.0 + z * (1.0 - s))


def conv_qk_fwd(x, w, colscale):
    t, c = x.shape
    tc = _pick(c, 256)

    def body(x_ref, w_ref, s_ref, o_ref):
        o_ref[...] = (_silu(_conv(x_ref[...], w_ref, MLSTM_CONV)) * s_ref[...]).astype(o_ref.dtype)

    return pl.pallas_call(
        body, grid=(c // tc,),
        in_specs=[pl.BlockSpec((t, tc), lambda j: (0, j)), pl.BlockSpec((MLSTM_CONV, tc), lambda j: (0, j)),
                  pl.BlockSpec((1, tc), lambda j: (0, j))],
        out_specs=pl.BlockSpec((t, tc), lambda j: (0, j)), out_shape=jax.ShapeDtypeStruct((t, c), BF16),
        compiler_params=_params("parallel"), name="conv_qk_fwd")(x, w, colscale)


def conv_qk_bwd(x, w, colscale, dq, dk):
    t, c = x.shape
    tc = _pick(c // 2, 256)
    half = (c // 2) // tc

    def body(x_ref, w_ref, s_ref, dq_ref, dk_ref, dx_ref, dw_ref):
        j = pl.program_id(0)
        x_ = x_ref[...]
        dy = jnp.where(j < half, dq_ref[...], dk_ref[...])
        dpre = dy * s_ref[...] * _dsilu(_conv(x_, w_ref, MLSTM_CONV))
        dx_ref[...] = _conv_bwd(x_, dpre, w_ref, dw_ref, MLSTM_CONV).astype(dx_ref.dtype)

    return pl.pallas_call(
        body, grid=(c // tc,),
        in_specs=[pl.BlockSpec((t, tc), lambda j: (0, j)), pl.BlockSpec((MLSTM_CONV, tc), lambda j: (0, j)),
                  pl.BlockSpec((1, tc), lambda j: (0, j)),
                  pl.BlockSpec((t, tc), lambda j: (0, jnp.minimum(j, half - 1))),
                  pl.BlockSpec((t, tc), lambda j: (0, jnp.maximum(j - half, 0)))],
        out_specs=[pl.BlockSpec((t, tc), lambda j: (0, j)), pl.BlockSpec((MLSTM_CONV, tc), lambda j: (0, j))],
        out_shape=[jax.ShapeDtypeStruct((t, c), BF16), jax.ShapeDtypeStruct((MLSTM_CONV, c), F32)],
        compiler_params=_params("parallel"), name="conv_qk_bwd")(x, w, colscale, dq, dk)


def _mlstm_chunk(q, k, v, i_col, i_row, f_col, f_row, c_mat, n_vec, m):
    shape = (CHUNK, CHUNK)
    r = lax.broadcasted_iota(jnp.int32, shape, 0)
    c = lax.broadcasted_iota(jnp.int32, shape, 1)
    tril = c <= r
    lf_col, lf_row = _log_sigmoid(f_col), _log_sigmoid(f_row)
    bc_col = jnp.sum(jnp.where(tril, lf_row, 0.0), axis=1, keepdims=True)
    bc_row = jnp.sum(jnp.where(r <= c, lf_col, 0.0), axis=0, keepdims=True)
    logw = jnp.where(tril, bc_col - bc_row + i_row, -jnp.inf)
    inter = bc_col + m
    m_t = lax.stop_gradient(jnp.maximum(inter, jnp.max(logw, axis=1, keepdims=True)))
    w_intra = jnp.exp(logw - m_t)
    w_inter = jnp.exp(inter - m_t)
    sc = _dnt(q, k) * w_intra
    num = w_inter * _dnn(q, c_mat) + _dnn(sc, v)
    den = w_inter * jnp.sum(q * n_vec, axis=1, keepdims=True) + jnp.sum(sc, axis=1, keepdims=True)
    h = num / jnp.maximum(jnp.abs(den), jnp.exp(-m_t))
    b_last = jnp.sum(lf_row, axis=1, keepdims=True)
    m_new = lax.stop_gradient(jnp.maximum(b_last + m, jnp.max(b_last - bc_row + i_row, axis=1, keepdims=True)))
    decay = jnp.exp(b_last + m - m_new)
    uk = jnp.exp(b_last - bc_col + i_col - m_new) * k
    return h, decay * c_mat + _dtn(uk, v), decay * n_vec + jnp.sum(uk, axis=0, keepdims=True), m_new


def _mlstm_group(heads):
    return MLSTM_HEADS_PER_STEP if heads % MLSTM_HEADS_PER_STEP == 0 else 1


def _mlstm_specs(heads, grp, rev, nc):
    ci = (lambda c: nc - 1 - c) if rev else (lambda c: c)
    return dict(
        q=pl.BlockSpec((CHUNK, grp * MLSTM_DK), lambda g, c: (ci(c), g)),
        k=pl.BlockSpec((CHUNK, grp * MLSTM_DK), lambda g, c: (ci(c), heads // grp + g)),
        v=pl.BlockSpec((CHUNK, grp * MLSTM_DV), lambda g, c: (ci(c), g)),
        gc=pl.BlockSpec((grp, None, CHUNK, 2), lambda g, c: (g, ci(c), 0, 0)),
        gr=pl.BlockSpec((grp, None, 2, CHUNK), lambda g, c: (g, ci(c), 0, 0)),
        b=pl.BlockSpec((grp, 1, 2), lambda g, c: (g, 0, 0)),
        cm=pl.BlockSpec((grp, None, MLSTM_DK, MLSTM_DV), lambda g, c: (g, ci(c), 0, 0)),
        vec=pl.BlockSpec((grp, None, 1, LANE), lambda g, c: (g, ci(c), 0, 0)),
    )


def _gates(gc_ref, gr_ref, b_ref, s):
    bi, bf = b_ref[s, :, 0:1], b_ref[s, :, 1:2]
    return gc_ref[s, :, 0:1] + bi, gr_ref[s, 0:1, :] + bi, gc_ref[s, :, 1:2] + bf, gr_ref[s, 1:2, :] + bf


def mlstm_fwd(qk, v, gcol, grow, bias, heads):
    t = qk.shape[0]
    nc = t // CHUNK
    grp = _mlstm_group(heads)
    sp = _mlstm_specs(heads, grp, False, nc)

    def body(q_ref, k_ref, v_ref, gc_ref, gr_ref, b_ref, h_ref, c_ref, n_ref, m_ref, c_s, n_s, m_s):
        @pl.when(pl.program_id(1) == 0)
        def _():
            c_s[...] = jnp.zeros_like(c_s)
            n_s[...] = jnp.zeros_like(n_s)
            m_s[...] = jnp.zeros_like(m_s)

        c_ref[...] = c_s[...]
        n_ref[...] = n_s[...]
        m_ref[...] = m_s[...]
        for s in range(grp):
            qs, vs = slice(s * MLSTM_DK, (s + 1) * MLSTM_DK), slice(s * MLSTM_DV, (s + 1) * MLSTM_DV)
            q, k, v_ = _f32(q_ref[:, qs], k_ref[:, qs], v_ref[:, vs])
            h, c_new, n_new, m_new = _mlstm_chunk(q, k, v_, *_gates(gc_ref, gr_ref, b_ref, s), c_s[s], n_s[s], m_s[s, :, 0:1])
            h_ref[:, vs] = h
            c_s[s] = c_new
            n_s[s] = n_new
            m_s[s] = jnp.broadcast_to(m_new, (1, LANE))

    return pl.pallas_call(
        body, grid=(heads // grp, nc),
        in_specs=[sp["q"], sp["k"], sp["v"], sp["gc"], sp["gr"], sp["b"]],
        out_specs=[sp["v"], sp["cm"], sp["vec"], sp["vec"]],
        out_shape=[jax.ShapeDtypeStruct((t, heads * MLSTM_DV), F32),
                   jax.ShapeDtypeStruct((heads, nc, MLSTM_DK, MLSTM_DV), F32),
                   jax.ShapeDtypeStruct((heads, nc, 1, LANE), F32), jax.ShapeDtypeStruct((heads, nc, 1, LANE), F32)],
        scratch_shapes=[pltpu.VMEM((grp, MLSTM_DK, MLSTM_DV), F32), pltpu.VMEM((grp, 1, LANE), F32),
                        pltpu.VMEM((grp, 1, LANE), F32)],
        compiler_params=_params("parallel", "arbitrary"), name="mlstm_fwd")(qk, qk, v, gcol, grow, bias)


def mlstm_bwd(qk, v, gcol, grow, bias, c_all, n_all, m_all, dh, heads):
    t = qk.shape[0]
    nc = t // CHUNK
    grp = _mlstm_group(heads)
    sp = _mlstm_specs(heads, grp, True, nc)

    def body(q_ref, k_ref, v_ref, gc_ref, gr_ref, b_ref, c_ref, n_ref, m_ref, dh_ref,
             dq_ref, dk_ref, dv_ref, dgc_ref, dgr_ref, dc_s, dn_s):
        @pl.when(pl.program_id(1) == 0)
        def _():
            dc_s[...] = jnp.zeros_like(dc_s)
            dn_s[...] = jnp.zeros_like(dn_s)

        for s in range(grp):
            qs, vs = slice(s * MLSTM_DK, (s + 1) * MLSTM_DK), slice(s * MLSTM_DV, (s + 1) * MLSTM_DV)
            q, k, v_ = _f32(q_ref[:, qs], k_ref[:, qs], v_ref[:, vs])
            m = m_ref[s, :, 0:1]
            _, pull = jax.vjp(lambda *a: _mlstm_chunk(*a, m)[:3], q, k, v_, *_gates(gc_ref, gr_ref, b_ref, s),
                              c_ref[s], n_ref[s])
            dq, dk, dv, di_col, di_row, df_col, df_row, dc, dn = pull((dh_ref[:, vs], dc_s[s], dn_s[s]))
            dq_ref[:, qs] = dq
            dk_ref[:, qs] = dk
            dv_ref[:, vs] = dv.astype(dv_ref.dtype)
            dgc_ref[s, :, 0:1] = di_col
            dgc_ref[s, :, 1:2] = df_col
            dgr_ref[s, 0:1, :] = di_row
            dgr_ref[s, 1:2, :] = df_row
            dc_s[s] = dc
            dn_s[s] = dn

    qspec = pl.BlockSpec((CHUNK, grp * MLSTM_DK), lambda g, c: (nc - 1 - c, g))
    return pl.pallas_call(
        body, grid=(heads // grp, nc),
        in_specs=[sp["q"], sp["k"], sp["v"], sp["gc"], sp["gr"], sp["b"], sp["cm"], sp["vec"], sp["vec"], sp["v"]],
        out_specs=[qspec, qspec, sp["v"], sp["gc"], sp["gr"]],
        out_shape=[jax.ShapeDtypeStruct((t, heads * MLSTM_DK), F32), jax.ShapeDtypeStruct((t, heads * MLSTM_DK), F32),
                   jax.ShapeDtypeStruct((t, heads * MLSTM_DV), BF16),
                   jax.ShapeDtypeStruct(gcol.shape, F32), jax.ShapeDtypeStruct(grow.shape, F32)],
        scratch_shapes=[pltpu.VMEM((grp, MLSTM_DK, MLSTM_DV), F32), pltpu.VMEM((grp, 1, LANE), F32)],
        compiler_params=_params("parallel", "arbitrary"),
        name="mlstm_bwd")(qk, qk, v, gcol, grow, bias, c_all, n_all, m_all, dh)


def _hnorm_gate(h, zo, g):
    return _rms(h, g, MLSTM_DV) * _sigmoid(zo)


def _cross_core(q, k, v, g_q, g_k, heads):
    scale = CROSS_DH ** -0.5
    outs = []
    for h in range(heads):
        s = _dnt(_rms(q[h], g_q, CROSS_DH), _rms(k[h], g_k, CROSS_DH)) * scale
        p = jnp.exp(s - lax.stop_gradient(jnp.max(s, axis=1, keepdims=True)))
        p = p / jnp.sum(p, axis=1, keepdims=True)
        outs.append(_dnn(p, v[h]))
    return jnp.concatenate(outs, axis=1)


def _split_heads(ref, heads):
    return [ref[:, h * CROSS_DH:(h + 1) * CROSS_DH].astype(F32) for h in range(heads)]


def cross_fwd(q, k, v, g_q, g_k, heads):
    t = q.shape[0]
    tm = _pick(t, ATT_TILE)
    full = lambda a: pl.BlockSpec(a.shape, lambda i: (0, 0))

    def body(q_ref, k_ref, v_ref, gq_ref, gk_ref, o_ref):
        o_ref[...] = _cross_core(_split_heads(q_ref, heads), _split_heads(k_ref, heads), _split_heads(v_ref, heads),
                                 gq_ref[...], gk_ref[...], heads).astype(o_ref.dtype)

    return pl.pallas_call(
        body, grid=(t // tm,), in_specs=[pl.BlockSpec((tm, q.shape[1]), lambda i: (i, 0)), full(k), full(v), full(g_q), full(g_k)],
        out_specs=pl.BlockSpec((tm, q.shape[1]), lambda i: (i, 0)), out_shape=jax.ShapeDtypeStruct(q.shape, BF16),
        compiler_params=_params("parallel"), name="cross_fwd")(q, k, v, g_q, g_k)


def cross_bwd(q, k, v, g_q, g_k, do, heads):
    t, w = q.shape
    tm = _pick(t, ATT_TILE)
    full = lambda a: pl.BlockSpec(a.shape, lambda i: (0, 0))

    def body(q_ref, k_ref, v_ref, gq_ref, gk_ref, do_ref, dq_ref, dk_ref, dv_ref, dgq_ref, dgk_ref):
        qs, ks, vs = _split_heads(q_ref, heads), _split_heads(k_ref, heads), _split_heads(v_ref, heads)
        _, pull = jax.vjp(lambda a, b, c, d, e: _cross_core(a, b, c, d, e, heads), qs, ks, vs, gq_ref[...], gk_ref[...])
        dqs, dks, dvs, dgq, dgk = pull(do_ref[...])
        first = pl.program_id(0) == 0
        for h in range(heads):
            cols = slice(h * CROSS_DH, (h + 1) * CROSS_DH)
            dq_ref[:, cols] = dqs[h].astype(dq_ref.dtype)
            _store(dk_ref.at[:, cols], dks[h], first)
            _store(dv_ref.at[:, cols], dvs[h], first)
        _store(dgq_ref, dgq, first)
        _store(dgk_ref, dgk, first)

    row = pl.BlockSpec((tm, w), lambda i: (i, 0))
    return pl.pallas_call(
        body, grid=(t // tm,), in_specs=[row, full(k), full(v), full(g_q), full(g_k), row],
        out_specs=[row, full(k), full(v), full(g_q), full(g_k)],
        out_shape=[jax.ShapeDtypeStruct(q.shape, BF16), jax.ShapeDtypeStruct(k.shape, F32), jax.ShapeDtypeStruct(v.shape, F32),
                   jax.ShapeDtypeStruct(g_q.shape, F32), jax.ShapeDtypeStruct(g_k.shape, F32)],
        compiler_params=_params("arbitrary"), name="cross_bwd")(q, k, v, g_q, g_k, do)


def ffn_glu_fwd(hg, hv, wg, wv, bg, bv):
    t, f = hg.shape
    tc = _pick(f, LANE)
    col = pl.BlockSpec((t, tc), lambda j: (0, j))
    tap = pl.BlockSpec((FFN_CONV, tc), lambda j: (0, j))
    one = pl.BlockSpec((1, tc), lambda j: (0, j))

    def body(hg_ref, hv_ref, wg_ref, wv_ref, bg_ref, bv_ref, o_ref):
        gate = _conv(hg_ref[...], wg_ref, FFN_CONV) + bg_ref[...]
        val = _conv(hv_ref[...], wv_ref, FFN_CONV) + bv_ref[...]
        o_ref[...] = (_silu(gate) * val).astype(o_ref.dtype)

    return pl.pallas_call(body, grid=(f // tc,), in_specs=[col, col, tap, tap, one, one], out_specs=col,
                          out_shape=jax.ShapeDtypeStruct((t, f), BF16), compiler_params=_params("parallel"),
                          name="ffn_glu_fwd")(hg, hv, wg, wv, bg, bv)


def ffn_glu_bwd(hg, hv, wg, wv, bg, bv, dact):
    t, f = hg.shape
    tc = _pick(f, LANE)
    col = pl.BlockSpec((t, tc), lambda j: (0, j))
    tap = pl.BlockSpec((FFN_CONV, tc), lambda j: (0, j))
    one = pl.BlockSpec((1, tc), lambda j: (0, j))

    def body(hg_ref, hv_ref, wg_ref, wv_ref, bg_ref, bv_ref, da_ref, dhg_ref, dhv_ref, dwg_ref, dwv_ref, dbg_ref, dbv_ref):
        xg, xv, da = hg_ref[...], hv_ref[...], da_ref[...]
        gate = _conv(xg, wg_ref, FFN_CONV) + bg_ref[...]
        val = _conv(xv, wv_ref, FFN_CONV) + bv_ref[...]
        dgate = da * val * _dsilu(gate)
        dval = da * _silu(gate)
        dbg_ref[...] = jnp.sum(dgate, axis=0, keepdims=True)
        dbv_ref[...] = jnp.sum(dval, axis=0, keepdims=True)
        dhg_ref[...] = _conv_bwd(xg, dgate, wg_ref, dwg_ref, FFN_CONV).astype(dhg_ref.dtype)
        dhv_ref[...] = _conv_bwd(xv, dval, wv_ref, dwv_ref, FFN_CONV).astype(dhv_ref.dtype)

    return pl.pallas_call(
        body, grid=(f // tc,), in_specs=[col, col, tap, tap, one, one, col], out_specs=[col, col, tap, tap, one, one],
        out_shape=[jax.ShapeDtypeStruct((t, f), BF16), jax.ShapeDtypeStruct((t, f), BF16),
                   jax.ShapeDtypeStruct((FFN_CONV, f), F32), jax.ShapeDtypeStruct((FFN_CONV, f), F32),
                   jax.ShapeDtypeStruct((1, f), F32), jax.ShapeDtypeStruct((1, f), F32)],
        compiler_params=_params("parallel"), name="ffn_glu_bwd")(hg, hv, wg, wv, bg, bv, dact)


def _adamw(g, w, m, v):
    m = ADAM_B1 * m + (1.0 - ADAM_B1) * g
    v = ADAM_B2 * v + (1.0 - ADAM_B2) * (g * g)
    m_hat = m / (1.0 - ADAM_B1 ** ADAM_STEP)
    v_hat = v / (1.0 - ADAM_B2 ** ADAM_STEP)
    return -ADAM_LR * (m_hat / (jnp.sqrt(v_hat) + ADAM_EPS) + ADAM_WD * w), m, v


def adamw(g, w, m, v, name):
    r, c = g.shape
    tr = r
    for cand in (256, 128, 64, 32, 16, 8):
        if r % cand == 0 and cand * c * 4 <= (1 << 21):
            tr = cand
            break
    return ew(_adamw, [(g, "r"), (w, "r"), (m, "r"), (v, "r")], [((r, c), F32, "r")] * 3, gr=r // tr, name=name)


ANY = pl.BlockSpec(memory_space=pl.ANY)


def _place():
    x, y, c = lax.axis_index("x"), lax.axis_index("y"), lax.axis_index("c")
    return x, y, c, [(1 - x, y), (x, 1 - y), (1 - x, 1 - y)]


def _rcopy(src, dst, send, recv, k, to):
    return pltpu.make_async_remote_copy(src_ref=src, dst_ref=dst, send_sem=send.at[k], recv_sem=recv.at[k],
                                        device_id=to, device_id_type=MESH)


def gather_shards(bigs, smalls):
    nb, na = len(bigs), len(bigs) + len(smalls)
    arrays = list(bigs) + list(smalls)

    def body(*refs):
        ins, outs = refs[:na], refs[na:2 * na]
        send, recv = refs[2 * na:]
        x, y, c, chips = _place()
        me, sib = 2 * x + y, (x, y, 1 - c)

        def half(ref, a, which):
            rows = arrays[a].shape[0] // 2
            return ref.at[pl.ds(which * rows, rows)]

        started = []
        for a in range(na):
            for j, (cx, cy) in enumerate(chips):
                if a < nb:
                    cp = _rcopy(half(ins[a], a, c), half(outs[a].at[me], a, c), send, recv, 6 * a + j, (cx, cy, c))
                else:
                    cp = _rcopy(ins[a], outs[a].at[me], send, recv, 6 * nb + 3 * (a - nb) + j, (cx, cy, c))
                cp.start()
                started.append(cp)
        for a in range(nb):
            for j, (cx, cy) in enumerate(chips):
                landed = half(outs[a].at[2 * cx + cy], a, c)
                _rcopy(landed, landed, send, recv, 6 * a + j, (cx, cy, c)).wait_recv()
                cp = _rcopy(landed, landed, send, recv, 6 * a + 3 + j, sib)
                cp.start()
                started.append(cp)
        for a in range(na):
            for j, (cx, cy) in enumerate(chips):
                if a < nb:
                    dst = half(outs[a].at[2 * cx + cy], a, 1 - c)
                    _rcopy(dst, dst, send, recv, 6 * a + 3 + j, sib).wait_recv()
                else:
                    dst = outs[a].at[2 * cx + cy]
                    _rcopy(dst, dst, send, recv, 6 * nb + 3 * (a - nb) + j, (cx, cy, c)).wait_recv()
        for cp in started:
            cp.wait_send()

    n_sem = 6 * nb + 3 * (na - nb)
    gathered = pl.pallas_call(
        body, in_specs=[ANY] * na, out_specs=[ANY] * na,
        out_shape=[jax.ShapeDtypeStruct((N_CHIPS,) + a.shape, a.dtype) for a in arrays],
        scratch_shapes=[pltpu.SemaphoreType.DMA((n_sem,)), pltpu.SemaphoreType.DMA((n_sem,))],
        name="gather_shards")(*arrays)
    chip = 2 * lax.axis_index("x") + lax.axis_index("y")
    return [lax.dynamic_update_slice(g, a[None], (chip, 0, 0)) for g, a in zip(gathered, arrays)]


def sibling_halves(grads):
    na = len(grads)

    def body(*refs):
        ins, outs = refs[:na], refs[na:2 * na]
        send, recv = refs[2 * na:]
        x, y, c, _ = _place()
        cps = []
        for a in range(na):
            rows = grads[a].shape[1] // 2
            cp = _rcopy(ins[a].at[:, pl.ds((1 - c) * rows, rows)], outs[a], send, recv, a, (x, y, 1 - c))
            cp.start()
            cps.append(cp)
        for cp in cps:
            cp.wait()

    return pl.pallas_call(
        body, in_specs=[ANY] * na, out_specs=[ANY] * na,
        out_shape=[jax.ShapeDtypeStruct((g.shape[0], g.shape[1] // 2, g.shape[2]), g.dtype) for g in grads],
        scratch_shapes=[pltpu.SemaphoreType.DMA((na,)), pltpu.SemaphoreType.DMA((na,))], name="sibling_halves")(*grads)


def join_halves(halves):
    na = len(halves)

    def body(*refs):
        ins, outs = refs[:na], refs[na:2 * na]
        send, recv = refs[2 * na:]
        x, y, c, _ = _place()
        cps = []
        for a in range(na):
            cp = _rcopy(ins[a].at[c], outs[a].at[c], send, recv, a, (x, y, 1 - c))
            cp.start()
            cps.append(cp)
        for a in range(na):
            dst = outs[a].at[1 - c]
            _rcopy(dst, dst, send, recv, a, (x, y, 1 - c)).wait_recv()
        for cp in cps:
            cp.wait_send()

    return pl.pallas_call(
        body, in_specs=[ANY] * na, out_specs=[ANY] * na,
        out_shape=[jax.ShapeDtypeStruct(h.shape, h.dtype) for h in halves],
        input_output_aliases={a: a for a in range(na)},
        scratch_shapes=[pltpu.SemaphoreType.DMA((na,)), pltpu.SemaphoreType.DMA((na,))],
        name="join_halves")(*halves)


HBM = pl.BlockSpec(memory_space=pltpu.HBM)
SEM = pl.BlockSpec(memory_space=pltpu.SEMAPHORE)
SIDE_EFFECT = pltpu.SideEffectType.DATAFLOW_SIDE_EFFECTING


def split_start(name, srcs, land_shapes, n_copies, copies_fn, after):
    ns, nl = len(srcs), len(land_shapes)
    afters = tuple(after) if isinstance(after, (tuple, list)) else (after,)

    def body(*refs):
        ins, lands = refs[:ns], refs[ns:ns + nl]
        send, recv, token = refs[ns + nl + len(afters)], refs[ns + nl + len(afters) + 1], refs[-1]
        for k, (src, dst, dev) in enumerate(copies_fn(ins, lands, False)):
            pltpu.make_async_remote_copy(src_ref=src, dst_ref=dst, send_sem=send.at[k], recv_sem=recv.at[k],
                                         device_id=dev, device_id_type=MESH).start()
        token[...] = jnp.zeros_like(token)

    outs = pl.pallas_call(
        body, name=name,
        out_shape=(pltpu.SemaphoreType.DMA((n_copies,)), pltpu.SemaphoreType.DMA((n_copies,)),
                   *[pltpu.HBM(a.shape, a.dtype) for a in srcs], *[pltpu.HBM(s, dt) for s, dt in land_shapes],
                   jax.ShapeDtypeStruct((8, LANE), F32)),
        in_specs=[HBM] * (ns + nl) + [ANY] * len(afters),
        out_specs=(SEM, SEM, *[HBM] * (ns + nl), pl.BlockSpec(memory_space=pltpu.VMEM)),
        input_output_aliases={i: 2 + i for i in range(ns + nl)},
        compiler_params=pltpu.CompilerParams(has_side_effects=SIDE_EFFECT),
    )(*[pltpu.with_memory_space_constraint(a, pltpu.HBM) for a in srcs],
      *[pltpu.with_memory_space_constraint(lax.empty(s, dt), pltpu.HBM) for s, dt in land_shapes], *afters)
    return outs[0], outs[1], list(outs[2:2 + ns]), list(outs[2 + ns:2 + ns + nl]), outs[-1]


def split_wait(name, started, n_copies, copies_fn, after):
    send, recv, srcs, lands, _ = started
    ns, nl = len(srcs), len(lands)
    afters = tuple(after) if isinstance(after, (tuple, list)) else (after,)

    def body(*refs):
        ins, lnd = refs[:ns], refs[ns:ns + nl]
        send_ref, recv_ref = refs[ns + nl], refs[ns + nl + 1]
        for k, (src, dst, dev) in enumerate(copies_fn(ins, lnd, True)):
            cp = pltpu.make_async_remote_copy(src_ref=src, dst_ref=dst, send_sem=send_ref.at[k], recv_sem=recv_ref.at[k],
                                              device_id=dev, device_id_type=MESH)
            cp.wait_send()
            cp.wait_recv()

    outs = pl.pallas_call(
        body, name=name,
        out_shape=tuple(pltpu.HBM(a.shape, a.dtype) for a in srcs + lands),
        in_specs=[HBM] * (ns + nl) + [SEM, SEM] + [ANY] * len(afters), out_specs=tuple([HBM] * (ns + nl)),
        input_output_aliases={i: i for i in range(ns + nl)},
        compiler_params=pltpu.CompilerParams(has_side_effects=SIDE_EFFECT),
    )(*srcs, *lands, send, recv, *afters)
    return list(outs[:ns]), list(outs[ns:])


def _gather_copies(ins, lands, waiting):
    x, y, c, chips = _place()
    return [(ins[a], lands[a].at[2 * cx + cy] if waiting else lands[a].at[2 * x + y], (cx, cy, c))
            for a in range(len(ins)) for cx, cy in chips]


def _scatter_copies(ins, lands, waiting):
    del waiting
    _, _, c, chips = _place()
    return [(ins[a].at[2 * cx + cy], lands[a].at[j], (cx, cy, c)) for a in range(len(ins)) for j, (cx, cy) in enumerate(chips)]


def allreduce_small(vec):
    r = vec.shape[0]

    def body(x_ref, sum_ref, all_ref, send, recv):
        x, y, c, _ = _place()
        me = 4 * x + 2 * y + c
        all_ref[me] = x_ref[...]
        cps, peers = [], []
        for mask in range(1, 8):
            px = 1 - x if mask & 4 else x
            py = 1 - y if mask & 2 else y
            pc = 1 - c if mask & 1 else c
            peers.append(4 * px + 2 * py + pc)
            cp = _rcopy(x_ref, all_ref.at[me], send, recv, mask - 1, (px, py, pc))
            cp.start()
            cps.append(cp)
        for k, cp in enumerate(cps):
            _rcopy(x_ref, all_ref.at[peers[k]], send, recv, k, (x, y, c)).wait_recv()
        for cp in cps:
            cp.wait_send()
        total = all_ref[0]
        for d in range(1, 8):
            total = total + all_ref[d]
        sum_ref[...] = total

    vm = pl.BlockSpec(memory_space=pltpu.VMEM)
    return pl.pallas_call(
        body, in_specs=[vm], out_specs=vm, out_shape=jax.ShapeDtypeStruct((r, LANE), F32),
        scratch_shapes=[pltpu.VMEM((8, r, LANE), F32), pltpu.SemaphoreType.DMA((7,)), pltpu.SemaphoreType.DMA((7,))],
        compiler_params=pltpu.CompilerParams(vmem_limit_bytes=VMEM_LIMIT), name="allreduce_small")(vec)


def _row_tile(rows):
    for cand in (256, 128, 64, 32, 16):
        if rows % cand == 0:
            return cand
    return rows


def add_sibling(grad, recv, c_idx):
    _, rows, cols = grad.shape
    hr = rows // 2
    tr = _row_tile(hr)
    nb = hr // tr

    def body(c_ref, g_ref, r_ref, o_ref):
        o_ref[...] = (g_ref[...].astype(F32) + r_ref[...].astype(F32)).astype(o_ref.dtype)

    return pl.pallas_call(
        body,
        grid_spec=pltpu.PrefetchScalarGridSpec(
            num_scalar_prefetch=1, grid=(N_CHIPS, nb),
            in_specs=[pl.BlockSpec((None, tr, cols), lambda k, r, c_ref: (k, c_ref[0] * nb + r, 0)),
                      pl.BlockSpec((None, tr, cols), lambda k, r, c_ref: (k, r, 0))],
            out_specs=pl.BlockSpec((None, tr, cols), lambda k, r, c_ref: (k, r, 0))),
        out_shape=jax.ShapeDtypeStruct((N_CHIPS, hr, cols), BF16),
        compiler_params=_params("parallel", "parallel"), name="add_sibling")(c_idx, grad, recv)


def sum_chips(part, others, place_idx):
    _, hr, cols = part.shape
    tr = _row_tile(hr)

    def body(k_ref, p_ref, o0_ref, o1_ref, o2_ref, out_ref):
        out_ref[...] = ((p_ref[...].astype(F32) + o0_ref[...].astype(F32)) + o1_ref[...].astype(F32)) + o2_ref[...].astype(F32)

    other = lambda j: pl.BlockSpec((None, tr, cols), lambda r, k_ref: (j, r, 0))
    return pl.pallas_call(
        body,
        grid_spec=pltpu.PrefetchScalarGridSpec(
            num_scalar_prefetch=1, grid=(hr // tr,),
            in_specs=[pl.BlockSpec((None, tr, cols), lambda r, k_ref: (k_ref[0], r, 0)), other(0), other(1), other(2)],
            out_specs=pl.BlockSpec((None, tr, cols), lambda r, k_ref: (k_ref[1], r, 0))),
        out_shape=jax.ShapeDtypeStruct((2, hr, cols), F32),
        compiler_params=_params("parallel"), name="sum_chips")(place_idx, part, others, others, others)


def _pad_lanes(a, width=LANE):
    return jnp.pad(a, ((0, 0), (0, width - a.shape[1])))


def _cols_from_shards(g):
    return jnp.transpose(g, (1, 0, 2)).reshape(g.shape[1], -1)


def _cols_to_shards(w):
    k, n4 = w.shape
    return jnp.transpose(w.reshape(k, N_CHIPS, n4 // N_CHIPS), (1, 0, 2))


def kernel(x, mem, positions, g_mix, w_in, g_qa, w_qb, g_kva, w_kvb, g_qn_nope, g_qn_pe, g_kn_nope, g_kn_pe, conv_qk, b_if, g_hnorm, p_a, p_b, w_out, g_cross, g_mem, wq_c, wk_c, wv_c, g_cq, g_ck, wo_c, g_ffn, w_up, conv_ffn, b_conv_ffn, w_down, loss_target, m_g_mix, m_w_in, m_g_qa, m_w_qb, m_g_kva, m_w_kvb, m_g_qn_nope, m_g_qn_pe, m_g_kn_nope, m_g_kn_pe, m_conv_qk, m_b_if, m_g_hnorm, m_p_a, m_p_b, m_w_out, m_g_cross, m_g_mem, m_wq_c, m_wk_c, m_wv_c, m_g_cq, m_g_ck, m_wo_c, m_g_ffn, m_w_up, m_conv_ffn, m_b_conv_ffn, m_w_down, v_g_mix, v_w_in, v_g_qa, v_w_qb, v_g_kva, v_w_kvb, v_g_qn_nope, v_g_qn_pe, v_g_kn_nope, v_g_kn_pe, v_conv_qk, v_b_if, v_g_hnorm, v_p_a, v_p_b, v_w_out, v_g_cross, v_g_mem, v_wq_c, v_wk_c, v_wv_c, v_g_cq, v_g_ck, v_wo_c, v_g_ffn, v_w_up, v_conv_ffn, v_b_conv_ffn, v_w_down):
    names = ["g_mix", "w_in", "g_qa", "w_qb", "g_kva", "w_kvb", "g_qn_nope", "g_qn_pe", "g_kn_nope", "g_kn_pe", "conv_qk",
             "b_if", "g_hnorm", "p_a", "p_b", "w_out", "g_cross", "g_mem", "wq_c", "wk_c", "wv_c", "g_cq", "g_ck", "wo_c",
             "g_ffn", "w_up", "conv_ffn", "b_conv_ffn", "w_down"]
    env = locals()
    wts = {n: env[n] for n in names}
    mom = {n: env["m_" + n] for n in names}
    var = {n: env["v_" + n] for n in names}

    xi, yi, ci = lax.axis_index("x"), lax.axis_index("y"), lax.axis_index("c")
    chip = 2 * xi + yi
    place_arr = jnp.stack([chip, ci]).astype(jnp.int32)
    c_arr = jnp.reshape(ci, (1,)).astype(jnp.int32)

    x2d, tgt, mem2d = x[0], loss_target[0], mem[0]
    t, d = x2d.shape
    mla_h = w_qb.shape[2] * N_CHIPS // (NOPE + ROPE)
    ml_h = b_if.shape[1] // 2
    cr_h = wq_c.shape[2] // CROSS_DH
    f_dim = w_down.shape[1] * N_CHIPS
    q_rank, kv_rank = g_qa.shape[1], g_kva.shape[1]
    qk_w, v_w = ml_h * MLSTM_DK, ml_h * MLSTM_DV
    nc = t // CHUNK

    big_names = ["w_in", "w_qb", "w_kvb", "p_a", "p_b", "w_out", "wq_c", "wk_c", "wv_c", "wo_c", "w_up", "w_down"]
    col_sharded = {"w_in", "w_qb", "w_kvb", "wo_c", "w_up"}
    small_sharded = ["conv_qk", "g_hnorm", "conv_ffn"]
    early_big, early_small = ["w_in", "w_qb", "w_kvb"], ["conv_qk", "g_hnorm"]
    late_groups = [["p_a", "p_b", "w_out", "wq_c", "wk_c", "wv_c", "wo_c"], ["w_up", "w_down", "conv_ffn"]]
    full = {}

    def unshard(n, g):
        if n == "w_up":
            full[n] = g
        else:
            full[n] = _cols_from_shards(g) if (n in col_sharded or n in small_sharded) else g.reshape(-1, g.shape[2])

    gathered = gather_shards([wts[n][0].astype(BF16) for n in early_big], [wts[n][0] for n in early_small])
    for n, g in zip(early_big + early_small, gathered):
        unshard(n, g)
    late, order_after = [], gathered[0]
    for gi, group in enumerate(late_groups):
        src = [wts[n][0].astype(BF16) if n in big_names else wts[n][0] for n in group]
        late.append(split_start("gather_late%d_start" % gi, src, [((N_CHIPS,) + a.shape, a.dtype) for a in src],
                                3 * len(src), _gather_copies, order_after))
        order_after = late[-1][4]
    g_mix_fwd = g_mix + order_after[0:1, 0:1]

    def land_late(gi, after):
        group = late_groups[gi]
        own, landed = split_wait("gather_late%d_wait" % gi, late[gi], 3 * len(group), _gather_copies, after)
        for n, g, o in zip(group, landed, own):
            unshard(n, lax.dynamic_update_slice(g, o[None], (chip, 0, 0)))

    o_qa, o_kv, o_kpe = 0, q_rank, q_rank + kv_rank
    o_q = o_kpe + ROPE
    o_v = o_q + 2 * qk_w
    o_if = o_v + v_w
    o_o = o_if + 2 * ml_h
    o_ga, o_gb = o_o + v_w, o_o + v_w + d
    wi = full["w_in"]
    pad_kpe = jnp.zeros((d, LANE - ROPE), BF16)
    pad_if = jnp.zeros((d, LANE - 2 * ml_h), BF16)
    w_small = jnp.concatenate([wi[:, o_qa:o_q], pad_kpe, wi[:, o_if:o_o], pad_if], axis=1)
    o_kpe_s, o_if_s = o_kpe, o_kpe + LANE
    w_qk, w_v, w_o, w_ga, w_gb = wi[:, o_q:o_v], wi[:, o_v:o_if], wi[:, o_o:o_ga], wi[:, o_ga:o_gb], wi[:, o_gb:]

    wq3 = full["w_qb"].reshape(q_rank, mla_h, NOPE + ROPE)
    wq_nope = wq3[:, :, :NOPE].reshape(q_rank, mla_h * NOPE)
    wq_pe = jnp.pad(wq3[:, :, NOPE:], ((0, 0), (0, 0), (0, LANE - ROPE))).reshape(q_rank, mla_h * LANE)
    wkv3 = full["w_kvb"].reshape(kv_rank, mla_h, NOPE + VHEAD)
    wk_nope = wkv3[:, :, :NOPE].reshape(kv_rank, mla_h * NOPE)
    wv_mla = wkv3[:, :, NOPE:].reshape(kv_rank, mla_h * VHEAD)

    inv_freq = ROPE_BASE ** (-jnp.arange(0, ROPE, 2, dtype=F32) / ROPE)
    ang = positions[0].astype(F32)[:, None] * inv_freq
    cos, sin = jnp.cos(ang), jnp.sin(ang)
    zero_h = jnp.zeros_like(cos)
    tabs = [_pad_lanes(jnp.concatenate([cos, cos], axis=1)), _pad_lanes(-sin), _pad_lanes(jnp.concatenate([zero_h, sin], axis=1))]
    mla_gains = [g_qn_nope, _pad_lanes(g_qn_pe), g_kn_nope, _pad_lanes(g_kn_pe)]

    u1 = rms_fwd(x2d, g_mix_fwd, "rms_mix")
    z_small = mm(u1, w_small, name="in_small")
    z_qa, z_kv = z_small[:, o_qa:o_kv], z_small[:, o_kv:o_kpe]
    z_kpe, z_if = z_small[:, o_kpe_s:o_kpe_s + LANE], z_small[:, o_if_s:o_if_s + 2 * ml_h]
    z_qk = mm(u1, w_qk, name="in_qk")
    z_v = mm(u1, w_v, name="in_v")
    z_o = mm(u1, w_o, name="in_o")
    z_ga = mm(u1, w_ga, name="in_ga")
    z_gb = mm(u1, w_gb, name="in_gb")

    qa_n = rms_fwd(z_qa, g_qa, "rms_qa")
    kv_n = rms_fwd(z_kv, g_kva, "rms_kva")
    qn_raw = mm(qa_n, wq_nope, name="q_nope")
    qp_raw = mm(qa_n, wq_pe, name="q_pe")
    kn_raw = mm(kv_n, wk_nope, name="k_nope")
    v_mla = mm(kv_n, wv_mla, out_dtype=BF16, name="v_mla")
    q_att, k_att = mla_prep_fwd(qn_raw, qp_raw, kn_raw, z_kpe, tabs, mla_gains, mla_h)
    y_a, lse_row = attn_fwd(q_att, k_att, jnp.transpose(v_mla), mla_h)

    colscale = jnp.concatenate([jnp.full((1, qk_w), MLSTM_DK ** -0.5, F32), jnp.ones((1, qk_w), F32)], axis=1)
    qk_c = conv_qk_fwd(z_qk, full["conv_qk"], colscale)
    gates4 = z_if.reshape(nc, CHUNK, 2, ml_h)
    gcol = jnp.transpose(gates4, (3, 0, 1, 2))
    grow = jnp.transpose(gates4, (3, 0, 2, 1))
    bias = jnp.transpose(b_if.reshape(2, ml_h), (1, 0)).reshape(ml_h, 1, 2)
    h_raw, c_all, n_all, m_all = mlstm_fwd(qk_c, z_v, gcol, grow, bias, ml_h)
    g_hn = full["g_hnorm"].reshape(1, v_w)
    hn_gr, hd_gr = t // _pick(t, ROW_TILE), t // _pick(t, HEAD_ROW_TILE)
    y_b = ew(lambda *a: (_hnorm_gate(*a),), [(h_raw, "rc"), (z_o, "rc"), (g_hn, "c")], [((t, v_w), BF16, "rc")], gr=hd_gr, gc=ml_h,
             name="hnorm_gate")[0]

    land_late(0, y_b)

    pa = mm(y_a, full["p_a"], name="proj_a")
    pb = mm(y_b, full["p_b"], name="proj_b")
    merge_fn = lambda ga, gb, a, b: (_sigmoid(ga) * a + _sigmoid(gb) * b,)
    merged = ew(merge_fn, [(z_ga, "r"), (z_gb, "r"), (pa, "r"), (pb, "r")], [((t, d), BF16, "r")], gr=hn_gr, name="merge")[0]
    x1 = mm(merged, full["w_out"], add=x2d, name="out_proj")

    uc = rms_fwd(x1, g_cross, "rms_cross")
    mem_n = rms_fwd(mem2d, g_mem, "rms_mem")
    qc = mm(uc, full["wq_c"], name="cross_q")
    kc = mm(mem_n, full["wk_c"], name="cross_k")
    vc = mm(mem_n, full["wv_c"], name="cross_v")
    oc = cross_fwd(qc, kc, vc, g_cq, g_ck, cr_h)
    x2 = mm(oc, full["wo_c"], add=x1, name="cross_out")

    land_late(1, x2)
    half = N_CHIPS // 2
    u3 = rms_fwd(x2, g_ffn, "rms_ffn")
    hg = mm(u3, full["w_up"], b_shards=(0, half), name="ffn_up_gate")
    hv = mm(u3, full["w_up"], b_shards=(half, half), name="ffn_up_val")
    cw, cb = full["conv_ffn"], b_conv_ffn
    act = ffn_glu_fwd(hg, hv, cw[:, :f_dim], cw[:, f_dim:], cb[:, :f_dim], cb[:, f_dim:])
    y = mm(act, full["w_down"], add=x2, name="ffn_down")

    def loss_fn(y_, t_):
        err = y_ - t_
        part = jnp.sum(jnp.sum(err * err, axis=1, keepdims=True), axis=0, keepdims=True) * (0.5 / d)
        return err * (1.0 / d), err * (1.0 / d), jnp.broadcast_to(part, (1, LANE))

    dy, dy_mx, loss_part = ew(loss_fn, [(y, "r"), (tgt, "r")], [((t, d), F32, "r"), ((t, d), BF16, "r"), ((1, LANE), F32, "f")],
                              gr=hn_gr, name="loss")

    gw = {}
    gw["w_down"] = mm(act, dy_mx, ta=True, out_dtype=BF16, name="dw_down")
    dact = mm(dy_mx, full["w_down"], tb=True, name="d_act")
    dhg, dhv, dcw_g, dcw_v, dcb_g, dcb_v = ffn_glu_bwd(hg, hv, cw[:, :f_dim], cw[:, f_dim:], cb[:, :f_dim], cb[:, f_dim:], dact)
    gw["conv_ffn"] = jnp.concatenate([dcw_g, dcw_v], axis=1)
    gw["b_conv_ffn"] = jnp.concatenate([dcb_g, dcb_v], axis=1)
    dwup_g = mm(u3, dhg, ta=True, out_dtype=BF16, out_shards=half, name="dw_up_gate")
    dwup_v = mm(u3, dhv, ta=True, out_dtype=BF16, out_shards=half, name="dw_up_val")
    gw["w_up"] = jnp.concatenate([dwup_g, dwup_v], axis=0)

    def shard_major(n):
        if n == "w_up":
            return gw[n]
        return _cols_to_shards(gw[n]) if n in col_sharded else gw[n].reshape(N_CHIPS, -1, gw[n].shape[1])

    def chip_partials(group):
        grads_sm = [shard_major(n) for n in group]
        return [add_sibling(g, r, c_arr) for g, r in zip(grads_sm, sibling_halves(grads_sm))]

    def scatter_group(group, tag, after):
        parts_ = chip_partials(group)
        return split_start("scatter_start_" + tag, parts_, [((3,) + p.shape[1:], p.dtype) for p in parts_],
                           3 * len(parts_), _scatter_copies, after)

    group_a = ["w_up", "w_down"]
    started_a = scatter_group(group_a, "a", gw["w_up"])
    g_ffn_bwd = g_ffn + started_a[4][0:1, 0:1]
    du3 = mm(dhg, full["w_up"], tb=True, b_shards=(0, half), name="d_u3_gate")
    du3 = mm(dhv, full["w_up"], tb=True, b_shards=(half, half), add=du3, name="d_u3_val")
    dx2, gw["g_ffn"] = rms_bwd(x2, g_ffn_bwd, du3, dy, "rms_ffn_bwd")

    gw["wo_c"] = mm(oc, dx2, ta=True, out_dtype=BF16, name="dw_cross_out")
    doc = mm(dx2, full["wo_c"], tb=True, name="d_cross_o")
    dqc, dkc, dvc, gw["g_cq"], gw["g_ck"] = cross_bwd(qc, kc, vc, g_cq, g_ck, doc, cr_h)
    gw["wq_c"] = mm(uc, dqc, ta=True, out_dtype=BF16, name="dw_cross_q")
    gw["wk_c"] = mm(mem_n, dkc, ta=True, out_dtype=BF16, name="dw_cross_k")
    gw["wv_c"] = mm(mem_n, dvc, ta=True, out_dtype=BF16, name="dw_cross_v")
    duc = mm(dqc, full["wq_c"], tb=True, name="d_uc")
    dmem_n = mm(dkc, full["wk_c"], tb=True, name="d_mem_k")
    dmem_n = mm(dvc, full["wv_c"], tb=True, add=dmem_n, name="d_mem_v")
    _, gw["g_mem"] = rms_bwd(mem2d, g_mem, dmem_n, None, "rms_mem_bwd")
    dx1, gw["g_cross"] = rms_bwd(x1, g_cross, duc, dx2, "rms_cross_bwd")

    gw["w_out"] = mm(merged, dx1, ta=True, out_dtype=BF16, name="dw_out")
    dmerged = mm(dx1, full["w_out"], tb=True, name="d_merged")

    def merge_bwd(ga, gb, a, b, dm):
        _, pull = jax.vjp(lambda *args: merge_fn(*args)[0], ga, gb, a, b)
        return pull(dm)

    dz_ga, dz_gb, dpa, dpb = ew(merge_bwd, [(z_ga, "r"), (z_gb, "r"), (pa, "r"), (pb, "r"), (dmerged, "r")],
                                [((t, d), BF16, "r")] * 4, gr=hn_gr, name="merge_bwd")
    gw["p_a"] = mm(y_a, dpa, ta=True, out_dtype=BF16, name="dw_proj_a")
    gw["p_b"] = mm(y_b, dpb, ta=True, out_dtype=BF16, name="dw_proj_b")
    group_b = ["wo_c", "wq_c", "wk_c", "wv_c", "w_out", "p_a", "p_b"]
    started_b = scatter_group(group_b, "b", gw["p_b"])
    g_hn_bwd = g_hn + started_b[4][0:1, 0:1]
    dy_a = mm(dpa, full["p_a"], tb=True, name="d_ya")
    dy_b = mm(dpb, full["p_b"], tb=True, name="d_yb")

    def hnorm_bwd(h_, zo_, g_, dyb_):
        _, pull = jax.vjp(_hnorm_gate, h_, zo_, g_)
        return pull(dyb_)

    dh_raw, dz_o, dg_hn = ew(hnorm_bwd, [(h_raw, "rc"), (z_o, "rc"), (g_hn_bwd, "c"), (dy_b, "rc")],
                             [((t, v_w), F32, "rc"), ((t, v_w), BF16, "rc"), ((1, v_w), F32, "c")],
                             gr=hd_gr, gc=ml_h, order="cr", name="hnorm_gate_bwd")
    gw["g_hnorm"] = dg_hn.reshape(ml_h, MLSTM_DV)
    dq_m, dk_m, dz_v, dgcol, dgrow = mlstm_bwd(qk_c, z_v, gcol, grow, bias, c_all, n_all, m_all, dh_raw, ml_h)
    dgates4 = jnp.transpose(dgcol, (1, 2, 3, 0)) + jnp.transpose(dgrow, (1, 3, 2, 0))
    dz_if = dgates4.reshape(t, 2 * ml_h)
    gw["b_if"] = ew(lambda a: (jnp.sum(a, axis=0, keepdims=True),), [(dz_if, "r")], [((1, 2 * ml_h), F32, "f")],
                    gr=hn_gr, name="bias_if_bwd")[0]
    dz_qk, gw["conv_qk"] = conv_qk_bwd(z_qk, full["conv_qk"], colscale, dq_m, dk_m)

    dq_att, dk_att, dv_mla = attn_bwd(q_att, k_att, v_mla, y_a, dy_a, lse_row, mla_h)
    dqn_raw, dqp_raw, dkn_raw, dz_kpe, gw["g_qn_nope"], dg_qp, gw["g_kn_nope"], dg_kp = mla_prep_bwd(
        qn_raw, qp_raw, kn_raw, z_kpe, tabs, mla_gains, dq_att, dk_att, mla_h)
    gw["g_qn_pe"], gw["g_kn_pe"] = dg_qp[:, :ROPE], dg_kp[:, :ROPE]
    dwq_nope = mm(qa_n, dqn_raw, ta=True, out_dtype=BF16, name="dw_q_nope")
    dwq_pe = mm(qa_n, dqp_raw, ta=True, out_dtype=BF16, name="dw_q_pe")
    dwk_nope = mm(kv_n, dkn_raw, ta=True, out_dtype=BF16, name="dw_k_nope")
    dwv_mla = mm(kv_n, dv_mla, ta=True, out_dtype=BF16, name="dw_v_mla")
    dqa_n = mm(dqn_raw, wq_nope, tb=True, name="d_qa_nope")
    dqa_n = mm(dqp_raw, wq_pe, tb=True, add=dqa_n, name="d_qa_pe")
    dkv_n = mm(dkn_raw, wk_nope, tb=True, name="d_kv_nope")
    dkv_n = mm(dv_mla, wv_mla, tb=True, add=dkv_n, name="d_kv_v")
    dz_qa, gw["g_qa"] = rms_bwd(z_qa, g_qa, dqa_n, None, "rms_qa_bwd", BF16)
    dz_kv, gw["g_kva"] = rms_bwd(z_kv, g_kva, dkv_n, None, "rms_kva_bwd", BF16)
    gw["w_qb"] = jnp.concatenate([dwq_nope.reshape(q_rank, mla_h, NOPE), dwq_pe.reshape(q_rank, mla_h, LANE)[:, :, :ROPE]],
                                 axis=2).reshape(q_rank, -1)
    gw["w_kvb"] = jnp.concatenate([dwk_nope.reshape(kv_rank, mla_h, NOPE), dwv_mla.reshape(kv_rank, mla_h, VHEAD)],
                                  axis=2).reshape(kv_rank, -1)

    dz_small = jnp.concatenate([dz_qa, dz_kv, dz_kpe.astype(BF16), _pad_lanes(dz_if).astype(BF16)], axis=1)
    dw_small = mm(u1, dz_small, ta=True, out_dtype=BF16, name="dw_in_small")
    du1 = mm(dz_small, w_small, tb=True, name="d_u1_small")
    dw_segs = []
    for nm, dz, w_seg in (("qk", dz_qk, w_qk), ("v", dz_v, w_v), ("o", dz_o, w_o), ("ga", dz_ga, w_ga), ("gb", dz_gb, w_gb)):
        dw_segs.append(mm(u1, dz, ta=True, out_dtype=BF16, name="dw_in_" + nm))
        du1 = mm(dz, w_seg, tb=True, add=du1, name="d_u1_" + nm)
    gw["w_in"] = jnp.concatenate([dw_small[:, :o_kpe_s + ROPE], dw_segs[0], dw_segs[1],
                                  dw_small[:, o_if_s:o_if_s + 2 * ml_h], dw_segs[2], dw_segs[3], dw_segs[4]], axis=1)
    grad_x, gw["g_mix"] = rms_bwd(x2d, g_mix, du1, dx1, "rms_mix_bwd")

    group_c = ["w_in", "w_qb", "w_kvb"]
    started_c = scatter_group(group_c, "c", grad_x)
    parts_a, others_a = split_wait("scatter_wait_a", started_a, 3 * len(group_a), _scatter_copies, grad_x)
    parts_b, others_b = split_wait("scatter_wait_b", started_b, 3 * len(group_b), _scatter_copies, grad_x)
    place_ab = place_arr + started_c[4][0, 0].astype(jnp.int32)
    grads, deltas, new_m, new_v = {}, {}, {}, {}

    def finish(group, parts_, others_, place):
        joined = join_halves([sum_chips(p, o, place) for p, o in zip(parts_, others_)])
        for n, j in zip(group, joined):
            grads[n] = j.reshape(-1, j.shape[2])
            deltas[n], new_m[n], new_v[n] = adamw(grads[n], wts[n][0], mom[n][0], var[n][0], "adamw_" + n)

    finish(group_a + group_b, parts_a + parts_b, others_a + others_b, place_ab)

    small_names = [n for n in names if n not in big_names]
    pieces = [loss_part]
    for n in small_names:
        flat = gw[n].reshape(1, -1)
        pieces.append(jnp.pad(flat, ((0, 0), (0, (-flat.shape[1]) % LANE))))
    packed = jnp.concatenate(pieces, axis=1)
    packed = jnp.pad(packed, ((0, 0), (0, (-packed.shape[1]) % (8 * LANE)))).reshape(-1, LANE)
    total = allreduce_small(packed).reshape(1, -1)
    loss = total[0, 0]
    small_grads, off = {}, LANE
    for n in small_names:
        size = gw[n].size
        g_full = total[:, off:off + size].reshape(gw[n].shape)
        off += size + (-size) % LANE
        if n in small_sharded:
            width = wts[n].shape[-1]
            g_full = lax.dynamic_slice_in_dim(g_full, chip * width, width, axis=g_full.ndim - 1)
        small_grads[n] = g_full.reshape(wts[n].shape[1:])

    def pack_small(tree):
        flat = jnp.concatenate([tree[n].reshape(1, -1) for n in small_names], axis=1)
        return jnp.pad(flat, ((0, 0), (0, (-flat.shape[1]) % (8 * LANE)))).reshape(8, -1)

    sg = pack_small(small_grads)
    sd, sm, sv = adamw(sg, pack_small({n: wts[n][0] for n in small_names}), pack_small({n: mom[n][0] for n in small_names}),
                       pack_small({n: var[n][0] for n in small_names}), "adamw_small")
    off = 0
    for n in small_names:
        size = small_grads[n].size
        shp = wts[n].shape[1:]
        grads[n] = small_grads[n]
        for dst, src in ((deltas, sd), (new_m, sm), (new_v, sv)):
            dst[n] = src.reshape(1, -1)[:, off:off + size].reshape(shp)
        off += size

    parts_c, others_c = split_wait("scatter_wait_c", started_c, 3 * len(group_c), _scatter_copies,
                                   (sv, new_v[group_b[-1]], new_v[group_a[0]]))
    finish(group_c, parts_c, others_c, place_arr)

    def out(tree):
        return [tree[n].reshape(wts[n].shape) for n in names]

    return (loss, grad_x.reshape(x.shape), *out(grads), *out(deltas), *out(new_m), *out(new_v))
```

```python
import functools
import math

import jax
import jax.numpy as jnp
from jax import lax
from jax.experimental import pallas as pl
from jax.experimental.pallas import tpu as pltpu

F32, BF16 = jnp.float32, jnp.bfloat16
MESH = pl.DeviceIdType.MESH

EPS = 1e-6
CHUNK = 64
LOG2_CHUNK = 6
NOPE, ROPE, VHEAD = 128, 64, 128
MLSTM_DK, MLSTM_DV, MLSTM_CONV = 128, 256, 4
MLSTM_HEADS_PER_STEP = 8
CROSS_DH = 128
FFN_CONV = 3
ROPE_BASE = 10000.0
LOG2_E = math.log2(math.e)
ADAM_LR, ADAM_B1, ADAM_B2, ADAM_EPS, ADAM_WD, ADAM_STEP = 0.001, 0.9, 0.999, 1e-08, 0.01, 10

LANE = 128
ROW_TILE = 256
EW_CHAIN_ELEMS = 16384
HEAD_ROW_TILE = 1024
ATT_TILE = 1024
ATT_SUB = 1024
MM_TILES = (1024, 1024, 2048)
MM_SHARD_TILE = 1536
VMEM_LIMIT = 56 * 1024 * 1024
N_CHIPS = 4

NN = ((1,), (0,))
NT = ((1,), (1,))
TN = ((0,), (0,))


def _pick(dim, pref):
    if dim <= pref:
        return dim
    for t in range(pref, 0, -LANE):
        if dim % t == 0:
            return t
    return dim


def _bdot(a, b, dims):
    return lax.dot_general(a.astype(BF16), b.astype(BF16), (dims, ((), ())), preferred_element_type=F32)


@jax.custom_vjp
def _dnn(a, b):
    return _bdot(a, b, NN)


_dnn.defvjp(lambda a, b: (_bdot(a, b, NN), (a, b)),
            lambda r, g: (_bdot(g, r[1], NT), _bdot(r[0], g, TN)))


@jax.custom_vjp
def _dnt(a, b):
    return _bdot(a, b, NT)


_dnt.defvjp(lambda a, b: (_bdot(a, b, NT), (a, b)),
            lambda r, g: (_bdot(g, r[1], NN), _bdot(g, r[0], TN)))


@jax.custom_vjp
def _dtn(a, b):
    return _bdot(a, b, TN)


_dtn.defvjp(lambda a, b: (_bdot(a, b, TN), (a, b)),
            lambda r, g: (_bdot(r[1], g, NT), _bdot(r[0], g, NN)))


@functools.partial(jax.custom_vjp, nondiff_argnums=(1,))
def _lane_roll(x, shift):
    return pltpu.roll(x, shift, 1)


_lane_roll.defvjp(lambda x, shift: (pltpu.roll(x, shift, 1), None),
                  lambda shift, _, g: (pltpu.roll(g, (LANE - shift) % LANE, 1),))


def _params(*sem):
    return pltpu.CompilerParams(dimension_semantics=sem, vmem_limit_bytes=VMEM_LIMIT)


def mm(a, b, *, ta=False, tb=False, add=None, out_dtype=F32, name, b_shards=None, out_shards=None):
    m_dim, k_dim = (a.shape[1], a.shape[0]) if ta else a.shape
    if b_shards is None:
        n_dim = b.shape[0] if tb else b.shape[1]
        assert k_dim == (b.shape[1] if tb else b.shape[0]), (name, a.shape, b.shape)
    else:
        n_dim = b.shape[1] if tb else b_shards[1] * b.shape[2]
        assert k_dim == (b_shards[1] * b.shape[2] if tb else b.shape[1]), (name, a.shape, b.shape)
    tm, tn, tk = _pick(m_dim, MM_TILES[0]), _pick(n_dim, MM_TILES[1]), _pick(k_dim, MM_TILES[2])
    if b_shards is not None and tb:
        tk = _pick(b.shape[2], MM_SHARD_TILE)
    if (b_shards is not None and not tb) or out_shards is not None:
        tn = _pick(n_dim // (out_shards or b_shards[1]), MM_SHARD_TILE)
    nk = k_dim // tk
    dims = ((0,) if ta else (1,), (1,) if tb else (0,))
    has_add = add is not None

    def body(*refs):
        a_ref, b_ref = refs[0], refs[1]
        c_ref = refs[2] if has_add else None
        o_ref = refs[3] if has_add else refs[2]
        prod = _bdot(a_ref[...], b_ref[...], dims)
        if nk == 1:
            o_ref[...] = (prod + c_ref[...].astype(F32) if has_add else prod).astype(o_ref.dtype)
            return
        acc = refs[-1]
        k = pl.program_id(2)

        @pl.when(k == 0)
        def _():
            acc[...] = prod + c_ref[...].astype(F32) if has_add else prod

        @pl.when(k > 0)
        def _():
            acc[...] += prod

        @pl.when(k == nk - 1)
        def _():
            o_ref[...] = acc[...].astype(o_ref.dtype)

    if b_shards is None:
        b_spec = pl.BlockSpec((tn, tk), lambda i, j, k: (j, k)) if tb else pl.BlockSpec((tk, tn), lambda i, j, k: (k, j))
    elif tb:
        per = b.shape[2] // tk
        b_spec = pl.BlockSpec((None, tn, tk), lambda i, j, k: (b_shards[0] + k // per, j, k % per))
    else:
        per = b.shape[2] // tn
        b_spec = pl.BlockSpec((None, tk, tn), lambda i, j, k: (b_shards[0] + j // per, k, j % per))
    in_specs = [pl.BlockSpec((tk, tm), lambda i, j, k: (k, i)) if ta else pl.BlockSpec((tm, tk), lambda i, j, k: (i, k)), b_spec]
    if out_shards is None:
        out_spec, out_shape = pl.BlockSpec((tm, tn), lambda i, j, k: (i, j)), (m_dim, n_dim)
    else:
        per_o = n_dim // out_shards // tn
        out_spec = pl.BlockSpec((None, tm, tn), lambda i, j, k: (j // per_o, i, j % per_o))
        out_shape = (out_shards, m_dim, n_dim // out_shards)
    args = [a, b]
    if has_add:
        in_specs.append(pl.BlockSpec((tm, tn), lambda i, j, k: (i, j)))
        args.append(add)
    return pl.pallas_call(
        body, grid=(m_dim // tm, n_dim // tn, nk), in_specs=in_specs, out_specs=out_spec,
        out_shape=jax.ShapeDtypeStruct(out_shape, out_dtype),
        scratch_shapes=[pltpu.VMEM((tm, tn), F32)] if nk > 1 else [],
        compiler_params=_params("parallel", "parallel", "arbitrary"), name=name)(*args)


def ew(fn, ins, outs, *, gr, gc=1, order="rc", name):
    n_in = len(ins)

    def block(shape, kind):
        r, c = shape
        return (r // gr if kind in ("rc", "r") else r, c // gc if kind in ("rc", "c") else c)

    def imap(kind):
        def f(p0, p1):
            i, j = (p0, p1) if order == "rc" else (p1, p0)
            return {"rc": (i, j), "r": (i, 0), "c": (0, j), "f": (0, 0)}[kind]
        return f

    row_kinds = [k for _, k in ins if k in ("rc", "r")] + [k for _, _, k in outs if k in ("rc", "r")]
    tm = ([block(a.shape, k)[0] for a, k in ins if k in ("rc", "r")] + [block(s_, k)[0] for s_, _, k in outs if k in ("rc", "r")])[0]
    widest = max([block(a.shape, k)[1] for a, k in ins] + [block(s_, k)[1] for s_, _, k in outs])
    sub = max(16, (EW_CHAIN_ELEMS // widest) // 16 * 16)
    n_sub = tm // sub if (row_kinds and tm % sub == 0 and tm > sub) else 1

    def body(*refs):
        p0, p1 = pl.program_id(0), pl.program_id(1)
        i, j = (p0, p1) if order == "rc" else (p1, p0)
        firsts = [{"rc": None, "r": (j == 0) if gc > 1 else None, "c": (i == 0) if gr > 1 else None,
                   "f": ((i == 0) & (j == 0)) if gr * gc > 1 else None}[kind] for _, _, kind in outs]

        def chain(rows, first_rows):
            def view(ref, kind):
                return ref if (rows is None or kind in ("c", "f")) else ref.at[rows, :]
            vals = fn(*[view(r, k)[...] for r, (_, k) in zip(refs[:n_in], ins)])
            for ref, val, (_, dtype, kind), first in zip(refs[n_in:], vals, outs, firsts):
                if kind in ("c", "f") and first_rows is not None:
                    first = first_rows if first is None else (first & first_rows)
                _store(view(ref, kind), val.astype(dtype), first)

        if n_sub == 1:
            chain(None, None)
        else:
            def band(r, carry):
                chain(pl.ds(pl.multiple_of(r * sub, sub), sub), r == 0)
                return carry
            lax.fori_loop(0, n_sub, band, 0)

    grid = (gr, gc) if order == "rc" else (gc, gr)
    return pl.pallas_call(
        body, grid=grid,
        in_specs=[pl.BlockSpec(block(a.shape, k), imap(k)) for a, k in ins],
        out_specs=[pl.BlockSpec(block(s, k), imap(k)) for s, _, k in outs],
        out_shape=[jax.ShapeDtypeStruct(s, d) for s, d, _ in outs],
        compiler_params=_params("arbitrary", "arbitrary"), name=name)(*[a for a, _ in ins])


def _store(ref, val, first):
    if first is None:
        ref[...] = val
        return

    @pl.when(first)
    def _():
        ref[...] = val

    @pl.when(jnp.logical_not(first))
    def _():
        ref[...] += val


def _f32(*xs):
    return [x.astype(F32) for x in xs]


def _rms(x, g, n):
    ms = jnp.sum(x * x, axis=-1, keepdims=True) * (1.0 / n)
    return x * lax.rsqrt(ms + EPS) * g


def _sigmoid(x):
    return 1.0 / (1.0 + jnp.exp(-x))


def _silu(x):
    return x * _sigmoid(x)


def _log_sigmoid(x):
    return jnp.minimum(x, 0.0) - jnp.log(1.0 + jnp.exp(-jnp.abs(x)))


def rms_fwd(x, g, name, out_dtype=BF16):
    t, w = x.shape
    return ew(lambda x_, g_: (_rms(x_, g_, w),), [(x, "r"), (g, "f")], [((t, w), out_dtype, "r")],
              gr=t // _pick(t, ROW_TILE), name=name)[0]


def rms_bwd(x, g, du, res, name, out_dtype=F32):
    t, w = x.shape

    def fn(x_, g_, du_, *res_):
        _, pull = jax.vjp(lambda a, b: _rms(a, b, w), x_, g_)
        dx, dg = pull(du_.astype(F32))
        return (dx + res_[0] if res_ else dx), dg

    ins = [(x, "r"), (g, "f"), (du, "r")] + ([(res, "r")] if res is not None else [])
    return ew(fn, ins, [((t, w), out_dtype, "r"), ((1, w), F32, "f")], gr=t // _pick(t, ROW_TILE), name=name)


def _rope(x, cos_t, sin_lo, sin_hi):
    return x * cos_t + _lane_roll(x, LANE - ROPE // 2) * sin_lo + _lane_roll(x, ROPE // 2) * sin_hi


def _mla_prep(qn, qp, kn, kp, cos_t, sin_lo, sin_hi, g_qn, g_qp, g_kn, g_kp):
    q = jnp.concatenate([_rms(qn, g_qn, NOPE), _rope(_rms(qp, g_qp, ROPE), cos_t, sin_lo, sin_hi)], axis=1)
    k = jnp.concatenate([_rms(kn, g_kn, NOPE), _rope(_rms(kp, g_kp, ROPE), cos_t, sin_lo, sin_hi)], axis=1)
    return q, k


def mla_prep_fwd(qn, qp, kn, kp, tabs, gains, heads):
    t = qn.shape[0]
    ins = [(qn, "rc"), (qp, "rc"), (kn, "rc"), (kp, "r")] + [(a, "r") for a in tabs] + [(g, "f") for g in gains]
    return ew(lambda *a: _mla_prep(*_f32(*a)), ins,
              [((t, heads * 2 * LANE), BF16, "rc"), ((t, heads * 2 * LANE), BF16, "rc")],
              gr=t // _pick(t, HEAD_ROW_TILE), gc=heads, name="mla_prep_fwd")


def mla_prep_bwd(qn, qp, kn, kp, tabs, gains, dq, dk, heads):
    t = qn.shape[0]

    def fn(qn_, qp_, kn_, kp_, c_, s1_, s2_, g1, g2, g3, g4, dq_, dk_):
        _, pull = jax.vjp(lambda a, b, c, d, e, f, g, h: _mla_prep(a, b, c, d, c_, s1_, s2_, e, f, g, h),
                          qn_, qp_, kn_, kp_, g1, g2, g3, g4)
        return pull((dq_, dk_))

    ins = ([(qn, "rc"), (qp, "rc"), (kn, "rc"), (kp, "r")] + [(a, "r") for a in tabs] + [(g, "f") for g in gains]
           + [(dq, "rc"), (dk, "rc")])
    hw = heads * LANE
    outs = [((t, hw), BF16, "rc"), ((t, hw), BF16, "rc"), ((t, hw), BF16, "rc"), ((t, LANE), F32, "r")] \
        + [((1, LANE), F32, "f")] * 4
    return ew(fn, ins, outs, gr=t // _pick(t, HEAD_ROW_TILE), gc=heads, name="mla_prep_bwd")


def _chunk_mask(row0, col0, shape, rows_are_queries):
    r = jnp.right_shift(row0 + lax.broadcasted_iota(jnp.int32, shape, 0), LOG2_CHUNK)
    c = jnp.right_shift(col0 + lax.broadcasted_iota(jnp.int32, shape, 1), LOG2_CHUNK)
    return (c <= r) if rows_are_queries else (r <= c)


def _block_pairs(nq, queries_outer):
    if queries_outer:
        pairs = [(i, j) for i in range(nq) for j in range(i + 1)]
    else:
        pairs = [(i, j) for j in range(nq) for i in range(j, nq)]
    return jnp.asarray([p[0] for p in pairs], jnp.int32), jnp.asarray([p[1] for p in pairs], jnp.int32)


def attn_fwd(q, k, vt, heads):
    t = q.shape[0]
    tq = _pick(t, ATT_TILE)
    sub = _pick(tq, ATT_SUB)
    qi, kj = _block_pairs(t // tq, True)
    scale = (NOPE + ROPE) ** -0.5
    scale2 = scale * LOG2_E

    def body(qi_ref, kj_ref, q_ref, k_ref, vt_ref, o_ref, lse_ref, m_s, l_s, acc):
        p = pl.program_id(1)
        i, j = qi_ref[p], kj_ref[p]

        @pl.when(j == 0)
        def _():
            m_s[...] = jnp.full_like(m_s, -jnp.inf)
            l_s[...] = jnp.zeros_like(l_s)
            acc[...] = jnp.zeros_like(acc)

        def step(diagonal):
            for b in range(tq // sub):
                cols = pl.ds(b * sub, sub)
                st = _bdot(k_ref[...], q_ref[cols, :], NT)
                if diagonal:
                    st = jnp.where(_chunk_mask(0, b * sub, (tq, sub), False), st, -jnp.inf)
                m_old = m_s[:, cols]
                m_new = jnp.maximum(m_old, jnp.max(st, axis=0, keepdims=True))
                alpha = jnp.exp2((m_old - m_new) * scale2)
                pt = jnp.exp2((st - m_new) * scale2)
                l_s[:, cols] = alpha * l_s[:, cols] + jnp.sum(pt, axis=0, keepdims=True)
                acc[:, cols] = alpha * acc[:, cols] + _bdot(vt_ref[...], pt, NN)
                m_s[:, cols] = m_new

        pl.when(j < i)(functools.partial(step, False))

        @pl.when(j == i)
        def _():
            step(True)
            o_ref[...] = jnp.transpose(acc[...] / l_s[...])
            lse_ref[...] = m_s[...] * scale + jnp.log(l_s[...])

    return pl.pallas_call(
        body,
        grid_spec=pltpu.PrefetchScalarGridSpec(
            num_scalar_prefetch=2, grid=(heads, qi.shape[0]),
            in_specs=[pl.BlockSpec((tq, 2 * LANE), lambda h, p, qi_, kj_: (qi_[p], h)),
                      pl.BlockSpec((tq, 2 * LANE), lambda h, p, qi_, kj_: (kj_[p], h)),
                      pl.BlockSpec((VHEAD, tq), lambda h, p, qi_, kj_: (h, kj_[p]))],
            out_specs=[pl.BlockSpec((tq, VHEAD), lambda h, p, qi_, kj_: (qi_[p], h)),
                       pl.BlockSpec((None, 1, tq), lambda h, p, qi_, kj_: (h, 0, qi_[p]))],
            scratch_shapes=[pltpu.VMEM((1, tq), F32), pltpu.VMEM((1, tq), F32), pltpu.VMEM((VHEAD, tq), F32)]),
        out_shape=[jax.ShapeDtypeStruct((t, heads * VHEAD), F32), jax.ShapeDtypeStruct((heads, 1, t), F32)],
        compiler_params=_params("parallel", "arbitrary"), name="mla_attn_fwd")(qi, kj, q, k, vt)


def attn_bwd(q, k, v, o, do, lse_row, heads):
    t = q.shape[0]
    tq = _pick(t, ATT_TILE)
    sub = _pick(tq, ATT_SUB)
    qi, kj = _block_pairs(t // tq, False)
    scale = (NOPE + ROPE) ** -0.5
    scale2 = scale * LOG2_E

    def body(qi_ref, kj_ref, q_ref, k_ref, v_ref, o_ref, do_ref, lse_ref, dq_ref, dk_ref, dv_ref):
        p = pl.program_id(1)
        i, j = qi_ref[p], kj_ref[p]

        @pl.when(p == 0)
        def _():
            dq_ref[...] = jnp.zeros_like(dq_ref)

        @pl.when(i == j)
        def _():
            dk_ref[...] = jnp.zeros_like(dk_ref)
            dv_ref[...] = jnp.zeros_like(dv_ref)

        def step(diagonal):
            for b in range(tq // sub):
                cols = pl.ds(b * sub, sub)
                do_i = do_ref[cols, :]
                prod = do_i * o_ref[cols, :]
                hi = prod.astype(BF16)
                mid = (prod - hi.astype(F32)).astype(BF16)
                lo = (prod - hi.astype(F32) - mid.astype(F32)).astype(BF16)
                ones = jnp.ones((8, VHEAD), BF16)
                delta = (_bdot(ones, hi, NT) + _bdot(ones, mid, NT) + _bdot(ones, lo, NT))[0:1, :]
                q_b = q_ref[cols, :]
                st = _bdot(k_ref[...], q_b, NT)
                pt = jnp.exp2(st * scale2 - lse_ref[:, cols] * LOG2_E)
                if diagonal:
                    pt = jnp.where(_chunk_mask(0, b * sub, (tq, sub), False), pt, 0.0)
                dv_ref[...] += _bdot(pt, do_i, NN)
                dpt = _bdot(v_ref[...], do_i, NT)
                dst = pt * (dpt - delta) * scale
                dk_ref[...] += _bdot(dst, q_b, NN)
                rows = pl.ds(pl.multiple_of(i * tq + b * sub, sub), sub)
                dq_ref[rows, :] += _bdot(dst, k_ref[...], TN)

        pl.when(i > j)(functools.partial(step, False))
        pl.when(i == j)(functools.partial(step, True))

    qmap = lambda h, p, qi_, kj_: (qi_[p], h)
    kmap = lambda h, p, qi_, kj_: (kj_[p], h)
    return pl.pallas_call(
        body,
        grid_spec=pltpu.PrefetchScalarGridSpec(
            num_scalar_prefetch=2, grid=(heads, qi.shape[0]),
            in_specs=[pl.BlockSpec((tq, 2 * LANE), qmap), pl.BlockSpec((tq, 2 * LANE), kmap),
                      pl.BlockSpec((tq, VHEAD), kmap), pl.BlockSpec((tq, VHEAD), qmap), pl.BlockSpec((tq, VHEAD), qmap),
                      pl.BlockSpec((None, 1, tq), lambda h, p, qi_, kj_: (h, 0, qi_[p]))],
            out_specs=[pl.BlockSpec((t, 2 * LANE), lambda h, p, qi_, kj_: (0, h)),
                       pl.BlockSpec((tq, 2 * LANE), kmap), pl.BlockSpec((tq, VHEAD), kmap)]),
        out_shape=[jax.ShapeDtypeStruct((t, heads * 2 * LANE), F32), jax.ShapeDtypeStruct((t, heads * 2 * LANE), F32),
                   jax.ShapeDtypeStruct((t, heads * VHEAD), F32)],
        compiler_params=_params("parallel", "arbitrary"), name="mla_attn_bwd")(qi, kj, q, k, v, o, do, lse_row)


def _shift_down(x, s):
    if s == 0:
        return x
    rows = lax.broadcasted_iota(jnp.int32, x.shape, 0)
    return jnp.where(rows >= s, pltpu.roll(x, s, 0), 0.0)


def _shift_up(x, s):
    if s == 0:
        return x
    t = x.shape[0]
    rows = lax.broadcasted_iota(jnp.int32, x.shape, 0)
    return jnp.where(rows < t - s, pltpu.roll(x, t - s, 0), 0.0)


def _conv(x, w_ref, width):
    return sum(_shift_down(x, width - 1 - j) * w_ref[j:j + 1, :] for j in range(width))


def _conv_bwd(x, dpre, w_ref, dw_ref, width):
    dx = sum(_shift_up(dpre, width - 1 - j) * w_ref[j:j + 1, :] for j in range(width))
    for j in range(width):
        dw_ref[j:j + 1, :] = jnp.sum(dpre * _shift_down(x, width - 1 - j), axis=0, keepdims=True)
    return dx


def _dsilu(z):
    s = _sigmoid(z)
    return s * (1.0 + z * (1.0 - s))


def conv_qk_fwd(x, w, colscale):
    t, c = x.shape
    tc = _pick(c, 256)

    def body(x_ref, w_ref, s_ref, o_ref):
        o_ref[...] = (_silu(_conv(x_ref[...], w_ref, MLSTM_CONV)) * s_ref[...]).astype(o_ref.dtype)

    return pl.pallas_call(
        body, grid=(c // tc,),
        in_specs=[pl.BlockSpec((t, tc), lambda j: (0, j)), pl.BlockSpec((MLSTM_CONV, tc), lambda j: (0, j)),
                  pl.BlockSpec((1, tc), lambda j: (0, j))],
        out_specs=pl.BlockSpec((t, tc), lambda j: (0, j)), out_shape=jax.ShapeDtypeStruct((t, c), BF16),
        compiler_params=_params("parallel"), name="conv_qk_fwd")(x, w, colscale)


def conv_qk_bwd(x, w, colscale, dq, dk):
    t, c = x.shape
    tc = _pick(c // 2, 256)
    half = (c // 2) // tc

    def body(x_ref, w_ref, s_ref, dq_ref, dk_ref, dx_ref, dw_ref):
        j = pl.program_id(0)
        x_ = x_ref[...]
        dy = jnp.where(j < half, dq_ref[...], dk_ref[...])
        dpre = dy * s_ref[...] * _dsilu(_conv(x_, w_ref, MLSTM_CONV))
        dx_ref[...] = _conv_bwd(x_, dpre, w_ref, dw_ref, MLSTM_CONV).astype(dx_ref.dtype)

    return pl.pallas_call(
        body, grid=(c // tc,),
        in_specs=[pl.BlockSpec((t, tc), lambda j: (0, j)), pl.BlockSpec((MLSTM_CONV, tc), lambda j: (0, j)),
                  pl.BlockSpec((1, tc), lambda j: (0, j)),
                  pl.BlockSpec((t, tc), lambda j: (0, jnp.minimum(j, half - 1))),
                  pl.BlockSpec((t, tc), lambda j: (0, jnp.maximum(j - half, 0)))],
        out_specs=[pl.BlockSpec((t, tc), lambda j: (0, j)), pl.BlockSpec((MLSTM_CONV, tc), lambda j: (0, j))],
        out_shape=[jax.ShapeDtypeStruct((t, c), BF16), jax.ShapeDtypeStruct((MLSTM_CONV, c), F32)],
        compiler_params=_params("parallel"), name="conv_qk_bwd")(x, w, colscale, dq, dk)


def _mlstm_chunk(q, k, v, i_col, i_row, f_col, f_row, c_mat, n_vec, m):
    shape = (CHUNK, CHUNK)
    r = lax.broadcasted_iota(jnp.int32, shape, 0)
    c = lax.broadcasted_iota(jnp.int32, shape, 1)
    tril = c <= r
    lf_col, lf_row = _log_sigmoid(f_col), _log_sigmoid(f_row)
    bc_col = jnp.sum(jnp.where(tril, lf_row, 0.0), axis=1, keepdims=True)
    bc_row = jnp.sum(jnp.where(r <= c, lf_col, 0.0), axis=0, keepdims=True)
    logw = jnp.where(tril, bc_col - bc_row + i_row, -jnp.inf)
    inter = bc_col + m
    m_t = lax.stop_gradient(jnp.maximum(inter, jnp.max(logw, axis=1, keepdims=True)))
    w_intra = jnp.exp(logw - m_t)
    w_inter = jnp.exp(inter - m_t)
    sc = _dnt(q, k) * w_intra
    num = w_inter * _dnn(q, c_mat) + _dnn(sc, v)
    den = w_inter * jnp.sum(q * n_vec, axis=1, keepdims=True) + jnp.sum(sc, axis=1, keepdims=True)
    h = num / jnp.maximum(jnp.abs(den), jnp.exp(-m_t))
    b_last = jnp.sum(lf_row, axis=1, keepdims=True)
    m_new = lax.stop_gradient(jnp.maximum(b_last + m, jnp.max(b_last - bc_row + i_row, axis=1, keepdims=True)))
    decay = jnp.exp(b_last + m - m_new)
    uk = jnp.exp(b_last - bc_col + i_col - m_new) * k
    return h, decay * c_mat + _dtn(uk, v), decay * n_vec + jnp.sum(uk, axis=0, keepdims=True), m_new


def _mlstm_group(heads):
    return MLSTM_HEADS_PER_STEP if heads % MLSTM_HEADS_PER_STEP == 0 else 1


def _mlstm_specs(heads, grp, rev, nc):
    ci = (lambda c: nc - 1 - c) if rev else (lambda c: c)
    return dict(
        q=pl.BlockSpec((CHUNK, grp * MLSTM_DK), lambda g, c: (ci(c), g)),
        k=pl.BlockSpec((CHUNK, grp * MLSTM_DK), lambda g, c: (ci(c), heads // grp + g)),
        v=pl.BlockSpec((CHUNK, grp * MLSTM_DV), lambda g, c: (ci(c), g)),
        gc=pl.BlockSpec((grp, None, CHUNK, 2), lambda g, c: (g, ci(c), 0, 0)),
        gr=pl.BlockSpec((grp, None, 2, CHUNK), lambda g, c: (g, ci(c), 0, 0)),
        b=pl.BlockSpec((grp, 1, 2), lambda g, c: (g, 0, 0)),
        cm=pl.BlockSpec((grp, None, MLSTM_DK, MLSTM_DV), lambda g, c: (g, ci(c), 0, 0)),
        vec=pl.BlockSpec((grp, None, 1, LANE), lambda g, c: (g, ci(c), 0, 0)),
    )


def _gates(gc_ref, gr_ref, b_ref, s):
    bi, bf = b_ref[s, :, 0:1], b_ref[s, :, 1:2]
    return gc_ref[s, :, 0:1] + bi, gr_ref[s, 0:1, :] + bi, gc_ref[s, :, 1:2] + bf, gr_ref[s, 1:2, :] + bf


def mlstm_fwd(qk, v, gcol, grow, bias, heads):
    t = qk.shape[0]
    nc = t // CHUNK
    grp = _mlstm_group(heads)
    sp = _mlstm_specs(heads, grp, False, nc)

    def body(q_ref, k_ref, v_ref, gc_ref, gr_ref, b_ref, h_ref, c_ref, n_ref, m_ref, c_s, n_s, m_s):
        @pl.when(pl.program_id(1) == 0)
        def _():
            c_s[...] = jnp.zeros_like(c_s)
            n_s[...] = jnp.zeros_like(n_s)
            m_s[...] = jnp.zeros_like(m_s)

        c_ref[...] = c_s[...]
        n_ref[...] = n_s[...]
        m_ref[...] = m_s[...]
        for s in range(grp):
            qs, vs = slice(s * MLSTM_DK, (s + 1) * MLSTM_DK), slice(s * MLSTM_DV, (s + 1) * MLSTM_DV)
            q, k, v_ = _f32(q_ref[:, qs], k_ref[:, qs], v_ref[:, vs])
            h, c_new, n_new, m_new = _mlstm_chunk(q, k, v_, *_gates(gc_ref, gr_ref, b_ref, s), c_s[s], n_s[s], m_s[s, :, 0:1])
            h_ref[:, vs] = h
            c_s[s] = c_new
            n_s[s] = n_new
            m_s[s] = jnp.broadcast_to(m_new, (1, LANE))

    return pl.pallas_call(
        body, grid=(heads // grp, nc),
        in_specs=[sp["q"], sp["k"], sp["v"], sp["gc"], sp["gr"], sp["b"]],
        out_specs=[sp["v"], sp["cm"], sp["vec"], sp["vec"]],
        out_shape=[jax.ShapeDtypeStruct((t, heads * MLSTM_DV), F32),
                   jax.ShapeDtypeStruct((heads, nc, MLSTM_DK, MLSTM_DV), F32),
                   jax.ShapeDtypeStruct((heads, nc, 1, LANE), F32), jax.ShapeDtypeStruct((heads, nc, 1, LANE), F32)],
        scratch_shapes=[pltpu.VMEM((grp, MLSTM_DK, MLSTM_DV), F32), pltpu.VMEM((grp, 1, LANE), F32),
                        pltpu.VMEM((grp, 1, LANE), F32)],
        compiler_params=_params("parallel", "arbitrary"), name="mlstm_fwd")(qk, qk, v, gcol, grow, bias)


def mlstm_bwd(qk, v, gcol, grow, bias, c_all, n_all, m_all, dh, heads):
    t = qk.shape[0]
    nc = t // CHUNK
    grp = _mlstm_group(heads)
    sp = _mlstm_specs(heads, grp, True, nc)

    def body(q_ref, k_ref, v_ref, gc_ref, gr_ref, b_ref, c_ref, n_ref, m_ref, dh_ref,
             dq_ref, dk_ref, dv_ref, dgc_ref, dgr_ref, dc_s, dn_s):
        @pl.when(pl.program_id(1) == 0)
        def _():
            dc_s[...] = jnp.zeros_like(dc_s)
            dn_s[...] = jnp.zeros_like(dn_s)

        for s in range(grp):
            qs, vs = slice(s * MLSTM_DK, (s + 1) * MLSTM_DK), slice(s * MLSTM_DV, (s + 1) * MLSTM_DV)
            q, k, v_ = _f32(q_ref[:, qs], k_ref[:, qs], v_ref[:, vs])
            m = m_ref[s, :, 0:1]
            _, pull = jax.vjp(lambda *a: _mlstm_chunk(*a, m)[:3], q, k, v_, *_gates(gc_ref, gr_ref, b_ref, s),
                              c_ref[s], n_ref[s])
            dq, dk, dv, di_col, di_row, df_col, df_row, dc, dn = pull((dh_ref[:, vs], dc_s[s], dn_s[s]))
            dq_ref[:, qs] = dq
            dk_ref[:, qs] = dk
            dv_ref[:, vs] = dv.astype(dv_ref.dtype)
            dgc_ref[s, :, 0:1] = di_col
            dgc_ref[s, :, 1:2] = df_col
            dgr_ref[s, 0:1, :] = di_row
            dgr_ref[s, 1:2, :] = df_row
            dc_s[s] = dc
            dn_s[s] = dn

    qspec = pl.BlockSpec((CHUNK, grp * MLSTM_DK), lambda g, c: (nc - 1 - c, g))
    return pl.pallas_call(
        body, grid=(heads // grp, nc),
        in_specs=[sp["q"], sp["k"], sp["v"], sp["gc"], sp["gr"], sp["b"], sp["cm"], sp["vec"], sp["vec"], sp["v"]],
        out_specs=[qspec, qspec, sp["v"], sp["gc"], sp["gr"]],
        out_shape=[jax.ShapeDtypeStruct((t, heads * MLSTM_DK), F32), jax.ShapeDtypeStruct((t, heads * MLSTM_DK), F32),
                   jax.ShapeDtypeStruct((t, heads * MLSTM_DV), BF16),
                   jax.ShapeDtypeStruct(gcol.shape, F32), jax.ShapeDtypeStruct(grow.shape, F32)],
        scratch_shapes=[pltpu.VMEM((grp, MLSTM_DK, MLSTM_DV), F32), pltpu.VMEM((grp, 1, LANE), F32)],
        compiler_params=_params("parallel", "arbitrary"),
        name="mlstm_bwd")(qk, qk, v, gcol, grow, bias, c_all, n_all, m_all, dh)


def _hnorm_gate(h, zo, g):
    return _rms(h, g, MLSTM_DV) * _sigmoid(zo)


def _cross_core(q, k, v, g_q, g_k, heads):
    scale = CROSS_DH ** -0.5
    outs = []
    for h in range(heads):
        s = _dnt(_rms(q[h], g_q, CROSS_DH), _rms(k[h], g_k, CROSS_DH)) * scale
        p = jnp.exp(s - lax.stop_gradient(jnp.max(s, axis=1, keepdims=True)))
        p = p / jnp.sum(p, axis=1, keepdims=True)
        outs.append(_dnn(p, v[h]))
    return jnp.concatenate(outs, axis=1)


def _split_heads(ref, heads):
    return [ref[:, h * CROSS_DH:(h + 1) * CROSS_DH].astype(F32) for h in range(heads)]


def cross_fwd(q, k, v, g_q, g_k, heads):
    t = q.shape[0]
    tm = _pick(t, ATT_TILE)
    full = lambda a: pl.BlockSpec(a.shape, lambda i: (0, 0))

    def body(q_ref, k_ref, v_ref, gq_ref, gk_ref, o_ref):
        o_ref[...] = _cross_core(_split_heads(q_ref, heads), _split_heads(k_ref, heads), _split_heads(v_ref, heads),
                                 gq_ref[...], gk_ref[...], heads).astype(o_ref.dtype)

    return pl.pallas_call(
        body, grid=(t // tm,), in_specs=[pl.BlockSpec((tm, q.shape[1]), lambda i: (i, 0)), full(k), full(v), full(g_q), full(g_k)],
        out_specs=pl.BlockSpec((tm, q.shape[1]), lambda i: (i, 0)), out_shape=jax.ShapeDtypeStruct(q.shape, BF16),
        compiler_params=_params("parallel"), name="cross_fwd")(q, k, v, g_q, g_k)


def cross_bwd(q, k, v, g_q, g_k, do, heads):
    t, w = q.shape
    tm = _pick(t, ATT_TILE)
    full = lambda a: pl.BlockSpec(a.shape, lambda i: (0, 0))

    def body(q_ref, k_ref, v_ref, gq_ref, gk_ref, do_ref, dq_ref, dk_ref, dv_ref, dgq_ref, dgk_ref):
        qs, ks, vs = _split_heads(q_ref, heads), _split_heads(k_ref, heads), _split_heads(v_ref, heads)
        _, pull = jax.vjp(lambda a, b, c, d, e: _cross_core(a, b, c, d, e, heads), qs, ks, vs, gq_ref[...], gk_ref[...])
        dqs, dks, dvs, dgq, dgk = pull(do_ref[...])
        first = pl.program_id(0) == 0
        for h in range(heads):
            cols = slice(h * CROSS_DH, (h + 1) * CROSS_DH)
            dq_ref[:, cols] = dqs[h].astype(dq_ref.dtype)
            _store(dk_ref.at[:, cols], dks[h], first)
            _store(dv_ref.at[:, cols], dvs[h], first)
        _store(dgq_ref, dgq, first)
        _store(dgk_ref, dgk, first)

    row = pl.BlockSpec((tm, w), lambda i: (i, 0))
    return pl.pallas_call(
        body, grid=(t // tm,), in_specs=[row, full(k), full(v), full(g_q), full(g_k), row],
        out_specs=[row, full(k), full(v), full(g_q), full(g_k)],
        out_shape=[jax.ShapeDtypeStruct(q.shape, BF16), jax.ShapeDtypeStruct(k.shape, F32), jax.ShapeDtypeStruct(v.shape, F32),
                   jax.ShapeDtypeStruct(g_q.shape, F32), jax.ShapeDtypeStruct(g_k.shape, F32)],
        compiler_params=_params("arbitrary"), name="cross_bwd")(q, k, v, g_q, g_k, do)


def ffn_glu_fwd(hg, hv, wg, wv, bg, bv):
    t, f = hg.shape
    tc = _pick(f, LANE)
    col = pl.BlockSpec((t, tc), lambda j: (0, j))
    tap = pl.BlockSpec((FFN_CONV, tc), lambda j: (0, j))
    one = pl.BlockSpec((1, tc), lambda j: (0, j))

    def body(hg_ref, hv_ref, wg_ref, wv_ref, bg_ref, bv_ref, o_ref):
        gate = _conv(hg_ref[...], wg_ref, FFN_CONV) + bg_ref[...]
        val = _conv(hv_ref[...], wv_ref, FFN_CONV) + bv_ref[...]
        o_ref[...] = (_silu(gate) * val).astype(o_ref.dtype)

    return pl.pallas_call(body, grid=(f // tc,), in_specs=[col, col, tap, tap, one, one], out_specs=col,
                          out_shape=jax.ShapeDtypeStruct((t, f), BF16), compiler_params=_params("parallel"),
                          name="ffn_glu_fwd")(hg, hv, wg, wv, bg, bv)


def ffn_glu_bwd(hg, hv, wg, wv, bg, bv, dact):
    t, f = hg.shape
    tc = _pick(f, LANE)
    col = pl.BlockSpec((t, tc), lambda j: (0, j))
    tap = pl.BlockSpec((FFN_CONV, tc), lambda j: (0, j))
    one = pl.BlockSpec((1, tc), lambda j: (0, j))

    def body(hg_ref, hv_ref, wg_ref, wv_ref, bg_ref, bv_ref, da_ref, dhg_ref, dhv_ref, dwg_ref, dwv_ref, dbg_ref, dbv_ref):
        xg, xv, da = hg_ref[...], hv_ref[...], da_ref[...]
        gate = _conv(xg, wg_ref, FFN_CONV) + bg_ref[...]
        val = _conv(xv, wv_ref, FFN_CONV) + bv_ref[...]
        dgate = da * val * _dsilu(gate)
        dval = da * _silu(gate)
        dbg_ref[...] = jnp.sum(dgate, axis=0, keepdims=True)
        dbv_ref[...] = jnp.sum(dval, axis=0, keepdims=True)
        dhg_ref[...] = _conv_bwd(xg, dgate, wg_ref, dwg_ref, FFN_CONV).astype(dhg_ref.dtype)
        dhv_ref[...] = _conv_bwd(xv, dval, wv_ref, dwv_ref, FFN_CONV).astype(dhv_ref.dtype)

    return pl.pallas_call(
        body, grid=(f // tc,), in_specs=[col, col, tap, tap, one, one, col], out_specs=[col, col, tap, tap, one, one],
        out_shape=[jax.ShapeDtypeStruct((t, f), BF16), jax.ShapeDtypeStruct((t, f), BF16),
                   jax.ShapeDtypeStruct((FFN_CONV, f), F32), jax.ShapeDtypeStruct((FFN_CONV, f), F32),
                   jax.ShapeDtypeStruct((1, f), F32), jax.ShapeDtypeStruct((1, f), F32)],
        compiler_params=_params("parallel"), name="ffn_glu_bwd")(hg, hv, wg, wv, bg, bv, dact)


def _adamw(g, w, m, v):
    m = ADAM_B1 * m + (1.0 - ADAM_B1) * g
    v = ADAM_B2 * v + (1.0 - ADAM_B2) * (g * g)
    m_hat = m / (1.0 - ADAM_B1 ** ADAM_STEP)
    v_hat = v / (1.0 - ADAM_B2 ** ADAM_STEP)
    return -ADAM_LR * (m_hat / (jnp.sqrt(v_hat) + ADAM_EPS) + ADAM_WD * w), m, v


def adamw(g, w, m, v, name):
    r, c = g.shape
    tr = r
    for cand in (256, 128, 64, 32, 16, 8):
        if r % cand == 0 and cand * c * 4 <= (1 << 21):
            tr = cand
            break
    return ew(_adamw, [(g, "r"), (w, "r"), (m, "r"), (v, "r")], [((r, c), F32, "r")] * 3, gr=r // tr, name=name)


ANY = pl.BlockSpec(memory_space=pl.ANY)


def _place():
    x, y, c = lax.axis_index("x"), lax.axis_index("y"), lax.axis_index("c")
    return x, y, c, [(1 - x, y), (x, 1 - y), (1 - x, 1 - y)]


def _rcopy(src, dst, send, recv, k, to):
    return pltpu.make_async_remote_copy(src_ref=src, dst_ref=dst, send_sem=send.at[k], recv_sem=recv.at[k],
                                        device_id=to, device_id_type=MESH)


def gather_shards(bigs, smalls):
    nb, na = len(bigs), len(bigs) + len(smalls)
    arrays = list(bigs) + list(smalls)

    def body(*refs):
        ins, outs = refs[:na], refs[na:2 * na]
        send, recv = refs[2 * na:]
        x, y, c, chips = _place()
        me, sib = 2 * x + y, (x, y, 1 - c)

        def half(ref, a, which):
            rows = arrays[a].shape[0] // 2
            return ref.at[pl.ds(which * rows, rows)]

        started = []
        for a in range(na):
            for j, (cx, cy) in enumerate(chips):
                if a < nb:
                    cp = _rcopy(half(ins[a], a, c), half(outs[a].at[me], a, c), send, recv, 6 * a + j, (cx, cy, c))
                else:
                    cp = _rcopy(ins[a], outs[a].at[me], send, recv, 6 * nb + 3 * (a - nb) + j, (cx, cy, c))
                cp.start()
                started.append(cp)
        for a in range(nb):
            for j, (cx, cy) in enumerate(chips):
                landed = half(outs[a].at[2 * cx + cy], a, c)
                _rcopy(landed, landed, send, recv, 6 * a + j, (cx, cy, c)).wait_recv()
                cp = _rcopy(landed, landed, send, recv, 6 * a + 3 + j, sib)
                cp.start()
                started.append(cp)
        for a in range(na):
            for j, (cx, cy) in enumerate(chips):
                if a < nb:
                    dst = half(outs[a].at[2 * cx + cy], a, 1 - c)
                    _rcopy(dst, dst, send, recv, 6 * a + 3 + j, sib).wait_recv()
                else:
                    dst = outs[a].at[2 * cx + cy]
                    _rcopy(dst, dst, send, recv, 6 * nb + 3 * (a - nb) + j, (cx, cy, c)).wait_recv()
        for cp in started:
            cp.wait_send()

    n_sem = 6 * nb + 3 * (na - nb)
    gathered = pl.pallas_call(
        body, in_specs=[ANY] * na, out_specs=[ANY] * na,
        out_shape=[jax.ShapeDtypeStruct((N_CHIPS,) + a.shape, a.dtype) for a in arrays],
        scratch_shapes=[pltpu.SemaphoreType.DMA((n_sem,)), pltpu.SemaphoreType.DMA((n_sem,))],
        name="gather_shards")(*arrays)
    chip = 2 * lax.axis_index("x") + lax.axis_index("y")
    return [lax.dynamic_update_slice(g, a[None], (chip, 0, 0)) for g, a in zip(gathered, arrays)]


def sibling_halves(grads):
    na = len(grads)

    def body(*refs):
        ins, outs = refs[:na], refs[na:2 * na]
        send, recv = refs[2 * na:]
        x, y, c, _ = _place()
        cps = []
        for a in range(na):
            rows = grads[a].shape[1] // 2
            cp = _rcopy(ins[a].at[:, pl.ds((1 - c) * rows, rows)], outs[a], send, recv, a, (x, y, 1 - c))
            cp.start()
            cps.append(cp)
        for cp in cps:
            cp.wait()

    return pl.pallas_call(
        body, in_specs=[ANY] * na, out_specs=[ANY] * na,
        out_shape=[jax.ShapeDtypeStruct((g.shape[0], g.shape[1] // 2, g.shape[2]), g.dtype) for g in grads],
        scratch_shapes=[pltpu.SemaphoreType.DMA((na,)), pltpu.SemaphoreType.DMA((na,))], name="sibling_halves")(*grads)


def join_halves(halves):
    na = len(halves)

    def body(*refs):
        ins, outs = refs[:na], refs[na:2 * na]
        send, recv = refs[2 * na:]
        x, y, c, _ = _place()
        cps = []
        for a in range(na):
            cp = _rcopy(ins[a].at[c], outs[a].at[c], send, recv, a, (x, y, 1 - c))
            cp.start()
            cps.append(cp)
        for a in range(na):
            dst = outs[a].at[1 - c]
            _rcopy(dst, dst, send, recv, a, (x, y, 1 - c)).wait_recv()
        for cp in cps:
            cp.wait_send()

    return pl.pallas_call(
        body, in_specs=[ANY] * na, out_specs=[ANY] * na,
        out_shape=[jax.ShapeDtypeStruct(h.shape, h.dtype) for h in halves],
        input_output_aliases={a: a for a in range(na)},
        scratch_shapes=[pltpu.SemaphoreType.DMA((na,)), pltpu.SemaphoreType.DMA((na,))],
        name="join_halves")(*halves)


HBM = pl.BlockSpec(memory_space=pltpu.HBM)
SEM = pl.BlockSpec(memory_space=pltpu.SEMAPHORE)
SIDE_EFFECT = pltpu.SideEffectType.DATAFLOW_SIDE_EFFECTING


def split_start(name, srcs, land_shapes, n_copies, copies_fn, after):
    ns, nl = len(srcs), len(land_shapes)
    afters = tuple(after) if isinstance(after, (tuple, list)) else (after,)

    def body(*refs):
        ins, lands = refs[:ns], refs[ns:ns + nl]
        send, recv, token = refs[ns + nl + len(afters)], refs[ns + nl + len(afters) + 1], refs[-1]
        for k, (src, dst, dev) in enumerate(copies_fn(ins, lands, False)):
            pltpu.make_async_remote_copy(src_ref=src, dst_ref=dst, send_sem=send.at[k], recv_sem=recv.at[k],
                                         device_id=dev, device_id_type=MESH).start()
        token[...] = jnp.zeros_like(token)

    outs = pl.pallas_call(
        body, name=name,
        out_shape=(pltpu.SemaphoreType.DMA((n_copies,)), pltpu.SemaphoreType.DMA((n_copies,)),
                   *[pltpu.HBM(a.shape, a.dtype) for a in srcs], *[pltpu.HBM(s, dt) for s, dt in land_shapes],
                   jax.ShapeDtypeStruct((8, LANE), F32)),
        in_specs=[HBM] * (ns + nl) + [ANY] * len(afters),
        out_specs=(SEM, SEM, *[HBM] * (ns + nl), pl.BlockSpec(memory_space=pltpu.VMEM)),
        input_output_aliases={i: 2 + i for i in range(ns + nl)},
        compiler_params=pltpu.CompilerParams(has_side_effects=SIDE_EFFECT),
    )(*[pltpu.with_memory_space_constraint(a, pltpu.HBM) for a in srcs],
      *[pltpu.with_memory_space_constraint(lax.empty(s, dt), pltpu.HBM) for s, dt in land_shapes], *afters)
    return outs[0], outs[1], list(outs[2:2 + ns]), list(outs[2 + ns:2 + ns + nl]), outs[-1]


def split_wait(name, started, n_copies, copies_fn, after):
    send, recv, srcs, lands, _ = started
    ns, nl = len(srcs), len(lands)
    afters = tuple(after) if isinstance(after, (tuple, list)) else (after,)

    def body(*refs):
        ins, lnd = refs[:ns], refs[ns:ns + nl]
        send_ref, recv_ref = refs[ns + nl], refs[ns + nl + 1]
        for k, (src, dst, dev) in enumerate(copies_fn(ins, lnd, True)):
            cp = pltpu.make_async_remote_copy(src_ref=src, dst_ref=dst, send_sem=send_ref.at[k], recv_sem=recv_ref.at[k],
                                              device_id=dev, device_id_type=MESH)
            cp.wait_send()
            cp.wait_recv()

    outs = pl.pallas_call(
        body, name=name,
        out_shape=tuple(pltpu.HBM(a.shape, a.dtype) for a in srcs + lands),
        in_specs=[HBM] * (ns + nl) + [SEM, SEM] + [ANY] * len(afters), out_specs=tuple([HBM] * (ns + nl)),
        input_output_aliases={i: i for i in range(ns + nl)},
        compiler_params=pltpu.CompilerParams(has_side_effects=SIDE_EFFECT),
    )(*srcs, *lands, send, recv, *afters)
    return list(outs[:ns]), list(outs[ns:])


def _gather_copies(ins, lands, waiting):
    x, y, c, chips = _place()
    return [(ins[a], lands[a].at[2 * cx + cy] if waiting else lands[a].at[2 * x + y], (cx, cy, c))
            for a in range(len(ins)) for cx, cy in chips]


def _scatter_copies(ins, lands, waiting):
    del waiting
    _, _, c, chips = _place()
    return [(ins[a].at[2 * cx + cy], lands[a].at[j], (cx, cy, c)) for a in range(len(ins)) for j, (cx, cy) in enumerate(chips)]


def allreduce_small(vec):
    r = vec.shape[0]

    def body(x_ref, sum_ref, all_ref, send, recv):
        x, y, c, _ = _place()
        me = 4 * x + 2 * y + c
        all_ref[me] = x_ref[...]
        cps, peers = [], []
        for mask in range(1, 8):
            px = 1 - x if mask & 4 else x
            py = 1 - y if mask & 2 else y
            pc = 1 - c if mask & 1 else c
            peers.append(4 * px + 2 * py + pc)
            cp = _rcopy(x_ref, all_ref.at[me], send, recv, mask - 1, (px, py, pc))
            cp.start()
            cps.append(cp)
        for k, cp in enumerate(cps):
            _rcopy(x_ref, all_ref.at[peers[k]], send, recv, k, (x, y, c)).wait_recv()
        for cp in cps:
            cp.wait_send()
        total = all_ref[0]
        for d in range(1, 8):
            total = total + all_ref[d]
        sum_ref[...] = total

    vm = pl.BlockSpec(memory_space=pltpu.VMEM)
    return pl.pallas_call(
        body, in_specs=[vm], out_specs=vm, out_shape=jax.ShapeDtypeStruct((r, LANE), F32),
        scratch_shapes=[pltpu.VMEM((8, r, LANE), F32), pltpu.SemaphoreType.DMA((7,)), pltpu.SemaphoreType.DMA((7,))],
        compiler_params=pltpu.CompilerParams(vmem_limit_bytes=VMEM_LIMIT), name="allreduce_small")(vec)


def _row_tile(rows):
    for cand in (256, 128, 64, 32, 16):
        if rows % cand == 0:
            return cand
    return rows


def add_sibling(grad, recv, c_idx):
    _, rows, cols = grad.shape
    hr = rows // 2
    tr = _row_tile(hr)
    nb = hr // tr

    def body(c_ref, g_ref, r_ref, o_ref):
        o_ref[...] = (g_ref[...].astype(F32) + r_ref[...].astype(F32)).astype(o_ref.dtype)

    return pl.pallas_call(
        body,
        grid_spec=pltpu.PrefetchScalarGridSpec(
            num_scalar_prefetch=1, grid=(N_CHIPS, nb),
            in_specs=[pl.BlockSpec((None, tr, cols), lambda k, r, c_ref: (k, c_ref[0] * nb + r, 0)),
                      pl.BlockSpec((None, tr, cols), lambda k, r, c_ref: (k, r, 0))],
            out_specs=pl.BlockSpec((None, tr, cols), lambda k, r, c_ref: (k, r, 0))),
        out_shape=jax.ShapeDtypeStruct((N_CHIPS, hr, cols), BF16),
        compiler_params=_params("parallel", "parallel"), name="add_sibling")(c_idx, grad, recv)


def sum_chips(part, others, place_idx):
    _, hr, cols = part.shape
    tr = _row_tile(hr)

    def body(k_ref, p_ref, o0_ref, o1_ref, o2_ref, out_ref):
        out_ref[...] = ((p_ref[...].astype(F32) + o0_ref[...].astype(F32)) + o1_ref[...].astype(F32)) + o2_ref[...].astype(F32)

    other = lambda j: pl.BlockSpec((None, tr, cols), lambda r, k_ref: (j, r, 0))
    return pl.pallas_call(
        body,
        grid_spec=pltpu.PrefetchScalarGridSpec(
            num_scalar_prefetch=1, grid=(hr // tr,),
            in_specs=[pl.BlockSpec((None, tr, cols), lambda r, k_ref: (k_ref[0], r, 0)), other(0), other(1), other(2)],
            out_specs=pl.BlockSpec((None, tr, cols), lambda r, k_ref: (k_ref[1], r, 0))),
        out_shape=jax.ShapeDtypeStruct((2, hr, cols), F32),
        compiler_params=_params("parallel"), name="sum_chips")(place_idx, part, others, others, others)


def _pad_lanes(a, width=LANE):
    return jnp.pad(a, ((0, 0), (0, width - a.shape[1])))


def _cols_from_shards(g):
    return jnp.transpose(g, (1, 0, 2)).reshape(g.shape[1], -1)


def _cols_to_shards(w):
    k, n4 = w.shape
    return jnp.transpose(w.reshape(k, N_CHIPS, n4 // N_CHIPS), (1, 0, 2))


def kernel(x, mem, positions, g_mix, w_in, g_qa, w_qb, g_kva, w_kvb, g_qn_nope, g_qn_pe, g_kn_nope, g_kn_pe, conv_qk, b_if, g_hnorm, p_a, p_b, w_out, g_cross, g_mem, wq_c, wk_c, wv_c, g_cq, g_ck, wo_c, g_ffn, w_up, conv_ffn, b_conv_ffn, w_down, loss_target, m_g_mix, m_w_in, m_g_qa, m_w_qb, m_g_kva, m_w_kvb, m_g_qn_nope, m_g_qn_pe, m_g_kn_nope, m_g_kn_pe, m_conv_qk, m_b_if, m_g_hnorm, m_p_a, m_p_b, m_w_out, m_g_cross, m_g_mem, m_wq_c, m_wk_c, m_wv_c, m_g_cq, m_g_ck, m_wo_c, m_g_ffn, m_w_up, m_conv_ffn, m_b_conv_ffn, m_w_down, v_g_mix, v_w_in, v_g_qa, v_w_qb, v_g_kva, v_w_kvb, v_g_qn_nope, v_g_qn_pe, v_g_kn_nope, v_g_kn_pe, v_conv_qk, v_b_if, v_g_hnorm, v_p_a, v_p_b, v_w_out, v_g_cross, v_g_mem, v_wq_c, v_wk_c, v_wv_c, v_g_cq, v_g_ck, v_wo_c, v_g_ffn, v_w_up, v_conv_ffn, v_b_conv_ffn, v_w_down):
    names = ["g_mix", "w_in", "g_qa", "w_qb", "g_kva", "w_kvb", "g_qn_nope", "g_qn_pe", "g_kn_nope", "g_kn_pe", "conv_qk",
             "b_if", "g_hnorm", "p_a", "p_b", "w_out", "g_cross", "g_mem", "wq_c", "wk_c", "wv_c", "g_cq", "g_ck", "wo_c",
             "g_ffn", "w_up", "conv_ffn", "b_conv_ffn", "w_down"]
    env = locals()
    wts = {n: env[n] for n in names}
    mom = {n: env["m_" + n] for n in names}
    var = {n: env["v_" + n] for n in names}

    xi, yi, ci = lax.axis_index("x"), lax.axis_index("y"), lax.axis_index("c")
    chip = 2 * xi + yi
    place_arr = jnp.stack([chip, ci]).astype(jnp.int32)
    c_arr = jnp.reshape(ci, (1,)).astype(jnp.int32)

    x2d, tgt, mem2d = x[0], loss_target[0], mem[0]
    t, d = x2d.shape
    mla_h = w_qb.shape[2] * N_CHIPS // (NOPE + ROPE)
    ml_h = b_if.shape[1] // 2
    cr_h = wq_c.shape[2] // CROSS_DH
    f_dim = w_down.shape[1] * N_CHIPS
    q_rank, kv_rank = g_qa.shape[1], g_kva.shape[1]
    qk_w, v_w = ml_h * MLSTM_DK, ml_h * MLSTM_DV
    nc = t // CHUNK

    big_names = ["w_in", "w_qb", "w_kvb", "p_a", "p_b", "w_out", "wq_c", "wk_c", "wv_c", "wo_c", "w_up", "w_down"]
    col_sharded = {"w_in", "w_qb", "w_kvb", "wo_c", "w_up"}
    small_sharded = ["conv_qk", "g_hnorm", "conv_ffn"]
    early_big, early_small = ["w_in", "w_qb", "w_kvb"], ["conv_qk", "g_hnorm"]
    late_groups = [["p_a", "p_b", "w_out", "wq_c", "wk_c", "wv_c", "wo_c"], ["w_up", "w_down", "conv_ffn"]]
    full = {}

    def unshard(n, g):
        if n == "w_up":
            full[n] = g
        else:
            full[n] = _cols_from_shards(g) if (n in col_sharded or n in small_sharded) else g.reshape(-1, g.shape[2])

    gathered = gather_shards([wts[n][0].astype(BF16) for n in early_big], [wts[n][0] for n in early_small])
    for n, g in zip(early_big + early_small, gathered):
        unshard(n, g)
    late, order_after = [], gathered[0]
    for gi, group in enumerate(late_groups):
        src = [wts[n][0].astype(BF16) if n in big_names else wts[n][0] for n in group]
        late.append(split_start("gather_late%d_start" % gi, src, [((N_CHIPS,) + a.shape, a.dtype) for a in src],
                                3 * len(src), _gather_copies, order_after))
        order_after = late[-1][4]
    g_mix_fwd = g_mix + order_after[0:1, 0:1]

    def land_late(gi, after):
        group = late_groups[gi]
        own, landed = split_wait("gather_late%d_wait" % gi, late[gi], 3 * len(group), _gather_copies, after)
        for n, g, o in zip(group, landed, own):
            unshard(n, lax.dynamic_update_slice(g, o[None], (chip, 0, 0)))

    o_qa, o_kv, o_kpe = 0, q_rank, q_rank + kv_rank
    o_q = o_kpe + ROPE
    o_v = o_q + 2 * qk_w
    o_if = o_v + v_w
    o_o = o_if + 2 * ml_h
    o_ga, o_gb = o_o + v_w, o_o + v_w + d
    wi = full["w_in"]
    pad_kpe = jnp.zeros((d, LANE - ROPE), BF16)
    pad_if = jnp.zeros((d, LANE - 2 * ml_h), BF16)
    w_small = jnp.concatenate([wi[:, o_qa:o_q], pad_kpe, wi[:, o_if:o_o], pad_if], axis=1)
    o_kpe_s, o_if_s = o_kpe, o_kpe + LANE
    w_qk, w_v, w_o, w_ga, w_gb = wi[:, o_q:o_v], wi[:, o_v:o_if], wi[:, o_o:o_ga], wi[:, o_ga:o_gb], wi[:, o_gb:]

    wq3 = full["w_qb"].reshape(q_rank, mla_h, NOPE + ROPE)
    wq_nope = wq3[:, :, :NOPE].reshape(q_rank, mla_h * NOPE)
    wq_pe = jnp.pad(wq3[:, :, NOPE:], ((0, 0), (0, 0), (0, LANE - ROPE))).reshape(q_rank, mla_h * LANE)
    wkv3 = full["w_kvb"].reshape(kv_rank, mla_h, NOPE + VHEAD)
    wk_nope = wkv3[:, :, :NOPE].reshape(kv_rank, mla_h * NOPE)
    wv_mla = wkv3[:, :, NOPE:].reshape(kv_rank, mla_h * VHEAD)

    inv_freq = ROPE_BASE ** (-jnp.arange(0, ROPE, 2, dtype=F32) / ROPE)
    ang = positions[0].astype(F32)[:, None] * inv_freq
    cos, sin = jnp.cos(ang), jnp.sin(ang)
    zero_h = jnp.zeros_like(cos)
    tabs = [_pad_lanes(jnp.concatenate([cos, cos], axis=1)), _pad_lanes(-sin), _pad_lanes(jnp.concatenate([zero_h, sin], axis=1))]
    mla_gains = [g_qn_nope, _pad_lanes(g_qn_pe), g_kn_nope, _pad_lanes(g_kn_pe)]

    u1 = rms_fwd(x2d, g_mix_fwd, "rms_mix")
    z_small = mm(u1, w_small, name="in_small")
    z_qa, z_kv = z_small[:, o_qa:o_kv], z_small[:, o_kv:o_kpe]
    z_kpe, z_if = z_small[:, o_kpe_s:o_kpe_s + LANE], z_small[:, o_if_s:o_if_s + 2 * ml_h]
    z_qk = mm(u1, w_qk, name="in_qk")
    z_v = mm(u1, w_v, name="in_v")
    z_o = mm(u1, w_o, name="in_o")
    z_ga = mm(u1, w_ga, name="in_ga")
    z_gb = mm(u1, w_gb, name="in_gb")

    qa_n = rms_fwd(z_qa, g_qa, "rms_qa")
    kv_n = rms_fwd(z_kv, g_kva, "rms_kva")
    qn_raw = mm(qa_n, wq_nope, name="q_nope")
    qp_raw = mm(qa_n, wq_pe, name="q_pe")
    kn_raw = mm(kv_n, wk_nope, name="k_nope")
    v_mla = mm(kv_n, wv_mla, out_dtype=BF16, name="v_mla")
    q_att, k_att = mla_prep_fwd(qn_raw, qp_raw, kn_raw, z_kpe, tabs, mla_gains, mla_h)
    y_a, lse_row = attn_fwd(q_att, k_att, jnp.transpose(v_mla), mla_h)

    colscale = jnp.concatenate([jnp.full((1, qk_w), MLSTM_DK ** -0.5, F32), jnp.ones((1, qk_w), F32)], axis=1)
    qk_c = conv_qk_fwd(z_qk, full["conv_qk"], colscale)
    gates4 = z_if.reshape(nc, CHUNK, 2, ml_h)
    gcol = jnp.transpose(gates4, (3, 0, 1, 2))
    grow = jnp.transpose(gates4, (3, 0, 2, 1))
    bias = jnp.transpose(b_if.reshape(2, ml_h), (1, 0)).reshape(ml_h, 1, 2)
    h_raw, c_all, n_all, m_all = mlstm_fwd(qk_c, z_v, gcol, grow, bias, ml_h)
    g_hn = full["g_hnorm"].reshape(1, v_w)
    hn_gr, hd_gr = t // _pick(t, ROW_TILE), t // _pick(t, HEAD_ROW_TILE)
    y_b = ew(lambda *a: (_hnorm_gate(*a),), [(h_raw, "rc"), (z_o, "rc"), (g_hn, "c")], [((t, v_w), BF16, "rc")], gr=hd_gr, gc=ml_h,
             name="hnorm_gate")[0]

    land_late(0, y_b)

    pa = mm(y_a, full["p_a"], name="proj_a")
    pb = mm(y_b, full["p_b"], name="proj_b")
    merge_fn = lambda ga, gb, a, b: (_sigmoid(ga) * a + _sigmoid(gb) * b,)
    merged = ew(merge_fn, [(z_ga, "r"), (z_gb, "r"), (pa, "r"), (pb, "r")], [((t, d), BF16, "r")], gr=hn_gr, name="merge")[0]
    x1 = mm(merged, full["w_out"], add=x2d, name="out_proj")

    uc = rms_fwd(x1, g_cross, "rms_cross")
    mem_n = rms_fwd(mem2d, g_mem, "rms_mem")
    qc = mm(uc, full["wq_c"], name="cross_q")
    kc = mm(mem_n, full["wk_c"], name="cross_k")
    vc = mm(mem_n, full["wv_c"], name="cross_v")
    oc = cross_fwd(qc, kc, vc, g_cq, g_ck, cr_h)
    x2 = mm(oc, full["wo_c"], add=x1, name="cross_out")

    land_late(1, x2)
    half = N_CHIPS // 2
    u3 = rms_fwd(x2, g_ffn, "rms_ffn")
    hg = mm(u3, full["w_up"], b_shards=(0, half), name="ffn_up_gate")
    hv = mm(u3, full["w_up"], b_shards=(half, half), name="ffn_up_val")
    cw, cb = full["conv_ffn"], b_conv_ffn
    act = ffn_glu_fwd(hg, hv, cw[:, :f_dim], cw[:, f_dim:], cb[:, :f_dim], cb[:, f_dim:])
    y = mm(act, full["w_down"], add=x2, name="ffn_down")

    def loss_fn(y_, t_):
        err = y_ - t_
        part = jnp.sum(jnp.sum(err * err, axis=1, keepdims=True), axis=0, keepdims=True) * (0.5 / d)
        return err * (1.0 / d), err * (1.0 / d), jnp.broadcast_to(part, (1, LANE))

    dy, dy_mx, loss_part = ew(loss_fn, [(y, "r"), (tgt, "r")], [((t, d), F32, "r"), ((t, d), BF16, "r"), ((1, LANE), F32, "f")],
                              gr=hn_gr, name="loss")

    gw = {}
    gw["w_down"] = mm(act, dy_mx, ta=True, out_dtype=BF16, name="dw_down")
    dact = mm(dy_mx, full["w_down"], tb=True, name="d_act")
    dhg, dhv, dcw_g, dcw_v, dcb_g, dcb_v = ffn_glu_bwd(hg, hv, cw[:, :f_dim], cw[:, f_dim:], cb[:, :f_dim], cb[:, f_dim:], dact)
    gw["conv_ffn"] = jnp.concatenate([dcw_g, dcw_v], axis=1)
    gw["b_conv_ffn"] = jnp.concatenate([dcb_g, dcb_v], axis=1)
    dwup_g = mm(u3, dhg, ta=True, out_dtype=BF16, out_shards=half, name="dw_up_gate")
    dwup_v = mm(u3, dhv, ta=True, out_dtype=BF16, out_shards=half, name="dw_up_val")
    gw["w_up"] = jnp.concatenate([dwup_g, dwup_v], axis=0)

    def shard_major(n):
        if n == "w_up":
            return gw[n]
        return _cols_to_shards(gw[n]) if n in col_sharded else gw[n].reshape(N_CHIPS, -1, gw[n].shape[1])

    def chip_partials(group):
        grads_sm = [shard_major(n) for n in group]
        return [add_sibling(g, r, c_arr) for g, r in zip(grads_sm, sibling_halves(grads_sm))]

    def scatter_group(group, tag, after):
        parts_ = chip_partials(group)
        return split_start("scatter_start_" + tag, parts_, [((3,) + p.shape[1:], p.dtype) for p in parts_],
                           3 * len(parts_), _scatter_copies, after)

    group_a = ["w_up", "w_down"]
    started_a = scatter_group(group_a, "a", gw["w_up"])
    g_ffn_bwd = g_ffn + started_a[4][0:1, 0:1]
    du3 = mm(dhg, full["w_up"], tb=True, b_shards=(0, half), name="d_u3_gate")
    du3 = mm(dhv, full["w_up"], tb=True, b_shards=(half, half), add=du3, name="d_u3_val")
    dx2, gw["g_ffn"] = rms_bwd(x2, g_ffn_bwd, du3, dy, "rms_ffn_bwd")

    gw["wo_c"] = mm(oc, dx2, ta=True, out_dtype=BF16, name="dw_cross_out")
    doc = mm(dx2, full["wo_c"], tb=True, name="d_cross_o")
    dqc, dkc, dvc, gw["g_cq"], gw["g_ck"] = cross_bwd(qc, kc, vc, g_cq, g_ck, doc, cr_h)
    gw["wq_c"] = mm(uc, dqc, ta=True, out_dtype=BF16, name="dw_cross_q")
    gw["wk_c"] = mm(mem_n, dkc, ta=True, out_dtype=BF16, name="dw_cross_k")
    gw["wv_c"] = mm(mem_n, dvc, ta=True, out_dtype=BF16, name="dw_cross_v")
    duc = mm(dqc, full["wq_c"], tb=True, name="d_uc")
    dmem_n = mm(dkc, full["wk_c"], tb=True, name="d_mem_k")
    dmem_n = mm(dvc, full["wv_c"], tb=True, add=dmem_n, name="d_mem_v")
    _, gw["g_mem"] = rms_bwd(mem2d, g_mem, dmem_n, None, "rms_mem_bwd")
    dx1, gw["g_cross"] = rms_bwd(x1, g_cross, duc, dx2, "rms_cross_bwd")

    gw["w_out"] = mm(merged, dx1, ta=True, out_dtype=BF16, name="dw_out")
    dmerged = mm(dx1, full["w_out"], tb=True, name="d_merged")

    def merge_bwd(ga, gb, a, b, dm):
        _, pull = jax.vjp(lambda *args: merge_fn(*args)[0], ga, gb, a, b)
        return pull(dm)

    dz_ga, dz_gb, dpa, dpb = ew(merge_bwd, [(z_ga, "r"), (z_gb, "r"), (pa, "r"), (pb, "r"), (dmerged, "r")],
                                [((t, d), BF16, "r")] * 4, gr=hn_gr, name="merge_bwd")
    gw["p_a"] = mm(y_a, dpa, ta=True, out_dtype=BF16, name="dw_proj_a")
    gw["p_b"] = mm(y_b, dpb, ta=True, out_dtype=BF16, name="dw_proj_b")
    group_b = ["wo_c", "wq_c", "wk_c", "wv_c", "w_out", "p_a", "p_b"]
    started_b = scatter_group(group_b, "b", gw["p_b"])
    g_hn_bwd = g_hn + started_b[4][0:1, 0:1]
    dy_a = mm(dpa, full["p_a"], tb=True, name="d_ya")
    dy_b = mm(dpb, full["p_b"], tb=True, name="d_yb")

    def hnorm_bwd(h_, zo_, g_, dyb_):
        _, pull = jax.vjp(_hnorm_gate, h_, zo_, g_)
        return pull(dyb_)

    dh_raw, dz_o, dg_hn = ew(hnorm_bwd, [(h_raw, "rc"), (z_o, "rc"), (g_hn_bwd, "c"), (dy_b, "rc")],
                             [((t, v_w), F32, "rc"), ((t, v_w), BF16, "rc"), ((1, v_w), F32, "c")],
                             gr=hd_gr, gc=ml_h, order="cr", name="hnorm_gate_bwd")
    gw["g_hnorm"] = dg_hn.reshape(ml_h, MLSTM_DV)
    dq_m, dk_m, dz_v, dgcol, dgrow = mlstm_bwd(qk_c, z_v, gcol, grow, bias, c_all, n_all, m_all, dh_raw, ml_h)
    dgates4 = jnp.transpose(dgcol, (1, 2, 3, 0)) + jnp.transpose(dgrow, (1, 3, 2, 0))
    dz_if = dgates4.reshape(t, 2 * ml_h)
    gw["b_if"] = ew(lambda a: (jnp.sum(a, axis=0, keepdims=True),), [(dz_if, "r")], [((1, 2 * ml_h), F32, "f")],
                    gr=hn_gr, name="bias_if_bwd")[0]
    dz_qk, gw["conv_qk"] = conv_qk_bwd(z_qk, full["conv_qk"], colscale, dq_m, dk_m)

    dq_att, dk_att, dv_mla = attn_bwd(q_att, k_att, v_mla, y_a, dy_a, lse_row, mla_h)
    dqn_raw, dqp_raw, dkn_raw, dz_kpe, gw["g_qn_nope"], dg_qp, gw["g_kn_nope"], dg_kp = mla_prep_bwd(
        qn_raw, qp_raw, kn_raw, z_kpe, tabs, mla_gains, dq_att, dk_att, mla_h)
    gw["g_qn_pe"], gw["g_kn_pe"] = dg_qp[:, :ROPE], dg_kp[:, :ROPE]
    dwq_nope = mm(qa_n, dqn_raw, ta=True, out_dtype=BF16, name="dw_q_nope")
    dwq_pe = mm(qa_n, dqp_raw, ta=True, out_dtype=BF16, name="dw_q_pe")
    dwk_nope = mm(kv_n, dkn_raw, ta=True, out_dtype=BF16, name="dw_k_nope")
    dwv_mla = mm(kv_n, dv_mla, ta=True, out_dtype=BF16, name="dw_v_mla")
    dqa_n = mm(dqn_raw, wq_nope, tb=True, name="d_qa_nope")
    dqa_n = mm(dqp_raw, wq_pe, tb=True, add=dqa_n, name="d_qa_pe")
    dkv_n = mm(dkn_raw, wk_nope, tb=True, name="d_kv_nope")
    dkv_n = mm(dv_mla, wv_mla, tb=True, add=dkv_n, name="d_kv_v")
    dz_qa, gw["g_qa"] = rms_bwd(z_qa, g_qa, dqa_n, None, "rms_qa_bwd", BF16)
    dz_kv, gw["g_kva"] = rms_bwd(z_kv, g_kva, dkv_n, None, "rms_kva_bwd", BF16)
    gw["w_qb"] = jnp.concatenate([dwq_nope.reshape(q_rank, mla_h, NOPE), dwq_pe.reshape(q_rank, mla_h, LANE)[:, :, :ROPE]],
                                 axis=2).reshape(q_rank, -1)
    gw["w_kvb"] = jnp.concatenate([dwk_nope.reshape(kv_rank, mla_h, NOPE), dwv_mla.reshape(kv_rank, mla_h, VHEAD)],
                                  axis=2).reshape(kv_rank, -1)

    dz_small = jnp.concatenate([dz_qa, dz_kv, dz_kpe.astype(BF16), _pad_lanes(dz_if).astype(BF16)], axis=1)
    dw_small = mm(u1, dz_small, ta=True, out_dtype=BF16, name="dw_in_small")
    du1 = mm(dz_small, w_small, tb=True, name="d_u1_small")
    dw_segs = []
    for nm, dz, w_seg in (("qk", dz_qk, w_qk), ("v", dz_v, w_v), ("o", dz_o, w_o), ("ga", dz_ga, w_ga), ("gb", dz_gb, w_gb)):
        dw_segs.append(mm(u1, dz, ta=True, out_dtype=BF16, name="dw_in_" + nm))
        du1 = mm(dz, w_seg, tb=True, add=du1, name="d_u1_" + nm)
    gw["w_in"] = jnp.concatenate([dw_small[:, :o_kpe_s + ROPE], dw_segs[0], dw_segs[1],
                                  dw_small[:, o_if_s:o_if_s + 2 * ml_h], dw_segs[2], dw_segs[3], dw_segs[4]], axis=1)
    grad_x, gw["g_mix"] = rms_bwd(x2d, g_mix, du1, dx1, "rms_mix_bwd")

    group_c = ["w_in", "w_qb", "w_kvb"]
    started_c = scatter_group(group_c, "c", grad_x)
    parts_a, others_a = split_wait("scatter_wait_a", started_a, 3 * len(group_a), _scatter_copies, grad_x)
    parts_b, others_b = split_wait("scatter_wait_b", started_b, 3 * len(group_b), _scatter_copies, grad_x)
    place_ab = place_arr + started_c[4][0, 0].astype(jnp.int32)
    grads, deltas, new_m, new_v = {}, {}, {}, {}

    def finish(group, parts_, others_, place):
        joined = join_halves([sum_chips(p, o, place) for p, o in zip(parts_, others_)])
        for n, j in zip(group, joined):
            grads[n] = j.reshape(-1, j.shape[2])
            deltas[n], new_m[n], new_v[n] = adamw(grads[n], wts[n][0], mom[n][0], var[n][0], "adamw_" + n)

    finish(group_a + group_b, parts_a + parts_b, others_a + others_b, place_ab)

    small_names = [n for n in names if n not in big_names]
    pieces = [loss_part]
    for n in small_names:
        flat = gw[n].reshape(1, -1)
        pieces.append(jnp.pad(flat, ((0, 0), (0, (-flat.shape[1]) % LANE))))
    packed = jnp.concatenate(pieces, axis=1)
    packed = jnp.pad(packed, ((0, 0), (0, (-packed.shape[1]) % (8 * LANE)))).reshape(-1, LANE)
    total = allreduce_small(packed).reshape(1, -1)
    loss = total[0, 0]
    small_grads, off = {}, LANE
    for n in small_names:
        size = gw[n].size
        g_full = total[:, off:off + size].reshape(gw[n].shape)
        off += size + (-size) % LANE
        if n in small_sharded:
            width = wts[n].shape[-1]
            g_full = lax.dynamic_slice_in_dim(g_full, chip * width, width, axis=g_full.ndim - 1)
        small_grads[n] = g_full.reshape(wts[n].shape[1:])

    def pack_small(tree):
        flat = jnp.concatenate([tree[n].reshape(1, -1) for n in small_names], axis=1)
        return jnp.pad(flat, ((0, 0), (0, (-flat.shape[1]) % (8 * LANE)))).reshape(8, -1)

    sg = pack_small(small_grads)
    sd, sm, sv = adamw(sg, pack_small({n: wts[n][0] for n in small_names}), pack_small({n: mom[n][0] for n in small_names}),
                       pack_small({n: var[n][0] for n in small_names}), "adamw_small")
    off = 0
    for n in small_names:
        size = small_grads[n].size
        shp = wts[n].shape[1:]
        grads[n] = small_grads[n]
        for dst, src in ((deltas, sd), (new_m, sm), (new_v, sv)):
            dst[n] = src.reshape(1, -1)[:, off:off + size].reshape(shp)
        off += size

    parts_c, others_c = split_wait("scatter_wait_c", started_c, 3 * len(group_c), _scatter_copies,
                                   (sv, new_v[group_b[-1]], new_v[group_a[0]]))
    finish(group_c, parts_c, others_c, place_arr)

    def out(tree):
        return [tree[n].reshape(wts[n].shape) for n in names]

    return (loss, grad_x.reshape(x.shape), *out(grads), *out(deltas), *out(new_m), *out(new_v))
```

```python
import functools
import math

import jax
import jax.numpy as jnp
from jax import lax
from jax.experimental import pallas as pl
from jax.experimental.pallas import tpu as pltpu

F32, BF16 = jnp.float32, jnp.bfloat16
MESH = pl.DeviceIdType.MESH

EPS = 1e-6
CHUNK = 64
LOG2_CHUNK = 6
NOPE, ROPE, VHEAD = 128, 64, 128
MLSTM_DK, MLSTM_DV, MLSTM_CONV = 128, 256, 4
MLSTM_HEADS_PER_STEP = 8
CROSS_DH = 128
FFN_CONV = 3
ROPE_BASE = 10000.0
LOG2_E = math.log2(math.e)
ADAM_LR, ADAM_B1, ADAM_B2, ADAM_EPS, ADAM_WD, ADAM_STEP = 0.001, 0.9, 0.999, 1e-08, 0.01, 10

LANE = 128
ROW_TILE = 256
HEAD_ROW_TILE = 1024
ATT_TILE = 1024
ATT_SUB = 1024
MM_TILES = (1024, 1024, 2048)
MM_SHARD_TILE = 1536
VMEM_LIMIT = 56 * 1024 * 1024
N_CHIPS = 4

NN = ((1,), (0,))
NT = ((1,), (1,))
TN = ((0,), (0,))


def _pick(dim, pref):
    if dim <= pref:
        return dim
    for t in range(pref, 0, -LANE):
        if dim % t == 0:
            return t
    return dim


def _bdot(a, b, dims):
    return lax.dot_general(a.astype(BF16), b.astype(BF16), (dims, ((), ())), preferred_element_type=F32)


@jax.custom_vjp
def _dnn(a, b):
    return _bdot(a, b, NN)


_dnn.defvjp(lambda a, b: (_bdot(a, b, NN), (a, b)),
            lambda r, g: (_bdot(g, r[1], NT), _bdot(r[0], g, TN)))


@jax.custom_vjp
def _dnt(a, b):
    return _bdot(a, b, NT)


_dnt.defvjp(lambda a, b: (_bdot(a, b, NT), (a, b)),
            lambda r, g: (_bdot(g, r[1], NN), _bdot(g, r[0], TN)))


@jax.custom_vjp
def _dtn(a, b):
    return _bdot(a, b, TN)


_dtn.defvjp(lambda a, b: (_bdot(a, b, TN), (a, b)),
            lambda r, g: (_bdot(r[1], g, NT), _bdot(r[0], g, NN)))


@functools.partial(jax.custom_vjp, nondiff_argnums=(1,))
def _lane_roll(x, shift):
    return pltpu.roll(x, shift, 1)


_lane_roll.defvjp(lambda x, shift: (pltpu.roll(x, shift, 1), None),
                  lambda shift, _, g: (pltpu.roll(g, (LANE - shift) % LANE, 1),))


def _params(*sem):
    return pltpu.CompilerParams(dimension_semantics=sem, vmem_limit_bytes=VMEM_LIMIT)


def mm(a, b, *, ta=False, tb=False, add=None, out_dtype=F32, name, b_shards=None, out_shards=None):
    m_dim, k_dim = (a.shape[1], a.shape[0]) if ta else a.shape
    if b_shards is None:
        n_dim = b.shape[0] if tb else b.shape[1]
        assert k_dim == (b.shape[1] if tb else b.shape[0]), (name, a.shape, b.shape)
    else:
        n_dim = b.shape[1] if tb else b_shards[1] * b.shape[2]
        assert k_dim == (b_shards[1] * b.shape[2] if tb else b.shape[1]), (name, a.shape, b.shape)
    tm, tn, tk = _pick(m_dim, MM_TILES[0]), _pick(n_dim, MM_TILES[1]), _pick(k_dim, MM_TILES[2])
    if ta and out_shards is None and a.dtype == BF16 and b.dtype == BF16:
        tk = _pick(k_dim, 2 * MM_TILES[2])
    if b_shards is not None and tb:
        tk = _pick(b.shape[2], MM_SHARD_TILE)
    if (b_shards is not None and not tb) or out_shards is not None:
        tn = _pick(n_dim // (out_shards or b_shards[1]), MM_SHARD_TILE)
    nk = k_dim // tk
    dims = ((0,) if ta else (1,), (1,) if tb else (0,))
    has_add = add is not None

    def body(*refs):
        a_ref, b_ref = refs[0], refs[1]
        c_ref = refs[2] if has_add else None
        o_ref = refs[3] if has_add else refs[2]
        prod = _bdot(a_ref[...], b_ref[...], dims)
        if nk == 1:
            o_ref[...] = (prod + c_ref[...].astype(F32) if has_add else prod).astype(o_ref.dtype)
            return
        acc = refs[-1]
        k = pl.program_id(2)

        @pl.when(k == 0)
        def _():
            acc[...] = prod + c_ref[...].astype(F32) if has_add else prod

        @pl.when(k > 0)
        def _():
            acc[...] += prod

        @pl.when(k == nk - 1)
        def _():
            o_ref[...] = acc[...].astype(o_ref.dtype)

    if b_shards is None:
        b_spec = pl.BlockSpec((tn, tk), lambda i, j, k: (j, k)) if tb else pl.BlockSpec((tk, tn), lambda i, j, k: (k, j))
    elif tb:
        per = b.shape[2] // tk
        b_spec = pl.BlockSpec((None, tn, tk), lambda i, j, k: (b_shards[0] + k // per, j, k % per))
    else:
        per = b.shape[2] // tn
        b_spec = pl.BlockSpec((None, tk, tn), lambda i, j, k: (b_shards[0] + j // per, k, j % per))
    in_specs = [pl.BlockSpec((tk, tm), lambda i, j, k: (k, i)) if ta else pl.BlockSpec((tm, tk), lambda i, j, k: (i, k)), b_spec]
    if out_shards is None:
        out_spec, out_shape = pl.BlockSpec((tm, tn), lambda i, j, k: (i, j)), (m_dim, n_dim)
    else:
        per_o = n_dim // out_shards // tn
        out_spec = pl.BlockSpec((None, tm, tn), lambda i, j, k: (j // per_o, i, j % per_o))
        out_shape = (out_shards, m_dim, n_dim // out_shards)
    args = [a, b]
    if has_add:
        in_specs.append(pl.BlockSpec((tm, tn), lambda i, j, k: (i, j)))
        args.append(add)
    return pl.pallas_call(
        body, grid=(m_dim // tm, n_dim // tn, nk), in_specs=in_specs, out_specs=out_spec,
        out_shape=jax.ShapeDtypeStruct(out_shape, out_dtype),
        scratch_shapes=[pltpu.VMEM((tm, tn), F32)] if nk > 1 else [],
        compiler_params=_params("parallel", "parallel", "arbitrary"), name=name)(*args)


def ew(fn, ins, outs, *, gr, gc=1, order="rc", name):
    n_in = len(ins)

    def block(shape, kind):
        r, c = shape
        return (r // gr if kind in ("rc", "r") else r, c // gc if kind in ("rc", "c") else c)

    def imap(kind):
        def f(p0, p1):
            i, j = (p0, p1) if order == "rc" else (p1, p0)
            return {"rc": (i, j), "r": (i, 0), "c": (0, j), "f": (0, 0)}[kind]
        return f

    def body(*refs):
        p0, p1 = pl.program_id(0), pl.program_id(1)
        i, j = (p0, p1) if order == "rc" else (p1, p0)
        vals = fn(*[r[...] for r in refs[:n_in]])
        for ref, val, (_, dtype, kind) in zip(refs[n_in:], vals, outs):
            first = {"rc": None, "r": (j == 0) if gc > 1 else None, "c": (i == 0) if gr > 1 else None,
                     "f": ((i == 0) & (j == 0)) if gr * gc > 1 else None}[kind]
            _store(ref, val.astype(dtype), first)

    grid = (gr, gc) if order == "rc" else (gc, gr)
    return pl.pallas_call(
        body, grid=grid,
        in_specs=[pl.BlockSpec(block(a.shape, k), imap(k)) for a, k in ins],
        out_specs=[pl.BlockSpec(block(s, k), imap(k)) for s, _, k in outs],
        out_shape=[jax.ShapeDtypeStruct(s, d) for s, d, _ in outs],
        compiler_params=_params("arbitrary", "arbitrary"), name=name)(*[a for a, _ in ins])


def _store(ref, val, first):
    if first is None:
        ref[...] = val
        return

    @pl.when(first)
    def _():
        ref[...] = val

    @pl.when(jnp.logical_not(first))
    def _():
        ref[...] += val


def _f32(*xs):
    return [x.astype(F32) for x in xs]


def _rms(x, g, n):
    ms = jnp.sum(x * x, axis=-1, keepdims=True) * (1.0 / n)
    return x * lax.rsqrt(ms + EPS) * g


def _sigmoid(x):
    return 1.0 / (1.0 + jnp.exp(-x))


def _silu(x):
    return x * _sigmoid(x)


def _log_sigmoid(x):
    return jnp.minimum(x, 0.0) - jnp.log(1.0 + jnp.exp(-jnp.abs(x)))


def rms_fwd(x, g, name, out_dtype=BF16):
    t, w = x.shape
    return ew(lambda x_, g_: (_rms(x_, g_, w),), [(x, "r"), (g, "f")], [((t, w), out_dtype, "r")],
              gr=t // _pick(t, ROW_TILE), name=name)[0]


def rms_bwd(x, g, du, res, name, out_dtype=F32):
    t, w = x.shape

    def fn(x_, g_, du_, *res_):
        _, pull = jax.vjp(lambda a, b: _rms(a, b, w), x_, g_)
        dx, dg = pull(du_.astype(F32))
        return (dx + res_[0] if res_ else dx), dg

    ins = [(x, "r"), (g, "f"), (du, "r")] + ([(res, "r")] if res is not None else [])
    return ew(fn, ins, [((t, w), out_dtype, "r"), ((1, w), F32, "f")], gr=t // _pick(t, ROW_TILE), name=name)


def _rope(x, cos_t, sin_lo, sin_hi):
    return x * cos_t + _lane_roll(x, LANE - ROPE // 2) * sin_lo + _lane_roll(x, ROPE // 2) * sin_hi


def _mla_prep(qn, qp, kn, kp, cos_t, sin_lo, sin_hi, g_qn, g_qp, g_kn, g_kp):
    q = jnp.concatenate([_rms(qn, g_qn, NOPE), _rope(_rms(qp, g_qp, ROPE), cos_t, sin_lo, sin_hi)], axis=1)
    k = jnp.concatenate([_rms(kn, g_kn, NOPE), _rope(_rms(kp, g_kp, ROPE), cos_t, sin_lo, sin_hi)], axis=1)
    return q, k


def mla_prep_fwd(qn, qp, kn, kp, tabs, gains, heads):
    t = qn.shape[0]
    ins = [(qn, "rc"), (qp, "rc"), (kn, "rc"), (kp, "r")] + [(a, "r") for a in tabs] + [(g, "f") for g in gains]
    return ew(lambda *a: _mla_prep(*_f32(*a)), ins,
              [((t, heads * 2 * LANE), BF16, "rc"), ((t, heads * 2 * LANE), BF16, "rc")],
              gr=t // _pick(t, HEAD_ROW_TILE), gc=heads, name="mla_prep_fwd")


def mla_prep_bwd(qn, qp, kn, kp, tabs, gains, dq, dk, heads):
    t = qn.shape[0]

    def fn(qn_, qp_, kn_, kp_, c_, s1_, s2_, g1, g2, g3, g4, dq_, dk_):
        _, pull = jax.vjp(lambda a, b, c, d, e, f, g, h: _mla_prep(a, b, c, d, c_, s1_, s2_, e, f, g, h),
                          qn_, qp_, kn_, kp_, g1, g2, g3, g4)
        return pull((dq_, dk_))

    ins = ([(qn, "rc"), (qp, "rc"), (kn, "rc"), (kp, "r")] + [(a, "r") for a in tabs] + [(g, "f") for g in gains]
           + [(dq, "rc"), (dk, "rc")])
    hw = heads * LANE
    outs = [((t, hw), BF16, "rc"), ((t, hw), BF16, "rc"), ((t, hw), BF16, "rc"), ((t, LANE), F32, "r")] \
        + [((1, LANE), F32, "f")] * 4
    return ew(fn, ins, outs, gr=t // _pick(t, HEAD_ROW_TILE), gc=heads, name="mla_prep_bwd")


def _chunk_mask(row0, col0, shape, rows_are_queries):
    r = jnp.right_shift(row0 + lax.broadcasted_iota(jnp.int32, shape, 0), LOG2_CHUNK)
    c = jnp.right_shift(col0 + lax.broadcasted_iota(jnp.int32, shape, 1), LOG2_CHUNK)
    return (c <= r) if rows_are_queries else (r <= c)


def _block_pairs(nq, queries_outer):
    if queries_outer:
        pairs = [(i, j) for i in range(nq) for j in range(i + 1)]
    else:
        pairs = [(i, j) for j in range(nq) for i in range(j, nq)]
    return jnp.asarray([p[0] for p in pairs], jnp.int32), jnp.asarray([p[1] for p in pairs], jnp.int32)


def attn_fwd(q, k, vt, heads):
    t = q.shape[0]
    tq = _pick(t, ATT_TILE)
    sub = _pick(tq, ATT_SUB)
    qi, kj = _block_pairs(t // tq, True)
    scale = (NOPE + ROPE) ** -0.5
    scale2 = scale * LOG2_E

    def body(qi_ref, kj_ref, q_ref, k_ref, vt_ref, o_ref, lse_ref, m_s, l_s, acc):
        p = pl.program_id(1)
        i, j = qi_ref[p], kj_ref[p]

        @pl.when(j == 0)
        def _():
            m_s[...] = jnp.full_like(m_s, -jnp.inf)
            l_s[...] = jnp.zeros_like(l_s)
            acc[...] = jnp.zeros_like(acc)

        def step(diagonal):
            for b in range(tq // sub):
                cols = pl.ds(b * sub, sub)
                st = _bdot(k_ref[...], q_ref[cols, :], NT)
                if diagonal:
                    st = jnp.where(_chunk_mask(0, b * sub, (tq, sub), False), st, -jnp.inf)
                m_old = m_s[:, cols]
                m_new = jnp.maximum(m_old, jnp.max(st, axis=0, keepdims=True))
                alpha = jnp.exp2((m_old - m_new) * scale2)
                pt = jnp.exp2((st - m_new) * scale2)
                l_s[:, cols] = alpha * l_s[:, cols] + jnp.sum(pt, axis=0, keepdims=True)
                acc[:, cols] = alpha * acc[:, cols] + _bdot(vt_ref[...], pt, NN)
                m_s[:, cols] = m_new

        pl.when(j < i)(functools.partial(step, False))

        @pl.when(j == i)
        def _():
            step(True)
            o_ref[...] = jnp.transpose(acc[...] / l_s[...])
            lse_ref[...] = m_s[...] * scale + jnp.log(l_s[...])

    return pl.pallas_call(
        body,
        grid_spec=pltpu.PrefetchScalarGridSpec(
            num_scalar_prefetch=2, grid=(heads, qi.shape[0]),
            in_specs=[pl.BlockSpec((tq, 2 * LANE), lambda h, p, qi_, kj_: (qi_[p], h)),
                      pl.BlockSpec((tq, 2 * LANE), lambda h, p, qi_, kj_: (kj_[p], h)),
                      pl.BlockSpec((VHEAD, tq), lambda h, p, qi_, kj_: (h, kj_[p]))],
            out_specs=[pl.BlockSpec((tq, VHEAD), lambda h, p, qi_, kj_: (qi_[p], h)),
                       pl.BlockSpec((None, 1, tq), lambda h, p, qi_, kj_: (h, 0, qi_[p]))],
            scratch_shapes=[pltpu.VMEM((1, tq), F32), pltpu.VMEM((1, tq), F32), pltpu.VMEM((VHEAD, tq), F32)]),
        out_shape=[jax.ShapeDtypeStruct((t, heads * VHEAD), F32), jax.ShapeDtypeStruct((heads, 1, t), F32)],
        compiler_params=_params("parallel", "arbitrary"), name="mla_attn_fwd")(qi, kj, q, k, vt)


def attn_bwd(q, k, v, o, do, lse_row, heads):
    t = q.shape[0]
    tq = _pick(t, ATT_TILE)
    sub = _pick(tq, ATT_SUB)
    qi, kj = _block_pairs(t // tq, False)
    scale = (NOPE + ROPE) ** -0.5
    scale2 = scale * LOG2_E

    def body(qi_ref, kj_ref, q_ref, k_ref, v_ref, o_ref, do_ref, lse_ref, dq_ref, dk_ref, dv_ref):
        p = pl.program_id(1)
        i, j = qi_ref[p], kj_ref[p]

        @pl.when(p == 0)
        def _():
            dq_ref[...] = jnp.zeros_like(dq_ref)

        @pl.when(i == j)
        def _():
            dk_ref[...] = jnp.zeros_like(dk_ref)
            dv_ref[...] = jnp.zeros_like(dv_ref)

        def step(diagonal):
            for b in range(tq // sub):
                cols = pl.ds(b * sub, sub)
                do_i = do_ref[cols, :]
                prod = do_i * o_ref[cols, :]
                hi = prod.astype(BF16)
                mid = (prod - hi.astype(F32)).astype(BF16)
                lo = (prod - hi.astype(F32) - mid.astype(F32)).astype(BF16)
                ones = jnp.ones((8, VHEAD), BF16)
                delta = (_bdot(ones, hi, NT) + _bdot(ones, mid, NT) + _bdot(ones, lo, NT))[0:1, :]
                q_b = q_ref[cols, :]
                st = _bdot(k_ref[...], q_b, NT)
                pt = jnp.exp2(st * scale2 - lse_ref[:, cols] * LOG2_E)
                if diagonal:
                    pt = jnp.where(_chunk_mask(0, b * sub, (tq, sub), False), pt, 0.0)
                dv_ref[...] += _bdot(pt, do_i, NN)
                dpt = _bdot(v_ref[...], do_i, NT)
                dst = pt * (dpt - delta) * scale
                dk_ref[...] += _bdot(dst, q_b, NN)
                rows = pl.ds(pl.multiple_of(i * tq + b * sub, sub), sub)
                dq_ref[rows, :] += _bdot(dst, k_ref[...], TN)

        pl.when(i > j)(functools.partial(step, False))
        pl.when(i == j)(functools.partial(step, True))

    qmap = lambda h, p, qi_, kj_: (qi_[p], h)
    kmap = lambda h, p, qi_, kj_: (kj_[p], h)
    return pl.pallas_call(
        body,
        grid_spec=pltpu.PrefetchScalarGridSpec(
            num_scalar_prefetch=2, grid=(heads, qi.shape[0]),
            in_specs=[pl.BlockSpec((tq, 2 * LANE), qmap), pl.BlockSpec((tq, 2 * LANE), kmap),
                      pl.BlockSpec((tq, VHEAD), kmap), pl.BlockSpec((tq, VHEAD), qmap), pl.BlockSpec((tq, VHEAD), qmap),
                      pl.BlockSpec((None, 1, tq), lambda h, p, qi_, kj_: (h, 0, qi_[p]))],
            out_specs=[pl.BlockSpec((t, 2 * LANE), lambda h, p, qi_, kj_: (0, h)),
                       pl.BlockSpec((tq, 2 * LANE), kmap), pl.BlockSpec((tq, VHEAD), kmap)]),
        out_shape=[jax.ShapeDtypeStruct((t, heads * 2 * LANE), F32), jax.ShapeDtypeStruct((t, heads * 2 * LANE), F32),
                   jax.ShapeDtypeStruct((t, heads * VHEAD), F32)],
        compiler_params=_params("parallel", "arbitrary"), name="mla_attn_bwd")(qi, kj, q, k, v, o, do, lse_row)


def _shift_down(x, s):
    if s == 0:
        return x
    rows = lax.broadcasted_iota(jnp.int32, x.shape, 0)
    return jnp.where(rows >= s, pltpu.roll(x, s, 0), 0.0)


def _shift_up(x, s):
    if s == 0:
        return x
    t = x.shape[0]
    rows = lax.broadcasted_iota(jnp.int32, x.shape, 0)
    return jnp.where(rows < t - s, pltpu.roll(x, t - s, 0), 0.0)


def _conv(x, w_ref, width):
    return sum(_shift_down(x, width - 1 - j) * w_ref[j:j + 1, :] for j in range(width))


def _conv_bwd(x, dpre, w_ref, dw_ref, width):
    dx = sum(_shift_up(dpre, width - 1 - j) * w_ref[j:j + 1, :] for j in range(width))
    for j in range(width):
        dw_ref[j:j + 1, :] = jnp.sum(dpre * _shift_down(x, width - 1 - j), axis=0, keepdims=True)
    return dx


def _dsilu(z):
    s = _sigmoid(z)
    return s * (1.0 + z * (1.0 - s))


def conv_qk_fwd(x, w, colscale):
    t, c = x.shape
    tc = _pick(c, 256)

    def body(x_ref, w_ref, s_ref, o_ref):
        o_ref[...] = (_silu(_conv(x_ref[...], w_ref, MLSTM_CONV)) * s_ref[...]).astype(o_ref.dtype)

    return pl.pallas_call(
        body, grid=(c // tc,),
        in_specs=[pl.BlockSpec((t, tc), lambda j: (0, j)), pl.BlockSpec((MLSTM_CONV, tc), lambda j: (0, j)),
                  pl.BlockSpec((1, tc), lambda j: (0, j))],
        out_specs=pl.BlockSpec((t, tc), lambda j: (0, j)), out_shape=jax.ShapeDtypeStruct((t, c), BF16),
        compiler_params=_params("parallel"), name="conv_qk_fwd")(x, w, colscale)


def conv_qk_bwd(x, w, colscale, dq, dk):
    t, c = x.shape
    tc = _pick(c // 2, 256)
    half = (c // 2) // tc

    def body(x_ref, w_ref, s_ref, dq_ref, dk_ref, dx_ref, dw_ref):
        j = pl.program_id(0)
        x_ = x_ref[...]
        dy = jnp.where(j < half, dq_ref[...], dk_ref[...])
        dpre = dy * s_ref[...] * _dsilu(_conv(x_, w_ref, MLSTM_CONV))
        dx_ref[...] = _conv_bwd(x_, dpre, w_ref, dw_ref, MLSTM_CONV).astype(dx_ref.dtype)

    return pl.pallas_call(
        body, grid=(c // tc,),
        in_specs=[pl.BlockSpec((t, tc), lambda j: (0, j)), pl.BlockSpec((MLSTM_CONV, tc), lambda j: (0, j)),
                  pl.BlockSpec((1, tc), lambda j: (0, j)),
                  pl.BlockSpec((t, tc), lambda j: (0, jnp.minimum(j, half - 1))),
                  pl.BlockSpec((t, tc), lambda j: (0, jnp.maximum(j - half, 0)))],
        out_specs=[pl.BlockSpec((t, tc), lambda j: (0, j)), pl.BlockSpec((MLSTM_CONV, tc), lambda j: (0, j))],
        out_shape=[jax.ShapeDtypeStruct((t, c), BF16), jax.ShapeDtypeStruct((MLSTM_CONV, c), F32)],
        compiler_params=_params("parallel"), name="conv_qk_bwd")(x, w, colscale, dq, dk)


def _mlstm_chunk(q, k, v, i_col, i_row, f_col, f_row, c_mat, n_vec, m):
    shape = (CHUNK, CHUNK)
    r = lax.broadcasted_iota(jnp.int32, shape, 0)
    c = lax.broadcasted_iota(jnp.int32, shape, 1)
    tril = c <= r
    lf_col, lf_row = _log_sigmoid(f_col), _log_sigmoid(f_row)
    bc_col = jnp.sum(jnp.where(tril, lf_row, 0.0), axis=1, keepdims=True)
    bc_row = jnp.sum(jnp.where(r <= c, lf_col, 0.0), axis=0, keepdims=True)
    logw = jnp.where(tril, bc_col - bc_row + i_row, -jnp.inf)
    inter = bc_col + m
    m_t = lax.stop_gradient(jnp.maximum(inter, jnp.max(logw, axis=1, keepdims=True)))
    w_intra = jnp.exp(logw - m_t)
    w_inter = jnp.exp(inter - m_t)
    sc = _dnt(q, k) * w_intra
    num = w_inter * _dnn(q, c_mat) + _dnn(sc, v)
    den = w_inter * jnp.sum(q * n_vec, axis=1, keepdims=True) + jnp.sum(sc, axis=1, keepdims=True)
    h = num / jnp.maximum(jnp.abs(den), jnp.exp(-m_t))
    b_last = jnp.sum(lf_row, axis=1, keepdims=True)
    m_new = lax.stop_gradient(jnp.maximum(b_last + m, jnp.max(b_last - bc_row + i_row, axis=1, keepdims=True)))
    decay = jnp.exp(b_last + m - m_new)
    uk = jnp.exp(b_last - bc_col + i_col - m_new) * k
    return h, decay * c_mat + _dtn(uk, v), decay * n_vec + jnp.sum(uk, axis=0, keepdims=True), m_new


def _mlstm_group(heads):
    return MLSTM_HEADS_PER_STEP if heads % MLSTM_HEADS_PER_STEP == 0 else 1


def _mlstm_specs(heads, grp, rev, nc):
    ci = (lambda c: nc - 1 - c) if rev else (lambda c: c)
    return dict(
        q=pl.BlockSpec((CHUNK, grp * MLSTM_DK), lambda g, c: (ci(c), g)),
        k=pl.BlockSpec((CHUNK, grp * MLSTM_DK), lambda g, c: (ci(c), heads // grp + g)),
        v=pl.BlockSpec((CHUNK, grp * MLSTM_DV), lambda g, c: (ci(c), g)),
        gc=pl.BlockSpec((grp, None, CHUNK, 2), lambda g, c: (g, ci(c), 0, 0)),
        gr=pl.BlockSpec((grp, None, 2, CHUNK), lambda g, c: (g, ci(c), 0, 0)),
        b=pl.BlockSpec((grp, 1, 2), lambda g, c: (g, 0, 0)),
        cm=pl.BlockSpec((grp, None, MLSTM_DK, MLSTM_DV), lambda g, c: (g, ci(c), 0, 0)),
        vec=pl.BlockSpec((grp, None, 1, LANE), lambda g, c: (g, ci(c), 0, 0)),
    )


def _gates(gc_ref, gr_ref, b_ref, s):
    bi, bf = b_ref[s, :, 0:1], b_ref[s, :, 1:2]
    return gc_ref[s, :, 0:1] + bi, gr_ref[s, 0:1, :] + bi, gc_ref[s, :, 1:2] + bf, gr_ref[s, 1:2, :] + bf


def mlstm_fwd(qk, v, gcol, grow, bias, heads):
    t = qk.shape[0]
    nc = t // CHUNK
    grp = _mlstm_group(heads)
    sp = _mlstm_specs(heads, grp, False, nc)

    def body(q_ref, k_ref, v_ref, gc_ref, gr_ref, b_ref, h_ref, c_ref, n_ref, m_ref, c_s, n_s, m_s):
        @pl.when(pl.program_id(1) == 0)
        def _():
            c_s[...] = jnp.zeros_like(c_s)
            n_s[...] = jnp.zeros_like(n_s)
            m_s[...] = jnp.zeros_like(m_s)

        c_ref[...] = c_s[...]
        n_ref[...] = n_s[...]
        m_ref[...] = m_s[...]
        for s in range(grp):
            qs, vs = slice(s * MLSTM_DK, (s + 1) * MLSTM_DK), slice(s * MLSTM_DV, (s + 1) * MLSTM_DV)
            q, k, v_ = _f32(q_ref[:, qs], k_ref[:, qs], v_ref[:, vs])
            h, c_new, n_new, m_new = _mlstm_chunk(q, k, v_, *_gates(gc_ref, gr_ref, b_ref, s), c_s[s], n_s[s], m_s[s, :, 0:1])
            h_ref[:, vs] = h
            c_s[s] = c_new
            n_s[s] = n_new
            m_s[s] = jnp.broadcast_to(m_new, (1, LANE))

    return pl.pallas_call(
        body, grid=(heads // grp, nc),
        in_specs=[sp["q"], sp["k"], sp["v"], sp["gc"], sp["gr"], sp["b"]],
        out_specs=[sp["v"], sp["cm"], sp["vec"], sp["vec"]],
        out_shape=[jax.ShapeDtypeStruct((t, heads * MLSTM_DV), F32),
                   jax.ShapeDtypeStruct((heads, nc, MLSTM_DK, MLSTM_DV), F32),
                   jax.ShapeDtypeStruct((heads, nc, 1, LANE), F32), jax.ShapeDtypeStruct((heads, nc, 1, LANE), F32)],
        scratch_shapes=[pltpu.VMEM((grp, MLSTM_DK, MLSTM_DV), F32), pltpu.VMEM((grp, 1, LANE), F32),
                        pltpu.VMEM((grp, 1, LANE), F32)],
        compiler_params=_params("parallel", "arbitrary"), name="mlstm_fwd")(qk, qk, v, gcol, grow, bias)


def mlstm_bwd(qk, v, gcol, grow, bias, c_all, n_all, m_all, dh, heads):
    t = qk.shape[0]
    nc = t // CHUNK
    grp = _mlstm_group(heads)
    sp = _mlstm_specs(heads, grp, True, nc)

    def body(q_ref, k_ref, v_ref, gc_ref, gr_ref, b_ref, c_ref, n_ref, m_ref, dh_ref,
             dq_ref, dk_ref, dv_ref, dgc_ref, dgr_ref, dc_s, dn_s):
        @pl.when(pl.program_id(1) == 0)
        def _():
            dc_s[...] = jnp.zeros_like(dc_s)
            dn_s[...] = jnp.zeros_like(dn_s)

        for s in range(grp):
            qs, vs = slice(s * MLSTM_DK, (s + 1) * MLSTM_DK), slice(s * MLSTM_DV, (s + 1) * MLSTM_DV)
            q, k, v_ = _f32(q_ref[:, qs], k_ref[:, qs], v_ref[:, vs])
            m = m_ref[s, :, 0:1]
            _, pull = jax.vjp(lambda *a: _mlstm_chunk(*a, m)[:3], q, k, v_, *_gates(gc_ref, gr_ref, b_ref, s),
                              c_ref[s], n_ref[s])
            dq, dk, dv, di_col, di_row, df_col, df_row, dc, dn = pull((dh_ref[:, vs], dc_s[s], dn_s[s]))
            dq_ref[:, qs] = dq
            dk_ref[:, qs] = dk
            dv_ref[:, vs] = dv.astype(dv_ref.dtype)
            dgc_ref[s, :, 0:1] = di_col
            dgc_ref[s, :, 1:2] = df_col
            dgr_ref[s, 0:1, :] = di_row
            dgr_ref[s, 1:2, :] = df_row
            dc_s[s] = dc
            dn_s[s] = dn

    qspec = pl.BlockSpec((CHUNK, grp * MLSTM_DK), lambda g, c: (nc - 1 - c, g))
    return pl.pallas_call(
        body, grid=(heads // grp, nc),
        in_specs=[sp["q"], sp["k"], sp["v"], sp["gc"], sp["gr"], sp["b"], sp["cm"], sp["vec"], sp["vec"], sp["v"]],
        out_specs=[qspec, qspec, sp["v"], sp["gc"], sp["gr"]],
        out_shape=[jax.ShapeDtypeStruct((t, heads * MLSTM_DK), F32), jax.ShapeDtypeStruct((t, heads * MLSTM_DK), F32),
                   jax.ShapeDtypeStruct((t, heads * MLSTM_DV), BF16),
                   jax.ShapeDtypeStruct(gcol.shape, F32), jax.ShapeDtypeStruct(grow.shape, F32)],
        scratch_shapes=[pltpu.VMEM((grp, MLSTM_DK, MLSTM_DV), F32), pltpu.VMEM((grp, 1, LANE), F32)],
        compiler_params=_params("parallel", "arbitrary"),
        name="mlstm_bwd")(qk, qk, v, gcol, grow, bias, c_all, n_all, m_all, dh)


def _hnorm_gate(h, zo, g):
    return _rms(h, g, MLSTM_DV) * _sigmoid(zo)


def _cross_core(q, k, v, g_q, g_k, heads):
    scale = CROSS_DH ** -0.5
    outs = []
    for h in range(heads):
        s = _dnt(_rms(q[h], g_q, CROSS_DH), _rms(k[h], g_k, CROSS_DH)) * scale
        p = jnp.exp(s - lax.stop_gradient(jnp.max(s, axis=1, keepdims=True)))
        p = p / jnp.sum(p, axis=1, keepdims=True)
        outs.append(_dnn(p, v[h]))
    return jnp.concatenate(outs, axis=1)


def _split_heads(ref, heads):
    return [ref[:, h * CROSS_DH:(h + 1) * CROSS_DH].astype(F32) for h in range(heads)]


def cross_fwd(q, k, v, g_q, g_k, heads):
    t = q.shape[0]
    tm = _pick(t, ATT_TILE)
    full = lambda a: pl.BlockSpec(a.shape, lambda i: (0, 0))

    def body(q_ref, k_ref, v_ref, gq_ref, gk_ref, o_ref):
        o_ref[...] = _cross_core(_split_heads(q_ref, heads), _split_heads(k_ref, heads), _split_heads(v_ref, heads),
                                 gq_ref[...], gk_ref[...], heads).astype(o_ref.dtype)

    return pl.pallas_call(
        body, grid=(t // tm,), in_specs=[pl.BlockSpec((tm, q.shape[1]), lambda i: (i, 0)), full(k), full(v), full(g_q), full(g_k)],
        out_specs=pl.BlockSpec((tm, q.shape[1]), lambda i: (i, 0)), out_shape=jax.ShapeDtypeStruct(q.shape, BF16),
        compiler_params=_params("parallel"), name="cross_fwd")(q, k, v, g_q, g_k)


def cross_bwd(q, k, v, g_q, g_k, do, heads):
    t, w = q.shape
    tm = _pick(t, ATT_TILE)
    full = lambda a: pl.BlockSpec(a.shape, lambda i: (0, 0))

    def body(q_ref, k_ref, v_ref, gq_ref, gk_ref, do_ref, dq_ref, dk_ref, dv_ref, dgq_ref, dgk_ref):
        qs, ks, vs = _split_heads(q_ref, heads), _split_heads(k_ref, heads), _split_heads(v_ref, heads)
        _, pull = jax.vjp(lambda a, b, c, d, e: _cross_core(a, b, c, d, e, heads), qs, ks, vs, gq_ref[...], gk_ref[...])
        dqs, dks, dvs, dgq, dgk = pull(do_ref[...])
        first = pl.program_id(0) == 0
        for h in range(heads):
            cols = slice(h * CROSS_DH, (h + 1) * CROSS_DH)
            dq_ref[:, cols] = dqs[h].astype(dq_ref.dtype)
            _store(dk_ref.at[:, cols], dks[h], first)
            _store(dv_ref.at[:, cols], dvs[h], first)
        _store(dgq_ref, dgq, first)
        _store(dgk_ref, dgk, first)

    row = pl.BlockSpec((tm, w), lambda i: (i, 0))
    return pl.pallas_call(
        body, grid=(t // tm,), in_specs=[row, full(k), full(v), full(g_q), full(g_k), row],
        out_specs=[row, full(k), full(v), full(g_q), full(g_k)],
        out_shape=[jax.ShapeDtypeStruct(q.shape, BF16), jax.ShapeDtypeStruct(k.shape, F32), jax.ShapeDtypeStruct(v.shape, F32),
                   jax.ShapeDtypeStruct(g_q.shape, F32), jax.ShapeDtypeStruct(g_k.shape, F32)],
        compiler_params=_params("arbitrary"), name="cross_bwd")(q, k, v, g_q, g_k, do)


def ffn_glu_fwd(hg, hv, wg, wv, bg, bv):
    t, f = hg.shape
    tc = _pick(f, LANE)
    col = pl.BlockSpec((t, tc), lambda j: (0, j))
    tap = pl.BlockSpec((FFN_CONV, tc), lambda j: (0, j))
    one = pl.BlockSpec((1, tc), lambda j: (0, j))

    def body(hg_ref, hv_ref, wg_ref, wv_ref, bg_ref, bv_ref, o_ref):
        gate = _conv(hg_ref[...], wg_ref, FFN_CONV) + bg_ref[...]
        val = _conv(hv_ref[...], wv_ref, FFN_CONV) + bv_ref[...]
        o_ref[...] = (_silu(gate) * val).astype(o_ref.dtype)

    return pl.pallas_call(body, grid=(f // tc,), in_specs=[col, col, tap, tap, one, one], out_specs=col,
                          out_shape=jax.ShapeDtypeStruct((t, f), BF16), compiler_params=_params("parallel"),
                          name="ffn_glu_fwd")(hg, hv, wg, wv, bg, bv)


def ffn_glu_bwd(hg, hv, wg, wv, bg, bv, dact):
    t, f = hg.shape
    tc = _pick(f, LANE)
    col = pl.BlockSpec((t, tc), lambda j: (0, j))
    tap = pl.BlockSpec((FFN_CONV, tc), lambda j: (0, j))
    one = pl.BlockSpec((1, tc), lambda j: (0, j))

    def body(hg_ref, hv_ref, wg_ref, wv_ref, bg_ref, bv_ref, da_ref, dhg_ref, dhv_ref, dwg_ref, dwv_ref, dbg_ref, dbv_ref):
        xg, xv, da = hg_ref[...], hv_ref[...], da_ref[...]
        gate = _conv(xg, wg_ref, FFN_CONV) + bg_ref[...]
        val = _conv(xv, wv_ref, FFN_CONV) + bv_ref[...]
        dgate = da * val * _dsilu(gate)
        dval = da * _silu(gate)
        dbg_ref[...] = jnp.sum(dgate, axis=0, keepdims=True)
        dbv_ref[...] = jnp.sum(dval, axis=0, keepdims=True)
        dhg_ref[...] = _conv_bwd(xg, dgate, wg_ref, dwg_ref, FFN_CONV).astype(dhg_ref.dtype)
        dhv_ref[...] = _conv_bwd(xv, dval, wv_ref, dwv_ref, FFN_CONV).astype(dhv_ref.dtype)

    return pl.pallas_call(
        body, grid=(f // tc,), in_specs=[col, col, tap, tap, one, one, col], out_specs=[col, col, tap, tap, one, one],
        out_shape=[jax.ShapeDtypeStruct((t, f), BF16), jax.ShapeDtypeStruct((t, f), BF16),
                   jax.ShapeDtypeStruct((FFN_CONV, f), F32), jax.ShapeDtypeStruct((FFN_CONV, f), F32),
                   jax.ShapeDtypeStruct((1, f), F32), jax.ShapeDtypeStruct((1, f), F32)],
        compiler_params=_params("parallel"), name="ffn_glu_bwd")(hg, hv, wg, wv, bg, bv, dact)


def _adamw(g, w, m, v):
    m = ADAM_B1 * m + (1.0 - ADAM_B1) * g
    v = ADAM_B2 * v + (1.0 - ADAM_B2) * (g * g)
    m_hat = m / (1.0 - ADAM_B1 ** ADAM_STEP)
    v_hat = v / (1.0 - ADAM_B2 ** ADAM_STEP)
    return -ADAM_LR * (m_hat / (jnp.sqrt(v_hat) + ADAM_EPS) + ADAM_WD * w), m, v


def adamw(g, w, m, v, name):
    r, c = g.shape
    tr = r
    for cand in (256, 128, 64, 32, 16, 8):
        if r % cand == 0 and cand * c * 4 <= (1 << 21):
            tr = cand
            break
    return ew(_adamw, [(g, "r"), (w, "r"), (m, "r"), (v, "r")], [((r, c), F32, "r")] * 3, gr=r // tr, name=name)


ANY = pl.BlockSpec(memory_space=pl.ANY)


def _place():
    x, y, c = lax.axis_index("x"), lax.axis_index("y"), lax.axis_index("c")
    return x, y, c, [(1 - x, y), (x, 1 - y), (1 - x, 1 - y)]


def _rcopy(src, dst, send, recv, k, to):
    return pltpu.make_async_remote_copy(src_ref=src, dst_ref=dst, send_sem=send.at[k], recv_sem=recv.at[k],
                                        device_id=to, device_id_type=MESH)


def gather_shards(bigs, smalls):
    nb, na = len(bigs), len(bigs) + len(smalls)
    arrays = list(bigs) + list(smalls)

    def body(*refs):
        ins, outs = refs[:na], refs[na:2 * na]
        send, recv = refs[2 * na:]
        x, y, c, chips = _place()
        me, sib = 2 * x + y, (x, y, 1 - c)

        def half(ref, a, which):
            rows = arrays[a].shape[0] // 2
            return ref.at[pl.ds(which * rows, rows)]

        started = []
        for a in range(na):
            for j, (cx, cy) in enumerate(chips):
                if a < nb:
                    cp = _rcopy(half(ins[a], a, c), half(outs[a].at[me], a, c), send, recv, 6 * a + j, (cx, cy, c))
                else:
                    cp = _rcopy(ins[a], outs[a].at[me], send, recv, 6 * nb + 3 * (a - nb) + j, (cx, cy, c))
                cp.start()
                started.append(cp)
        for a in range(nb):
            for j, (cx, cy) in enumerate(chips):
                landed = half(outs[a].at[2 * cx + cy], a, c)
                _rcopy(landed, landed, send, recv, 6 * a + j, (cx, cy, c)).wait_recv()
                cp = _rcopy(landed, landed, send, recv, 6 * a + 3 + j, sib)
                cp.start()
                started.append(cp)
        for a in range(na):
            for j, (cx, cy) in enumerate(chips):
                if a < nb:
                    dst = half(outs[a].at[2 * cx + cy], a, 1 - c)
                    _rcopy(dst, dst, send, recv, 6 * a + 3 + j, sib).wait_recv()
                else:
                    dst = outs[a].at[2 * cx + cy]
                    _rcopy(dst, dst, send, recv, 6 * nb + 3 * (a - nb) + j, (cx, cy, c)).wait_recv()
        for cp in started:
            cp.wait_send()

    n_sem = 6 * nb + 3 * (na - nb)
    gathered = pl.pallas_call(
        body, in_specs=[ANY] * na, out_specs=[ANY] * na,
        out_shape=[jax.ShapeDtypeStruct((N_CHIPS,) + a.shape, a.dtype) for a in arrays],
        scratch_shapes=[pltpu.SemaphoreType.DMA((n_sem,)), pltpu.SemaphoreType.DMA((n_sem,))],
        name="gather_shards")(*arrays)
    chip = 2 * lax.axis_index("x") + lax.axis_index("y")
    return [lax.dynamic_update_slice(g, a[None], (chip, 0, 0)) for g, a in zip(gathered, arrays)]


def sibling_halves(grads):
    na = len(grads)

    def body(*refs):
        ins, outs = refs[:na], refs[na:2 * na]
        send, recv = refs[2 * na:]
        x, y, c, _ = _place()
        cps = []
        for a in range(na):
            rows = grads[a].shape[1] // 2
            cp = _rcopy(ins[a].at[:, pl.ds((1 - c) * rows, rows)], outs[a], send, recv, a, (x, y, 1 - c))
            cp.start()
            cps.append(cp)
        for cp in cps:
            cp.wait()

    return pl.pallas_call(
        body, in_specs=[ANY] * na, out_specs=[ANY] * na,
        out_shape=[jax.ShapeDtypeStruct((g.shape[0], g.shape[1] // 2, g.shape[2]), g.dtype) for g in grads],
        scratch_shapes=[pltpu.SemaphoreType.DMA((na,)), pltpu.SemaphoreType.DMA((na,))], name="sibling_halves")(*grads)


def join_halves(halves):
    na = len(halves)

    def body(*refs):
        ins, outs = refs[:na], refs[na:2 * na]
        send, recv = refs[2 * na:]
        x, y, c, _ = _place()
        cps = []
        for a in range(na):
            cp = _rcopy(ins[a].at[c], outs[a].at[c], send, recv, a, (x, y, 1 - c))
            cp.start()
            cps.append(cp)
        for a in range(na):
            dst = outs[a].at[1 - c]
            _rcopy(dst, dst, send, recv, a, (x, y, 1 - c)).wait_recv()
        for cp in cps:
            cp.wait_send()

    return pl.pallas_call(
        body, in_specs=[ANY] * na, out_specs=[ANY] * na,
        out_shape=[jax.ShapeDtypeStruct(h.shape, h.dtype) for h in halves],
        input_output_aliases={a: a for a in range(na)},
        scratch_shapes=[pltpu.SemaphoreType.DMA((na,)), pltpu.SemaphoreType.DMA((na,))],
        name="join_halves")(*halves)


HBM = pl.BlockSpec(memory_space=pltpu.HBM)
SEM = pl.BlockSpec(memory_space=pltpu.SEMAPHORE)
SIDE_EFFECT = pltpu.SideEffectType.DATAFLOW_SIDE_EFFECTING


def split_start(name, srcs, land_shapes, n_copies, copies_fn, after):
    ns, nl = len(srcs), len(land_shapes)
    afters = tuple(after) if isinstance(after, (tuple, list)) else (after,)

    def body(*refs):
        ins, lands = refs[:ns], refs[ns:ns + nl]
        send, recv, token = refs[ns + nl + len(afters)], refs[ns + nl + len(afters) + 1], refs[-1]
        for k, (src, dst, dev) in enumerate(copies_fn(ins, lands, False)):
            pltpu.make_async_remote_copy(src_ref=src, dst_ref=dst, send_sem=send.at[k], recv_sem=recv.at[k],
                                         device_id=dev, device_id_type=MESH).start()
        token[...] = jnp.zeros_like(token)

    outs = pl.pallas_call(
        body, name=name,
        out_shape=(pltpu.SemaphoreType.DMA((n_copies,)), pltpu.SemaphoreType.DMA((n_copies,)),
                   *[pltpu.HBM(a.shape, a.dtype) for a in srcs], *[pltpu.HBM(s, dt) for s, dt in land_shapes],
                   jax.ShapeDtypeStruct((8, LANE), F32)),
        in_specs=[HBM] * (ns + nl) + [ANY] * len(afters),
        out_specs=(SEM, SEM, *[HBM] * (ns + nl), pl.BlockSpec(memory_space=pltpu.VMEM)),
        input_output_aliases={i: 2 + i for i in range(ns + nl)},
        compiler_params=pltpu.CompilerParams(has_side_effects=SIDE_EFFECT),
    )(*[pltpu.with_memory_space_constraint(a, pltpu.HBM) for a in srcs],
      *[pltpu.with_memory_space_constraint(lax.empty(s, dt), pltpu.HBM) for s, dt in land_shapes], *afters)
    return outs[0], outs[1], list(outs[2:2 + ns]), list(outs[2 + ns:2 + ns + nl]), outs[-1]


def split_wait(name, started, n_copies, copies_fn, after):
    send, recv, srcs, lands, _ = started
    ns, nl = len(srcs), len(lands)
    afters = tuple(after) if isinstance(after, (tuple, list)) else (after,)

    def body(*refs):
        ins, lnd = refs[:ns], refs[ns:ns + nl]
        send_ref, recv_ref = refs[ns + nl], refs[ns + nl + 1]
        for k, (src, dst, dev) in enumerate(copies_fn(ins, lnd, True)):
            cp = pltpu.make_async_remote_copy(src_ref=src, dst_ref=dst, send_sem=send_ref.at[k], recv_sem=recv_ref.at[k],
                                              device_id=dev, device_id_type=MESH)
            cp.wait_send()
            cp.wait_recv()

    outs = pl.pallas_call(
        body, name=name,
        out_shape=tuple(pltpu.HBM(a.shape, a.dtype) for a in srcs + lands),
        in_specs=[HBM] * (ns + nl) + [SEM, SEM] + [ANY] * len(afters), out_specs=tuple([HBM] * (ns + nl)),
        input_output_aliases={i: i for i in range(ns + nl)},
        compiler_params=pltpu.CompilerParams(has_side_effects=SIDE_EFFECT),
    )(*srcs, *lands, send, recv, *afters)
    return list(outs[:ns]), list(outs[ns:])


def _gather_copies(ins, lands, waiting):
    x, y, c, chips = _place()
    return [(ins[a], lands[a].at[2 * cx + cy] if waiting else lands[a].at[2 * x + y], (cx, cy, c))
            for a in range(len(ins)) for cx, cy in chips]


def _scatter_copies(ins, lands, waiting):
    del waiting
    _, _, c, chips = _place()
    return [(ins[a].at[2 * cx + cy], lands[a].at[j], (cx, cy, c)) for a in range(len(ins)) for j, (cx, cy) in enumerate(chips)]


def allreduce_small(vec):
    r = vec.shape[0]

    def body(x_ref, sum_ref, all_ref, send, recv):
        x, y, c, _ = _place()
        me = 4 * x + 2 * y + c
        all_ref[me] = x_ref[...]
        cps, peers = [], []
        for mask in range(1, 8):
            px = 1 - x if mask & 4 else x
            py = 1 - y if mask & 2 else y
            pc = 1 - c if mask & 1 else c
            peers.append(4 * px + 2 * py + pc)
            cp = _rcopy(x_ref, all_ref.at[me], send, recv, mask - 1, (px, py, pc))
            cp.start()
            cps.append(cp)
        for k, cp in enumerate(cps):
            _rcopy(x_ref, all_ref.at[peers[k]], send, recv, k, (x, y, c)).wait_recv()
        for cp in cps:
            cp.wait_send()
        total = all_ref[0]
        for d in range(1, 8):
            total = total + all_ref[d]
        sum_ref[...] = total

    vm = pl.BlockSpec(memory_space=pltpu.VMEM)
    return pl.pallas_call(
        body, in_specs=[vm], out_specs=vm, out_shape=jax.ShapeDtypeStruct((r, LANE), F32),
        scratch_shapes=[pltpu.VMEM((8, r, LANE), F32), pltpu.SemaphoreType.DMA((7,)), pltpu.SemaphoreType.DMA((7,))],
        compiler_params=pltpu.CompilerParams(vmem_limit_bytes=VMEM_LIMIT), name="allreduce_small")(vec)


def _row_tile(rows):
    for cand in (256, 128, 64, 32, 16):
        if rows % cand == 0:
            return cand
    return rows


def add_sibling(grad, recv, c_idx):
    _, rows, cols = grad.shape
    hr = rows // 2
    tr = _row_tile(hr)
    nb = hr // tr

    def body(c_ref, g_ref, r_ref, o_ref):
        o_ref[...] = (g_ref[...].astype(F32) + r_ref[...].astype(F32)).astype(o_ref.dtype)

    return pl.pallas_call(
        body,
        grid_spec=pltpu.PrefetchScalarGridSpec(
            num_scalar_prefetch=1, grid=(N_CHIPS, nb),
            in_specs=[pl.BlockSpec((None, tr, cols), lambda k, r, c_ref: (k, c_ref[0] * nb + r, 0)),
                      pl.BlockSpec((None, tr, cols), lambda k, r, c_ref: (k, r, 0))],
            out_specs=pl.BlockSpec((None, tr, cols), lambda k, r, c_ref: (k, r, 0))),
        out_shape=jax.ShapeDtypeStruct((N_CHIPS, hr, cols), BF16),
        compiler_params=_params("parallel", "parallel"), name="add_sibling")(c_idx, grad, recv)


def sum_chips(part, others, place_idx):
    _, hr, cols = part.shape
    tr = _row_tile(hr)

    def body(k_ref, p_ref, o0_ref, o1_ref, o2_ref, out_ref):
        out_ref[...] = ((p_ref[...].astype(F32) + o0_ref[...].astype(F32)) + o1_ref[...].astype(F32)) + o2_ref[...].astype(F32)

    other = lambda j: pl.BlockSpec((None, tr, cols), lambda r, k_ref: (j, r, 0))
    return pl.pallas_call(
        body,
        grid_spec=pltpu.PrefetchScalarGridSpec(
            num_scalar_prefetch=1, grid=(hr // tr,),
            in_specs=[pl.BlockSpec((None, tr, cols), lambda r, k_ref: (k_ref[0], r, 0)), other(0), other(1), other(2)],
            out_specs=pl.BlockSpec((None, tr, cols), lambda r, k_ref: (k_ref[1], r, 0))),
        out_shape=jax.ShapeDtypeStruct((2, hr, cols), F32),
        compiler_params=_params("parallel"), name="sum_chips")(place_idx, part, others, others, others)


def _pad_lanes(a, width=LANE):
    return jnp.pad(a, ((0, 0), (0, width - a.shape[1])))


def _cols_from_shards(g):
    return jnp.transpose(g, (1, 0, 2)).reshape(g.shape[1], -1)


def _cols_to_shards(w):
    k, n4 = w.shape
    return jnp.transpose(w.reshape(k, N_CHIPS, n4 // N_CHIPS), (1, 0, 2))


def kernel(x, mem, positions, g_mix, w_in, g_qa, w_qb, g_kva, w_kvb, g_qn_nope, g_qn_pe, g_kn_nope, g_kn_pe, conv_qk, b_if, g_hnorm, p_a, p_b, w_out, g_cross, g_mem, wq_c, wk_c, wv_c, g_cq, g_ck, wo_c, g_ffn, w_up, conv_ffn, b_conv_ffn, w_down, loss_target, m_g_mix, m_w_in, m_g_qa, m_w_qb, m_g_kva, m_w_kvb, m_g_qn_nope, m_g_qn_pe, m_g_kn_nope, m_g_kn_pe, m_conv_qk, m_b_if, m_g_hnorm, m_p_a, m_p_b, m_w_out, m_g_cross, m_g_mem, m_wq_c, m_wk_c, m_wv_c, m_g_cq, m_g_ck, m_wo_c, m_g_ffn, m_w_up, m_conv_ffn, m_b_conv_ffn, m_w_down, v_g_mix, v_w_in, v_g_qa, v_w_qb, v_g_kva, v_w_kvb, v_g_qn_nope, v_g_qn_pe, v_g_kn_nope, v_g_kn_pe, v_conv_qk, v_b_if, v_g_hnorm, v_p_a, v_p_b, v_w_out, v_g_cross, v_g_mem, v_wq_c, v_wk_c, v_wv_c, v_g_cq, v_g_ck, v_wo_c, v_g_ffn, v_w_up, v_conv_ffn, v_b_conv_ffn, v_w_down):
    names = ["g_mix", "w_in", "g_qa", "w_qb", "g_kva", "w_kvb", "g_qn_nope", "g_qn_pe", "g_kn_nope", "g_kn_pe", "conv_qk",
             "b_if", "g_hnorm", "p_a", "p_b", "w_out", "g_cross", "g_mem", "wq_c", "wk_c", "wv_c", "g_cq", "g_ck", "wo_c",
             "g_ffn", "w_up", "conv_ffn", "b_conv_ffn", "w_down"]
    env = locals()
    wts = {n: env[n] for n in names}
    mom = {n: env["m_" + n] for n in names}
    var = {n: env["v_" + n] for n in names}

    xi, yi, ci = lax.axis_index("x"), lax.axis_index("y"), lax.axis_index("c")
    chip = 2 * xi + yi
    place_arr = jnp.stack([chip, ci]).astype(jnp.int32)
    c_arr = jnp.reshape(ci, (1,)).astype(jnp.int32)

    x2d, tgt, mem2d = x[0], loss_target[0], mem[0]
    t, d = x2d.shape
    mla_h = w_qb.shape[2] * N_CHIPS // (NOPE + ROPE)
    ml_h = b_if.shape[1] // 2
    cr_h = wq_c.shape[2] // CROSS_DH
    f_dim = w_down.shape[1] * N_CHIPS
    q_rank, kv_rank = g_qa.shape[1], g_kva.shape[1]
    qk_w, v_w = ml_h * MLSTM_DK, ml_h * MLSTM_DV
    nc = t // CHUNK

    big_names = ["w_in", "w_qb", "w_kvb", "p_a", "p_b", "w_out", "wq_c", "wk_c", "wv_c", "wo_c", "w_up", "w_down"]
    col_sharded = {"w_in", "w_qb", "w_kvb", "wo_c", "w_up"}
    small_sharded = ["conv_qk", "g_hnorm", "conv_ffn"]
    early_big, early_small = ["w_in", "w_qb", "w_kvb"], ["conv_qk", "g_hnorm"]
    late_groups = [["p_a", "p_b", "w_out", "wq_c", "wk_c", "wv_c", "wo_c"], ["w_up", "w_down", "conv_ffn"]]
    full = {}

    def unshard(n, g):
        if n == "w_up":
            full[n] = g
        else:
            full[n] = _cols_from_shards(g) if (n in col_sharded or n in small_sharded) else g.reshape(-1, g.shape[2])

    gathered = gather_shards([wts[n][0].astype(BF16) for n in early_big], [wts[n][0] for n in early_small])
    for n, g in zip(early_big + early_small, gathered):
        unshard(n, g)
    late, order_after = [], gathered[0]
    for gi, group in enumerate(late_groups):
        src = [wts[n][0].astype(BF16) if n in big_names else wts[n][0] for n in group]
        late.append(split_start("gather_late%d_start" % gi, src, [((N_CHIPS,) + a.shape, a.dtype) for a in src],
                                3 * len(src), _gather_copies, order_after))
        order_after = late[-1][4]
    g_mix_fwd = g_mix + order_after[0:1, 0:1]

    def land_late(gi, after):
        group = late_groups[gi]
        own, landed = split_wait("gather_late%d_wait" % gi, late[gi], 3 * len(group), _gather_copies, after)
        for n, g, o in zip(group, landed, own):
            unshard(n, lax.dynamic_update_slice(g, o[None], (chip, 0, 0)))

    o_qa, o_kv, o_kpe = 0, q_rank, q_rank + kv_rank
    o_q = o_kpe + ROPE
    o_v = o_q + 2 * qk_w
    o_if = o_v + v_w
    o_o = o_if + 2 * ml_h
    o_ga, o_gb = o_o + v_w, o_o + v_w + d
    wi = full["w_in"]
    pad_kpe = jnp.zeros((d, LANE - ROPE), BF16)
    pad_if = jnp.zeros((d, LANE - 2 * ml_h), BF16)
    w_small = jnp.concatenate([wi[:, o_qa:o_q], pad_kpe, wi[:, o_if:o_o], pad_if], axis=1)
    o_kpe_s, o_if_s = o_kpe, o_kpe + LANE
    w_qk, w_v, w_o, w_ga, w_gb = wi[:, o_q:o_v], wi[:, o_v:o_if], wi[:, o_o:o_ga], wi[:, o_ga:o_gb], wi[:, o_gb:]

    wq3 = full["w_qb"].reshape(q_rank, mla_h, NOPE + ROPE)
    wq_nope = wq3[:, :, :NOPE].reshape(q_rank, mla_h * NOPE)
    wq_pe = jnp.pad(wq3[:, :, NOPE:], ((0, 0), (0, 0), (0, LANE - ROPE))).reshape(q_rank, mla_h * LANE)
    wkv3 = full["w_kvb"].reshape(kv_rank, mla_h, NOPE + VHEAD)
    wk_nope = wkv3[:, :, :NOPE].reshape(kv_rank, mla_h * NOPE)
    wv_mla = wkv3[:, :, NOPE:].reshape(kv_rank, mla_h * VHEAD)

    inv_freq = ROPE_BASE ** (-jnp.arange(0, ROPE, 2, dtype=F32) / ROPE)
    ang = positions[0].astype(F32)[:, None] * inv_freq
    cos, sin = jnp.cos(ang), jnp.sin(ang)
    zero_h = jnp.zeros_like(cos)
    tabs = [_pad_lanes(jnp.concatenate([cos, cos], axis=1)), _pad_lanes(-sin), _pad_lanes(jnp.concatenate([zero_h, sin], axis=1))]
    mla_gains = [g_qn_nope, _pad_lanes(g_qn_pe), g_kn_nope, _pad_lanes(g_kn_pe)]

    u1 = rms_fwd(x2d, g_mix_fwd, "rms_mix")
    z_small = mm(u1, w_small, name="in_small")
    z_qa, z_kv = z_small[:, o_qa:o_kv], z_small[:, o_kv:o_kpe]
    z_kpe, z_if = z_small[:, o_kpe_s:o_kpe_s + LANE], z_small[:, o_if_s:o_if_s + 2 * ml_h]
    z_qk = mm(u1, w_qk, name="in_qk")
    z_v = mm(u1, w_v, name="in_v")
    z_o = mm(u1, w_o, name="in_o")
    z_ga = mm(u1, w_ga, name="in_ga")
    z_gb = mm(u1, w_gb, name="in_gb")

    qa_n = rms_fwd(z_qa, g_qa, "rms_qa")
    kv_n = rms_fwd(z_kv, g_kva, "rms_kva")
    qn_raw = mm(qa_n, wq_nope, name="q_nope")
    qp_raw = mm(qa_n, wq_pe, name="q_pe")
    kn_raw = mm(kv_n, wk_nope, name="k_nope")
    v_mla = mm(kv_n, wv_mla, out_dtype=BF16, name="v_mla")
    q_att, k_att = mla_prep_fwd(qn_raw, qp_raw, kn_raw, z_kpe, tabs, mla_gains, mla_h)
    y_a, lse_row = attn_fwd(q_att, k_att, jnp.transpose(v_mla), mla_h)

    colscale = jnp.concatenate([jnp.full((1, qk_w), MLSTM_DK ** -0.5, F32), jnp.ones((1, qk_w), F32)], axis=1)
    qk_c = conv_qk_fwd(z_qk, full["conv_qk"], colscale)
    gates4 = z_if.reshape(nc, CHUNK, 2, ml_h)
    gcol = jnp.transpose(gates4, (3, 0, 1, 2))
    grow = jnp.transpose(gates4, (3, 0, 2, 1))
    bias = jnp.transpose(b_if.reshape(2, ml_h), (1, 0)).reshape(ml_h, 1, 2)
    h_raw, c_all, n_all, m_all = mlstm_fwd(qk_c, z_v, gcol, grow, bias, ml_h)
    g_hn = full["g_hnorm"].reshape(1, v_w)
    hn_gr, hd_gr = t // _pick(t, ROW_TILE), t // _pick(t, HEAD_ROW_TILE)
    y_b = ew(lambda *a: (_hnorm_gate(*a),), [(h_raw, "rc"), (z_o, "rc"), (g_hn, "c")], [((t, v_w), BF16, "rc")], gr=hd_gr, gc=ml_h,
             name="hnorm_gate")[0]

    land_late(0, y_b)

    pa = mm(y_a, full["p_a"], name="proj_a")
    pb = mm(y_b, full["p_b"], name="proj_b")
    merge_fn = lambda ga, gb, a, b: (_sigmoid(ga) * a + _sigmoid(gb) * b,)
    merged = ew(merge_fn, [(z_ga, "r"), (z_gb, "r"), (pa, "r"), (pb, "r")], [((t, d), BF16, "r")], gr=hn_gr, name="merge")[0]
    x1 = mm(merged, full["w_out"], add=x2d, name="out_proj")

    uc = rms_fwd(x1, g_cross, "rms_cross")
    mem_n = rms_fwd(mem2d, g_mem, "rms_mem")
    qc = mm(uc, full["wq_c"], name="cross_q")
    kc = mm(mem_n, full["wk_c"], name="cross_k")
    vc = mm(mem_n, full["wv_c"], name="cross_v")
    oc = cross_fwd(qc, kc, vc, g_cq, g_ck, cr_h)
    x2 = mm(oc, full["wo_c"], add=x1, name="cross_out")

    land_late(1, x2)
    half = N_CHIPS // 2
    u3 = rms_fwd(x2, g_ffn, "rms_ffn")
    hg = mm(u3, full["w_up"], b_shards=(0, half), name="ffn_up_gate")
    hv = mm(u3, full["w_up"], b_shards=(half, half), name="ffn_up_val")
    cw, cb = full["conv_ffn"], b_conv_ffn
    act = ffn_glu_fwd(hg, hv, cw[:, :f_dim], cw[:, f_dim:], cb[:, :f_dim], cb[:, f_dim:])
    y = mm(act, full["w_down"], add=x2, name="ffn_down")

    def loss_fn(y_, t_):
        err = y_ - t_
        part = jnp.sum(jnp.sum(err * err, axis=1, keepdims=True), axis=0, keepdims=True) * (0.5 / d)
        return err * (1.0 / d), err * (1.0 / d), jnp.broadcast_to(part, (1, LANE))

    dy, dy_mx, loss_part = ew(loss_fn, [(y, "r"), (tgt, "r")], [((t, d), F32, "r"), ((t, d), BF16, "r"), ((1, LANE), F32, "f")],
                              gr=hn_gr, name="loss")

    gw = {}
    gw["w_down"] = mm(act, dy_mx, ta=True, out_dtype=BF16, name="dw_down")
    dact = mm(dy_mx, full["w_down"], tb=True, name="d_act")
    dhg, dhv, dcw_g, dcw_v, dcb_g, dcb_v = ffn_glu_bwd(hg, hv, cw[:, :f_dim], cw[:, f_dim:], cb[:, :f_dim], cb[:, f_dim:], dact)
    gw["conv_ffn"] = jnp.concatenate([dcw_g, dcw_v], axis=1)
    gw["b_conv_ffn"] = jnp.concatenate([dcb_g, dcb_v], axis=1)
    dwup_g = mm(u3, dhg, ta=True, out_dtype=BF16, out_shards=half, name="dw_up_gate")
    dwup_v = mm(u3, dhv, ta=True, out_dtype=BF16, out_shards=half, name="dw_up_val")
    gw["w_up"] = jnp.concatenate([dwup_g, dwup_v], axis=0)

    def shard_major(n):
        if n == "w_up":
            return gw[n]
        return _cols_to_shards(gw[n]) if n in col_sharded else gw[n].reshape(N_CHIPS, -1, gw[n].shape[1])

    def chip_partials(group):
        grads_sm = [shard_major(n) for n in group]
        return [add_sibling(g, r, c_arr) for g, r in zip(grads_sm, sibling_halves(grads_sm))]

    def scatter_group(group, tag, after):
        parts_ = chip_partials(group)
        return split_start("scatter_start_" + tag, parts_, [((3,) + p.shape[1:], p.dtype) for p in parts_],
                           3 * len(parts_), _scatter_copies, after)

    group_a = ["w_up", "w_down"]
    started_a = scatter_group(group_a, "a", gw["w_up"])
    g_ffn_bwd = g_ffn + started_a[4][0:1, 0:1]
    du3 = mm(dhg, full["w_up"], tb=True, b_shards=(0, half), name="d_u3_gate")
    du3 = mm(dhv, full["w_up"], tb=True, b_shards=(half, half), add=du3, name="d_u3_val")
    dx2, gw["g_ffn"] = rms_bwd(x2, g_ffn_bwd, du3, dy, "rms_ffn_bwd")

    gw["wo_c"] = mm(oc, dx2, ta=True, out_dtype=BF16, name="dw_cross_out")
    doc = mm(dx2, full["wo_c"], tb=True, name="d_cross_o")
    dqc, dkc, dvc, gw["g_cq"], gw["g_ck"] = cross_bwd(qc, kc, vc, g_cq, g_ck, doc, cr_h)
    gw["wq_c"] = mm(uc, dqc, ta=True, out_dtype=BF16, name="dw_cross_q")
    gw["wk_c"] = mm(mem_n, dkc, ta=True, out_dtype=BF16, name="dw_cross_k")
    gw["wv_c"] = mm(mem_n, dvc, ta=True, out_dtype=BF16, name="dw_cross_v")
    duc = mm(dqc, full["wq_c"], tb=True, name="d_uc")
    dmem_n = mm(dkc, full["wk_c"], tb=True, name="d_mem_k")
    dmem_n = mm(dvc, full["wv_c"], tb=True, add=dmem_n, name="d_mem_v")
    _, gw["g_mem"] = rms_bwd(mem2d, g_mem, dmem_n, None, "rms_mem_bwd")
    dx1, gw["g_cross"] = rms_bwd(x1, g_cross, duc, dx2, "rms_cross_bwd")

    gw["w_out"] = mm(merged, dx1, ta=True, out_dtype=BF16, name="dw_out")
    dmerged = mm(dx1, full["w_out"], tb=True, name="d_merged")

    def merge_bwd(ga, gb, a, b, dm):
        _, pull = jax.vjp(lambda *args: merge_fn(*args)[0], ga, gb, a, b)
        return pull(dm)

    dz_ga, dz_gb, dpa, dpb = ew(merge_bwd, [(z_ga, "r"), (z_gb, "r"), (pa, "r"), (pb, "r"), (dmerged, "r")],
                                [((t, d), BF16, "r")] * 4, gr=hn_gr, name="merge_bwd")
    gw["p_a"] = mm(y_a, dpa, ta=True, out_dtype=BF16, name="dw_proj_a")
    gw["p_b"] = mm(y_b, dpb, ta=True, out_dtype=BF16, name="dw_proj_b")
    group_b = ["wo_c", "wq_c", "wk_c", "wv_c", "w_out", "p_a", "p_b"]
    started_b = scatter_group(group_b, "b", gw["p_b"])
    g_hn_bwd = g_hn + started_b[4][0:1, 0:1]
    dy_a = mm(dpa, full["p_a"], tb=True, name="d_ya")
    dy_b = mm(dpb, full["p_b"], tb=True, name="d_yb")

    def hnorm_bwd(h_, zo_, g_, dyb_):
        _, pull = jax.vjp(_hnorm_gate, h_, zo_, g_)
        return pull(dyb_)

    dh_raw, dz_o, dg_hn = ew(hnorm_bwd, [(h_raw, "rc"), (z_o, "rc"), (g_hn_bwd, "c"), (dy_b, "rc")],
                             [((t, v_w), F32, "rc"), ((t, v_w), BF16, "rc"), ((1, v_w), F32, "c")],
                             gr=hd_gr, gc=ml_h, order="cr", name="hnorm_gate_bwd")
    gw["g_hnorm"] = dg_hn.reshape(ml_h, MLSTM_DV)
    dq_m, dk_m, dz_v, dgcol, dgrow = mlstm_bwd(qk_c, z_v, gcol, grow, bias, c_all, n_all, m_all, dh_raw, ml_h)
    dgates4 = jnp.transpose(dgcol, (1, 2, 3, 0)) + jnp.transpose(dgrow, (1, 3, 2, 0))
    dz_if = dgates4.reshape(t, 2 * ml_h)
    gw["b_if"] = ew(lambda a: (jnp.sum(a, axis=0, keepdims=True),), [(dz_if, "r")], [((1, 2 * ml_h), F32, "f")],
                    gr=hn_gr, name="bias_if_bwd")[0]
    dz_qk, gw["conv_qk"] = conv_qk_bwd(z_qk, full["conv_qk"], colscale, dq_m, dk_m)

    dq_att, dk_att, dv_mla = attn_bwd(q_att, k_att, v_mla, y_a, dy_a, lse_row, mla_h)
    dqn_raw, dqp_raw, dkn_raw, dz_kpe, gw["g_qn_nope"], dg_qp, gw["g_kn_nope"], dg_kp = mla_prep_bwd(
        qn_raw, qp_raw, kn_raw, z_kpe, tabs, mla_gains, dq_att, dk_att, mla_h)
    gw["g_qn_pe"], gw["g_kn_pe"] = dg_qp[:, :ROPE], dg_kp[:, :ROPE]
    dwq_nope = mm(qa_n, dqn_raw, ta=True, out_dtype=BF16, name="dw_q_nope")
    dwq_pe = mm(qa_n, dqp_raw, ta=True, out_dtype=BF16, name="dw_q_pe")
    dwk_nope = mm(kv_n, dkn_raw, ta=True, out_dtype=BF16, name="dw_k_nope")
    dwv_mla = mm(kv_n, dv_mla, ta=True, out_dtype=BF16, name="dw_v_mla")
    dqa_n = mm(dqn_raw, wq_nope, tb=True, name="d_qa_nope")
    dqa_n = mm(dqp_raw, wq_pe, tb=True, add=dqa_n, name="d_qa_pe")
    dkv_n = mm(dkn_raw, wk_nope, tb=True, name="d_kv_nope")
    dkv_n = mm(dv_mla, wv_mla, tb=True, add=dkv_n, name="d_kv_v")
    dz_qa, gw["g_qa"] = rms_bwd(z_qa, g_qa, dqa_n, None, "rms_qa_bwd", BF16)
    dz_kv, gw["g_kva"] = rms_bwd(z_kv, g_kva, dkv_n, None, "rms_kva_bwd", BF16)
    gw["w_qb"] = jnp.concatenate([dwq_nope.reshape(q_rank, mla_h, NOPE), dwq_pe.reshape(q_rank, mla_h, LANE)[:, :, :ROPE]],
                                 axis=2).reshape(q_rank, -1)
    gw["w_kvb"] = jnp.concatenate([dwk_nope.reshape(kv_rank, mla_h, NOPE), dwv_mla.reshape(kv_rank, mla_h, VHEAD)],
                                  axis=2).reshape(kv_rank, -1)

    dz_small = jnp.concatenate([dz_qa, dz_kv, dz_kpe.astype(BF16), _pad_lanes(dz_if).astype(BF16)], axis=1)
    dw_small = mm(u1, dz_small, ta=True, out_dtype=BF16, name="dw_in_small")
    du1 = mm(dz_small, w_small, tb=True, name="d_u1_small")
    dw_segs = []
    for nm, dz, w_seg in (("qk", dz_qk, w_qk), ("v", dz_v, w_v), ("o", dz_o, w_o), ("ga", dz_ga, w_ga), ("gb", dz_gb, w_gb)):
        dw_segs.append(mm(u1, dz, ta=True, out_dtype=BF16, name="dw_in_" + nm))
        du1 = mm(dz, w_seg, tb=True, add=du1, name="d_u1_" + nm)
    gw["w_in"] = jnp.concatenate([dw_small[:, :o_kpe_s + ROPE], dw_segs[0], dw_segs[1],
                                  dw_small[:, o_if_s:o_if_s + 2 * ml_h], dw_segs[2], dw_segs[3], dw_segs[4]], axis=1)
    grad_x, gw["g_mix"] = rms_bwd(x2d, g_mix, du1, dx1, "rms_mix_bwd")

    group_c = ["w_in", "w_qb", "w_kvb"]
    started_c = scatter_group(group_c, "c", grad_x)
    parts_a, others_a = split_wait("scatter_wait_a", started_a, 3 * len(group_a), _scatter_copies, grad_x)
    parts_b, others_b = split_wait("scatter_wait_b", started_b, 3 * len(group_b), _scatter_copies, grad_x)
    place_ab = place_arr + started_c[4][0, 0].astype(jnp.int32)
    grads, deltas, new_m, new_v = {}, {}, {}, {}

    def finish(group, parts_, others_, place):
        joined = join_halves([sum_chips(p, o, place) for p, o in zip(parts_, others_)])
        for n, j in zip(group, joined):
            grads[n] = j.reshape(-1, j.shape[2])
            deltas[n], new_m[n], new_v[n] = adamw(grads[n], wts[n][0], mom[n][0], var[n][0], "adamw_" + n)

    finish(group_a + group_b, parts_a + parts_b, others_a + others_b, place_ab)

    small_names = [n for n in names if n not in big_names]
    pieces = [loss_part]
    for n in small_names:
        flat = gw[n].reshape(1, -1)
        pieces.append(jnp.pad(flat, ((0, 0), (0, (-flat.shape[1]) % LANE))))
    packed = jnp.concatenate(pieces, axis=1)
    packed = jnp.pad(packed, ((0, 0), (0, (-packed.shape[1]) % (8 * LANE)))).reshape(-1, LANE)
    total = allreduce_small(packed).reshape(1, -1)
    loss = total[0, 0]
    small_grads, off = {}, LANE
    for n in small_names:
        size = gw[n].size
        g_full = total[:, off:off + size].reshape(gw[n].shape)
        off += size + (-size) % LANE
        if n in small_sharded:
            width = wts[n].shape[-1]
            g_full = lax.dynamic_slice_in_dim(g_full, chip * width, width, axis=g_full.ndim - 1)
        small_grads[n] = g_full.reshape(wts[n].shape[1:])

    def pack_small(tree):
        flat = jnp.concatenate([tree[n].reshape(1, -1) for n in small_names], axis=1)
        return jnp.pad(flat, ((0, 0), (0, (-flat.shape[1]) % (8 * LANE)))).reshape(8, -1)

    sg = pack_small(small_grads)
    sd, sm, sv = adamw(sg, pack_small({n: wts[n][0] for n in small_names}), pack_small({n: mom[n][0] for n in small_names}),
                       pack_small({n: var[n][0] for n in small_names}), "adamw_small")
    off = 0
    for n in small_names:
        size = small_grads[n].size
        shp = wts[n].shape[1:]
        grads[n] = small_grads[n]
        for dst, src in ((deltas, sd), (new_m, sm), (new_v, sv)):
            dst[n] = src.reshape(1, -1)[:, off:off + size].reshape(shp)
        off += size

    parts_c, others_c = split_wait("scatter_wait_c", started_c, 3 * len(group_c), _scatter_copies,
                                   (sv, new_v[group_b[-1]], new_v[group_a[0]]))
    finish(group_c, parts_c, others_c, place_arr)

    def out(tree):
        return [tree[n].reshape(wts[n].shape) for n in names]

    return (loss, grad_x.reshape(x.shape), *out(grads), *out(deltas), *out(new_m), *out(new_v))
```
